```python
import math
import jax, jax.numpy as jnp
from jax import lax
import numpy as np

D_MODEL = 1024
BATCH = 2
SEQ = 8192
DEPTH = 4
DEC_BATCH = 32
DEC_SEQ = 4
PAST_LEN = 8192
PAGE_SIZE = 128

HEAD_DIM = 64
POOL_WINDOWS = (2, 4, 8, 16)
POOL_GROUPS = len(POOL_WINDOWS)
POOL_WIDTH = D_MODEL // 4
POOL_GROUP_DIM = POOL_WIDTH // POOL_GROUPS
POOL_STATE = max(POOL_WINDOWS) - 1
NSA_WIDTH = D_MODEL // 2
NSA_HEADS = NSA_WIDTH // HEAD_DIM
NSA_KV_HEADS = max(1, NSA_HEADS // 4)
NSA_GROUP = NSA_HEADS // NSA_KV_HEADS
KV_WIDTH = NSA_KV_HEADS * HEAD_DIM
CMP_STRIDE = 16
CMP_BLOCK = 2 * CMP_STRIDE
SEL_BLOCK = 64
N_SELECT = 16
WINDOW = 512
N_BRANCHES = 3
FORCE_SCORE = 1000.0
Q_BLOCK = 128
GMLP_WIDTH = D_MODEL // 4
GMLP_GROUPS = 4
GMLP_GROUP_DIM = GMLP_WIDTH // GMLP_GROUPS
GMLP_CHUNK = 128
D_FF = 4 * D_MODEL
N_BUCKETS = 32
MAX_DISTANCE = 128
EPS = 1e-6
MIX_WIDTH = POOL_WIDTH + NSA_WIDTH + GMLP_WIDTH
PROJ_WIDTH = POOL_WIDTH + NSA_WIDTH + 6 * KV_WIDTH + NSA_HEADS * N_BRANCHES + 2 * GMLP_WIDTH

kernel_name = 'hybrid_pool_nsa_gmlp_decoder_step'


def rms_norm(x, g):
    x32 = x.astype(jnp.float32)
    y = x32 * lax.rsqrt(jnp.mean(x32 * x32, axis=-1, keepdims=True) + EPS)
    return (y * g.astype(jnp.float32)).astype(x.dtype)


def rel_bucket(dist):
    d = jnp.maximum(dist, 0)
    n_exact = N_BUCKETS // 2
    d_f = jnp.maximum(d, 1).astype(jnp.float32)
    large = n_exact + (jnp.log(d_f / n_exact) / math.log(MAX_DISTANCE / n_exact)
                       * (N_BUCKETS - n_exact)).astype(jnp.int32)
    large = jnp.minimum(large, N_BUCKETS - 1)
    return jnp.where(d < n_exact, d, large)


def head_bias(rel_bias, dist):
    return jnp.take(rel_bias, rel_bucket(dist), axis=0).astype(jnp.float32)


def masked_softmax(logits, mask):
    neg = jnp.finfo(jnp.float32).min
    z = jnp.where(mask, logits, neg)
    m = jnp.max(z, axis=-1, keepdims=True)
    p = jnp.where(mask, jnp.exp(z - m), 0.0)
    return p / jnp.maximum(jnp.sum(p, axis=-1, keepdims=True), 1e-30)


def mixer_inputs(h, w_in, gmlp_norm):
    B, T = h.shape[:2]
    sizes = (POOL_WIDTH, NSA_WIDTH, 2 * KV_WIDTH, 2 * KV_WIDTH, 2 * KV_WIDTH,
             NSA_HEADS * N_BRANCHES, GMLP_WIDTH, GMLP_WIDTH)
    offsets = [int(o) for o in np.cumsum(sizes)[:-1]]
    p = jnp.einsum('btd,df->btf', h, w_in)
    xp, q, kvc, kvs, kvw, gt, u, v = jnp.split(p, offsets, axis=-1)
    kv_shape = (B, T, 2, NSA_KV_HEADS, HEAD_DIM)
    q = q.reshape(B, T, NSA_HEADS, HEAD_DIM)
    gates = jax.nn.sigmoid(gt).reshape(B, T, NSA_HEADS, N_BRANCHES)
    u = jax.nn.gelu(u)
    v = rms_norm(jax.nn.gelu(v).reshape(B, T, GMLP_GROUPS, GMLP_GROUP_DIM),
                 gmlp_norm.reshape(GMLP_GROUPS, GMLP_GROUP_DIM)).reshape(B, T, GMLP_WIDTH)
    return xp, q, kvc.reshape(kv_shape), kvs.reshape(kv_shape), kvw.reshape(kv_shape), gates, u, v


def pool_mix(xp_ext, n_prev, pos, w_pool, pool_scale):
    B = xp_ext.shape[0]
    T = pos.shape[0]
    x32 = xp_ext.astype(jnp.float32)
    cs = jnp.concatenate([jnp.zeros_like(x32[:, :1]), jnp.cumsum(x32, axis=1)], axis=1)
    hi = n_prev + jnp.arange(T) + 1
    cur = x32[:, n_prev:]
    outs = []
    for g, w in enumerate(POOL_WINDOWS):
        sl = slice(g * POOL_GROUP_DIM, (g + 1) * POOL_GROUP_DIM)
        lo = jnp.maximum(hi - w, 0)
        win_sum = cs[:, hi, sl] - cs[:, lo, sl]
        cnt = jnp.minimum(pos + 1, w).astype(jnp.float32)[None, :, None]
        outs.append(win_sum / cnt - cur[:, :, sl])
    pooled = jnp.stack(outs, axis=2).astype(xp_ext.dtype)
    y = jnp.einsum('btgc,gcd->btgd', pooled, w_pool).reshape(B, T, POOL_WIDTH)
    return y * pool_scale


def gmlp_mix(u, v, ws, wb):
    B, T, _ = v.shape
    n_chunks = -(-T // GMLP_CHUNK)
    pad = n_chunks * GMLP_CHUNK - T
    vg = jnp.pad(v, ((0, 0), (0, pad), (0, 0))).reshape(B, n_chunks, GMLP_CHUNK, GMLP_GROUPS, GMLP_GROUP_DIM)
    causal = jnp.tril(jnp.ones((GMLP_CHUNK, GMLP_CHUNK), dtype=bool))
    ws_c = jnp.where(causal, ws, 0)
    s = jnp.einsum('gij,bnjgc->bnigc', ws_c, vg) + wb.T[:, :, None]
    s = s.reshape(B, n_chunks * GMLP_CHUNK, GMLP_WIDTH)[:, :T]
    return u * s


def compress(kv, cmp_w, cmp_pe):
    B, L = kv.shape[:2]
    n = L // CMP_STRIDE
    ch = kv[:, :n * CMP_STRIDE].reshape(B, n, CMP_STRIDE, 2, NSA_KV_HEADS, HEAD_DIM)
    lo = jnp.einsum('bnjehd,ejh->bnehd', ch, cmp_w[:, :CMP_STRIDE])
    hi = jnp.einsum('bnjehd,ejh->bnehd', ch, cmp_w[:, CMP_STRIDE:])
    pe_term = jnp.einsum('ejhd,ejh->ehd', cmp_pe, cmp_w)
    comp = lo[:, :-1] + hi[:, 1:] + pe_term
    ends = jnp.arange(n - 1) * CMP_STRIDE + CMP_BLOCK - 1
    return comp, ends


def sel_blocks(kv):
    B, L = kv.shape[:2]
    ns = -(-L // SEL_BLOCK)
    kv = jnp.pad(kv, ((0, 0), (0, ns * SEL_BLOCK - L), (0, 0), (0, 0), (0, 0)))
    kv = kv.reshape(B, ns, SEL_BLOCK, 2, NSA_KV_HEADS, HEAD_DIM).transpose(3, 0, 4, 1, 2, 5)
    return kv[0], kv[1]


def nsa_attend(q, q_pos, gates, kc, vc, c_end, ks_blk, vs_blk, kw, vw, w_pos, rel_bias):
    f32 = jnp.float32
    B, Tq = q.shape[:2]
    KVH, G = NSA_KV_HEADS, NSA_GROUP
    scale = HEAD_DIM ** -0.5
    qg = q.reshape(B, Tq, KVH, G, HEAD_DIM).transpose(0, 2, 3, 1, 4)

    nc = kc.shape[1]
    dc = q_pos[:, None] - c_end[None, :]
    bias_c = head_bias(rel_bias, dc).reshape(Tq, nc, KVH, G).transpose(2, 3, 0, 1)
    lc = jnp.einsum('bhgtd,bchd->bhgtc', qg, kc, preferred_element_type=f32) * scale
    pc = masked_softmax(lc + bias_c, dc >= 0)
    oc = jnp.einsum('bhgtc,bchd->bhgtd', pc.astype(vc.dtype), vc)

    ns = ks_blk.shape[2]
    ratio = SEL_BLOCK // CMP_STRIDE
    p_grp = jnp.sum(pc, axis=2)
    p_grp = jnp.pad(p_grp, ((0, 0), (0, 0), (0, 0), (0, ns * ratio - nc)))
    p_slc = jnp.sum(p_grp.reshape(B, KVH, Tq, ns, ratio), axis=-1)
    blk = jnp.arange(ns)[None, :]
    cur = (q_pos // SEL_BLOCK)[:, None]
    valid = blk <= cur
    forced = (blk == 0) | (blk == cur) | (blk == cur - 1)
    score = jnp.where(valid, jnp.where(forced, FORCE_SCORE, p_slc), -jnp.inf)
    n_sel = min(N_SELECT, ns)
    top_s, idx = lax.top_k(score, n_sel)
    chosen = top_s > -jnp.inf
    b_idx = jnp.arange(B)[:, None, None, None]
    h_idx = jnp.arange(KVH)[None, :, None, None]
    ksel = ks_blk[b_idx, h_idx, idx]
    vsel = vs_blk[b_idx, h_idx, idx]
    s_pos = idx[..., None] * SEL_BLOCK + jnp.arange(SEL_BLOCK)
    ds = q_pos[None, None, :, None, None] - s_pos
    ms = chosen[..., None] & (ds >= 0)
    tab = rel_bias.reshape(N_BUCKETS, KVH, G).transpose(1, 0, 2)
    bias_s = tab[jnp.arange(KVH)[None, :, None, None, None], rel_bucket(ds)]
    bias_s = jnp.moveaxis(bias_s, -1, 2).astype(f32)
    ls = jnp.einsum('bhgtd,bhtksd->bhgtks', qg, ksel, preferred_element_type=f32) * scale + bias_s
    ps = masked_softmax(ls.reshape(B, KVH, G, Tq, n_sel * SEL_BLOCK),
                        ms.reshape(B, KVH, 1, Tq, n_sel * SEL_BLOCK)).reshape(ls.shape)
    osl = jnp.einsum('bhgtks,bhtksd->bhgtd', ps.astype(vsel.dtype), vsel)

    sw = kw.shape[1]
    dw = q_pos[:, None] - w_pos[None, :]
    mw = (dw >= 0) & (dw < WINDOW) & (w_pos[None, :] >= 0)
    bias_w = head_bias(rel_bias, dw).reshape(Tq, sw, KVH, G).transpose(2, 3, 0, 1)
    lw = jnp.einsum('bhgtd,bshd->bhgts', qg, kw, preferred_element_type=f32) * scale
    pw = masked_softmax(lw + bias_w, mw)
    ow = jnp.einsum('bhgts,bshd->bhgtd', pw.astype(vw.dtype), vw)

    g = gates.transpose(0, 2, 1, 3).reshape(B, KVH, G, Tq, N_BRANCHES)
    o = g[..., 0:1] * oc + g[..., 1:2] * osl + g[..., 2:3] * ow
    return o.transpose(0, 3, 1, 2, 4).reshape(B, Tq, NSA_WIDTH)


def nsa_prompt(q, gates, kvc, kvs, kvw, cmp_w, cmp_pe, rel_bias):
    B, T = q.shape[:2]
    comp, c_end = compress(kvc, cmp_w, cmp_pe)
    kc, vc = comp[:, :, 0], comp[:, :, 1]
    ks_blk, vs_blk = sel_blocks(kvs)
    kw_pad = jnp.pad(kvw, ((0, 0), (WINDOW, 0), (0, 0), (0, 0), (0, 0)))
    nq = T // Q_BLOCK
    qb = q.reshape(B, nq, Q_BLOCK, NSA_HEADS, HEAD_DIM).swapaxes(0, 1)
    gb = gates.reshape(B, nq, Q_BLOCK, NSA_HEADS, N_BRANCHES).swapaxes(0, 1)

    def one_block(args):
        i, q_i, g_i = args
        q0 = i * Q_BLOCK
        q_pos = q0 + jnp.arange(Q_BLOCK)
        win = lax.dynamic_slice_in_dim(kw_pad, q0, WINDOW + Q_BLOCK, axis=1)
        w_pos = q0 - WINDOW + jnp.arange(WINDOW + Q_BLOCK)
        return nsa_attend(q_i, q_pos, g_i, kc, vc, c_end, ks_blk, vs_blk,
                          win[:, :, 0], win[:, :, 1], w_pos, rel_bias)

    out = lax.map(one_block, (jnp.arange(nq), qb, gb))
    return out.swapaxes(0, 1).reshape(B, T, NSA_WIDTH)


def gather_pages(pool, page_table):
    g = pool[page_table]
    return g.reshape((page_table.shape[0], -1) + pool.shape[2:])


def ffn(x, g, w_up, w_down):
    a = jax.nn.relu(jnp.einsum('btd,df->btf', rms_norm(x, g), w_up))
    return x + jnp.einsum('btf,fd->btd', a * a, w_down)


def setup_inputs(seed: int = 0) -> dict:
    key = jax.random.key(seed)
    ks = jax.random.split(key, 24)
    f32 = jnp.float32

    def nrm(k, shape, s):
        return s * jax.random.normal(k, shape, f32)

    n_pages = PAST_LEN // PAGE_SIZE
    n_used = DEC_BATCH * n_pages
    n_phys = n_used + n_used // 4
    kv_shape = (DEPTH, n_phys, PAGE_SIZE, 2, NSA_KV_HEADS, HEAD_DIM)
    return {
        'x_prompt': nrm(ks[0], (BATCH, SEQ, D_MODEL), 1.0),
        'x_sample': nrm(ks[1], (DEC_BATCH, DEC_SEQ, D_MODEL), 1.0),
        'cache_cmp_kv': nrm(ks[2], kv_shape, 1.0),
        'cache_slc_kv': nrm(ks[3], kv_shape, 1.0),
        'state_win_kv': nrm(ks[4], (DEPTH, DEC_BATCH, min(WINDOW, PAST_LEN), 2, NSA_KV_HEADS, HEAD_DIM), 1.0),
        'state_pool': nrm(ks[5], (DEPTH, DEC_BATCH, POOL_STATE, POOL_WIDTH), 1.0),
        'page_table': jax.random.permutation(ks[6], n_phys)[:n_used].reshape(DEC_BATCH, n_pages).astype(jnp.int32),
        'w_in': nrm(ks[7], (DEPTH, D_MODEL, PROJ_WIDTH), D_MODEL ** -0.5),
        'w_out': nrm(ks[8], (DEPTH, MIX_WIDTH, D_MODEL), MIX_WIDTH ** -0.5),
        'norm_mix': 1.0 + nrm(ks[9], (DEPTH, D_MODEL), 0.02),
        'norm_ffn': 1.0 + nrm(ks[10], (DEPTH, D_MODEL), 0.02),
        'norm_final': 1.0 + nrm(ks[11], (D_MODEL,), 0.02),
        'w_pool': nrm(ks[12], (DEPTH, POOL_GROUPS, POOL_GROUP_DIM, POOL_GROUP_DIM), POOL_GROUP_DIM ** -0.5),
        'pool_scale': 1.0 + nrm(ks[13], (DEPTH, POOL_WIDTH), 0.1),
        'cmp_w': (1.0 + nrm(ks[14], (DEPTH, 2, CMP_BLOCK, NSA_KV_HEADS), 0.1)) / CMP_BLOCK,
        'cmp_pe': nrm(ks[15], (DEPTH, 2, CMP_BLOCK, NSA_KV_HEADS, HEAD_DIM), 0.1),
        'gmlp_ws': nrm(ks[16], (DEPTH, GMLP_GROUPS, GMLP_CHUNK, GMLP_CHUNK), GMLP_CHUNK ** -0.5),
        'gmlp_b': 1.0 + nrm(ks[17], (DEPTH, GMLP_GROUPS, GMLP_CHUNK), 0.02),
        'gmlp_norm': 1.0 + nrm(ks[18], (DEPTH, GMLP_WIDTH), 0.02),
        'w_up': nrm(ks[19], (DEPTH, D_MODEL, D_FF), D_MODEL ** -0.5),
        'w_down': nrm(ks[20], (DEPTH, D_FF, D_MODEL), D_FF ** -0.5),
        'rel_bias': nrm(ks[21], (N_BUCKETS, NSA_HEADS), 0.5),
    }


def reference(x_prompt, x_sample, cache_cmp_kv, cache_slc_kv, state_win_kv, state_pool, page_table,
              w_in, w_out, norm_mix, norm_ffn, norm_final, w_pool, pool_scale, cmp_w, cmp_pe,
              gmlp_ws, gmlp_b, gmlp_norm, w_up, w_down, rel_bias):
    T = x_prompt.shape[1]
    TS = x_sample.shape[1]
    past = page_table.shape[1] * PAGE_SIZE
    pos_p = jnp.arange(T)
    pos_s = past + jnp.arange(TS)
    n_buf = state_win_kv.shape[2]
    w_pos_s = past - n_buf + jnp.arange(n_buf + TS)

    xp, xs = x_prompt, x_sample
    p_cmp, p_slc, p_win, p_pool = [], [], [], []
    s_cmp, s_slc, s_win, s_pool, s_v = [], [], [], [], []
    for l in range(DEPTH):
        h = rms_norm(xp, norm_mix[l])
        pin, q, kvc, kvs, kvw, gates, u, v = mixer_inputs(h, w_in[l], gmlp_norm[l])
        y_a = pool_mix(pin, 0, pos_p, w_pool[l], pool_scale[l])
        y_b = nsa_prompt(q, gates, kvc, kvs, kvw, cmp_w[l], cmp_pe[l], rel_bias)
        y_c = gmlp_mix(u, v, gmlp_ws[l], gmlp_b[l])
        mix = jnp.concatenate([y_a, y_b, y_c], axis=-1)
        xp = xp + jnp.einsum('btm,md->btd', mix, w_out[l])
        xp = ffn(xp, norm_ffn[l], w_up[l], w_down[l])
        p_cmp.append(kvc)
        p_slc.append(kvs)
        p_win.append(kvw[:, T - min(WINDOW, T):])
        p_pool.append(pin[:, T - POOL_STATE:])

        h = rms_norm(xs, norm_mix[l])
        sin, q, kvc, kvs, kvw, gates, u, v = mixer_inputs(h, w_in[l], gmlp_norm[l])
        pool_ext = jnp.concatenate([state_pool[l], sin], axis=1)
        y_a = pool_mix(pool_ext, POOL_STATE, pos_s, w_pool[l], pool_scale[l])
        ctx_c = jnp.concatenate([gather_pages(cache_cmp_kv[l], page_table), kvc], axis=1)
        ctx_s = jnp.concatenate([gather_pages(cache_slc_kv[l], page_table), kvs], axis=1)
        win_ctx = jnp.concatenate([state_win_kv[l], kvw], axis=1)
        comp, c_end = compress(ctx_c, cmp_w[l], cmp_pe[l])
        ks_blk, vs_blk = sel_blocks(ctx_s)
        y_b = nsa_attend(q, pos_s, gates, comp[:, :, 0], comp[:, :, 1], c_end, ks_blk, vs_blk,
                         win_ctx[:, :, 0], win_ctx[:, :, 1], w_pos_s, rel_bias)
        y_c = gmlp_mix(u, v, gmlp_ws[l], gmlp_b[l])
        mix = jnp.concatenate([y_a, y_b, y_c], axis=-1)
        xs = xs + jnp.einsum('btm,md->btd', mix, w_out[l])
        xs = ffn(xs, norm_ffn[l], w_up[l], w_down[l])
        n_keep = min(WINDOW, past + TS)
        s_cmp.append(kvc)
        s_slc.append(kvs)
        s_win.append(win_ctx[:, n_buf + TS - n_keep:])
        s_pool.append(pool_ext[:, pool_ext.shape[1] - POOL_STATE:])
        s_v.append(v)

    y_prompt = rms_norm(xp, norm_final)
    y_sample = rms_norm(xs, norm_final)
    new_cmp_kv_prompt = jnp.stack(p_cmp)
    new_slc_kv_prompt = jnp.stack(p_slc)
    new_win_kv_prompt = jnp.stack(p_win)
    new_pool_prompt = jnp.stack(p_pool)
    new_cmp_kv_sample = jnp.stack(s_cmp)
    new_slc_kv_sample = jnp.stack(s_slc)
    new_win_kv_sample = jnp.stack(s_win)
    new_pool_sample = jnp.stack(s_pool)
    new_gmlp_v_sample = jnp.stack(s_v)
    return (y_prompt, y_sample, new_cmp_kv_prompt, new_slc_kv_prompt, new_win_kv_prompt, new_pool_prompt,
            new_cmp_kv_sample, new_slc_kv_sample, new_win_kv_sample, new_pool_sample, new_gmlp_v_sample)
```

```python
import functools
import math

import numpy as np
import jax
import jax.numpy as jnp
from jax import lax
from jax.experimental import pallas as pl
from jax.experimental.pallas import tpu as pltpu

F32 = jnp.float32
BF16 = jnp.bfloat16

D_MODEL = 1024
DEPTH = 4
PAGE_SIZE = 128
HEAD_DIM = 64
POOL_WINDOWS = (2, 4, 8, 16)
POOL_WIDTH = 256
POOL_GROUP_DIM = 64
POOL_STATE = 15
NSA_WIDTH = 512
NSA_HEADS = 8
NSA_KV_HEADS = 2
NSA_GROUP = 4
KV_ROW = 2 * NSA_KV_HEADS * HEAD_DIM
CMP_STRIDE = 16
CMP_BLOCK = 32
SEL_BLOCK = 64
N_SELECT = 16
WINDOW = 512
N_BRANCHES = 3
FORCE_SCORE = 1000.0
GMLP_WIDTH = 256
GMLP_GROUPS = 4
GMLP_CHUNK = 128
D_FF = 4096
N_BUCKETS = 32
MAX_DISTANCE = 128
EPS = 1e-6
SCALE = HEAD_DIM ** -0.5

LANES = 128
SUBLANES = 8
VMEM_LIMIT = 56 * 1024 * 1024

NEG = -1e30
CMP_PAD = 16
CMP_NEAR = 24
TQ = 128
TK_FAR = 512

PROJ_SEGS = (('xp', 256), ('q', 512), ('kvc', 256), ('kvs', 256), ('kvw', 256), ('gt', 24), ('u', 256), ('v', 256))


def _cparams(sem):
    return pltpu.CompilerParams(dimension_semantics=sem, vmem_limit_bytes=VMEM_LIMIT)


def _dot(a, b):
    return jnp.dot(a, b, preferred_element_type=F32)


def _dot_nt(a, b):
    return lax.dot_general(a, b, (((1,), (1,)), ((), ())), preferred_element_type=F32)


def _gelu(x):
    c = math.sqrt(2.0 / math.pi)
    return 0.5 * x * (1.0 + jnp.tanh(c * (x + 0.044715 * (x * x * x))))


def _rms(x, g):
    return x * lax.rsqrt(jnp.mean(x * x, axis=-1, keepdims=True) + EPS) * g


def _inproj_kernel(x_ref, g_ref, w_ref, gn_ref, xp_ref, q_ref, kvc_ref, kvs_ref, kvw_ref, gt_ref, u_ref, v_ref):
    x = x_ref[...]
    h = _rms(x, g_ref[...]).astype(BF16)

    def seg(lo, hi):
        return _dot(h, w_ref[:, lo:hi])

    xp_ref[...] = seg(0, 256)
    q_ref[...] = (seg(256, 768) * SCALE).astype(BF16)
    kvc_ref[...] = seg(768, 1024)
    kvs_ref[...] = seg(1024, 1280)
    kvw_ref[...] = seg(1280, 1536)
    u_ref[...] = _gelu(seg(1536, 1792))
    gv = _gelu(seg(1792, 2048))
    sq = gv * gv
    lane = lax.broadcasted_iota(jnp.int32, sq.shape, 1)
    ms = jnp.zeros_like(sq)
    for g in range(GMLP_GROUPS):
        in_g = (lane >= g * 64) & (lane < (g + 1) * 64)
        s = jnp.sum(jnp.where(in_g, sq, 0.0), axis=-1, keepdims=True) * (1.0 / 64.0)
        ms = jnp.where(in_g, s, ms)
    v_ref[...] = gv * lax.rsqrt(ms + EPS) * gn_ref[...]
    gt_ref[...] = jax.nn.sigmoid(seg(2048, 2176))


def _inproj(x, g, w, gn, tm):
    R = x.shape[0]
    f = lambda n, dt=F32: jax.ShapeDtypeStruct((R, n), dt)
    row = lambda n: pl.BlockSpec((tm, n), lambda i: (i, 0))
    full = lambda a: pl.BlockSpec(a.shape, lambda i: (0,) * a.ndim)
    return pl.pallas_call(
        _inproj_kernel,
        grid=(R // tm,),
        in_specs=[row(D_MODEL), full(g), full(w), full(gn)],
        out_specs=[row(256), row(512), row(256), row(256), row(256), row(128), row(256), row(256)],
        out_shape=[f(256), f(512, BF16), f(256), f(256), f(256), f(128), f(256), f(256)],
        compiler_params=_cparams(("parallel",)),
        name="inproj",
    )(x, g, w, gn)


def _out_proj(x, ya, yb, yc, wo_ref):
    acc = _dot(ya.astype(BF16), wo_ref[0:256, :])
    acc = acc + _dot(yb, wo_ref[256:768, :])
    acc = acc + _dot(yc.astype(BF16), wo_ref[768:1024, :])
    return x + acc


def _pool_tail(win2, win4, win8, win16, cur, cnt, wp_ref, ps_ref):
    lane = lax.broadcasted_iota(jnp.int32, cur.shape, 1)
    win = jnp.where(lane < 64, win2, jnp.where(lane < 128, win4, jnp.where(lane < 192, win8, win16)))
    pooled = win / cnt - cur
    return _dot(pooled.astype(BF16), wp_ref[...]) * ps_ref[...]


def _mix_prompt_kernel(x_ref, xp_ref, halo_ref, yb_ref, u_ref, v_ref, wp_ref, ps_ref, ws_ref, wb_ref, wo_ref,
                       o_ref, ext_ref, *, tm, seq):
    i = pl.program_id(0)
    t0 = (i * tm) % seq
    halo = jnp.where(t0 > 0, halo_ref[...], 0.0)
    cur = xp_ref[...]
    ext_ref[0:16, :] = halo
    ext_ref[16:16 + tm, :] = cur
    e = ext_ref[...]
    b2 = e[1:] + e[:-1]
    b4 = b2[2:] + b2[:-2]
    b8 = b4[4:] + b4[:-4]
    b16 = b8[8:] + b8[:-8]
    win2 = b2[15:15 + tm]
    win4 = b4[13:13 + tm]
    win8 = b8[9:9 + tm]
    win16 = b16[1:1 + tm]
    rowi = lax.broadcasted_iota(jnp.int32, cur.shape, 0)
    lane = lax.broadcasted_iota(jnp.int32, cur.shape, 1)
    wsz = jnp.where(lane < 64, 2, jnp.where(lane < 128, 4, jnp.where(lane < 192, 8, 16)))
    cnt = jnp.minimum(t0 + rowi + 1, wsz).astype(F32)
    ya = _pool_tail(win2, win4, win8, win16, cur, cnt, wp_ref, ps_ref)

    ci = lax.broadcasted_iota(jnp.int32, (GMLP_CHUNK, GMLP_CHUNK), 0)
    cj = lax.broadcasted_iota(jnp.int32, (GMLP_CHUNK, GMLP_CHUNK), 1)
    lane_c = lax.broadcasted_iota(jnp.int32, (GMLP_CHUNK, GMLP_WIDTH), 1)
    wts = [jnp.where(ci >= cj, ws_ref[g], 0.0).astype(BF16) for g in range(GMLP_GROUPS)]
    parts = []
    for c in range(tm // GMLP_CHUNK):
        vc = v_ref[c * GMLP_CHUNK:(c + 1) * GMLP_CHUNK, :].astype(BF16)
        s = jnp.zeros((GMLP_CHUNK, GMLP_WIDTH), F32)
        for g in range(GMLP_GROUPS):
            sg = _dot(wts[g], vc)
            s = jnp.where((lane_c >= g * 64) & (lane_c < (g + 1) * 64), sg, s)
        parts.append(u_ref[c * GMLP_CHUNK:(c + 1) * GMLP_CHUNK, :] * (s + wb_ref[...]))
    yc = jnp.concatenate(parts, axis=0) if len(parts) > 1 else parts[0]

    o_ref[...] = _out_proj(x_ref[...], ya, yb_ref[...], yc, wo_ref)


def _mix_prompt(x, xp, yb, u, v, wp, ps, ws, wb, wo, tm, seq):
    R = x.shape[0]
    row = lambda n: pl.BlockSpec((tm, n), lambda i: (i, 0))
    full = lambda a: pl.BlockSpec(a.shape, lambda i: (0,) * a.ndim)
    halo = pl.BlockSpec((16, 256), lambda i: (jnp.maximum(i * (tm // 16) - 1, 0), 0))
    return pl.pallas_call(
        functools.partial(_mix_prompt_kernel, tm=tm, seq=seq),
        grid=(R // tm,),
        in_specs=[row(D_MODEL), row(256), halo, row(512), row(256), row(256),
                  full(wp), full(ps), full(ws), full(wb), full(wo)],
        out_specs=row(D_MODEL),
        out_shape=jax.ShapeDtypeStruct((R, D_MODEL), F32),
        scratch_shapes=[pltpu.VMEM((tm + 16, 256), F32)],
        compiler_params=_cparams(("parallel",)),
        name="mix_prompt",
    )(x, xp, xp, yb, u, v, wp, ps, ws, wb, wo)


def _mix_sample_kernel(x_ref, ext_ref, yb_ref, u_ref, v_ref, wp_ref, ps_ref, wsx_ref, wb_ref, wo_ref, o_ref,
                       *, nb, ts, past):
    lane = lax.broadcasted_iota(jnp.int32, (nb, POOL_WIDTH), 1)
    wsz = jnp.where(lane < 64, 2, jnp.where(lane < 128, 4, jnp.where(lane < 192, 8, 16)))
    yas, ycs = [], []
    for t in range(ts):
        top = POOL_STATE + t
        acc = ext_ref[top] + ext_ref[top - 1]
        wins = [acc]
        for w in (4, 8, 16):
            for j in range(w // 2, w):
                acc = acc + ext_ref[top - j]
            wins.append(acc)
        cnt = jnp.minimum(past + t + 1, wsz).astype(F32)
        yas.append(_pool_tail(wins[0], wins[1], wins[2], wins[3], ext_ref[top], cnt, wp_ref, ps_ref))
        s = wb_ref[t:t + 1, :]
        s = jnp.broadcast_to(s, (nb, GMLP_WIDTH))
        for j in range(t + 1):
            s = s + wsx_ref[t, j:j + 1, :] * v_ref[j]
        ycs.append(u_ref[t] * s)
    ya = jnp.concatenate(yas, axis=0)
    yc = jnp.concatenate(ycs, axis=0)
    o_ref[...] = _out_proj(x_ref[...], ya, yb_ref[...], yc, wo_ref)


def _mix_sample(x, ext, yb, u, v, wp, ps, wsx, wb4, wo, nb, ts, past):
    R = x.shape[0]
    args = (x, ext, yb, u, v, wp, ps, wsx, wb4, wo)
    full = lambda a: pl.BlockSpec(a.shape, lambda i: (0,) * a.ndim)
    return pl.pallas_call(
        functools.partial(_mix_sample_kernel, nb=nb, ts=ts, past=past),
        grid=(1,),
        in_specs=[full(a) for a in args],
        out_specs=pl.BlockSpec((R, D_MODEL), lambda i: (0, 0)),
        out_shape=jax.ShapeDtypeStruct((R, D_MODEL), F32),
        compiler_params=_cparams(("arbitrary",)),
        name="mix_sample",
    )(*args)


def _ffn_kernel(x_ref, g_ref, wu_ref, wd_ref, gf_ref, o_ref, *, final, fc):
    x = x_ref[...]
    h = _rms(x, g_ref[...]).astype(BF16)
    acc = x
    for c in range(D_FF // fc):
        a = jnp.maximum(_dot(h, wu_ref[:, c * fc:(c + 1) * fc]), 0.0)
        acc = acc + _dot((a * a).astype(BF16), wd_ref[c * fc:(c + 1) * fc, :])
    if final:
        acc = _rms(acc, gf_ref[...])
    o_ref[...] = acc


def _ffn(x, g, wu, wd, gf, tm, final):
    R = x.shape[0]
    row = pl.BlockSpec((tm, D_MODEL), lambda i: (i, 0))
    full = lambda a: pl.BlockSpec(a.shape, lambda i: (0,) * a.ndim)
    wfull = lambda a: pl.BlockSpec(a.shape, lambda i: (0,) * a.ndim, pipeline_mode=pl.Buffered(1))
    return pl.pallas_call(
        functools.partial(_ffn_kernel, final=final, fc=1024),
        grid=(R // tm,),
        in_specs=[row, full(g), wfull(wu), wfull(wd), full(gf)],
        out_specs=row,
        out_shape=jax.ShapeDtypeStruct((R, D_MODEL), F32),
        compiler_params=_cparams(("parallel",)),
        name="ffn",
    )(x, g, wu, wd, gf)


def _compress_rows(read_rows, n, wx):
    lo = jnp.zeros((n, wx.shape[1]), F32)
    hi = jnp.zeros((n, wx.shape[1]), F32)
    for j in range(CMP_STRIDE):
        xj = read_rows(j)
        lo = lo + xj * wx[j:j + 1, :]
        hi = hi + xj * wx[CMP_STRIDE + j:CMP_STRIDE + j + 1, :]
    return lo, hi


def _combine_halves(lo, hi, pe, wx):
    n = lo.shape[0]
    pe_term = jnp.sum(pe * wx, axis=0, keepdims=True)
    comp = lo + pltpu.roll(hi, n - 1, 0) + pe_term
    rowi = lax.broadcasted_iota(jnp.int32, comp.shape, 0)
    return jnp.where(rowi < n - 1, comp, 0.0)


def _compress_prompt_kernel(x_ref, wx_ref, pe_ref, o_ref, *, n):
    wx = wx_ref[...]
    lo, hi = _compress_rows(lambda j: x_ref[0, pl.ds(j, n, stride=CMP_STRIDE), :], n, wx)
    o_ref[0, 0:CMP_PAD, :] = jnp.zeros((CMP_PAD, LANES), F32)
    o_ref[0, CMP_PAD:CMP_PAD + n, :] = _combine_halves(lo, hi, pe_ref[...], wx)


def _compress_prompt(kvc, wx, pe):
    B, T, _ = kvc.shape
    n = T // CMP_STRIDE
    half = lambda rows: pl.BlockSpec((rows, LANES), lambda b, e: (0, e))
    return pl.pallas_call(
        functools.partial(_compress_prompt_kernel, n=n),
        grid=(B, KV_ROW // LANES),
        in_specs=[pl.BlockSpec((1, T, LANES), lambda b, e: (b, 0, e)), half(CMP_BLOCK), half(CMP_BLOCK)],
        out_specs=pl.BlockSpec((1, n + CMP_PAD, LANES), lambda b, e: (b, 0, e)),
        out_shape=jax.ShapeDtypeStruct((B, n + CMP_PAD, KV_ROW), F32),
        compiler_params=_cparams(("parallel", "parallel")),
        name="compress_prompt",
    )(kvc, wx, pe)


def _select_topk(score, n_pick):
    s_iota = lax.broadcasted_iota(jnp.int32, score.shape, 0)
    big = score.shape[0]
    sel = jnp.zeros(score.shape, F32)
    for _ in range(n_pick):
        m = jnp.max(score, axis=0, keepdims=True)
        idx = jnp.min(jnp.where(score == m, s_iota, big), axis=0, keepdims=True)
        pick = s_iota == idx
        sel = jnp.where(pick & (m >= 0.0), 1.0, sel)
        score = jnp.where(pick, -2.0, score)
    return sel


def _softmax_rows(z):
    m = jnp.max(z, axis=0, keepdims=True)
    p = jnp.where(z > 0.5 * NEG, jnp.exp(z - m), 0.0)
    den = jnp.maximum(jnp.sum(p, axis=0, keepdims=True), 1e-30)
    return p * (1.0 / den)


def _cmp_prompt_kernel(q_ref, kc_ref, vct_ref, nb_ref, cf_ref, oc_ref, sn_ref, raw_ref, z_ref, ps_ref, *, ncp):
    i = pl.program_id(2)
    hh = pl.program_id(1)
    r0 = pl.multiple_of(i * (TQ // CMP_STRIDE), SUBLANES)
    q = q_ref[0]
    kc = kc_ref[0, 0]
    vct = vct_ref[0, 0]
    rowi = lax.broadcasted_iota(jnp.int32, (ncp, TQ), 0)
    far = (rowi >= CMP_PAD) & (rowi < r0)
    rown = lax.broadcasted_iota(jnp.int32, (CMP_NEAR, TQ), 0) + r0
    psum = jnp.zeros((ncp, TQ), F32)
    octs = []
    for g in range(NSA_GROUP):
        qg = q[:, g * HEAD_DIM:(g + 1) * HEAD_DIM]
        raw_ref[...] = _dot_nt(kc, qg)
        cf = cf_ref[hh * NSA_GROUP + g]
        z_ref[...] = jnp.where(far, raw_ref[...] + cf[0:1, :], NEG)
        nb = nb_ref[hh * NSA_GROUP + g]
        zn = raw_ref[pl.ds(r0, CMP_NEAR), :] + nb
        z_ref[pl.ds(r0, CMP_NEAR), :] = jnp.where((rown >= CMP_PAD) & (nb > 0.5 * NEG), zn, NEG)
        pc = _softmax_rows(z_ref[...])
        psum = psum + pc
        octs.append(_dot(vct, pc.astype(BF16)))
    oc_ref[0] = jnp.concatenate(octs, axis=0).T

    ps_ref[0:ncp, :] = psum
    if ncp < ps_ref.shape[0]:
        ps_ref[ncp:, :] = jnp.zeros((ps_ref.shape[0] - ncp, TQ), F32)
    ns = LANES
    p_slc = ps_ref[pl.ds(CMP_PAD, ns, stride=4), :]
    for j in range(1, 4):
        p_slc = p_slc + ps_ref[pl.ds(CMP_PAD + j, ns, stride=4), :]
    s_iota = lax.broadcasted_iota(jnp.int32, (ns, TQ), 0)
    tt = lax.broadcasted_iota(jnp.int32, (ns, TQ), 1)
    cur = (i * TQ + tt) // SEL_BLOCK
    forced = (s_iota == 0) | (s_iota == cur) | (s_iota == cur - 1)
    score = jnp.where(s_iota <= cur, jnp.where(forced, FORCE_SCORE, p_slc), -1.0)
    sel = _select_topk(score, N_SELECT)
    sn_ref[0, 0] = jnp.where(sel.T > 0.5, 0.0, NEG).astype(BF16)


def _cmp_prompt(q, kc, vct, nbc, cfar):
    B, T, _ = q.shape
    ncp = kc.shape[2]
    ns = LANES
    nq = T // TQ
    full = lambda a: pl.BlockSpec(a.shape, lambda b, h, i: (0,) * a.ndim)
    return pl.pallas_call(
        functools.partial(_cmp_prompt_kernel, ncp=ncp),
        grid=(B, NSA_KV_HEADS, nq),
        in_specs=[pl.BlockSpec((1, TQ, 256), lambda b, h, i: (b, i, h)),
                  pl.BlockSpec((1, 1, ncp, HEAD_DIM), lambda b, h, i: (b, h, 0, 0)),
                  pl.BlockSpec((1, 1, HEAD_DIM, ncp), lambda b, h, i: (b, h, 0, 0)),
                  full(nbc), full(cfar)],
        out_specs=[pl.BlockSpec((1, TQ, 256), lambda b, h, i: (b, i, h)),
                   pl.BlockSpec((1, 1, TQ, ns), lambda b, h, i: (b, h, i, 0))],
        out_shape=[jax.ShapeDtypeStruct((B, T, NSA_WIDTH), F32),
                   jax.ShapeDtypeStruct((B, NSA_KV_HEADS, T, ns), BF16)],
        scratch_shapes=[pltpu.VMEM((ncp, TQ), F32), pltpu.VMEM((ncp, TQ), F32),
                        pltpu.VMEM((max(ncp, CMP_PAD + 4 * LANES), TQ), F32)],
        compiler_params=_cparams(("parallel", "parallel", "parallel")),
        name="cmp_prompt",
    )(q, kc, vct, nbc, cfar)


def _selwin_prompt_kernel(q_ref, sn_ref, ka_ref, va_ref, kw_ref, vw_ref, g_ref, oc_ref, nb_ref, o_ref,
                          ms_ref, as_ref, mw_ref, aw_ref):
    i = pl.program_id(2)
    R4 = NSA_GROUP * TQ
    q = q_ref[0]
    q4 = jnp.concatenate([q[:, g * HEAD_DIM:(g + 1) * HEAD_DIM] for g in range(NSA_GROUP)], axis=0)
    sn = sn_ref[0, 0]
    qa = jnp.concatenate([jnp.concatenate([sn] * NSA_GROUP, axis=0), q4], axis=1)
    rowi = lax.broadcasted_iota(jnp.int32, (R4, LANES), 0)
    kk = lax.broadcasted_iota(jnp.int32, (R4, LANES), 1)
    dd0 = (rowi % TQ) - kk

    ms_ref[...] = jnp.full((R4, LANES), NEG, F32)
    mw_ref[...] = jnp.full((R4, LANES), NEG, F32)
    as_ref[...] = jnp.zeros((R4, LANES), F32)
    aw_ref[...] = jnp.zeros((R4, LANES), F32)

    def update(m_ref, a_ref, z, v):
        m_prev = m_ref[...]
        m_new = jnp.maximum(m_prev, jnp.max(z, axis=1, keepdims=True))
        reps = z.shape[1] // LANES
        m_rep = m_new if reps == 1 else jnp.concatenate([m_new] * reps, axis=1)
        p = jnp.exp(z - m_rep)
        a_ref[...] = jnp.exp(m_prev - m_new) * a_ref[...] + _dot(p.astype(BF16), v)
        m_ref[...] = m_new

    n_far = jnp.maximum(i - 1, 0) // (TK_FAR // TQ)
    n_near = i - n_far * (TK_FAR // TQ) + 1

    def near_body(delta, carry):
        k0 = pl.multiple_of((i - delta) * TQ, TQ)
        nb = nb_ref[0, jnp.minimum(delta, 2)]
        d = dd0 + delta * TQ

        @pl.when(delta < n_near)
        def _():
            zs = _dot_nt(qa, ka_ref[0, 0, pl.ds(k0, TQ), :]) + nb
            zs = jnp.where(d >= 0, zs, NEG)
            update(ms_ref, as_ref, zs, va_ref[0, 0, pl.ds(k0, TQ), :])

        zw = _dot_nt(q4, kw_ref[0, 0, pl.ds(k0, TQ), :]) + nb
        zw = jnp.where((d >= 0) & (d < WINDOW), zw, NEG)
        update(mw_ref, aw_ref, zw, vw_ref[0, 0, pl.ds(k0, TQ), :])
        return carry

    lax.fori_loop(0, jnp.minimum(i, WINDOW // TQ) + 1, near_body, 0)

    def far_body(j, carry):
        k0 = pl.multiple_of(j * TK_FAR, TK_FAR)
        zs = _dot_nt(qa, ka_ref[0, 0, pl.ds(k0, TK_FAR), :])
        update(ms_ref, as_ref, zs, va_ref[0, 0, pl.ds(k0, TK_FAR), :])
        return carry

    lax.fori_loop(0, n_far, far_body, 0)

    a_s = as_ref[...]
    a_w = aw_ref[...]
    o_s = a_s[:, 0:HEAD_DIM] / jnp.maximum(a_s[:, HEAD_DIM:2 * HEAD_DIM], 1e-30)
    o_w = a_w[:, 0:HEAD_DIM] / jnp.maximum(a_w[:, HEAD_DIM:2 * HEAD_DIM], 1e-30)
    gt = g_ref[0, 0]
    oc = oc_ref[0]
    outs = []
    for g in range(NSA_GROUP):
        sl = slice(g * TQ, (g + 1) * TQ)
        o = (gt[:, 3 * g:3 * g + 1] * oc[:, g * HEAD_DIM:(g + 1) * HEAD_DIM]
             + gt[:, 3 * g + 1:3 * g + 2] * o_s[sl] + gt[:, 3 * g + 2:3 * g + 3] * o_w[sl])
        outs.append(o)
    o_ref[0] = jnp.concatenate(outs, axis=1).astype(BF16)


def _selwin_prompt(q, sn, ka, va, kw, vw, gates, oc, nbs):
    B, T, _ = q.shape
    nq = T // TQ
    R4 = NSA_GROUP * TQ
    perbh = lambda n: pl.BlockSpec((1, 1, T, n), lambda b, h, i: (b, h, 0, 0))
    tile = lambda n: pl.BlockSpec((1, 1, TQ, n), lambda b, h, i: (b, h, i, 0))
    return pl.pallas_call(
        _selwin_prompt_kernel,
        grid=(B, NSA_KV_HEADS, nq),
        in_specs=[pl.BlockSpec((1, TQ, 256), lambda b, h, i: (b, i, h)),
                  tile(sn.shape[-1]), perbh(ka.shape[-1]), perbh(va.shape[-1]), perbh(kw.shape[-1]),
                  perbh(vw.shape[-1]), tile(LANES),
                  pl.BlockSpec((1, TQ, 256), lambda b, h, i: (b, i, h)),
                  pl.BlockSpec((1, 3, R4, LANES), lambda b, h, i: (h, 0, 0, 0))],
        out_specs=pl.BlockSpec((1, TQ, 256), lambda b, h, i: (b, i, h)),
        out_shape=jax.ShapeDtypeStruct((B, T, NSA_WIDTH), BF16),
        scratch_shapes=[pltpu.VMEM((R4, LANES), F32)] * 4,
        compiler_params=_cparams(("parallel", "parallel", "arbitrary")),
        name="selwin_prompt",
    )(q, sn, ka, va, kw, vw, gates, oc, nbs)


def _rel_bucket(dist):
    d = jnp.maximum(dist, 0)
    n_exact = N_BUCKETS // 2
    d_f = jnp.maximum(d, 1).astype(F32)
    large = n_exact + (jnp.log(d_f / n_exact) / math.log(MAX_DISTANCE / n_exact)
                       * (N_BUCKETS - n_exact)).astype(jnp.int32)
    large = jnp.minimum(large, N_BUCKETS - 1)
    return jnp.where(d < n_exact, d, large)


def _head_bias(rel_bias, dist):
    return jnp.take(rel_bias, _rel_bucket(dist), axis=0).astype(F32)


def _prompt_bias_tables(rel_bias):
    far = rel_bias[N_BUCKETS - 1].astype(F32)
    cfar = jnp.broadcast_to(far[:, None, None], (NSA_HEADS, SUBLANES, LANES))
    k = jnp.arange(CMP_NEAR)[:, None]
    tt = jnp.arange(TQ)[None, :]
    d = tt + CMP_PAD * CMP_STRIDE - (CMP_BLOCK - 1) - CMP_STRIDE * k
    nbc = jnp.where((d >= 0)[..., None], _head_bias(rel_bias, d), NEG).transpose(2, 0, 1)
    tq = jnp.arange(TQ)[:, None]
    kk = jnp.arange(TQ)[None, :]
    tiles = []
    for delta in (0, 1):
        b = _head_bias(rel_bias, delta * TQ + tq - kk) - far
        tiles.append(b.transpose(2, 0, 1))
    tiles.append(jnp.zeros_like(tiles[0]))
    nbs = jnp.stack(tiles, axis=1)
    nbs = nbs.reshape(NSA_KV_HEADS, NSA_GROUP, 3, TQ, TQ).transpose(0, 2, 1, 3, 4)
    return cfar, nbc, nbs.reshape(NSA_KV_HEADS, 3, NSA_GROUP * TQ, TQ)


def _prep_layer(l, w_in, w_out, norm_mix, norm_ffn, w_pool, pool_scale, cmp_w, cmp_pe, gmlp_ws, gmlp_b,
                gmlp_norm, w_up, w_down):
    w = w_in[l]
    offs = np.cumsum([0] + [n for _, n in PROJ_SEGS])
    seg = {name: w[:, offs[k]:offs[k + 1]] for k, (name, _) in enumerate(PROJ_SEGS)}
    order = ('xp', 'q', 'kvc', 'kvs', 'kvw', 'u', 'v', 'gt')
    w_perm = jnp.concatenate([seg[n] for n in order] + [jnp.zeros((D_MODEL, LANES - 24), w.dtype)], axis=1)
    eye = jnp.eye(len(POOL_WINDOWS), dtype=F32)
    wp = jnp.einsum('gcd,gh->gchd', w_pool[l], eye).reshape(POOL_WIDTH, POOL_WIDTH)
    wx = jnp.broadcast_to(cmp_w[l].transpose(1, 0, 2)[..., None], (CMP_BLOCK, 2, NSA_KV_HEADS, HEAD_DIM))
    return dict(
        w_in=w_perm.astype(BF16), g_mix=norm_mix[l][None, :], g_ffn=norm_ffn[l][None, :],
        gn=gmlp_norm[l][None, :], wp=wp.astype(BF16), ps=pool_scale[l][None, :],
        wx=wx.reshape(CMP_BLOCK, KV_ROW), pe=cmp_pe[l].transpose(1, 0, 2, 3).reshape(CMP_BLOCK, KV_ROW),
        ws=gmlp_ws[l], wb=jnp.repeat(gmlp_b[l].T, POOL_GROUP_DIM, axis=1),
        wsx=jnp.repeat(gmlp_ws[l].transpose(1, 2, 0), POOL_GROUP_DIM, axis=2)[:8, :8],
        w_out=w_out[l].astype(BF16), w_up=w_up[l].astype(BF16), w_down=w_down[l].astype(BF16))


def _split_heads(kv, part):
    B, T, _ = kv.shape
    return kv.reshape(B, T, 2, NSA_KV_HEADS, HEAD_DIM)[:, :, part].transpose(0, 2, 1, 3).astype(BF16)


def _prompt_layer(x, p, tabs, B, T, final, gf):
    cfar, nbc, nbs = tabs
    tm = min(512, B * T)
    xp, q, kvc, kvs, kvw, gt, u, v = _inproj(x, p['g_mix'], p['w_in'], p['gn'], tm)
    comp = _compress_prompt(kvc.reshape(B, T, KV_ROW), p['wx'], p['pe'])
    kc = _split_heads(comp, 0)
    vct = _split_heads(comp, 1).transpose(0, 1, 3, 2)
    q3 = q.reshape(B, T, NSA_WIDTH)
    oc, sn = _cmp_prompt(q3, kc, vct, nbc, cfar)
    blk = (jnp.arange(T)[:, None] // SEL_BLOCK == jnp.arange(LANES)[None, :]).astype(BF16)
    kvs3, kvw3 = kvs.reshape(B, T, KV_ROW), kvw.reshape(B, T, KV_ROW)
    ones = jnp.ones((B, NSA_KV_HEADS, T, HEAD_DIM), BF16)
    ka = jnp.concatenate([jnp.broadcast_to(blk, (B, NSA_KV_HEADS, T, LANES)), _split_heads(kvs3, 0)], axis=-1)
    va = jnp.concatenate([_split_heads(kvs3, 1), ones], axis=-1)
    kw = _split_heads(kvw3, 0)
    vw = jnp.concatenate([_split_heads(kvw3, 1), ones], axis=-1)
    g4 = gt[:, :NSA_HEADS * N_BRANCHES].reshape(B, T, NSA_KV_HEADS, NSA_GROUP * N_BRANCHES).transpose(0, 2, 1, 3)
    g4 = jnp.pad(g4, ((0, 0), (0, 0), (0, 0), (0, LANES - NSA_GROUP * N_BRANCHES)))
    yb = _selwin_prompt(q3, sn, ka, va, kw, vw, g4, oc, nbs)
    x1 = _mix_prompt(x, xp, yb.reshape(B * T, NSA_WIDTH), u, v, p['wp'], p['ps'], p['ws'], p['wb'], p['w_out'],
                     tm, T)
    x2 = _ffn(x1, p['g_ffn'], p['w_up'], p['w_down'], gf, tm, final)
    return x2, (xp, kvc, kvs, kvw)


PAGES_PER_STEP = 8
SEL_PER_PAGE = PAGE_SIZE // SEL_BLOCK


def _cmp_sample_kernel(pt_ref, *refs, ts):
    del pt_ref
    npg = PAGES_PER_STEP
    nhalf = KV_ROW // LANES
    pages = refs[:npg * nhalf]
    wx_ref, pe_ref, q_ref, tab_ref, oc_ref, sel_ref, lo_ref, hi_ref, ps_ref = refs[npg * nhalf:]
    s = pl.program_id(1)
    rows = PAGE_SIZE // CMP_STRIDE
    for k in range(npg):
        for e in range(nhalf):
            page = pages[k * nhalf + e]
            lo, hi = _compress_rows(lambda j, page=page: page[0, pl.ds(j, rows, stride=CMP_STRIDE), :], rows,
                                    wx_ref[:, e * LANES:(e + 1) * LANES])
            lo_ref[s, k * rows:(k + 1) * rows, e * LANES:(e + 1) * LANES] = lo
            hi_ref[s, k * rows:(k + 1) * rows, e * LANES:(e + 1) * LANES] = hi

    @pl.when(s == pl.num_programs(1) - 1)
    def _():
        nc = lo_ref.shape[0] * lo_ref.shape[1]
        comp = _combine_halves(lo_ref[...].reshape(nc, KV_ROW), hi_ref[...].reshape(nc, KV_ROW), pe_ref[...],
                               wx_ref[...])
        ns = nc // 4
        lane = lax.broadcasted_iota(jnp.int32, (ns, LANES), 1)
        s_iota = lax.broadcasted_iota(jnp.int32, (ns, LANES), 0)
        ncol = NSA_GROUP * ts
        selacc = jnp.zeros((ns, LANES), F32)
        for h in range(NSA_KV_HEADS):
            kc = comp[:, h * HEAD_DIM:(h + 1) * HEAD_DIM].astype(BF16)
            vc = comp[:, (NSA_KV_HEADS + h) * HEAD_DIM:(NSA_KV_HEADS + h + 1) * HEAD_DIM].astype(BF16)
            pc = _softmax_rows(_dot_nt(kc, q_ref[0, h]) + tab_ref[h])
            oc_ref[0, h] = pl.dot(pc.astype(BF16), vc, trans_a=True)[0:ncol, :]
            ps = pc
            for g in range(1, NSA_GROUP):
                ps = ps + pltpu.roll(pc, g * ts, 1)
            ps_ref[...] = ps
            p_slc = ps_ref[pl.ds(0, ns, stride=4), :]
            for j in range(1, 4):
                p_slc = p_slc + ps_ref[pl.ds(j, ns, stride=4), :]
            forced = (s_iota == 0) | (s_iota == ns - 1)
            sel = _select_topk(jnp.where(forced, FORCE_SCORE, p_slc), N_SELECT - 1)
            for g in range(NSA_GROUP):
                dst = h * ncol + g * ts
                shift = (dst - (ncol - ts)) % LANES
                moved = pltpu.roll(sel, shift, 1) if shift else sel
                selacc = jnp.where((lane >= dst) & (lane < dst + ts), moved, selacc)
        sel_ref[0] = selacc.reshape(sel_ref.shape[1:])


def _page_specs(n_pages, lanes=KV_ROW):
    def spec(k, e):
        return pl.BlockSpec((1, PAGE_SIZE, lanes),
                            lambda b, s, pt: (pt[b * n_pages + s * PAGES_PER_STEP + k], 0, e))
    return [spec(k, e) for k in range(PAGES_PER_STEP) for e in range(KV_ROW // lanes)]


def _cmp_sample(pt, cache, wx, pe, qs, tab, nb, n_pages, ts):
    steps = n_pages // PAGES_PER_STEP
    nc = n_pages * PAGE_SIZE // CMP_STRIDE
    rows = PAGES_PER_STEP * PAGE_SIZE // CMP_STRIDE
    ns = nc // 4
    full = lambda a: pl.BlockSpec(a.shape, lambda b, s, pt: (0,) * a.ndim)
    grid_spec = pltpu.PrefetchScalarGridSpec(
        num_scalar_prefetch=1,
        grid=(nb, steps),
        in_specs=_page_specs(n_pages, LANES) + [
            full(wx), full(pe),
            pl.BlockSpec((1, NSA_KV_HEADS, LANES, HEAD_DIM), lambda b, s, pt: (b, 0, 0, 0)),
            full(tab)],
        out_specs=[pl.BlockSpec((1, NSA_KV_HEADS, NSA_GROUP * ts, HEAD_DIM), lambda b, s, pt: (b, 0, 0, 0)),
                   pl.BlockSpec((1, steps, ns // steps, LANES), lambda b, s, pt: (b, 0, 0, 0))],
        scratch_shapes=[pltpu.VMEM((steps, rows, KV_ROW), F32), pltpu.VMEM((steps, rows, KV_ROW), F32),
                        pltpu.VMEM((nc, LANES), F32)])
    return pl.pallas_call(
        functools.partial(_cmp_sample_kernel, ts=ts),
        grid_spec=grid_spec,
        out_shape=[jax.ShapeDtypeStruct((nb, NSA_KV_HEADS, NSA_GROUP * ts, HEAD_DIM), F32),
                   jax.ShapeDtypeStruct((nb, steps, ns // steps, LANES), F32)],
        compiler_params=_cparams(("parallel", "arbitrary")),
        name="cmp_sample",
    )(pt, *([cache] * (PAGES_PER_STEP * (KV_ROW // LANES))), wx, pe, qs, tab)


def _local_softmax(z, mask):
    m = jnp.max(z, axis=0, keepdims=True)
    p = jnp.where(mask, jnp.exp(z - m), 0.0)
    return m, p, jnp.sum(p, axis=0, keepdims=True)


def _selwin_sample_kernel(pt_ref, *refs, ts):
    del pt_ref
    npg = PAGES_PER_STEP
    pages = refs[:npg]
    (qa_ref, sel_ref, cf_ref, nbl_ref, st_ref, kvn_ref, nbn_ref, nbw_ref, g_ref, oc_ref,
     o_ref, m_ref, l_ref, a_ref) = refs[npg:]
    s = pl.program_id(1)
    last = pl.num_programs(1) - 1
    ncol = NSA_KV_HEADS * NSA_GROUP * ts
    qa = qa_ref[0]
    rowi = lax.broadcasted_iota(jnp.int32, (PAGE_SIZE, LANES), 0)
    for k in range(npg):
        pg = pages[k][0].astype(BF16)
        is_last_page = jnp.logical_and(s == last, k == npg - 1)
        bias = jnp.where(is_last_page, nbl_ref[...], cf_ref[0:1, :])
        s0 = sel_ref[0, s, SEL_PER_PAGE * k:SEL_PER_PAGE * k + 1, :]
        s1 = sel_ref[0, s, SEL_PER_PAGE * k + 1:SEL_PER_PAGE * k + 2, :]
        mask = jnp.where(rowi < SEL_BLOCK, s0, s1) > 0.5
        z = jnp.where(mask, _dot(pg, qa) + bias, NEG)
        m, p, l = _local_softmax(z, mask)
        m_ref[s, k:k + 1, :] = m
        l_ref[s, k:k + 1, :] = l
        a_ref[s, k] = pl.dot(p.astype(BF16), pg, trans_a=True)[0:ncol, :]

    @pl.when(s == last)
    def _():
        nbn = nbn_ref[...]
        okn = nbn > 0.5 * NEG
        kn = kvn_ref[0, 0:SUBLANES, :].astype(BF16)
        m_n, p_n, l_n = _local_softmax(jnp.where(okn, _dot(kn, qa) + nbn, NEG), okn)
        a_n = pl.dot(p_n.astype(BF16), kn, trans_a=True)[0:ncol, :]
        nparts = m_ref.shape[0] * m_ref.shape[1]
        m_all = m_ref[...].reshape(nparts, LANES)
        l_all = l_ref[...].reshape(nparts, LANES)
        m_g = jnp.maximum(jnp.max(m_all, axis=0, keepdims=True), m_n)
        w_all = jnp.exp(m_all - m_g)
        w_n = jnp.exp(m_n - m_g)
        l_g = jnp.sum(w_all * l_all, axis=0, keepdims=True) + w_n * l_n

        nbw = nbw_ref[...]
        okw = nbw > 0.5 * NEG
        st = st_ref[0].astype(BF16)
        kwn = kvn_ref[0, SUBLANES:2 * SUBLANES, :].astype(BF16)
        zw = jnp.where(okw, _dot(st, qa) + nbw, NEG)
        zwn = jnp.where(okn, _dot(kwn, qa) + nbn, NEG)
        m_w = jnp.maximum(jnp.max(zw, axis=0, keepdims=True), jnp.max(zwn, axis=0, keepdims=True))
        pw = jnp.where(okw, jnp.exp(zw - m_w), 0.0)
        pwn = jnp.where(okn, jnp.exp(zwn - m_w), 0.0)
        l_w = jnp.sum(pw, axis=0, keepdims=True) + jnp.sum(pwn, axis=0, keepdims=True)
        a_w = (pl.dot(pw.astype(BF16), st, trans_a=True) + pl.dot(pwn.astype(BF16), kwn, trans_a=True))[0:ncol, :]

        assert nparts + 4 * SUBLANES <= LANES
        rows8 = lambda r: jnp.broadcast_to(r, (SUBLANES, LANES))
        x = jnp.concatenate([w_all, rows8(w_n), rows8(l_g), rows8(l_w),
                             jnp.zeros((LANES - nparts - 3 * SUBLANES, LANES), F32)], axis=0)
        xt = x.T
        acc = xt[0:ncol, nparts:nparts + 1] * a_n
        for pidx in range(nparts):
            acc = acc + xt[0:ncol, pidx:pidx + 1] * a_ref[pidx // npg, pidx % npg]
        o_s = acc / jnp.maximum(xt[0:ncol, nparts + SUBLANES:nparts + SUBLANES + 1], 1e-30)
        o_w = a_w / jnp.maximum(xt[0:ncol, nparts + 2 * SUBLANES:nparts + 2 * SUBLANES + 1], 1e-30)

        ri = lax.broadcasted_iota(jnp.int32, (ncol, HEAD_DIM), 0)
        v_of = lambda t: jnp.where(ri < ncol // NSA_KV_HEADS, t[:, 2 * HEAD_DIM:3 * HEAD_DIM],
                                   t[:, 3 * HEAD_DIM:4 * HEAD_DIM])
        gt = g_ref[0]
        oc = oc_ref[0].reshape(ncol, HEAD_DIM)
        o_ref[0] = gt[:, 0:1] * oc + gt[:, 1:2] * v_of(o_s) + gt[:, 2:3] * v_of(o_w)


def _selwin_sample(pt, cache, qa, sel, cfrow, nbl, state, st_off, kvn, nbn, nbw, gates, oc, nb, n_pages, ts):
    steps = n_pages // PAGES_PER_STEP
    ncol = NSA_KV_HEADS * NSA_GROUP * ts
    n_buf = state.shape[1]
    full = lambda a: pl.BlockSpec(a.shape, lambda b, s, pt: (0,) * a.ndim)
    perb = lambda a: pl.BlockSpec((1,) + a.shape[1:], lambda b, s, pt: (b,) + (0,) * (a.ndim - 1))
    grid_spec = pltpu.PrefetchScalarGridSpec(
        num_scalar_prefetch=1,
        grid=(nb, steps),
        in_specs=_page_specs(n_pages) + [
            perb(qa), perb(sel), full(cfrow), full(nbl),
            pl.BlockSpec((1, n_buf, KV_ROW), lambda b, s, pt: (st_off + b, 0, 0)),
            perb(kvn), full(nbn), full(nbw), perb(gates), perb(oc)],
        out_specs=pl.BlockSpec((1, ncol, HEAD_DIM), lambda b, s, pt: (b, 0, 0)),
        scratch_shapes=[pltpu.VMEM((steps, PAGES_PER_STEP, LANES), F32),
                        pltpu.VMEM((steps, PAGES_PER_STEP, LANES), F32),
                        pltpu.VMEM((steps, PAGES_PER_STEP, ncol, KV_ROW), F32)])
    return pl.pallas_call(
        functools.partial(_selwin_sample_kernel, ts=ts),
        grid_spec=grid_spec,
        out_shape=jax.ShapeDtypeStruct((nb, ncol, HEAD_DIM), F32),
        compiler_params=_cparams(("parallel", "arbitrary")),
        name="selwin_sample",
    )(pt, *([cache] * PAGES_PER_STEP), qa, sel, cfrow, nbl, state, kvn, nbn, nbw, gates, oc)


def _sample_bias_tables(rel_bias, past, ts, n_buf):
    ncol = NSA_KV_HEADS * NSA_GROUP * ts
    col = jnp.arange(ncol)
    head = col // ts
    t = col % ts
    pick = lambda b: jnp.take_along_axis(b, jnp.broadcast_to(head, b.shape[:-1])[..., None], axis=-1)[..., 0]
    bias = lambda d: pick(_head_bias(rel_bias, d))
    padc = lambda a, fill=0.0: jnp.pad(a, ((0, 0), (0, LANES - a.shape[1])), constant_values=fill)
    nc = past // CMP_STRIDE
    c_end = jnp.arange(nc)[:, None] * CMP_STRIDE + CMP_BLOCK - 1
    tab = jnp.where(jnp.arange(nc)[:, None] < nc - 1, bias(past + t[None, :] - c_end), NEG)
    hc = ncol // NSA_KV_HEADS
    tab_cmp = jnp.stack([padc(tab[:, h * hc:(h + 1) * hc]) for h in range(NSA_KV_HEADS)])
    far = rel_bias[N_BUCKETS - 1].astype(F32)[head]
    cfrow = jnp.broadcast_to(padc(far[None, :]), (SUBLANES, LANES))
    kk = jnp.arange(PAGE_SIZE)[:, None]
    nbl = padc(bias(PAGE_SIZE + t[None, :] - kk))
    j = jnp.arange(SUBLANES)[:, None]
    dn = t[None, :] - j
    nbn = padc(jnp.where((dn >= 0) & (j < ts), bias(dn), NEG))
    r = jnp.arange(n_buf)[:, None]
    dw = n_buf + t[None, :] - r
    nbw = padc(jnp.where((dw >= 0) & (dw < WINDOW), bias(dw), NEG))
    return tab_cmp, cfrow, nbl, nbn, nbw


def _sample_layer(x, p, tabs, l, pt, cache_cmp, cache_slc, state_win, state_pool_l, nb, ts, n_pages, final, gf):
    tab_cmp, cfrow, nbl, nbn, nbw = tabs
    past = n_pages * PAGE_SIZE
    R = ts * nb
    ncol = NSA_KV_HEADS * NSA_GROUP * ts
    xp, q, kvc, kvs, kvw, gt, u, v = _inproj(x, p['g_mix'], p['w_in'], p['gn'], R)
    q5 = q.reshape(ts, nb, NSA_KV_HEADS, NSA_GROUP, HEAD_DIM)
    qs = q5.transpose(1, 2, 3, 0, 4).reshape(nb, NSA_KV_HEADS, NSA_GROUP * ts, HEAD_DIM)
    qs = jnp.pad(qs, ((0, 0), (0, 0), (0, LANES - NSA_GROUP * ts), (0, 0)))
    oc, sel = _cmp_sample(pt, cache_cmp, p['wx'], p['pe'], qs, tab_cmp, nb, n_pages, ts)
    qt = q5.transpose(1, 2, 4, 3, 0).reshape(nb, NSA_KV_HEADS, HEAD_DIM, NSA_GROUP * ts)
    eye = jnp.eye(NSA_KV_HEADS, dtype=qt.dtype)
    qa = jnp.einsum('bhdc,hk->bhdkc', qt, eye).reshape(nb, NSA_KV_HEADS * HEAD_DIM, ncol)
    qa = jnp.pad(qa, ((0, 0), (0, KV_ROW - NSA_KV_HEADS * HEAD_DIM), (0, LANES - ncol)))
    rows_of = lambda a: jnp.pad(a.reshape(ts, nb, KV_ROW).transpose(1, 0, 2), ((0, 0), (0, SUBLANES - ts), (0, 0)))
    kvn = jnp.concatenate([rows_of(kvs), rows_of(kvw)], axis=1)
    g5 = gt[:, :NSA_HEADS * N_BRANCHES].reshape(ts, nb, NSA_KV_HEADS * NSA_GROUP, N_BRANCHES)
    gates = jnp.pad(g5.transpose(1, 2, 0, 3).reshape(nb, ncol, N_BRANCHES), ((0, 0), (0, 0), (0, LANES - N_BRANCHES)))
    o = _selwin_sample(pt, cache_slc, qa, sel, cfrow, nbl, state_win, l * nb, kvn, nbn, nbw, gates, oc,
                       nb, n_pages, ts)
    yb = o.reshape(nb, NSA_KV_HEADS * NSA_GROUP, ts, HEAD_DIM).transpose(2, 0, 1, 3).reshape(R, NSA_WIDTH)
    ext = jnp.concatenate([state_pool_l.transpose(1, 0, 2), xp.reshape(ts, nb, POOL_WIDTH)], axis=0)
    x1 = _mix_sample(x, ext, yb.astype(BF16), u.reshape(ts, nb, GMLP_WIDTH), v.reshape(ts, nb, GMLP_WIDTH),
                     p['wp'], p['ps'], p['wsx'], p['wb'][:SUBLANES], p['w_out'], nb, ts, past)
    x2 = _ffn(x1, p['g_ffn'], p['w_up'], p['w_down'], gf, R, final)
    return x2, (xp, kvc, kvs, kvw, v)


def kernel(x_prompt, x_sample, cache_cmp_kv, cache_slc_kv, state_win_kv, state_pool, page_table, w_in, w_out,
           norm_mix, norm_ffn, norm_final, w_pool, pool_scale, cmp_w, cmp_pe, gmlp_ws, gmlp_b, gmlp_norm, w_up,
           w_down, rel_bias):
    B, T, _ = x_prompt.shape
    nb, ts, _ = x_sample.shape
    n_pages = page_table.shape[1]
    n_phys = cache_cmp_kv.shape[1]
    n_buf = state_win_kv.shape[2]
    past = n_pages * PAGE_SIZE
    depth = w_in.shape[0]
    assert T % TK_FAR == 0 and T >= WINDOW and n_pages % PAGES_PER_STEP == 0
    assert ts <= SUBLANES and ts <= POOL_STATE and n_buf == WINDOW and past >= WINDOW
    kv_tail = (2, NSA_KV_HEADS, HEAD_DIM)

    cache_cmp = cache_cmp_kv.reshape(depth * n_phys, PAGE_SIZE, KV_ROW)
    cache_slc = cache_slc_kv.reshape(depth * n_phys, PAGE_SIZE, KV_ROW)
    state_win = state_win_kv.reshape(depth * nb, n_buf, KV_ROW)
    ptabs = _prompt_bias_tables(rel_bias)
    stabs = _sample_bias_tables(rel_bias, past, ts, n_buf)
    gf = norm_final[None, :]
    xp = x_prompt.reshape(B * T, D_MODEL)
    xs = x_sample.transpose(1, 0, 2).reshape(ts * nb, D_MODEL)
    unmajor = lambda a: a.reshape(ts, nb, a.shape[-1]).transpose(1, 0, 2)

    outs = [[] for _ in range(9)]
    for l in range(depth):
        p = _prep_layer(l, w_in, w_out, norm_mix, norm_ffn, w_pool, pool_scale, cmp_w, cmp_pe, gmlp_ws, gmlp_b,
                        gmlp_norm, w_up, w_down)
        final = l == depth - 1
        xp, (pin, kvc, kvs, kvw) = _prompt_layer(xp, p, ptabs, B, T, final, gf)
        outs[0].append(kvc.reshape((B, T) + kv_tail))
        outs[1].append(kvs.reshape((B, T) + kv_tail))
        outs[2].append(kvw.reshape((B, T) + kv_tail)[:, T - WINDOW:])
        outs[3].append(pin.reshape(B, T, POOL_WIDTH)[:, T - POOL_STATE:])

        pt = (page_table + l * n_phys).reshape(-1).astype(jnp.int32)
        xs, (sin, kvc_s, kvs_s, kvw_s, v_s) = _sample_layer(
            xs, p, stabs, l, pt, cache_cmp, cache_slc, state_win, state_pool[l], nb, ts, n_pages, final, gf)
        kvw_new = unmajor(kvw_s).reshape((nb, ts) + kv_tail)
        outs[4].append(unmajor(kvc_s).reshape((nb, ts) + kv_tail))
        outs[5].append(unmajor(kvs_s).reshape((nb, ts) + kv_tail))
        outs[6].append(jnp.concatenate([state_win_kv[l][:, ts:], kvw_new], axis=1))
        outs[7].append(jnp.concatenate([state_pool[l][:, ts:], unmajor(sin)], axis=1))
        outs[8].append(unmajor(v_s))

    y_prompt = xp.reshape(B, T, D_MODEL)
    y_sample = unmajor(xs)
    return (y_prompt, y_sample) + tuple(jnp.stack(o) for o in outs)
```

```python
import functools
import math

import numpy as np
import jax
import jax.numpy as jnp
from jax import lax
from jax.experimental import pallas as pl
from jax.experimental.pallas import tpu as pltpu

F32 = jnp.float32
BF16 = jnp.bfloat16

D_MODEL = 1024
DEPTH = 4
PAGE_SIZE = 128
HEAD_DIM = 64
POOL_WINDOWS = (2, 4, 8, 16)
POOL_WIDTH = 256
POOL_GROUP_DIM = 64
POOL_STATE = 15
NSA_WIDTH = 512
NSA_HEADS = 8
NSA_KV_HEADS = 2
NSA_GROUP = 4
KV_ROW = 2 * NSA_KV_HEADS * HEAD_DIM
CMP_STRIDE = 16
CMP_BLOCK = 32
SEL_BLOCK = 64
N_SELECT = 16
WINDOW = 512
N_BRANCHES = 3
FORCE_SCORE = 1000.0
GMLP_WIDTH = 256
GMLP_GROUPS = 4
GMLP_CHUNK = 128
D_FF = 4096
N_BUCKETS = 32
MAX_DISTANCE = 128
EPS = 1e-6
SCALE = HEAD_DIM ** -0.5

LANES = 128
SUBLANES = 8
VMEM_LIMIT = 56 * 1024 * 1024

NEG = -1e30
CMP_PAD = 16
CMP_NEAR = 24
TQ = 128
TK_FAR = 512
KPAD = 512

PROJ_SEGS = (('xp', 256), ('q', 512), ('kvc', 256), ('kvs', 256), ('kvw', 256), ('gt', 24), ('u', 256), ('v', 256))


def _cparams(sem):
    return pltpu.CompilerParams(dimension_semantics=sem, vmem_limit_bytes=VMEM_LIMIT)


def _dot(a, b):
    return jnp.dot(a, b, preferred_element_type=F32)


def _dot_nt(a, b):
    return lax.dot_general(a, b, (((1,), (1,)), ((), ())), preferred_element_type=F32)


def _gelu(x):
    c = math.sqrt(2.0 / math.pi)
    return 0.5 * x * (1.0 + jnp.tanh(c * (x + 0.044715 * (x * x * x))))


def _rms(x, g):
    return x * lax.rsqrt(jnp.mean(x * x, axis=-1, keepdims=True) + EPS) * g


def _inproj_kernel(x_ref, g_ref, w_ref, gn_ref, xp_ref, q_ref, kvc_ref, kvs_ref, kvw_ref, gt_ref, u_ref, v_ref):
    x = x_ref[...]
    h = _rms(x, g_ref[...]).astype(BF16)

    def seg(lo, hi):
        return _dot(h, w_ref[:, lo:hi])

    xp_ref[...] = seg(0, 256)
    q_ref[...] = (seg(256, 768) * SCALE).astype(BF16)
    kvc_ref[...] = seg(768, 1024)
    kvs_ref[...] = seg(1024, 1280)
    kvw_ref[...] = seg(1280, 1536)
    u_ref[...] = _gelu(seg(1536, 1792))
    gv = _gelu(seg(1792, 2048))
    sq = gv * gv
    lane = lax.broadcasted_iota(jnp.int32, sq.shape, 1)
    ms = jnp.zeros_like(sq)
    for g in range(GMLP_GROUPS):
        in_g = (lane >= g * 64) & (lane < (g + 1) * 64)
        s = jnp.sum(jnp.where(in_g, sq, 0.0), axis=-1, keepdims=True) * (1.0 / 64.0)
        ms = jnp.where(in_g, s, ms)
    v_ref[...] = gv * lax.rsqrt(ms + EPS) * gn_ref[...]
    gt_ref[...] = jax.nn.sigmoid(seg(2048, 2176))


def _inproj(x, g, w, gn, tm):
    R = x.shape[0]
    f = lambda n, dt=F32: jax.ShapeDtypeStruct((R, n), dt)
    row = lambda n: pl.BlockSpec((tm, n), lambda i: (i, 0))
    full = lambda a: pl.BlockSpec(a.shape, lambda i: (0,) * a.ndim)
    return pl.pallas_call(
        _inproj_kernel,
        grid=(R // tm,),
        in_specs=[row(D_MODEL), full(g), full(w), full(gn)],
        out_specs=[row(256), row(512), row(256), row(256), row(256), row(128), row(256), row(256)],
        out_shape=[f(256), f(512, BF16), f(256), f(256), f(256), f(128), f(256), f(256)],
        compiler_params=_cparams(("parallel",)),
        name="inproj",
    )(x, g, w, gn)


def _out_proj(x, ya, yb, yc, wo_ref):
    acc = _dot(ya.astype(BF16), wo_ref[0:256, :])
    acc = acc + _dot(yb, wo_ref[256:768, :])
    acc = acc + _dot(yc.astype(BF16), wo_ref[768:1024, :])
    return x + acc


def _pool_tail(win2, win4, win8, win16, cur, cnt, wp_ref, ps_ref):
    lane = lax.broadcasted_iota(jnp.int32, cur.shape, 1)
    win = jnp.where(lane < 64, win2, jnp.where(lane < 128, win4, jnp.where(lane < 192, win8, win16)))
    pooled = win / cnt - cur
    return _dot(pooled.astype(BF16), wp_ref[...]) * ps_ref[...]


def _mix_prompt_kernel(x_ref, xp_ref, halo_ref, yb_ref, u_ref, v_ref, wp_ref, ps_ref, ws_ref, wb_ref, wo_ref,
                       o_ref, ext_ref, *, tm, seq):
    i = pl.program_id(0)
    t0 = (i * tm) % seq
    halo = jnp.where(t0 > 0, halo_ref[...], 0.0)
    cur = xp_ref[...]
    ext_ref[0:16, :] = halo
    ext_ref[16:16 + tm, :] = cur
    e = ext_ref[...]
    b2 = e[1:] + e[:-1]
    b4 = b2[2:] + b2[:-2]
    b8 = b4[4:] + b4[:-4]
    b16 = b8[8:] + b8[:-8]
    win2 = b2[15:15 + tm]
    win4 = b4[13:13 + tm]
    win8 = b8[9:9 + tm]
    win16 = b16[1:1 + tm]
    rowi = lax.broadcasted_iota(jnp.int32, cur.shape, 0)
    lane = lax.broadcasted_iota(jnp.int32, cur.shape, 1)
    wsz = jnp.where(lane < 64, 2, jnp.where(lane < 128, 4, jnp.where(lane < 192, 8, 16)))
    cnt = jnp.minimum(t0 + rowi + 1, wsz).astype(F32)
    ya = _pool_tail(win2, win4, win8, win16, cur, cnt, wp_ref, ps_ref)

    ci = lax.broadcasted_iota(jnp.int32, (GMLP_CHUNK, GMLP_CHUNK), 0)
    cj = lax.broadcasted_iota(jnp.int32, (GMLP_CHUNK, GMLP_CHUNK), 1)
    lane_c = lax.broadcasted_iota(jnp.int32, (GMLP_CHUNK, GMLP_WIDTH), 1)
    wts = [jnp.where(ci >= cj, ws_ref[g], 0.0).astype(BF16) for g in range(GMLP_GROUPS)]
    parts = []
    for c in range(tm // GMLP_CHUNK):
        vc = v_ref[c * GMLP_CHUNK:(c + 1) * GMLP_CHUNK, :].astype(BF16)
        s = jnp.zeros((GMLP_CHUNK, GMLP_WIDTH), F32)
        for g in range(GMLP_GROUPS):
            sg = _dot(wts[g], vc)
            s = jnp.where((lane_c >= g * 64) & (lane_c < (g + 1) * 64), sg, s)
        parts.append(u_ref[c * GMLP_CHUNK:(c + 1) * GMLP_CHUNK, :] * (s + wb_ref[...]))
    yc = jnp.concatenate(parts, axis=0) if len(parts) > 1 else parts[0]

    o_ref[...] = _out_proj(x_ref[...], ya, yb_ref[...], yc, wo_ref)


def _mix_prompt(x, xp, yb, u, v, wp, ps, ws, wb, wo, tm, seq):
    R = x.shape[0]
    row = lambda n: pl.BlockSpec((tm, n), lambda i: (i, 0))
    full = lambda a: pl.BlockSpec(a.shape, lambda i: (0,) * a.ndim)
    halo = pl.BlockSpec((16, 256), lambda i: (jnp.maximum(i * (tm // 16) - 1, 0), 0))
    return pl.pallas_call(
        functools.partial(_mix_prompt_kernel, tm=tm, seq=seq),
        grid=(R // tm,),
        in_specs=[row(D_MODEL), row(256), halo, row(512), row(256), row(256),
                  full(wp), full(ps), full(ws), full(wb), full(wo)],
        out_specs=row(D_MODEL),
        out_shape=jax.ShapeDtypeStruct((R, D_MODEL), F32),
        scratch_shapes=[pltpu.VMEM((tm + 16, 256), F32)],
        compiler_params=_cparams(("parallel",)),
        name="mix_prompt",
    )(x, xp, xp, yb, u, v, wp, ps, ws, wb, wo)


def _mix_sample_kernel(x_ref, ext_ref, yb_ref, u_ref, v_ref, wp_ref, ps_ref, wsx_ref, wb_ref, wo_ref, o_ref,
                       *, nb, ts, past):
    lane = lax.broadcasted_iota(jnp.int32, (nb, POOL_WIDTH), 1)
    wsz = jnp.where(lane < 64, 2, jnp.where(lane < 128, 4, jnp.where(lane < 192, 8, 16)))
    yas, ycs = [], []
    for t in range(ts):
        top = POOL_STATE + t
        acc = ext_ref[top] + ext_ref[top - 1]
        wins = [acc]
        for w in (4, 8, 16):
            for j in range(w // 2, w):
                acc = acc + ext_ref[top - j]
            wins.append(acc)
        cnt = jnp.minimum(past + t + 1, wsz).astype(F32)
        yas.append(_pool_tail(wins[0], wins[1], wins[2], wins[3], ext_ref[top], cnt, wp_ref, ps_ref))
        s = wb_ref[t:t + 1, :]
        s = jnp.broadcast_to(s, (nb, GMLP_WIDTH))
        for j in range(t + 1):
            s = s + wsx_ref[t, j:j + 1, :] * v_ref[j]
        ycs.append(u_ref[t] * s)
    ya = jnp.concatenate(yas, axis=0)
    yc = jnp.concatenate(ycs, axis=0)
    o_ref[...] = _out_proj(x_ref[...], ya, yb_ref[...], yc, wo_ref)


def _mix_sample(x, ext, yb, u, v, wp, ps, wsx, wb4, wo, nb, ts, past):
    R = x.shape[0]
    args = (x, ext, yb, u, v, wp, ps, wsx, wb4, wo)
    full = lambda a: pl.BlockSpec(a.shape, lambda i: (0,) * a.ndim)
    return pl.pallas_call(
        functools.partial(_mix_sample_kernel, nb=nb, ts=ts, past=past),
        grid=(1,),
        in_specs=[full(a) for a in args],
        out_specs=pl.BlockSpec((R, D_MODEL), lambda i: (0, 0)),
        out_shape=jax.ShapeDtypeStruct((R, D_MODEL), F32),
        compiler_params=_cparams(("arbitrary",)),
        name="mix_sample",
    )(*args)


def _ffn_kernel(x_ref, g_ref, wu_ref, wd_ref, gf_ref, o_ref, *, final, fc):
    x = x_ref[...]
    h = _rms(x, g_ref[...]).astype(BF16)
    acc = x
    for c in range(D_FF // fc):
        a = jnp.maximum(_dot(h, wu_ref[:, c * fc:(c + 1) * fc]), 0.0)
        acc = acc + _dot((a * a).astype(BF16), wd_ref[c * fc:(c + 1) * fc, :])
    if final:
        acc = _rms(acc, gf_ref[...])
    o_ref[...] = acc


def _ffn(x, g, wu, wd, gf, tm, final):
    R = x.shape[0]
    row = pl.BlockSpec((tm, D_MODEL), lambda i: (i, 0))
    full = lambda a: pl.BlockSpec(a.shape, lambda i: (0,) * a.ndim)
    wfull = lambda a: pl.BlockSpec(a.shape, lambda i: (0,) * a.ndim, pipeline_mode=pl.Buffered(1))
    return pl.pallas_call(
        functools.partial(_ffn_kernel, final=final, fc=1024),
        grid=(R // tm,),
        in_specs=[row, full(g), wfull(wu), wfull(wd), full(gf)],
        out_specs=row,
        out_shape=jax.ShapeDtypeStruct((R, D_MODEL), F32),
        compiler_params=_cparams(("parallel",)),
        name="ffn",
    )(x, g, wu, wd, gf)


def _compress_rows(read_rows, n, wx):
    lo = jnp.zeros((n, wx.shape[1]), F32)
    hi = jnp.zeros((n, wx.shape[1]), F32)
    for j in range(CMP_STRIDE):
        xj = read_rows(j)
        lo = lo + xj * wx[j:j + 1, :]
        hi = hi + xj * wx[CMP_STRIDE + j:CMP_STRIDE + j + 1, :]
    return lo, hi


def _combine_halves(lo, hi, pe, wx):
    n = lo.shape[0]
    pe_term = jnp.sum(pe * wx, axis=0, keepdims=True)
    comp = lo + pltpu.roll(hi, n - 1, 0) + pe_term
    rowi = lax.broadcasted_iota(jnp.int32, comp.shape, 0)
    return jnp.where(rowi < n - 1, comp, 0.0)


def _compress_prompt_kernel(x_ref, wx_ref, pe_ref, o_ref, *, n):
    wx = wx_ref[...]
    lo, hi = _compress_rows(lambda j: x_ref[0, pl.ds(j, n, stride=CMP_STRIDE), :], n, wx)
    o_ref[0, 0:CMP_PAD, :] = jnp.zeros((CMP_PAD, LANES), F32)
    o_ref[0, CMP_PAD:CMP_PAD + n, :] = _combine_halves(lo, hi, pe_ref[...], wx)


def _compress_prompt(kvc, wx, pe):
    B, T, _ = kvc.shape
    n = T // CMP_STRIDE
    half = lambda rows: pl.BlockSpec((rows, LANES), lambda b, e: (0, e))
    return pl.pallas_call(
        functools.partial(_compress_prompt_kernel, n=n),
        grid=(B, KV_ROW // LANES),
        in_specs=[pl.BlockSpec((1, T, LANES), lambda b, e: (b, 0, e)), half(CMP_BLOCK), half(CMP_BLOCK)],
        out_specs=pl.BlockSpec((1, n + CMP_PAD, LANES), lambda b, e: (b, 0, e)),
        out_shape=jax.ShapeDtypeStruct((B, n + CMP_PAD, KV_ROW), F32),
        compiler_params=_cparams(("parallel", "parallel")),
        name="compress_prompt",
    )(kvc, wx, pe)


def _select_topk(score, n_pick):
    s_iota = lax.broadcasted_iota(jnp.int32, score.shape, 0)
    big = score.shape[0]
    sel = jnp.zeros(score.shape, F32)
    for _ in range(n_pick):
        m = jnp.max(score, axis=0, keepdims=True)
        idx = jnp.min(jnp.where(score == m, s_iota, big), axis=0, keepdims=True)
        pick = s_iota == idx
        sel = jnp.where(pick & (m >= 0.0), 1.0, sel)
        score = jnp.where(pick, -2.0, score)
    return sel


def _softmax_rows(z):
    m = jnp.max(z, axis=0, keepdims=True)
    p = jnp.where(z > 0.5 * NEG, jnp.exp(z - m), 0.0)
    den = jnp.maximum(jnp.sum(p, axis=0, keepdims=True), 1e-30)
    return p * (1.0 / den)


def _cmp_prompt_kernel(q_ref, kc_ref, vct_ref, nb_ref, cf_ref, oc_ref, sn_ref, raw_ref, z_ref, ps_ref, *, ncp):
    i = pl.program_id(2)
    hh = pl.program_id(1)
    r0 = pl.multiple_of(i * (TQ // CMP_STRIDE), SUBLANES)
    q = q_ref[0]
    kc = kc_ref[0, 0]
    vct = vct_ref[0, 0]
    rowi = lax.broadcasted_iota(jnp.int32, (ncp, TQ), 0)
    far = (rowi >= CMP_PAD) & (rowi < r0)
    rown = lax.broadcasted_iota(jnp.int32, (CMP_NEAR, TQ), 0) + r0
    psum = jnp.zeros((ncp, TQ), F32)
    octs = []
    for g in range(NSA_GROUP):
        qg = q[:, g * HEAD_DIM:(g + 1) * HEAD_DIM]
        raw_ref[...] = _dot_nt(kc, qg)
        cf = cf_ref[hh * NSA_GROUP + g]
        z_ref[...] = jnp.where(far, raw_ref[...] + cf[0:1, :], NEG)
        nb = nb_ref[hh * NSA_GROUP + g]
        zn = raw_ref[pl.ds(r0, CMP_NEAR), :] + nb
        z_ref[pl.ds(r0, CMP_NEAR), :] = jnp.where((rown >= CMP_PAD) & (nb > 0.5 * NEG), zn, NEG)
        pc = _softmax_rows(z_ref[...])
        psum = psum + pc
        octs.append(_dot(vct, pc.astype(BF16)))
    oc_ref[0] = jnp.concatenate(octs, axis=0).T

    ps_ref[0:ncp, :] = psum
    if ncp < ps_ref.shape[0]:
        ps_ref[ncp:, :] = jnp.zeros((ps_ref.shape[0] - ncp, TQ), F32)
    ns = LANES
    p_slc = ps_ref[pl.ds(CMP_PAD, ns, stride=4), :]
    for j in range(1, 4):
        p_slc = p_slc + ps_ref[pl.ds(CMP_PAD + j, ns, stride=4), :]
    s_iota = lax.broadcasted_iota(jnp.int32, (ns, TQ), 0)
    tt = lax.broadcasted_iota(jnp.int32, (ns, TQ), 1)
    cur = (i * TQ + tt) // SEL_BLOCK
    forced = (s_iota == 0) | (s_iota == cur) | (s_iota == cur - 1)
    score = jnp.where(s_iota <= cur, jnp.where(forced, FORCE_SCORE, p_slc), -1.0)
    sel = _select_topk(score, N_SELECT)
    sn_ref[0, 0] = jnp.where(sel.T > 0.5, 0.0, NEG).astype(BF16)


def _cmp_prompt(q, kc, vct, nbc, cfar):
    B, T, _ = q.shape
    ncp = kc.shape[2]
    ns = LANES
    nq = T // TQ
    full = lambda a: pl.BlockSpec(a.shape, lambda b, h, i: (0,) * a.ndim)
    return pl.pallas_call(
        functools.partial(_cmp_prompt_kernel, ncp=ncp),
        grid=(B, NSA_KV_HEADS, nq),
        in_specs=[pl.BlockSpec((1, TQ, 256), lambda b, h, i: (b, i, h)),
                  pl.BlockSpec((1, 1, ncp, HEAD_DIM), lambda b, h, i: (b, h, 0, 0)),
                  pl.BlockSpec((1, 1, HEAD_DIM, ncp), lambda b, h, i: (b, h, 0, 0)),
                  full(nbc), full(cfar)],
        out_specs=[pl.BlockSpec((1, TQ, 256), lambda b, h, i: (b, i, h)),
                   pl.BlockSpec((1, 1, TQ, ns), lambda b, h, i: (b, h, i, 0))],
        out_shape=[jax.ShapeDtypeStruct((B, T, NSA_WIDTH), F32),
                   jax.ShapeDtypeStruct((B, NSA_KV_HEADS, T, ns), BF16)],
        scratch_shapes=[pltpu.VMEM((ncp, TQ), F32), pltpu.VMEM((ncp, TQ), F32),
                        pltpu.VMEM((max(ncp, CMP_PAD + 4 * LANES), TQ), F32)],
        compiler_params=_cparams(("parallel", "parallel", "parallel")),
        name="cmp_prompt",
    )(q, kc, vct, nbc, cfar)


def _selwin_prompt_kernel(q_ref, sn_ref, ka_ref, va_ref, kw_ref, vw_ref, g_ref, oc_ref, nbn_ref, nbw_ref, o_ref,
                          ms_ref, as_ref, za_ref, zb_ref):
    i = pl.program_id(2)
    R4 = NSA_GROUP * TQ
    q = q_ref[0]
    q4 = jnp.concatenate([q[:, g * HEAD_DIM:(g + 1) * HEAD_DIM] for g in range(NSA_GROUP)], axis=0)
    sn = sn_ref[0, 0]
    blk = lax.broadcasted_iota(jnp.int32, (TQ, LANES), 1)
    first_near = (i - 1) * (TQ // SEL_BLOCK)
    sn_far = jnp.where(blk >= first_near, NEG, sn.astype(F32)).astype(BF16)
    qa_near = jnp.concatenate([jnp.concatenate([sn] * NSA_GROUP, axis=0), q4], axis=1)
    qa_far = jnp.concatenate([jnp.concatenate([sn_far] * NSA_GROUP, axis=0), q4], axis=1)
    qw = jnp.concatenate([q4, jnp.ones((R4, HEAD_DIM), BF16)], axis=1)

    def one_pass(z, v):
        m = jnp.max(z, axis=1, keepdims=True)
        acc = _dot(jnp.exp(z - m).astype(BF16), v)
        return m, acc

    kw0 = pl.multiple_of(i * TQ, TQ)
    zw = _dot_nt(qw, kw_ref[0, 0, pl.ds(kw0, WINDOW + TQ), :]) + nbw_ref[0]
    _, a_w = one_pass(zw, vw_ref[0, 0, pl.ds(kw0, WINDOW + TQ), :])

    kn0 = pl.multiple_of((i - 1) * TQ + KPAD, TQ)
    zn = _dot_nt(qa_near, ka_ref[0, 0, pl.ds(kn0, 2 * TQ), :]) + nbn_ref[0]
    m0, a0 = one_pass(zn, va_ref[0, 0, pl.ds(kn0, 2 * TQ), :])
    ms_ref[...] = jnp.broadcast_to(m0, (R4, LANES))
    as_ref[...] = a0

    def far_logits(j):
        k0 = pl.multiple_of(j * TK_FAR + KPAD, TK_FAR)
        return _dot_nt(qa_far, ka_ref[0, 0, pl.ds(k0, TK_FAR), :])

    def far_update(j, z):
        k0 = pl.multiple_of(j * TK_FAR + KPAD, TK_FAR)
        m_prev = ms_ref[...]
        m_new = jnp.maximum(m_prev, jnp.max(z, axis=1, keepdims=True))
        p = jnp.exp(z - jnp.concatenate([m_new] * (TK_FAR // LANES), axis=1))
        as_ref[...] = jnp.exp(m_prev - m_new) * as_ref[...] + _dot(p.astype(BF16), va_ref[0, 0, pl.ds(k0, TK_FAR), :])
        ms_ref[...] = m_new

    n_far = (jnp.maximum(i - 1, 0) * TQ + TK_FAR - 1) // TK_FAR
    odd = n_far % 2
    last = n_far - 1

    @pl.when(n_far > 0)
    def _():
        za_ref[...] = far_logits(-odd)

        def far_body(jj, carry):
            t = 2 * jj - odd
            zb_ref[...] = far_logits(t + 1)
            far_update(t, za_ref[...])
            za_ref[...] = far_logits(jnp.minimum(t + 2, last))
            far_update(t + 1, zb_ref[...])
            return carry

        lax.fori_loop(0, (n_far + 1) // 2, far_body, 0)

    a_s = as_ref[...]
    o_s = a_s[:, 0:HEAD_DIM] / jnp.maximum(a_s[:, HEAD_DIM:2 * HEAD_DIM], 1e-30)
    o_w = a_w[:, 0:HEAD_DIM] / jnp.maximum(a_w[:, HEAD_DIM:2 * HEAD_DIM], 1e-30)
    gt = g_ref[0, 0]
    oc = oc_ref[0]
    outs = []
    for g in range(NSA_GROUP):
        sl = slice(g * TQ, (g + 1) * TQ)
        o = (gt[:, 3 * g:3 * g + 1] * oc[:, g * HEAD_DIM:(g + 1) * HEAD_DIM]
             + gt[:, 3 * g + 1:3 * g + 2] * o_s[sl] + gt[:, 3 * g + 2:3 * g + 3] * o_w[sl])
        outs.append(o)
    o_ref[0] = jnp.concatenate(outs, axis=1).astype(BF16)


def _selwin_prompt(q, sn, ka, va, kw, vw, gates, oc, nbn, nbw):
    B, T, _ = q.shape
    nq = T // TQ
    R4 = NSA_GROUP * TQ
    perbh = lambda a: pl.BlockSpec((1, 1) + a.shape[2:], lambda b, h, i: (b, h, 0, 0))
    perh = lambda a: pl.BlockSpec((1,) + a.shape[1:], lambda b, h, i: (h, 0, 0))
    tile = lambda n: pl.BlockSpec((1, 1, TQ, n), lambda b, h, i: (b, h, i, 0))
    return pl.pallas_call(
        _selwin_prompt_kernel,
        grid=(B, NSA_KV_HEADS, nq),
        in_specs=[pl.BlockSpec((1, TQ, 256), lambda b, h, i: (b, i, h)),
                  tile(sn.shape[-1]), perbh(ka), perbh(va), perbh(kw), perbh(vw), tile(LANES),
                  pl.BlockSpec((1, TQ, 256), lambda b, h, i: (b, i, h)), perh(nbn), perh(nbw)],
        out_specs=pl.BlockSpec((1, TQ, 256), lambda b, h, i: (b, i, h)),
        out_shape=jax.ShapeDtypeStruct((B, T, NSA_WIDTH), BF16),
        scratch_shapes=[pltpu.VMEM((R4, LANES), F32)] * 2 + [pltpu.VMEM((R4, TK_FAR), F32)] * 2,
        compiler_params=_cparams(("parallel", "parallel", "arbitrary")),
        name="selwin_prompt",
    )(q, sn, ka, va, kw, vw, gates, oc, nbn, nbw)


def _rel_bucket(dist):
    d = jnp.maximum(dist, 0)
    n_exact = N_BUCKETS // 2
    d_f = jnp.maximum(d, 1).astype(F32)
    large = n_exact + (jnp.log(d_f / n_exact) / math.log(MAX_DISTANCE / n_exact)
                       * (N_BUCKETS - n_exact)).astype(jnp.int32)
    large = jnp.minimum(large, N_BUCKETS - 1)
    return jnp.where(d < n_exact, d, large)


def _head_bias(rel_bias, dist):
    return jnp.take(rel_bias, _rel_bucket(dist), axis=0).astype(F32)


def _prompt_bias_tables(rel_bias):
    far = rel_bias[N_BUCKETS - 1].astype(F32)
    cfar = jnp.broadcast_to(far[:, None, None], (NSA_HEADS, SUBLANES, LANES))
    k = jnp.arange(CMP_NEAR)[:, None]
    tt = jnp.arange(TQ)[None, :]
    d = tt + CMP_PAD * CMP_STRIDE - (CMP_BLOCK - 1) - CMP_STRIDE * k
    nbc = jnp.where((d >= 0)[..., None], _head_bias(rel_bias, d), NEG).transpose(2, 0, 1)
    tq = jnp.arange(TQ)[:, None]

    def tile_table(first_dist, n_keys, max_dist):
        d = first_dist + tq - jnp.arange(n_keys)[None, :]
        b = jnp.where(((d >= 0) & (d < max_dist))[..., None], _head_bias(rel_bias, d) - far, NEG)
        return b.transpose(2, 0, 1).reshape(NSA_KV_HEADS, NSA_GROUP * TQ, n_keys)

    nbn = tile_table(TQ, 2 * TQ, 2 * TQ + 1)
    nbw = tile_table(WINDOW, WINDOW + TQ, WINDOW)
    return cfar, nbc, nbn, nbw


def _prep_layer(l, w_in, w_out, norm_mix, norm_ffn, w_pool, pool_scale, cmp_w, cmp_pe, gmlp_ws, gmlp_b,
                gmlp_norm, w_up, w_down):
    w = w_in[l]
    offs = np.cumsum([0] + [n for _, n in PROJ_SEGS])
    seg = {name: w[:, offs[k]:offs[k + 1]] for k, (name, _) in enumerate(PROJ_SEGS)}
    order = ('xp', 'q', 'kvc', 'kvs', 'kvw', 'u', 'v', 'gt')
    w_perm = jnp.concatenate([seg[n] for n in order] + [jnp.zeros((D_MODEL, LANES - 24), w.dtype)], axis=1)
    eye = jnp.eye(len(POOL_WINDOWS), dtype=F32)
    wp = jnp.einsum('gcd,gh->gchd', w_pool[l], eye).reshape(POOL_WIDTH, POOL_WIDTH)
    wx = jnp.broadcast_to(cmp_w[l].transpose(1, 0, 2)[..., None], (CMP_BLOCK, 2, NSA_KV_HEADS, HEAD_DIM))
    return dict(
        w_in=w_perm.astype(BF16), g_mix=norm_mix[l][None, :], g_ffn=norm_ffn[l][None, :],
        gn=gmlp_norm[l][None, :], wp=wp.astype(BF16), ps=pool_scale[l][None, :],
        wx=wx.reshape(CMP_BLOCK, KV_ROW), pe=cmp_pe[l].transpose(1, 0, 2, 3).reshape(CMP_BLOCK, KV_ROW),
        ws=gmlp_ws[l], wb=jnp.repeat(gmlp_b[l].T, POOL_GROUP_DIM, axis=1),
        wsx=jnp.repeat(gmlp_ws[l].transpose(1, 2, 0), POOL_GROUP_DIM, axis=2)[:8, :8],
        w_out=w_out[l].astype(BF16), w_up=w_up[l].astype(BF16), w_down=w_down[l].astype(BF16))


def _split_heads(kv, part):
    B, T, _ = kv.shape
    return kv.reshape(B, T, 2, NSA_KV_HEADS, HEAD_DIM)[:, :, part].transpose(0, 2, 1, 3).astype(BF16)


def _prompt_layer(x, p, tabs, B, T, final, gf):
    cfar, nbc, nbn, nbw = tabs
    tm = min(512, B * T)
    xp, q, kvc, kvs, kvw, gt, u, v = _inproj(x, p['g_mix'], p['w_in'], p['gn'], tm)
    comp = _compress_prompt(kvc.reshape(B, T, KV_ROW), p['wx'], p['pe'])
    kc = _split_heads(comp, 0)
    vct = _split_heads(comp, 1).transpose(0, 1, 3, 2)
    q3 = q.reshape(B, T, NSA_WIDTH)
    oc, sn = _cmp_prompt(q3, kc, vct, nbc, cfar)
    pos = jnp.arange(-KPAD, T)
    blk_of = jnp.where(pos >= 0, pos // SEL_BLOCK, LANES - 1)
    onehot = (blk_of[:, None] == jnp.arange(LANES)[None, :]).astype(BF16)
    flag = jnp.where((pos < 0)[:, None] & (jnp.arange(HEAD_DIM) == 0)[None, :], NEG, 0.0).astype(BF16)
    kvs3, kvw3 = kvs.reshape(B, T, KV_ROW), kvw.reshape(B, T, KV_ROW)
    front = lambda a: jnp.pad(a, ((0, 0), (0, 0), (KPAD, 0), (0, 0)))
    bcast = lambda a: jnp.broadcast_to(a, (B, NSA_KV_HEADS) + a.shape)
    ones = jnp.ones((B, NSA_KV_HEADS, T + KPAD, HEAD_DIM), BF16)
    ka = jnp.concatenate([bcast(onehot), front(_split_heads(kvs3, 0))], axis=-1)
    va = jnp.concatenate([front(_split_heads(kvs3, 1)), ones], axis=-1)
    kw = jnp.concatenate([front(_split_heads(kvw3, 0)), bcast(flag)], axis=-1)
    vw = jnp.concatenate([front(_split_heads(kvw3, 1)), ones], axis=-1)
    g4 = gt[:, :NSA_HEADS * N_BRANCHES].reshape(B, T, NSA_KV_HEADS, NSA_GROUP * N_BRANCHES).transpose(0, 2, 1, 3)
    g4 = jnp.pad(g4, ((0, 0), (0, 0), (0, 0), (0, LANES - NSA_GROUP * N_BRANCHES)))
    yb = _selwin_prompt(q3, sn, ka, va, kw, vw, g4, oc, nbn, nbw)
    x1 = _mix_prompt(x, xp, yb.reshape(B * T, NSA_WIDTH), u, v, p['wp'], p['ps'], p['ws'], p['wb'], p['w_out'],
                     tm, T)
    x2 = _ffn(x1, p['g_ffn'], p['w_up'], p['w_down'], gf, tm, final)
    return x2, (xp, kvc, kvs, kvw)


PAGES_PER_STEP = 8
SEL_PER_PAGE = PAGE_SIZE // SEL_BLOCK


def _cmp_sample_kernel(pt_ref, *refs, ts):
    del pt_ref
    npg = PAGES_PER_STEP
    nhalf = KV_ROW // LANES
    pages = refs[:npg * nhalf]
    wx_ref, pe_ref, q_ref, tab_ref, oc_ref, sel_ref, lo_ref, hi_ref, ps_ref = refs[npg * nhalf:]
    s = pl.program_id(1)
    rows = PAGE_SIZE // CMP_STRIDE
    for k in range(npg):
        for e in range(nhalf):
            page = pages[k * nhalf + e]
            lo, hi = _compress_rows(lambda j, page=page: page[0, pl.ds(j, rows, stride=CMP_STRIDE), :], rows,
                                    wx_ref[:, e * LANES:(e + 1) * LANES])
            lo_ref[s, k * rows:(k + 1) * rows, e * LANES:(e + 1) * LANES] = lo
            hi_ref[s, k * rows:(k + 1) * rows, e * LANES:(e + 1) * LANES] = hi

    @pl.when(s == pl.num_programs(1) - 1)
    def _():
        nc = lo_ref.shape[0] * lo_ref.shape[1]
        comp = _combine_halves(lo_ref[...].reshape(nc, KV_ROW), hi_ref[...].reshape(nc, KV_ROW), pe_ref[...],
                               wx_ref[...])
        ns = nc // 4
        lane = lax.broadcasted_iota(jnp.int32, (ns, LANES), 1)
        s_iota = lax.broadcasted_iota(jnp.int32, (ns, LANES), 0)
        ncol = NSA_GROUP * ts
        selacc = jnp.zeros((ns, LANES), F32)
        for h in range(NSA_KV_HEADS):
            kc = comp[:, h * HEAD_DIM:(h + 1) * HEAD_DIM].astype(BF16)
            vc = comp[:, (NSA_KV_HEADS + h) * HEAD_DIM:(NSA_KV_HEADS + h + 1) * HEAD_DIM].astype(BF16)
            pc = _softmax_rows(_dot_nt(kc, q_ref[0, h]) + tab_ref[h])
            oc_ref[0, h] = pl.dot(pc.astype(BF16), vc, trans_a=True)[0:ncol, :]
            ps = pc
            for g in range(1, NSA_GROUP):
                ps = ps + pltpu.roll(pc, g * ts, 1)
            ps_ref[...] = ps
            p_slc = ps_ref[pl.ds(0, ns, stride=4), :]
            for j in range(1, 4):
                p_slc = p_slc + ps_ref[pl.ds(j, ns, stride=4), :]
            forced = (s_iota == 0) | (s_iota == ns - 1)
            sel = _select_topk(jnp.where(forced, FORCE_SCORE, p_slc), N_SELECT - 1)
            for g in range(NSA_GROUP):
                dst = h * ncol + g * ts
                shift = (dst - (ncol - ts)) % LANES
                moved = pltpu.roll(sel, shift, 1) if shift else sel
                selacc = jnp.where((lane >= dst) & (lane < dst + ts), moved, selacc)
        sel_ref[0] = selacc.reshape(sel_ref.shape[1:])


def _page_specs(n_pages, lanes=KV_ROW):
    def spec(k, e):
        return pl.BlockSpec((1, PAGE_SIZE, lanes),
                            lambda b, s, pt: (pt[b * n_pages + s * PAGES_PER_STEP + k], 0, e))
    return [spec(k, e) for k in range(PAGES_PER_STEP) for e in range(KV_ROW // lanes)]


def _cmp_sample(pt, cache, wx, pe, qs, tab, nb, n_pages, ts):
    steps = n_pages // PAGES_PER_STEP
    nc = n_pages * PAGE_SIZE // CMP_STRIDE
    rows = PAGES_PER_STEP * PAGE_SIZE // CMP_STRIDE
    ns = nc // 4
    full = lambda a: pl.BlockSpec(a.shape, lambda b, s, pt: (0,) * a.ndim)
    grid_spec = pltpu.PrefetchScalarGridSpec(
        num_scalar_prefetch=1,
        grid=(nb, steps),
        in_specs=_page_specs(n_pages, LANES) + [
            full(wx), full(pe),
            pl.BlockSpec((1, NSA_KV_HEADS, LANES, HEAD_DIM), lambda b, s, pt: (b, 0, 0, 0)),
            full(tab)],
        out_specs=[pl.BlockSpec((1, NSA_KV_HEADS, NSA_GROUP * ts, HEAD_DIM), lambda b, s, pt: (b, 0, 0, 0)),
                   pl.BlockSpec((1, steps, ns // steps, LANES), lambda b, s, pt: (b, 0, 0, 0))],
        scratch_shapes=[pltpu.VMEM((steps, rows, KV_ROW), F32), pltpu.VMEM((steps, rows, KV_ROW), F32),
                        pltpu.VMEM((nc, LANES), F32)])
    return pl.pallas_call(
        functools.partial(_cmp_sample_kernel, ts=ts),
        grid_spec=grid_spec,
        out_shape=[jax.ShapeDtypeStruct((nb, NSA_KV_HEADS, NSA_GROUP * ts, HEAD_DIM), F32),
                   jax.ShapeDtypeStruct((nb, steps, ns // steps, LANES), F32)],
        compiler_params=_cparams(("parallel", "arbitrary")),
        name="cmp_sample",
    )(pt, *([cache] * (PAGES_PER_STEP * (KV_ROW // LANES))), wx, pe, qs, tab)


def _local_softmax(z, mask):
    m = jnp.max(z, axis=0, keepdims=True)
    p = jnp.where(mask, jnp.exp(z - m), 0.0)
    return m, p, jnp.sum(p, axis=0, keepdims=True)


def _selwin_sample_kernel(pt_ref, *refs, ts):
    del pt_ref
    npg = PAGES_PER_STEP
    pages = refs[:npg]
    (qa_ref, sel_ref, cf_ref, nbl_ref, st_ref, kvn_ref, nbn_ref, nbw_ref, g_ref, oc_ref,
     o_ref, m_ref, l_ref, a_ref) = refs[npg:]
    s = pl.program_id(1)
    last = pl.num_programs(1) - 1
    ncol = NSA_KV_HEADS * NSA_GROUP * ts
    qa = qa_ref[0]
    rowi = lax.broadcasted_iota(jnp.int32, (PAGE_SIZE, LANES), 0)
    for k in range(npg):
        pg = pages[k][0].astype(BF16)
        is_last_page = jnp.logical_and(s == last, k == npg - 1)
        bias = jnp.where(is_last_page, nbl_ref[...], cf_ref[0:1, :])
        s0 = sel_ref[0, s, SEL_PER_PAGE * k:SEL_PER_PAGE * k + 1, :]
        s1 = sel_ref[0, s, SEL_PER_PAGE * k + 1:SEL_PER_PAGE * k + 2, :]
        mask = jnp.where(rowi < SEL_BLOCK, s0, s1) > 0.5
        z = jnp.where(mask, _dot(pg, qa) + bias, NEG)
        m, p, l = _local_softmax(z, mask)
        m_ref[s, k:k + 1, :] = m
        l_ref[s, k:k + 1, :] = l
        a_ref[s, k] = pl.dot(p.astype(BF16), pg, trans_a=True)[0:ncol, :]

    @pl.when(s == last)
    def _():
        nbn = nbn_ref[...]
        okn = nbn > 0.5 * NEG
        kn = kvn_ref[0, 0:SUBLANES, :].astype(BF16)
        m_n, p_n, l_n = _local_softmax(jnp.where(okn, _dot(kn, qa) + nbn, NEG), okn)
        a_n = pl.dot(p_n.astype(BF16), kn, trans_a=True)[0:ncol, :]
        nparts = m_ref.shape[0] * m_ref.shape[1]
        m_all = m_ref[...].reshape(nparts, LANES)
        l_all = l_ref[...].reshape(nparts, LANES)
        m_g = jnp.maximum(jnp.max(m_all, axis=0, keepdims=True), m_n)
        w_all = jnp.exp(m_all - m_g)
        w_n = jnp.exp(m_n - m_g)
        l_g = jnp.sum(w_all * l_all, axis=0, keepdims=True) + w_n * l_n

        nbw = nbw_ref[...]
        okw = nbw > 0.5 * NEG
        st = st_ref[0].astype(BF16)
        kwn = kvn_ref[0, SUBLANES:2 * SUBLANES, :].astype(BF16)
        zw = jnp.where(okw, _dot(st, qa) + nbw, NEG)
        zwn = jnp.where(okn, _dot(kwn, qa) + nbn, NEG)
        m_w = jnp.maximum(jnp.max(zw, axis=0, keepdims=True), jnp.max(zwn, axis=0, keepdims=True))
        pw = jnp.where(okw, jnp.exp(zw - m_w), 0.0)
        pwn = jnp.where(okn, jnp.exp(zwn - m_w), 0.0)
        l_w = jnp.sum(pw, axis=0, keepdims=True) + jnp.sum(pwn, axis=0, keepdims=True)
        a_w = (pl.dot(pw.astype(BF16), st, trans_a=True) + pl.dot(pwn.astype(BF16), kwn, trans_a=True))[0:ncol, :]

        assert nparts + 4 * SUBLANES <= LANES
        rows8 = lambda r: jnp.broadcast_to(r, (SUBLANES, LANES))
        x = jnp.concatenate([w_all, rows8(w_n), rows8(l_g), rows8(l_w),
                             jnp.zeros((LANES - nparts - 3 * SUBLANES, LANES), F32)], axis=0)
        xt = x.T
        acc = xt[0:ncol, nparts:nparts + 1] * a_n
        for pidx in range(nparts):
            acc = acc + xt[0:ncol, pidx:pidx + 1] * a_ref[pidx // npg, pidx % npg]
        o_s = acc / jnp.maximum(xt[0:ncol, nparts + SUBLANES:nparts + SUBLANES + 1], 1e-30)
        o_w = a_w / jnp.maximum(xt[0:ncol, nparts + 2 * SUBLANES:nparts + 2 * SUBLANES + 1], 1e-30)

        ri = lax.broadcasted_iota(jnp.int32, (ncol, HEAD_DIM), 0)
        v_of = lambda t: jnp.where(ri < ncol // NSA_KV_HEADS, t[:, 2 * HEAD_DIM:3 * HEAD_DIM],
                                   t[:, 3 * HEAD_DIM:4 * HEAD_DIM])
        gt = g_ref[0]
        oc = oc_ref[0].reshape(ncol, HEAD_DIM)
        o_ref[0] = gt[:, 0:1] * oc + gt[:, 1:2] * v_of(o_s) + gt[:, 2:3] * v_of(o_w)


def _selwin_sample(pt, cache, qa, sel, cfrow, nbl, state, st_off, kvn, nbn, nbw, gates, oc, nb, n_pages, ts):
    steps = n_pages // PAGES_PER_STEP
    ncol = NSA_KV_HEADS * NSA_GROUP * ts
    n_buf = state.shape[1]
    full = lambda a: pl.BlockSpec(a.shape, lambda b, s, pt: (0,) * a.ndim)
    perb = lambda a: pl.BlockSpec((1,) + a.shape[1:], lambda b, s, pt: (b,) + (0,) * (a.ndim - 1))
    grid_spec = pltpu.PrefetchScalarGridSpec(
        num_scalar_prefetch=1,
        grid=(nb, steps),
        in_specs=_page_specs(n_pages) + [
            perb(qa), perb(sel), full(cfrow), full(nbl),
            pl.BlockSpec((1, n_buf, KV_ROW), lambda b, s, pt: (st_off + b, 0, 0)),
            perb(kvn), full(nbn), full(nbw), perb(gates), perb(oc)],
        out_specs=pl.BlockSpec((1, ncol, HEAD_DIM), lambda b, s, pt: (b, 0, 0)),
        scratch_shapes=[pltpu.VMEM((steps, PAGES_PER_STEP, LANES), F32),
                        pltpu.VMEM((steps, PAGES_PER_STEP, LANES), F32),
                        pltpu.VMEM((steps, PAGES_PER_STEP, ncol, KV_ROW), F32)])
    return pl.pallas_call(
        functools.partial(_selwin_sample_kernel, ts=ts),
        grid_spec=grid_spec,
        out_shape=jax.ShapeDtypeStruct((nb, ncol, HEAD_DIM), F32),
        compiler_params=_cparams(("parallel", "arbitrary")),
        name="selwin_sample",
    )(pt, *([cache] * PAGES_PER_STEP), qa, sel, cfrow, nbl, state, kvn, nbn, nbw, gates, oc)


def _sample_bias_tables(rel_bias, past, ts, n_buf):
    ncol = NSA_KV_HEADS * NSA_GROUP * ts
    col = jnp.arange(ncol)
    head = col // ts
    t = col % ts
    pick = lambda b: jnp.take_along_axis(b, jnp.broadcast_to(head, b.shape[:-1])[..., None], axis=-1)[..., 0]
    bias = lambda d: pick(_head_bias(rel_bias, d))
    padc = lambda a, fill=0.0: jnp.pad(a, ((0, 0), (0, LANES - a.shape[1])), constant_values=fill)
    nc = past // CMP_STRIDE
    c_end = jnp.arange(nc)[:, None] * CMP_STRIDE + CMP_BLOCK - 1
    tab = jnp.where(jnp.arange(nc)[:, None] < nc - 1, bias(past + t[None, :] - c_end), NEG)
    hc = ncol // NSA_KV_HEADS
    tab_cmp = jnp.stack([padc(tab[:, h * hc:(h + 1) * hc]) for h in range(NSA_KV_HEADS)])
    far = rel_bias[N_BUCKETS - 1].astype(F32)[head]
    cfrow = jnp.broadcast_to(padc(far[None, :]), (SUBLANES, LANES))
    kk = jnp.arange(PAGE_SIZE)[:, None]
    nbl = padc(bias(PAGE_SIZE + t[None, :] - kk))
    j = jnp.arange(SUBLANES)[:, None]
    dn = t[None, :] - j
    nbn = padc(jnp.where((dn >= 0) & (j < ts), bias(dn), NEG))
    r = jnp.arange(n_buf)[:, None]
    dw = n_buf + t[None, :] - r
    nbw = padc(jnp.where((dw >= 0) & (dw < WINDOW), bias(dw), NEG))
    return tab_cmp, cfrow, nbl, nbn, nbw


def _sample_layer(x, p, tabs, l, pt, cache_cmp, cache_slc, state_win, state_pool_l, nb, ts, n_pages, final, gf):
    tab_cmp, cfrow, nbl, nbn, nbw = tabs
    past = n_pages * PAGE_SIZE
    R = ts * nb
    ncol = NSA_KV_HEADS * NSA_GROUP * ts
    xp, q, kvc, kvs, kvw, gt, u, v = _inproj(x, p['g_mix'], p['w_in'], p['gn'], R)
    q5 = q.reshape(ts, nb, NSA_KV_HEADS, NSA_GROUP, HEAD_DIM)
    qs = q5.transpose(1, 2, 3, 0, 4).reshape(nb, NSA_KV_HEADS, NSA_GROUP * ts, HEAD_DIM)
    qs = jnp.pad(qs, ((0, 0), (0, 0), (0, LANES - NSA_GROUP * ts), (0, 0)))
    oc, sel = _cmp_sample(pt, cache_cmp, p['wx'], p['pe'], qs, tab_cmp, nb, n_pages, ts)
    qt = q5.transpose(1, 2, 4, 3, 0).reshape(nb, NSA_KV_HEADS, HEAD_DIM, NSA_GROUP * ts)
    eye = jnp.eye(NSA_KV_HEADS, dtype=qt.dtype)
    qa = jnp.einsum('bhdc,hk->bhdkc', qt, eye).reshape(nb, NSA_KV_HEADS * HEAD_DIM, ncol)
    qa = jnp.pad(qa, ((0, 0), (0, KV_ROW - NSA_KV_HEADS * HEAD_DIM), (0, LANES - ncol)))
    rows_of = lambda a: jnp.pad(a.reshape(ts, nb, KV_ROW).transpose(1, 0, 2), ((0, 0), (0, SUBLANES - ts), (0, 0)))
    kvn = jnp.concatenate([rows_of(kvs), rows_of(kvw)], axis=1)
    g5 = gt[:, :NSA_HEADS * N_BRANCHES].reshape(ts, nb, NSA_KV_HEADS * NSA_GROUP, N_BRANCHES)
    gates = jnp.pad(g5.transpose(1, 2, 0, 3).reshape(nb, ncol, N_BRANCHES), ((0, 0), (0, 0), (0, LANES - N_BRANCHES)))
    o = _selwin_sample(pt, cache_slc, qa, sel, cfrow, nbl, state_win, l * nb, kvn, nbn, nbw, gates, oc,
                       nb, n_pages, ts)
    yb = o.reshape(nb, NSA_KV_HEADS * NSA_GROUP, ts, HEAD_DIM).transpose(2, 0, 1, 3).reshape(R, NSA_WIDTH)
    ext = jnp.concatenate([state_pool_l.transpose(1, 0, 2), xp.reshape(ts, nb, POOL_WIDTH)], axis=0)
    x1 = _mix_sample(x, ext, yb.astype(BF16), u.reshape(ts, nb, GMLP_WIDTH), v.reshape(ts, nb, GMLP_WIDTH),
                     p['wp'], p['ps'], p['wsx'], p['wb'][:SUBLANES], p['w_out'], nb, ts, past)
    x2 = _ffn(x1, p['g_ffn'], p['w_up'], p['w_down'], gf, R, final)
    return x2, (xp, kvc, kvs, kvw, v)


def kernel(x_prompt, x_sample, cache_cmp_kv, cache_slc_kv, state_win_kv, state_pool, page_table, w_in, w_out,
           norm_mix, norm_ffn, norm_final, w_pool, pool_scale, cmp_w, cmp_pe, gmlp_ws, gmlp_b, gmlp_norm, w_up,
           w_down, rel_bias):
    B, T, _ = x_prompt.shape
    nb, ts, _ = x_sample.shape
    n_pages = page_table.shape[1]
    n_phys = cache_cmp_kv.shape[1]
    n_buf = state_win_kv.shape[2]
    past = n_pages * PAGE_SIZE
    depth = w_in.shape[0]
    assert T % TK_FAR == 0 and T >= WINDOW and n_pages % PAGES_PER_STEP == 0
    assert ts <= SUBLANES and ts <= POOL_STATE and n_buf == WINDOW and past >= WINDOW
    kv_tail = (2, NSA_KV_HEADS, HEAD_DIM)

    cache_cmp = cache_cmp_kv.reshape(depth * n_phys, PAGE_SIZE, KV_ROW)
    cache_slc = cache_slc_kv.reshape(depth * n_phys, PAGE_SIZE, KV_ROW)
    state_win = state_win_kv.reshape(depth * nb, n_buf, KV_ROW)
    ptabs = _prompt_bias_tables(rel_bias)
    stabs = _sample_bias_tables(rel_bias, past, ts, n_buf)
    gf = norm_final[None, :]
    xp = x_prompt.reshape(B * T, D_MODEL)
    xs = x_sample.transpose(1, 0, 2).reshape(ts * nb, D_MODEL)
    unmajor = lambda a: a.reshape(ts, nb, a.shape[-1]).transpose(1, 0, 2)

    outs = [[] for _ in range(9)]
    for l in range(depth):
        p = _prep_layer(l, w_in, w_out, norm_mix, norm_ffn, w_pool, pool_scale, cmp_w, cmp_pe, gmlp_ws, gmlp_b,
                        gmlp_norm, w_up, w_down)
        final = l == depth - 1
        xp, (pin, kvc, kvs, kvw) = _prompt_layer(xp, p, ptabs, B, T, final, gf)
        outs[0].append(kvc.reshape((B, T) + kv_tail))
        outs[1].append(kvs.reshape((B, T) + kv_tail))
        outs[2].append(kvw.reshape((B, T) + kv_tail)[:, T - WINDOW:])
        outs[3].append(pin.reshape(B, T, POOL_WIDTH)[:, T - POOL_STATE:])

        pt = (page_table + l * n_phys).reshape(-1).astype(jnp.int32)
        xs, (sin, kvc_s, kvs_s, kvw_s, v_s) = _sample_layer(
            xs, p, stabs, l, pt, cache_cmp, cache_slc, state_win, state_pool[l], nb, ts, n_pages, final, gf)
        kvw_new = unmajor(kvw_s).reshape((nb, ts) + kv_tail)
        outs[4].append(unmajor(kvc_s).reshape((nb, ts) + kv_tail))
        outs[5].append(unmajor(kvs_s).reshape((nb, ts) + kv_tail))
        outs[6].append(jnp.concatenate([state_win_kv[l][:, ts:], kvw_new], axis=1))
        outs[7].append(jnp.concatenate([state_pool[l][:, ts:], unmajor(sin)], axis=1))
        outs[8].append(unmajor(v_s))

    y_prompt = xp.reshape(B, T, D_MODEL)
    y_sample = unmajor(xs)
    return (y_prompt, y_sample) + tuple(jnp.stack(o) for o in outs)
```

```python
import functools
import math

import numpy as np
import jax
import jax.numpy as jnp
from jax import lax
from jax.experimental import pallas as pl
from jax.experimental.pallas import tpu as pltpu

F32 = jnp.float32
BF16 = jnp.bfloat16

D_MODEL = 1024
DEPTH = 4
PAGE_SIZE = 128
HEAD_DIM = 64
POOL_WINDOWS = (2, 4, 8, 16)
POOL_WIDTH = 256
POOL_GROUP_DIM = 64
POOL_STATE = 15
NSA_WIDTH = 512
NSA_HEADS = 8
NSA_KV_HEADS = 2
NSA_GROUP = 4
KV_ROW = 2 * NSA_KV_HEADS * HEAD_DIM
CMP_STRIDE = 16
CMP_BLOCK = 32
SEL_BLOCK = 64
N_SELECT = 16
WINDOW = 512
N_BRANCHES = 3
FORCE_SCORE = 1000.0
GMLP_WIDTH = 256
GMLP_GROUPS = 4
GMLP_CHUNK = 128
D_FF = 4096
N_BUCKETS = 32
MAX_DISTANCE = 128
EPS = 1e-6
SCALE = HEAD_DIM ** -0.5

LANES = 128
SUBLANES = 8
VMEM_LIMIT = 56 * 1024 * 1024

NEG = -1e30
CMP_PAD = 16
CMP_NEAR = 24
TQ = 128
TK_FAR = 512
KPAD = 512
GATE_LANES = 2 * LANES

PROJ_SEGS = (('xp', 256), ('q', 512), ('kvc', 256), ('kvs', 256), ('kvw', 256), ('gt', 24), ('u', 256), ('v', 256))


def _cparams(sem):
    return pltpu.CompilerParams(dimension_semantics=sem, vmem_limit_bytes=VMEM_LIMIT)


def _dot(a, b):
    return jnp.dot(a, b, preferred_element_type=F32)


def _dot_nt(a, b):
    return lax.dot_general(a, b, (((1,), (1,)), ((), ())), preferred_element_type=F32)


def _gelu(x):
    c = math.sqrt(2.0 / math.pi)
    return 0.5 * x * (1.0 + jnp.tanh(c * (x + 0.044715 * (x * x * x))))


def _rms(x, g):
    return x * lax.rsqrt(jnp.mean(x * x, axis=-1, keepdims=True) + EPS) * g


def _inproj_kernel(x_ref, g_ref, w_ref, gn_ref, *refs, channel_major):
    if channel_major:
        wt_ref, xp_ref, q_ref, gt_ref, u_ref, v_ref, kvc_ref, *kv_refs = refs
    else:
        xp_ref, q_ref, gt_ref, u_ref, v_ref, kvc_ref, *kv_refs = refs
    x = x_ref[...]
    h = _rms(x, g_ref[...]).astype(BF16)

    def seg(lo, hi):
        return _dot(h, w_ref[:, lo:hi])

    xp_ref[...] = seg(0, 256)
    q_ref[...] = (seg(256, 768) * SCALE).astype(BF16)
    kvc_ref[...] = seg(768, 1024)
    if channel_major:
        for k, t_ref in enumerate(kv_refs):
            t_ref[0] = _dot_nt(wt_ref[k * KV_ROW:(k + 1) * KV_ROW, :], h)
    else:
        kv_refs[0][...] = seg(1024, 1280)
        kv_refs[1][...] = seg(1280, 1536)
    u_ref[...] = _gelu(seg(1536, 1792))
    gv = _gelu(seg(1792, 2048))
    sq = gv * gv
    lane = lax.broadcasted_iota(jnp.int32, sq.shape, 1)
    ms = jnp.zeros_like(sq)
    for g in range(GMLP_GROUPS):
        in_g = (lane >= g * 64) & (lane < (g + 1) * 64)
        s = jnp.sum(jnp.where(in_g, sq, 0.0), axis=-1, keepdims=True) * (1.0 / 64.0)
        ms = jnp.where(in_g, s, ms)
    v_ref[...] = gv * lax.rsqrt(ms + EPS) * gn_ref[...]
    gt_ref[...] = jax.nn.sigmoid(seg(2048, 2048 + GATE_LANES))


def _inproj(x, g, w, gn, tm, seq=None, wt=None):
    R = x.shape[0]
    f = lambda n, dt=F32: jax.ShapeDtypeStruct((R, n), dt)
    row = lambda n: pl.BlockSpec((tm, n), lambda i: (i, 0))
    full = lambda a: pl.BlockSpec(a.shape, lambda i: (0,) * a.ndim)
    out_specs = [row(256), row(512), row(GATE_LANES), row(256), row(256), row(256)]
    out_shape = [f(256), f(512, BF16), f(GATE_LANES), f(256), f(256), f(256)]
    if seq is None:
        out_specs += [row(256), row(256)]
        out_shape += [f(256), f(256)]
    else:
        per_seq = seq // tm
        out_specs += [pl.BlockSpec((1, KV_ROW, tm), lambda i: (i // per_seq, 0, i % per_seq))] * 3
        out_shape += [jax.ShapeDtypeStruct((R // seq, KV_ROW, seq), F32)] * 3
    args = (x, g, w, gn) if seq is None else (x, g, w, gn, wt)
    return pl.pallas_call(
        functools.partial(_inproj_kernel, channel_major=seq is not None),
        grid=(R // tm,),
        in_specs=[row(D_MODEL)] + [full(a) for a in args[1:]],
        out_specs=out_specs,
        out_shape=out_shape,
        compiler_params=_cparams(("parallel",)),
        name="inproj",
    )(*args)


def _out_proj(x, ya, yb, yc, wo_ref):
    acc = _dot(ya.astype(BF16), wo_ref[0:256, :])
    acc = acc + _dot(yb, wo_ref[256:768, :])
    acc = acc + _dot(yc.astype(BF16), wo_ref[768:1024, :])
    return x + acc


def _pool_tail(win2, win4, win8, win16, cur, cnt, wp_ref, ps_ref):
    lane = lax.broadcasted_iota(jnp.int32, cur.shape, 1)
    win = jnp.where(lane < 64, win2, jnp.where(lane < 128, win4, jnp.where(lane < 192, win8, win16)))
    pooled = win / cnt - cur
    return _dot(pooled.astype(BF16), wp_ref[...]) * ps_ref[...]


def _mix_prompt_kernel(x_ref, xp_ref, halo_ref, yb_ref, u_ref, v_ref, wp_ref, ps_ref, ws_ref, wb_ref, wo_ref,
                       o_ref, ext_ref, *, tm, seq):
    i = pl.program_id(0)
    t0 = (i * tm) % seq
    halo = jnp.where(t0 > 0, halo_ref[...], 0.0)
    cur = xp_ref[...]
    ext_ref[0:16, :] = halo
    ext_ref[16:16 + tm, :] = cur
    e = ext_ref[...]
    b2 = e[1:] + e[:-1]
    b4 = b2[2:] + b2[:-2]
    b8 = b4[4:] + b4[:-4]
    b16 = b8[8:] + b8[:-8]
    win2 = b2[15:15 + tm]
    win4 = b4[13:13 + tm]
    win8 = b8[9:9 + tm]
    win16 = b16[1:1 + tm]
    rowi = lax.broadcasted_iota(jnp.int32, cur.shape, 0)
    lane = lax.broadcasted_iota(jnp.int32, cur.shape, 1)
    wsz = jnp.where(lane < 64, 2, jnp.where(lane < 128, 4, jnp.where(lane < 192, 8, 16)))
    cnt = jnp.minimum(t0 + rowi + 1, wsz).astype(F32)
    ya = _pool_tail(win2, win4, win8, win16, cur, cnt, wp_ref, ps_ref)

    ci = lax.broadcasted_iota(jnp.int32, (GMLP_CHUNK, GMLP_CHUNK), 0)
    cj = lax.broadcasted_iota(jnp.int32, (GMLP_CHUNK, GMLP_CHUNK), 1)
    lane_c = lax.broadcasted_iota(jnp.int32, (GMLP_CHUNK, GMLP_WIDTH), 1)
    wts = [jnp.where(ci >= cj, ws_ref[g], 0.0).astype(BF16) for g in range(GMLP_GROUPS)]
    parts = []
    for c in range(tm // GMLP_CHUNK):
        vc = v_ref[c * GMLP_CHUNK:(c + 1) * GMLP_CHUNK, :].astype(BF16)
        s = jnp.zeros((GMLP_CHUNK, GMLP_WIDTH), F32)
        for g in range(GMLP_GROUPS):
            sg = _dot(wts[g], vc)
            s = jnp.where((lane_c >= g * 64) & (lane_c < (g + 1) * 64), sg, s)
        parts.append(u_ref[c * GMLP_CHUNK:(c + 1) * GMLP_CHUNK, :] * (s + wb_ref[...]))
    yc = jnp.concatenate(parts, axis=0) if len(parts) > 1 else parts[0]

    o_ref[...] = _out_proj(x_ref[...], ya, yb_ref[...], yc, wo_ref)


def _mix_prompt(x, xp, yb, u, v, wp, ps, ws, wb, wo, tm, seq):
    R = x.shape[0]
    row = lambda n: pl.BlockSpec((tm, n), lambda i: (i, 0))
    full = lambda a: pl.BlockSpec(a.shape, lambda i: (0,) * a.ndim)
    halo = pl.BlockSpec((16, 256), lambda i: (jnp.maximum(i * (tm // 16) - 1, 0), 0))
    return pl.pallas_call(
        functools.partial(_mix_prompt_kernel, tm=tm, seq=seq),
        grid=(R // tm,),
        in_specs=[row(D_MODEL), row(256), halo, row(512), row(256), row(256),
                  full(wp), full(ps), full(ws), full(wb), full(wo)],
        out_specs=row(D_MODEL),
        out_shape=jax.ShapeDtypeStruct((R, D_MODEL), F32),
        scratch_shapes=[pltpu.VMEM((tm + 16, 256), F32)],
        compiler_params=_cparams(("parallel",)),
        name="mix_prompt",
    )(x, xp, xp, yb, u, v, wp, ps, ws, wb, wo)


def _mix_sample_kernel(x_ref, ext_ref, yb_ref, u_ref, v_ref, wp_ref, ps_ref, wsx_ref, wb_ref, wo_ref, o_ref,
                       *, nb, ts, past):
    lane = lax.broadcasted_iota(jnp.int32, (nb, POOL_WIDTH), 1)
    wsz = jnp.where(lane < 64, 2, jnp.where(lane < 128, 4, jnp.where(lane < 192, 8, 16)))
    yas, ycs = [], []
    for t in range(ts):
        top = POOL_STATE + t
        acc = ext_ref[top] + ext_ref[top - 1]
        wins = [acc]
        for w in (4, 8, 16):
            for j in range(w // 2, w):
                acc = acc + ext_ref[top - j]
            wins.append(acc)
        cnt = jnp.minimum(past + t + 1, wsz).astype(F32)
        yas.append(_pool_tail(wins[0], wins[1], wins[2], wins[3], ext_ref[top], cnt, wp_ref, ps_ref))
        s = wb_ref[t:t + 1, :]
        s = jnp.broadcast_to(s, (nb, GMLP_WIDTH))
        for j in range(t + 1):
            s = s + wsx_ref[t, j:j + 1, :] * v_ref[j]
        ycs.append(u_ref[t] * s)
    ya = jnp.concatenate(yas, axis=0)
    yc = jnp.concatenate(ycs, axis=0)
    o_ref[...] = _out_proj(x_ref[...], ya, yb_ref[...], yc, wo_ref)


def _mix_sample(x, ext, yb, u, v, wp, ps, wsx, wb4, wo, nb, ts, past):
    R = x.shape[0]
    args = (x, ext, yb, u, v, wp, ps, wsx, wb4, wo)
    full = lambda a: pl.BlockSpec(a.shape, lambda i: (0,) * a.ndim)
    return pl.pallas_call(
        functools.partial(_mix_sample_kernel, nb=nb, ts=ts, past=past),
        grid=(1,),
        in_specs=[full(a) for a in args],
        out_specs=pl.BlockSpec((R, D_MODEL), lambda i: (0, 0)),
        out_shape=jax.ShapeDtypeStruct((R, D_MODEL), F32),
        compiler_params=_cparams(("arbitrary",)),
        name="mix_sample",
    )(*args)


def _ffn_kernel(x_ref, g_ref, wu_ref, wd_ref, gf_ref, o_ref, *, final, fc):
    x = x_ref[...]
    h = _rms(x, g_ref[...]).astype(BF16)
    acc = x
    for c in range(D_FF // fc):
        a = jnp.maximum(_dot(h, wu_ref[:, c * fc:(c + 1) * fc]), 0.0)
        acc = acc + _dot((a * a).astype(BF16), wd_ref[c * fc:(c + 1) * fc, :])
    if final:
        acc = _rms(acc, gf_ref[...])
    o_ref[...] = acc


def _ffn(x, g, wu, wd, gf, tm, final):
    R = x.shape[0]
    row = pl.BlockSpec((tm, D_MODEL), lambda i: (i, 0))
    full = lambda a: pl.BlockSpec(a.shape, lambda i: (0,) * a.ndim)
    wfull = lambda a: pl.BlockSpec(a.shape, lambda i: (0,) * a.ndim, pipeline_mode=pl.Buffered(1))
    return pl.pallas_call(
        functools.partial(_ffn_kernel, final=final, fc=1024),
        grid=(R // tm,),
        in_specs=[row, full(g), wfull(wu), wfull(wd), full(gf)],
        out_specs=row,
        out_shape=jax.ShapeDtypeStruct((R, D_MODEL), F32),
        compiler_params=_cparams(("parallel",)),
        name="ffn",
    )(x, g, wu, wd, gf)


def _compress_rows(read_rows, n, wx):
    lo = jnp.zeros((n, wx.shape[1]), F32)
    hi = jnp.zeros((n, wx.shape[1]), F32)
    for j in range(CMP_STRIDE):
        xj = read_rows(j)
        lo = lo + xj * wx[j:j + 1, :]
        hi = hi + xj * wx[CMP_STRIDE + j:CMP_STRIDE + j + 1, :]
    return lo, hi


def _combine_halves(lo, hi, pe, wx):
    n = lo.shape[0]
    pe_term = jnp.sum(pe * wx, axis=0, keepdims=True)
    comp = lo + pltpu.roll(hi, n - 1, 0) + pe_term
    rowi = lax.broadcasted_iota(jnp.int32, comp.shape, 0)
    return jnp.where(rowi < n - 1, comp, 0.0)


def _compress_prompt_kernel(x_ref, wx_ref, pe_ref, o_ref, *, n):
    wx = wx_ref[...]
    lo, hi = _compress_rows(lambda j: x_ref[0, pl.ds(j, n, stride=CMP_STRIDE), :], n, wx)
    o_ref[0, 0:CMP_PAD, :] = jnp.zeros((CMP_PAD, LANES), F32)
    o_ref[0, CMP_PAD:CMP_PAD + n, :] = _combine_halves(lo, hi, pe_ref[...], wx)


def _compress_prompt(kvc, wx, pe):
    B, T, _ = kvc.shape
    n = T // CMP_STRIDE
    half = lambda rows: pl.BlockSpec((rows, LANES), lambda b, e: (0, e))
    return pl.pallas_call(
        functools.partial(_compress_prompt_kernel, n=n),
        grid=(B, KV_ROW // LANES),
        in_specs=[pl.BlockSpec((1, T, LANES), lambda b, e: (b, 0, e)), half(CMP_BLOCK), half(CMP_BLOCK)],
        out_specs=pl.BlockSpec((1, n + CMP_PAD, LANES), lambda b, e: (b, 0, e)),
        out_shape=jax.ShapeDtypeStruct((B, n + CMP_PAD, KV_ROW), F32),
        compiler_params=_cparams(("parallel", "parallel")),
        name="compress_prompt",
    )(kvc, wx, pe)


def _select_topk(score, n_pick):
    s_iota = lax.broadcasted_iota(jnp.int32, score.shape, 0)
    big = score.shape[0]
    sel = jnp.zeros(score.shape, F32)
    for _ in range(n_pick):
        m = jnp.max(score, axis=0, keepdims=True)
        idx = jnp.min(jnp.where(score == m, s_iota, big), axis=0, keepdims=True)
        pick = s_iota == idx
        sel = jnp.where(pick & (m >= 0.0), 1.0, sel)
        score = jnp.where(pick, -2.0, score)
    return sel


def _softmax_rows(z):
    m = jnp.max(z, axis=0, keepdims=True)
    p = jnp.where(z > 0.5 * NEG, jnp.exp(z - m), 0.0)
    den = jnp.maximum(jnp.sum(p, axis=0, keepdims=True), 1e-30)
    return p * (1.0 / den)


def _cmp_prompt_kernel(q_ref, kc_ref, vct_ref, nb_ref, cf_ref, oc_ref, sn_ref, raw_ref, z_ref, ps_ref, *, ncp):
    i = pl.program_id(2)
    hh = pl.program_id(1)
    r0 = pl.multiple_of(i * (TQ // CMP_STRIDE), SUBLANES)
    q = q_ref[0]
    kc = kc_ref[0, 0]
    vct = vct_ref[0, 0]
    rowi = lax.broadcasted_iota(jnp.int32, (ncp, TQ), 0)
    far = (rowi >= CMP_PAD) & (rowi < r0)
    rown = lax.broadcasted_iota(jnp.int32, (CMP_NEAR, TQ), 0) + r0
    psum = jnp.zeros((ncp, TQ), F32)
    octs = []
    for g in range(NSA_GROUP):
        qg = q[:, g * HEAD_DIM:(g + 1) * HEAD_DIM]
        raw_ref[...] = _dot_nt(kc, qg)
        cf = cf_ref[hh * NSA_GROUP + g]
        z_ref[...] = jnp.where(far, raw_ref[...] + cf[0:1, :], NEG)
        nb = nb_ref[hh * NSA_GROUP + g]
        zn = raw_ref[pl.ds(r0, CMP_NEAR), :] + nb
        z_ref[pl.ds(r0, CMP_NEAR), :] = jnp.where((rown >= CMP_PAD) & (nb > 0.5 * NEG), zn, NEG)
        pc = _softmax_rows(z_ref[...])
        psum = psum + pc
        octs.append(_dot(vct, pc.astype(BF16)))
    oc_ref[0] = jnp.concatenate(octs, axis=0).T

    ps_ref[0:ncp, :] = psum
    if ncp < ps_ref.shape[0]:
        ps_ref[ncp:, :] = jnp.zeros((ps_ref.shape[0] - ncp, TQ), F32)
    ns = LANES
    p_slc = ps_ref[pl.ds(CMP_PAD, ns, stride=4), :]
    for j in range(1, 4):
        p_slc = p_slc + ps_ref[pl.ds(CMP_PAD + j, ns, stride=4), :]
    s_iota = lax.broadcasted_iota(jnp.int32, (ns, TQ), 0)
    tt = lax.broadcasted_iota(jnp.int32, (ns, TQ), 1)
    cur = (i * TQ + tt) // SEL_BLOCK
    forced = (s_iota == 0) | (s_iota == cur) | (s_iota == cur - 1)
    score = jnp.where(s_iota <= cur, jnp.where(forced, FORCE_SCORE, p_slc), -1.0)
    sel = _select_topk(score, N_SELECT)
    sn_ref[0, 0] = jnp.where(sel.T > 0.5, 0.0, NEG).astype(BF16)


def _cmp_prompt(q, kc, vct, nbc, cfar):
    B, T, _ = q.shape
    ncp = kc.shape[2]
    ns = LANES
    nq = T // TQ
    full = lambda a: pl.BlockSpec(a.shape, lambda b, h, i: (0,) * a.ndim)
    return pl.pallas_call(
        functools.partial(_cmp_prompt_kernel, ncp=ncp),
        grid=(B, NSA_KV_HEADS, nq),
        in_specs=[pl.BlockSpec((1, TQ, 256), lambda b, h, i: (b, i, h)),
                  pl.BlockSpec((1, 1, ncp, HEAD_DIM), lambda b, h, i: (b, h, 0, 0)),
                  pl.BlockSpec((1, 1, HEAD_DIM, ncp), lambda b, h, i: (b, h, 0, 0)),
                  full(nbc), full(cfar)],
        out_specs=[pl.BlockSpec((1, TQ, 256), lambda b, h, i: (b, i, h)),
                   pl.BlockSpec((1, 1, TQ, ns), lambda b, h, i: (b, h, i, 0))],
        out_shape=[jax.ShapeDtypeStruct((B, T, NSA_WIDTH), F32),
                   jax.ShapeDtypeStruct((B, NSA_KV_HEADS, T, ns), BF16)],
        scratch_shapes=[pltpu.VMEM((ncp, TQ), F32), pltpu.VMEM((ncp, TQ), F32),
                        pltpu.VMEM((max(ncp, CMP_PAD + 4 * LANES), TQ), F32)],
        compiler_params=_cparams(("parallel", "parallel", "parallel")),
        name="cmp_prompt",
    )(q, kc, vct, nbc, cfar)


def _selwin_prompt_kernel(q_ref, sn_ref, ks_ref, vs_ref, kw_ref, vw_ref, oh_ref, g_ref, oc_ref, nbn_ref, nbw_ref,
                          o_ref, ka_ref, va_ref, kwa_ref, vwa_ref, ms_ref, as_ref, za_ref, zb_ref):
    i = pl.program_id(2)
    R4 = NSA_GROUP * TQ
    ncols = ka_ref.shape[1]

    @pl.when(i == 0)
    def _():
        zpad = jnp.zeros((HEAD_DIM, KPAD), BF16)
        ones = jnp.ones((HEAD_DIM, ncols), BF16)
        row = lax.broadcasted_iota(jnp.int32, (HEAD_DIM, ncols), 0)
        col = lax.broadcasted_iota(jnp.int32, (HEAD_DIM, ncols), 1)
        ka_ref[0:LANES, :] = oh_ref[...]
        kwa_ref[HEAD_DIM:, :] = jnp.where((row == 0) & (col < KPAD), NEG, 0.0).astype(BF16)
        for dst, src in ((ka_ref.at[LANES:], ks_ref), (va_ref.at[0:HEAD_DIM], vs_ref),
                         (kwa_ref.at[0:HEAD_DIM], kw_ref), (vwa_ref.at[0:HEAD_DIM], vw_ref)):
            dst[:, 0:KPAD] = zpad
            dst[:, KPAD:] = src[0].astype(BF16)
        va_ref[HEAD_DIM:, :] = ones
        vwa_ref[HEAD_DIM:, :] = ones

    q = q_ref[0]
    q4 = jnp.concatenate([q[:, g * HEAD_DIM:(g + 1) * HEAD_DIM] for g in range(NSA_GROUP)], axis=0)
    sn = sn_ref[0, 0]
    blk = lax.broadcasted_iota(jnp.int32, (TQ, LANES), 1)
    first_near = (i - 1) * (TQ // SEL_BLOCK)
    sn_far = jnp.where(blk >= first_near, NEG, sn.astype(F32)).astype(BF16)
    qa_near = jnp.concatenate([jnp.concatenate([sn] * NSA_GROUP, axis=0), q4], axis=1)
    qa_far = jnp.concatenate([jnp.concatenate([sn_far] * NSA_GROUP, axis=0), q4], axis=1)
    qw = jnp.concatenate([q4, jnp.ones((R4, HEAD_DIM), BF16)], axis=1)

    def one_pass(z, vt):
        m = jnp.max(z, axis=1, keepdims=True)
        acc = _dot_nt(jnp.exp(z - m).astype(BF16), vt)
        return m, acc

    kw0 = pl.multiple_of(i * TQ, TQ)
    zw = _dot(qw, kwa_ref[:, pl.ds(kw0, WINDOW + TQ)]) + nbw_ref[0]
    _, a_w = one_pass(zw, vwa_ref[:, pl.ds(kw0, WINDOW + TQ)])

    kn0 = pl.multiple_of((i - 1) * TQ + KPAD, TQ)
    zn = _dot(qa_near, ka_ref[:, pl.ds(kn0, 2 * TQ)]) + nbn_ref[0]
    m0, a0 = one_pass(zn, va_ref[:, pl.ds(kn0, 2 * TQ)])
    ms_ref[...] = jnp.broadcast_to(m0, (R4, LANES))
    as_ref[...] = a0

    def far_logits(j):
        k0 = pl.multiple_of(j * TK_FAR + KPAD, TK_FAR)
        return _dot(qa_far, ka_ref[:, pl.ds(k0, TK_FAR)])

    def far_update(j, z):
        k0 = pl.multiple_of(j * TK_FAR + KPAD, TK_FAR)
        m_prev = ms_ref[...]
        m_new = jnp.maximum(m_prev, jnp.max(z, axis=1, keepdims=True))
        p = jnp.exp(z - jnp.concatenate([m_new] * (TK_FAR // LANES), axis=1))
        as_ref[...] = jnp.exp(m_prev - m_new) * as_ref[...] + _dot_nt(p.astype(BF16), va_ref[:, pl.ds(k0, TK_FAR)])
        ms_ref[...] = m_new

    n_far = (jnp.maximum(i - 1, 0) * TQ + TK_FAR - 1) // TK_FAR
    odd = n_far % 2
    last = n_far - 1

    @pl.when(n_far > 0)
    def _():
        za_ref[...] = far_logits(-odd)

        def far_body(jj, carry):
            t = 2 * jj - odd
            zb_ref[...] = far_logits(t + 1)
            far_update(t, za_ref[...])
            za_ref[...] = far_logits(jnp.minimum(t + 2, last))
            far_update(t + 1, zb_ref[...])
            return carry

        lax.fori_loop(0, (n_far + 1) // 2, far_body, 0)

    a_s = as_ref[...]
    o_s = a_s[:, 0:HEAD_DIM] / jnp.maximum(a_s[:, HEAD_DIM:2 * HEAD_DIM], 1e-30)
    o_w = a_w[:, 0:HEAD_DIM] / jnp.maximum(a_w[:, HEAD_DIM:2 * HEAD_DIM], 1e-30)
    gt = g_ref[...]
    oc = oc_ref[0]
    outs = []
    for g in range(NSA_GROUP):
        sl = slice(g * TQ, (g + 1) * TQ)
        o = (gt[:, 3 * g:3 * g + 1] * oc[:, g * HEAD_DIM:(g + 1) * HEAD_DIM]
             + gt[:, 3 * g + 1:3 * g + 2] * o_s[sl] + gt[:, 3 * g + 2:3 * g + 3] * o_w[sl])
        outs.append(o)
    o_ref[0] = jnp.concatenate(outs, axis=1).astype(BF16)


def _selwin_prompt(q, sn, kvs_t, kvw_t, onehot_t, gates, oc, nbn, nbw):
    B, T, _ = q.shape
    nq = T // TQ
    R4 = NSA_GROUP * TQ
    ncols = T + KPAD
    k_of = pl.BlockSpec((1, HEAD_DIM, T), lambda b, h, i: (b, h, 0))
    v_of = pl.BlockSpec((1, HEAD_DIM, T), lambda b, h, i: (b, NSA_KV_HEADS + h, 0))
    perh = lambda a: pl.BlockSpec((1,) + a.shape[1:], lambda b, h, i: (h, 0, 0))
    qtile = pl.BlockSpec((1, TQ, 256), lambda b, h, i: (b, i, h))
    return pl.pallas_call(
        _selwin_prompt_kernel,
        grid=(B, NSA_KV_HEADS, nq),
        in_specs=[qtile, pl.BlockSpec((1, 1, TQ, sn.shape[-1]), lambda b, h, i: (b, h, i, 0)),
                  k_of, v_of, k_of, v_of, pl.BlockSpec(onehot_t.shape, lambda b, h, i: (0, 0)),
                  pl.BlockSpec((TQ, LANES), lambda b, h, i: (b * nq + i, h)), qtile, perh(nbn), perh(nbw)],
        out_specs=qtile,
        out_shape=jax.ShapeDtypeStruct((B, T, NSA_WIDTH), BF16),
        scratch_shapes=[pltpu.VMEM((LANES + HEAD_DIM, ncols), BF16)] + [pltpu.VMEM((LANES, ncols), BF16)] * 3
        + [pltpu.VMEM((R4, LANES), F32)] * 2 + [pltpu.VMEM((R4, TK_FAR), F32)] * 2,
        compiler_params=_cparams(("parallel", "parallel", "arbitrary")),
        name="selwin_prompt",
    )(q, sn, kvs_t, kvs_t, kvw_t, kvw_t, onehot_t, gates, oc, nbn, nbw)


def _rel_bucket(dist):
    d = jnp.maximum(dist, 0)
    n_exact = N_BUCKETS // 2
    d_f = jnp.maximum(d, 1).astype(F32)
    large = n_exact + (jnp.log(d_f / n_exact) / math.log(MAX_DISTANCE / n_exact)
                       * (N_BUCKETS - n_exact)).astype(jnp.int32)
    large = jnp.minimum(large, N_BUCKETS - 1)
    return jnp.where(d < n_exact, d, large)


def _head_bias(rel_bias, dist):
    return jnp.take(rel_bias, _rel_bucket(dist), axis=0).astype(F32)


def _prompt_bias_tables(rel_bias):
    far = rel_bias[N_BUCKETS - 1].astype(F32)
    cfar = jnp.broadcast_to(far[:, None, None], (NSA_HEADS, SUBLANES, LANES))
    k = jnp.arange(CMP_NEAR)[:, None]
    tt = jnp.arange(TQ)[None, :]
    d = tt + CMP_PAD * CMP_STRIDE - (CMP_BLOCK - 1) - CMP_STRIDE * k
    nbc = jnp.where((d >= 0)[..., None], _head_bias(rel_bias, d), NEG).transpose(2, 0, 1)
    tq = jnp.arange(TQ)[:, None]

    def tile_table(first_dist, n_keys, max_dist):
        d = first_dist + tq - jnp.arange(n_keys)[None, :]
        b = jnp.where(((d >= 0) & (d < max_dist))[..., None], _head_bias(rel_bias, d) - far, NEG)
        return b.transpose(2, 0, 1).reshape(NSA_KV_HEADS, NSA_GROUP * TQ, n_keys)

    nbn = tile_table(TQ, 2 * TQ, 2 * TQ + 1)
    nbw = tile_table(WINDOW, WINDOW + TQ, WINDOW)
    return cfar, nbc, nbn, nbw


def _prep_layer(l, w_in, w_out, norm_mix, norm_ffn, w_pool, pool_scale, cmp_w, cmp_pe, gmlp_ws, gmlp_b,
                gmlp_norm, w_up, w_down):
    w = w_in[l]
    offs = np.cumsum([0] + [n for _, n in PROJ_SEGS])
    seg = {name: w[:, offs[k]:offs[k + 1]] for k, (name, _) in enumerate(PROJ_SEGS)}
    order = ('xp', 'q', 'kvc', 'kvs', 'kvw', 'u', 'v')
    ng = NSA_GROUP * N_BRANCHES
    gpad = jnp.zeros((D_MODEL, LANES - ng), w.dtype)
    gate_cols = [c for h in range(NSA_KV_HEADS) for c in (seg['gt'][:, h * ng:(h + 1) * ng], gpad)]
    w_perm = jnp.concatenate([seg[n] for n in order] + gate_cols, axis=1)
    eye = jnp.eye(len(POOL_WINDOWS), dtype=F32)
    wp = jnp.einsum('gcd,gh->gchd', w_pool[l], eye).reshape(POOL_WIDTH, POOL_WIDTH)
    wx = jnp.broadcast_to(cmp_w[l].transpose(1, 0, 2)[..., None], (CMP_BLOCK, 2, NSA_KV_HEADS, HEAD_DIM))
    return dict(
        w_in=w_perm.astype(BF16), g_mix=norm_mix[l][None, :],
        w_kv_t=jnp.concatenate([seg['kvc'], seg['kvs'], seg['kvw']], axis=1).T.astype(BF16),
        g_ffn=norm_ffn[l][None, :],
        gn=gmlp_norm[l][None, :], wp=wp.astype(BF16), ps=pool_scale[l][None, :],
        wx=wx.reshape(CMP_BLOCK, KV_ROW), pe=cmp_pe[l].transpose(1, 0, 2, 3).reshape(CMP_BLOCK, KV_ROW),
        ws=gmlp_ws[l], wb=jnp.repeat(gmlp_b[l].T, POOL_GROUP_DIM, axis=1),
        wsx=jnp.repeat(gmlp_ws[l].transpose(1, 2, 0), POOL_GROUP_DIM, axis=2)[:8, :8],
        w_out=w_out[l].astype(BF16), w_up=w_up[l].astype(BF16), w_down=w_down[l].astype(BF16))


def _split_heads(kv, part):
    B, T, _ = kv.shape
    return kv.reshape(B, T, 2, NSA_KV_HEADS, HEAD_DIM)[:, :, part].transpose(0, 2, 1, 3).astype(BF16)


def _prompt_layer(x, p, tabs, B, T, final, gf):
    cfar, nbc, nbn, nbw = tabs
    tm = min(512, B * T)
    xp, q, gt, u, v, kvc, kvc_t, kvs_t, kvw_t = _inproj(x, p['g_mix'], p['w_in'], p['gn'], tm, seq=T,
                                                        wt=p['w_kv_t'])
    comp = _compress_prompt(kvc.reshape(B, T, KV_ROW), p['wx'], p['pe'])
    kc = _split_heads(comp, 0)
    vct = _split_heads(comp, 1).transpose(0, 1, 3, 2)
    q3 = q.reshape(B, T, NSA_WIDTH)
    oc, sn = _cmp_prompt(q3, kc, vct, nbc, cfar)
    pos = jnp.arange(-KPAD, T)
    blk_of = jnp.where(pos >= 0, pos // SEL_BLOCK, LANES - 1)
    onehot_t = (jnp.arange(LANES)[:, None] == blk_of[None, :]).astype(BF16)
    yb = _selwin_prompt(q3, sn, kvs_t, kvw_t, onehot_t, gt, oc, nbn, nbw)
    x1 = _mix_prompt(x, xp, yb.reshape(B * T, NSA_WIDTH), u, v, p['wp'], p['ps'], p['ws'], p['wb'], p['w_out'],
                     tm, T)
    x2 = _ffn(x1, p['g_ffn'], p['w_up'], p['w_down'], gf, tm, final)
    return x2, (xp, kvc_t, kvs_t, kvw_t)


PAGES_PER_STEP = 8
SEL_PER_PAGE = PAGE_SIZE // SEL_BLOCK


def _cmp_sample_kernel(pt_ref, *refs, ts):
    del pt_ref
    npg = PAGES_PER_STEP
    nhalf = KV_ROW // LANES
    pages = refs[:npg]
    wx_ref, pe_ref, q_ref, tab_ref, oc_ref, sel_ref, lo_ref, hi_ref, ps_ref, pg_ref = refs[npg:]
    s = pl.program_id(1)
    rows = PAGE_SIZE // CMP_STRIDE
    for k in range(npg):
        for e in range(nhalf):
            pg_ref[...] = _page_half_rows(pages[k], e)
            lo, hi = _compress_rows(lambda j: pg_ref[pl.ds(j, rows, stride=CMP_STRIDE), :], rows,
                                    wx_ref[:, e * LANES:(e + 1) * LANES])
            lo_ref[s, k * rows:(k + 1) * rows, e * LANES:(e + 1) * LANES] = lo
            hi_ref[s, k * rows:(k + 1) * rows, e * LANES:(e + 1) * LANES] = hi

    @pl.when(s == pl.num_programs(1) - 1)
    def _():
        nc = lo_ref.shape[0] * lo_ref.shape[1]
        comp = _combine_halves(lo_ref[...].reshape(nc, KV_ROW), hi_ref[...].reshape(nc, KV_ROW), pe_ref[...],
                               wx_ref[...])
        ns = nc // 4
        lane = lax.broadcasted_iota(jnp.int32, (ns, LANES), 1)
        s_iota = lax.broadcasted_iota(jnp.int32, (ns, LANES), 0)
        ncol = NSA_GROUP * ts
        selacc = jnp.zeros((ns, LANES), F32)
        for h in range(NSA_KV_HEADS):
            kc = comp[:, h * HEAD_DIM:(h + 1) * HEAD_DIM].astype(BF16)
            vc = comp[:, (NSA_KV_HEADS + h) * HEAD_DIM:(NSA_KV_HEADS + h + 1) * HEAD_DIM].astype(BF16)
            pc = _softmax_rows(_dot_nt(kc, q_ref[0, h]) + tab_ref[h])
            oc_ref[0, h] = pl.dot(pc.astype(BF16), vc, trans_a=True)[0:ncol, :]
            ps = pc
            for g in range(1, NSA_GROUP):
                ps = ps + pltpu.roll(pc, g * ts, 1)
            ps_ref[...] = ps
            p_slc = ps_ref[pl.ds(0, ns, stride=4), :]
            for j in range(1, 4):
                p_slc = p_slc + ps_ref[pl.ds(j, ns, stride=4), :]
            forced = (s_iota == 0) | (s_iota == ns - 1)
            sel = _select_topk(jnp.where(forced, FORCE_SCORE, p_slc), N_SELECT - 1)
            for g in range(NSA_GROUP):
                dst = h * ncol + g * ts
                shift = (dst - (ncol - ts)) % LANES
                moved = pltpu.roll(sel, shift, 1) if shift else sel
                selacc = jnp.where((lane >= dst) & (lane < dst + ts), moved, selacc)
        sel_ref[0] = selacc.reshape(sel_ref.shape[1:])


def _page_half_rows(page_ref, e):
    t = page_ref[0, e]
    return t.reshape(NSA_KV_HEADS * HEAD_DIM, t.shape[-1]).T


def _native_rows(a, lead):
    nd = a.ndim
    t = jnp.transpose(a, tuple(range(nd - 4)) + (nd - 3, nd - 2, nd - 1, nd - 4))
    return t.reshape((lead,) + t.shape[nd - 4:])


def _page_specs(n_pages):
    def spec(k):
        return pl.BlockSpec((1, 2, NSA_KV_HEADS, HEAD_DIM, PAGE_SIZE),
                            lambda b, s, pt: (pt[b * n_pages + s * PAGES_PER_STEP + k], 0, 0, 0, 0))
    return [spec(k) for k in range(PAGES_PER_STEP)]


def _cmp_sample(pt, cache, wx, pe, qs, tab, nb, n_pages, ts):
    steps = n_pages // PAGES_PER_STEP
    nc = n_pages * PAGE_SIZE // CMP_STRIDE
    rows = PAGES_PER_STEP * PAGE_SIZE // CMP_STRIDE
    ns = nc // 4
    full = lambda a: pl.BlockSpec(a.shape, lambda b, s, pt: (0,) * a.ndim)
    grid_spec = pltpu.PrefetchScalarGridSpec(
        num_scalar_prefetch=1,
        grid=(nb, steps),
        in_specs=_page_specs(n_pages) + [
            full(wx), full(pe),
            pl.BlockSpec((1, NSA_KV_HEADS, LANES, HEAD_DIM), lambda b, s, pt: (b, 0, 0, 0)),
            full(tab)],
        out_specs=[pl.BlockSpec((1, NSA_KV_HEADS, NSA_GROUP * ts, HEAD_DIM), lambda b, s, pt: (b, 0, 0, 0)),
                   pl.BlockSpec((1, steps, ns // steps, LANES), lambda b, s, pt: (b, 0, 0, 0))],
        scratch_shapes=[pltpu.VMEM((steps, rows, KV_ROW), F32), pltpu.VMEM((steps, rows, KV_ROW), F32),
                        pltpu.VMEM((nc, LANES), F32), pltpu.VMEM((PAGE_SIZE, LANES), F32)])
    return pl.pallas_call(
        functools.partial(_cmp_sample_kernel, ts=ts),
        grid_spec=grid_spec,
        out_shape=[jax.ShapeDtypeStruct((nb, NSA_KV_HEADS, NSA_GROUP * ts, HEAD_DIM), F32),
                   jax.ShapeDtypeStruct((nb, steps, ns // steps, LANES), F32)],
        compiler_params=_cparams(("parallel", "arbitrary")),
        name="cmp_sample",
    )(pt, *([cache] * PAGES_PER_STEP), wx, pe, qs, tab)


def _local_softmax(z, mask):
    m = jnp.max(z, axis=0, keepdims=True)
    p = jnp.where(mask, jnp.exp(z - m), 0.0)
    return m, p, jnp.sum(p, axis=0, keepdims=True)


def _selwin_sample_kernel(pt_ref, *refs, ts):
    del pt_ref
    npg = PAGES_PER_STEP
    pages = refs[:npg]
    (qa_ref, sel_ref, cf_ref, nbl_ref, st_ref, kvn_ref, nbn_ref, nbw_ref, g_ref, oc_ref,
     o_ref, m_ref, l_ref, a_ref) = refs[npg:]
    s = pl.program_id(1)
    last = pl.num_programs(1) - 1
    ncol = NSA_KV_HEADS * NSA_GROUP * ts
    qa = qa_ref[0]
    rowi = lax.broadcasted_iota(jnp.int32, (PAGE_SIZE, LANES), 0)
    for k in range(npg):
        pg = jnp.concatenate([_page_half_rows(pages[k], 0), _page_half_rows(pages[k], 1)],
                             axis=1).astype(BF16)
        is_last_page = jnp.logical_and(s == last, k == npg - 1)
        bias = jnp.where(is_last_page, nbl_ref[...], cf_ref[0:1, :])
        s0 = sel_ref[0, s, SEL_PER_PAGE * k:SEL_PER_PAGE * k + 1, :]
        s1 = sel_ref[0, s, SEL_PER_PAGE * k + 1:SEL_PER_PAGE * k + 2, :]
        mask = jnp.where(rowi < SEL_BLOCK, s0, s1) > 0.5
        z = jnp.where(mask, _dot(pg, qa) + bias, NEG)
        m, p, l = _local_softmax(z, mask)
        m_ref[s, k:k + 1, :] = m
        l_ref[s, k:k + 1, :] = l
        a_ref[s, k] = pl.dot(p.astype(BF16), pg, trans_a=True)[0:ncol, :]

    @pl.when(s == last)
    def _():
        nbn = nbn_ref[...]
        okn = nbn > 0.5 * NEG
        kn = kvn_ref[0, 0:SUBLANES, :].astype(BF16)
        m_n, p_n, l_n = _local_softmax(jnp.where(okn, _dot(kn, qa) + nbn, NEG), okn)
        a_n = pl.dot(p_n.astype(BF16), kn, trans_a=True)[0:ncol, :]
        nparts = m_ref.shape[0] * m_ref.shape[1]
        m_all = m_ref[...].reshape(nparts, LANES)
        l_all = l_ref[...].reshape(nparts, LANES)
        m_g = jnp.maximum(jnp.max(m_all, axis=0, keepdims=True), m_n)
        w_all = jnp.exp(m_all - m_g)
        w_n = jnp.exp(m_n - m_g)
        l_g = jnp.sum(w_all * l_all, axis=0, keepdims=True) + w_n * l_n

        nbw = nbw_ref[...]
        okw = nbw > 0.5 * NEG
        st = jnp.concatenate([_page_half_rows(st_ref, 0), _page_half_rows(st_ref, 1)],
                             axis=1).astype(BF16)
        kwn = kvn_ref[0, SUBLANES:2 * SUBLANES, :].astype(BF16)
        zw = jnp.where(okw, _dot(st, qa) + nbw, NEG)
        zwn = jnp.where(okn, _dot(kwn, qa) + nbn, NEG)
        m_w = jnp.maximum(jnp.max(zw, axis=0, keepdims=True), jnp.max(zwn, axis=0, keepdims=True))
        pw = jnp.where(okw, jnp.exp(zw - m_w), 0.0)
        pwn = jnp.where(okn, jnp.exp(zwn - m_w), 0.0)
        l_w = jnp.sum(pw, axis=0, keepdims=True) + jnp.sum(pwn, axis=0, keepdims=True)
        a_w = (pl.dot(pw.astype(BF16), st, trans_a=True) + pl.dot(pwn.astype(BF16), kwn, trans_a=True))[0:ncol, :]

        assert nparts + 4 * SUBLANES <= LANES
        rows8 = lambda r: jnp.broadcast_to(r, (SUBLANES, LANES))
        x = jnp.concatenate([w_all, rows8(w_n), rows8(l_g), rows8(l_w),
                             jnp.zeros((LANES - nparts - 3 * SUBLANES, LANES), F32)], axis=0)
        xt = x.T
        acc = xt[0:ncol, nparts:nparts + 1] * a_n
        for pidx in range(nparts):
            acc = acc + xt[0:ncol, pidx:pidx + 1] * a_ref[pidx // npg, pidx % npg]
        o_s = acc / jnp.maximum(xt[0:ncol, nparts + SUBLANES:nparts + SUBLANES + 1], 1e-30)
        o_w = a_w / jnp.maximum(xt[0:ncol, nparts + 2 * SUBLANES:nparts + 2 * SUBLANES + 1], 1e-30)

        ri = lax.broadcasted_iota(jnp.int32, (ncol, HEAD_DIM), 0)
        v_of = lambda t: jnp.where(ri < ncol // NSA_KV_HEADS, t[:, 2 * HEAD_DIM:3 * HEAD_DIM],
                                   t[:, 3 * HEAD_DIM:4 * HEAD_DIM])
        gt = g_ref[0]
        oc = oc_ref[0].reshape(ncol, HEAD_DIM)
        o_ref[0] = gt[:, 0:1] * oc + gt[:, 1:2] * v_of(o_s) + gt[:, 2:3] * v_of(o_w)


def _selwin_sample(pt, cache, qa, sel, cfrow, nbl, state, st_off, kvn, nbn, nbw, gates, oc, nb, n_pages, ts):
    steps = n_pages // PAGES_PER_STEP
    ncol = NSA_KV_HEADS * NSA_GROUP * ts
    full = lambda a: pl.BlockSpec(a.shape, lambda b, s, pt: (0,) * a.ndim)
    perb = lambda a: pl.BlockSpec((1,) + a.shape[1:], lambda b, s, pt: (b,) + (0,) * (a.ndim - 1))
    grid_spec = pltpu.PrefetchScalarGridSpec(
        num_scalar_prefetch=1,
        grid=(nb, steps),
        in_specs=_page_specs(n_pages) + [
            perb(qa), perb(sel), full(cfrow), full(nbl),
            pl.BlockSpec((1,) + state.shape[1:], lambda b, s, pt: (st_off + b, 0, 0, 0, 0)),
            perb(kvn), full(nbn), full(nbw), perb(gates), perb(oc)],
        out_specs=pl.BlockSpec((1, ncol, HEAD_DIM), lambda b, s, pt: (b, 0, 0)),
        scratch_shapes=[pltpu.VMEM((steps, PAGES_PER_STEP, LANES), F32),
                        pltpu.VMEM((steps, PAGES_PER_STEP, LANES), F32),
                        pltpu.VMEM((steps, PAGES_PER_STEP, ncol, KV_ROW), F32)])
    return pl.pallas_call(
        functools.partial(_selwin_sample_kernel, ts=ts),
        grid_spec=grid_spec,
        out_shape=jax.ShapeDtypeStruct((nb, ncol, HEAD_DIM), F32),
        compiler_params=_cparams(("parallel", "arbitrary")),
        name="selwin_sample",
    )(pt, *([cache] * PAGES_PER_STEP), qa, sel, cfrow, nbl, state, kvn, nbn, nbw, gates, oc)


def _sample_bias_tables(rel_bias, past, ts, n_buf):
    ncol = NSA_KV_HEADS * NSA_GROUP * ts
    col = jnp.arange(ncol)
    head = col // ts
    t = col % ts
    pick = lambda b: jnp.take_along_axis(b, jnp.broadcast_to(head, b.shape[:-1])[..., None], axis=-1)[..., 0]
    bias = lambda d: pick(_head_bias(rel_bias, d))
    padc = lambda a, fill=0.0: jnp.pad(a, ((0, 0), (0, LANES - a.shape[1])), constant_values=fill)
    nc = past // CMP_STRIDE
    c_end = jnp.arange(nc)[:, None] * CMP_STRIDE + CMP_BLOCK - 1
    tab = jnp.where(jnp.arange(nc)[:, None] < nc - 1, bias(past + t[None, :] - c_end), NEG)
    hc = ncol // NSA_KV_HEADS
    tab_cmp = jnp.stack([padc(tab[:, h * hc:(h + 1) * hc]) for h in range(NSA_KV_HEADS)])
    far = rel_bias[N_BUCKETS - 1].astype(F32)[head]
    cfrow = jnp.broadcast_to(padc(far[None, :]), (SUBLANES, LANES))
    kk = jnp.arange(PAGE_SIZE)[:, None]
    nbl = padc(bias(PAGE_SIZE + t[None, :] - kk))
    j = jnp.arange(SUBLANES)[:, None]
    dn = t[None, :] - j
    nbn = padc(jnp.where((dn >= 0) & (j < ts), bias(dn), NEG))
    r = jnp.arange(n_buf)[:, None]
    dw = n_buf + t[None, :] - r
    nbw = padc(jnp.where((dw >= 0) & (dw < WINDOW), bias(dw), NEG))
    return tab_cmp, cfrow, nbl, nbn, nbw


def _sample_layer(x, p, tabs, l, pt, cache_cmp, cache_slc, state_win, state_pool_l, nb, ts, n_pages, final, gf):
    tab_cmp, cfrow, nbl, nbn, nbw = tabs
    past = n_pages * PAGE_SIZE
    R = ts * nb
    ncol = NSA_KV_HEADS * NSA_GROUP * ts
    xp, q, gt, u, v, kvc, kvs, kvw = _inproj(x, p['g_mix'], p['w_in'], p['gn'], R)
    q5 = q.reshape(ts, nb, NSA_KV_HEADS, NSA_GROUP, HEAD_DIM)
    qs = q5.transpose(1, 2, 3, 0, 4).reshape(nb, NSA_KV_HEADS, NSA_GROUP * ts, HEAD_DIM)
    qs = jnp.pad(qs, ((0, 0), (0, 0), (0, LANES - NSA_GROUP * ts), (0, 0)))
    oc, sel = _cmp_sample(pt, cache_cmp, p['wx'], p['pe'], qs, tab_cmp, nb, n_pages, ts)
    qt = q5.transpose(1, 2, 4, 3, 0).reshape(nb, NSA_KV_HEADS, HEAD_DIM, NSA_GROUP * ts)
    eye = jnp.eye(NSA_KV_HEADS, dtype=qt.dtype)
    qa = jnp.einsum('bhdc,hk->bhdkc', qt, eye).reshape(nb, NSA_KV_HEADS * HEAD_DIM, ncol)
    qa = jnp.pad(qa, ((0, 0), (0, KV_ROW - NSA_KV_HEADS * HEAD_DIM), (0, LANES - ncol)))
    rows_of = lambda a: jnp.pad(a.reshape(ts, nb, KV_ROW).transpose(1, 0, 2), ((0, 0), (0, SUBLANES - ts), (0, 0)))
    kvn = jnp.concatenate([rows_of(kvs), rows_of(kvw)], axis=1)
    ng = NSA_GROUP * N_BRANCHES
    g5 = jnp.concatenate([gt[:, h * LANES:h * LANES + ng] for h in range(NSA_KV_HEADS)], axis=1)
    g5 = g5.reshape(ts, nb, NSA_KV_HEADS * NSA_GROUP, N_BRANCHES)
    gates = jnp.pad(g5.transpose(1, 2, 0, 3).reshape(nb, ncol, N_BRANCHES), ((0, 0), (0, 0), (0, LANES - N_BRANCHES)))
    o = _selwin_sample(pt, cache_slc, qa, sel, cfrow, nbl, state_win, l * nb, kvn, nbn, nbw, gates, oc,
                       nb, n_pages, ts)
    yb = o.reshape(nb, NSA_KV_HEADS * NSA_GROUP, ts, HEAD_DIM).transpose(2, 0, 1, 3).reshape(R, NSA_WIDTH)
    ext = jnp.concatenate([state_pool_l.transpose(1, 0, 2), xp.reshape(ts, nb, POOL_WIDTH)], axis=0)
    x1 = _mix_sample(x, ext, yb.astype(BF16), u.reshape(ts, nb, GMLP_WIDTH), v.reshape(ts, nb, GMLP_WIDTH),
                     p['wp'], p['ps'], p['wsx'], p['wb'][:SUBLANES], p['w_out'], nb, ts, past)
    x2 = _ffn(x1, p['g_ffn'], p['w_up'], p['w_down'], gf, R, final)
    return x2, (xp, kvc, kvs, kvw, v)


def kernel(x_prompt, x_sample, cache_cmp_kv, cache_slc_kv, state_win_kv, state_pool, page_table, w_in, w_out,
           norm_mix, norm_ffn, norm_final, w_pool, pool_scale, cmp_w, cmp_pe, gmlp_ws, gmlp_b, gmlp_norm, w_up,
           w_down, rel_bias):
    B, T, _ = x_prompt.shape
    nb, ts, _ = x_sample.shape
    n_pages = page_table.shape[1]
    n_phys = cache_cmp_kv.shape[1]
    n_buf = state_win_kv.shape[2]
    past = n_pages * PAGE_SIZE
    depth = w_in.shape[0]
    assert T % TK_FAR == 0 and T >= WINDOW and n_pages % PAGES_PER_STEP == 0
    assert ts <= SUBLANES and ts <= POOL_STATE and n_buf == WINDOW and past >= WINDOW
    kv_tail = (2, NSA_KV_HEADS, HEAD_DIM)

    cache_cmp = _native_rows(cache_cmp_kv, depth * n_phys)
    cache_slc = _native_rows(cache_slc_kv, depth * n_phys)
    state_win = _native_rows(state_win_kv, depth * nb)
    ptabs = _prompt_bias_tables(rel_bias)
    stabs = _sample_bias_tables(rel_bias, past, ts, n_buf)
    gf = norm_final[None, :]
    xp = x_prompt.reshape(B * T, D_MODEL)
    xs = x_sample.transpose(1, 0, 2).reshape(ts * nb, D_MODEL)
    unmajor = lambda a: a.reshape(ts, nb, a.shape[-1]).transpose(1, 0, 2)

    outs = [[] for _ in range(9)]
    for l in range(depth):
        p = _prep_layer(l, w_in, w_out, norm_mix, norm_ffn, w_pool, pool_scale, cmp_w, cmp_pe, gmlp_ws, gmlp_b,
                        gmlp_norm, w_up, w_down)
        final = l == depth - 1
        xp, (pin, kvc_t, kvs_t, kvw_t) = _prompt_layer(xp, p, ptabs, B, T, final, gf)
        rows_of = lambda a: a.reshape((B,) + kv_tail + (a.shape[-1],)).transpose(0, 4, 1, 2, 3)
        outs[0].append(rows_of(kvc_t))
        outs[1].append(rows_of(kvs_t))
        outs[2].append(rows_of(kvw_t[:, :, T - WINDOW:]))
        outs[3].append(pin.reshape(B, T, POOL_WIDTH)[:, T - POOL_STATE:])

        pt = (page_table + l * n_phys).reshape(-1).astype(jnp.int32)
        xs, (sin, kvc_s, kvs_s, kvw_s, v_s) = _sample_layer(
            xs, p, stabs, l, pt, cache_cmp, cache_slc, state_win, state_pool[l], nb, ts, n_pages, final, gf)
        kvw_new = unmajor(kvw_s).reshape((nb, ts) + kv_tail)
        outs[4].append(unmajor(kvc_s).reshape((nb, ts) + kv_tail))
        outs[5].append(unmajor(kvs_s).reshape((nb, ts) + kv_tail))
        outs[6].append(jnp.concatenate([state_win_kv[l][:, ts:], kvw_new], axis=1))
        outs[7].append(jnp.concatenate([state_pool[l][:, ts:], unmajor(sin)], axis=1))
        outs[8].append(unmajor(v_s))

    y_prompt = xp.reshape(B, T, D_MODEL)
    y_sample = unmajor(xs)
    return (y_prompt, y_sample) + tuple(jnp.stack(o) for o in outs)
```

```python
import functools
import math

import numpy as np
import jax
import jax.numpy as jnp
from jax import lax
from jax.experimental import pallas as pl
from jax.experimental.pallas import tpu as pltpu

F32 = jnp.float32
BF16 = jnp.bfloat16

D_MODEL = 1024
DEPTH = 4
PAGE_SIZE = 128
HEAD_DIM = 64
POOL_WINDOWS = (2, 4, 8, 16)
POOL_WIDTH = 256
POOL_GROUP_DIM = 64
POOL_STATE = 15
NSA_WIDTH = 512
NSA_HEADS = 8
NSA_KV_HEADS = 2
NSA_GROUP = 4
KV_ROW = 2 * NSA_KV_HEADS * HEAD_DIM
CMP_STRIDE = 16
CMP_BLOCK = 32
SEL_BLOCK = 64
N_SELECT = 16
WINDOW = 512
N_BRANCHES = 3
FORCE_SCORE = 1000.0
GMLP_WIDTH = 256
GMLP_GROUPS = 4
GMLP_CHUNK = 128
D_FF = 4096
N_BUCKETS = 32
MAX_DISTANCE = 128
EPS = 1e-6
SCALE = HEAD_DIM ** -0.5

LANES = 128
SUBLANES = 8
VMEM_LIMIT = 56 * 1024 * 1024

NEG = -1e30
CMP_PAD = 16
TQ = 128
TQC = 256
CMP_NEAR = CMP_PAD + TQC // CMP_STRIDE
TK_FAR = 512
KPAD = 512
GATE_LANES = 2 * LANES

PROJ_SEGS = (('xp', 256), ('q', 512), ('kvc', 256), ('kvs', 256), ('kvw', 256), ('gt', 24), ('u', 256), ('v', 256))


def _cparams(sem):
    return pltpu.CompilerParams(dimension_semantics=sem, vmem_limit_bytes=VMEM_LIMIT)


def _dot(a, b):
    return jnp.dot(a, b, preferred_element_type=F32)


def _dot_nt(a, b):
    return lax.dot_general(a, b, (((1,), (1,)), ((), ())), preferred_element_type=F32)


def _gelu(x):
    c = math.sqrt(2.0 / math.pi)
    return 0.5 * x * (1.0 + jnp.tanh(c * (x + 0.044715 * (x * x * x))))


def _rms(x, g):
    return x * lax.rsqrt(jnp.mean(x * x, axis=-1, keepdims=True) + EPS) * g


def _inproj_kernel(x_ref, g_ref, w_ref, gn_ref, *refs, channel_major):
    if channel_major:
        wt_ref, xp_ref, q_ref, gt_ref, u_ref, v_ref, kvc_ref, *kv_refs = refs
    else:
        xp_ref, q_ref, gt_ref, u_ref, v_ref, kvc_ref, *kv_refs = refs
    x = x_ref[...]
    h = _rms(x, g_ref[...]).astype(BF16)

    def seg(lo, hi):
        return _dot(h, w_ref[:, lo:hi])

    xp_ref[...] = seg(0, 256)
    q_ref[...] = (seg(256, 768) * SCALE).astype(BF16)
    kvc_ref[...] = seg(768, 1024)
    if channel_major:
        for k, t_ref in enumerate(kv_refs):
            t_ref[0] = _dot_nt(wt_ref[k * KV_ROW:(k + 1) * KV_ROW, :], h)
    else:
        kv_refs[0][...] = seg(1024, 1280)
        kv_refs[1][...] = seg(1280, 1536)
    u_ref[...] = _gelu(seg(1536, 1792))
    gv = _gelu(seg(1792, 2048))
    sq = gv * gv
    lane = lax.broadcasted_iota(jnp.int32, sq.shape, 1)
    ms = jnp.zeros_like(sq)
    for g in range(GMLP_GROUPS):
        in_g = (lane >= g * 64) & (lane < (g + 1) * 64)
        s = jnp.sum(jnp.where(in_g, sq, 0.0), axis=-1, keepdims=True) * (1.0 / 64.0)
        ms = jnp.where(in_g, s, ms)
    v_ref[...] = gv * lax.rsqrt(ms + EPS) * gn_ref[...]
    gt_ref[...] = jax.nn.sigmoid(seg(2048, 2048 + GATE_LANES))


def _inproj(x, g, w, gn, tm, seq=None, wt=None):
    R = x.shape[0]
    f = lambda n, dt=F32: jax.ShapeDtypeStruct((R, n), dt)
    row = lambda n: pl.BlockSpec((tm, n), lambda i: (i, 0))
    full = lambda a: pl.BlockSpec(a.shape, lambda i: (0,) * a.ndim)
    out_specs = [row(256), row(512), row(GATE_LANES), row(256), row(256), row(256)]
    out_shape = [f(256), f(512, BF16), f(GATE_LANES), f(256), f(256), f(256)]
    if seq is None:
        out_specs += [row(256), row(256)]
        out_shape += [f(256), f(256)]
    else:
        per_seq = seq // tm
        out_specs += [pl.BlockSpec((1, KV_ROW, tm), lambda i: (i // per_seq, 0, i % per_seq))] * 3
        out_shape += [jax.ShapeDtypeStruct((R // seq, KV_ROW, seq), F32)] * 3
    args = (x, g, w, gn) if seq is None else (x, g, w, gn, wt)
    return pl.pallas_call(
        functools.partial(_inproj_kernel, channel_major=seq is not None),
        grid=(R // tm,),
        in_specs=[row(D_MODEL)] + [full(a) for a in args[1:]],
        out_specs=out_specs,
        out_shape=out_shape,
        compiler_params=_cparams(("parallel",)),
        name="inproj",
    )(*args)


def _out_proj(x, ya, yb, yc, wo_ref):
    acc = _dot(ya.astype(BF16), wo_ref[0:256, :])
    acc = acc + _dot(yb, wo_ref[256:768, :])
    acc = acc + _dot(yc.astype(BF16), wo_ref[768:1024, :])
    return x + acc


def _pool_tail(win2, win4, win8, win16, cur, cnt, wp_ref, ps_ref):
    lane = lax.broadcasted_iota(jnp.int32, cur.shape, 1)
    win = jnp.where(lane < 64, win2, jnp.where(lane < 128, win4, jnp.where(lane < 192, win8, win16)))
    pooled = win / cnt - cur
    return _dot(pooled.astype(BF16), wp_ref[...]) * ps_ref[...]


def _mix_prompt_kernel(x_ref, xp_ref, halo_ref, yb_ref, u_ref, v_ref, wp_ref, ps_ref, ws_ref, wb_ref, wo_ref,
                       o_ref, ext_ref, *, tm, seq):
    i = pl.program_id(0)
    t0 = (i * tm) % seq
    halo = jnp.where(t0 > 0, halo_ref[...], 0.0)
    cur = xp_ref[...]
    ext_ref[0:16, :] = halo
    ext_ref[16:16 + tm, :] = cur
    e = ext_ref[...]
    b2 = e[1:] + e[:-1]
    b4 = b2[2:] + b2[:-2]
    b8 = b4[4:] + b4[:-4]
    b16 = b8[8:] + b8[:-8]
    win2 = b2[15:15 + tm]
    win4 = b4[13:13 + tm]
    win8 = b8[9:9 + tm]
    win16 = b16[1:1 + tm]
    rowi = lax.broadcasted_iota(jnp.int32, cur.shape, 0)
    lane = lax.broadcasted_iota(jnp.int32, cur.shape, 1)
    wsz = jnp.where(lane < 64, 2, jnp.where(lane < 128, 4, jnp.where(lane < 192, 8, 16)))
    cnt = jnp.minimum(t0 + rowi + 1, wsz).astype(F32)
    ya = _pool_tail(win2, win4, win8, win16, cur, cnt, wp_ref, ps_ref)

    ci = lax.broadcasted_iota(jnp.int32, (GMLP_CHUNK, GMLP_CHUNK), 0)
    cj = lax.broadcasted_iota(jnp.int32, (GMLP_CHUNK, GMLP_CHUNK), 1)
    lane_c = lax.broadcasted_iota(jnp.int32, (GMLP_CHUNK, GMLP_WIDTH), 1)
    wts = [jnp.where(ci >= cj, ws_ref[g], 0.0).astype(BF16) for g in range(GMLP_GROUPS)]
    parts = []
    for c in range(tm // GMLP_CHUNK):
        vc = v_ref[c * GMLP_CHUNK:(c + 1) * GMLP_CHUNK, :].astype(BF16)
        s = jnp.zeros((GMLP_CHUNK, GMLP_WIDTH), F32)
        for g in range(GMLP_GROUPS):
            sg = _dot(wts[g], vc)
            s = jnp.where((lane_c >= g * 64) & (lane_c < (g + 1) * 64), sg, s)
        parts.append(u_ref[c * GMLP_CHUNK:(c + 1) * GMLP_CHUNK, :] * (s + wb_ref[...]))
    yc = jnp.concatenate(parts, axis=0) if len(parts) > 1 else parts[0]

    o_ref[...] = _out_proj(x_ref[...], ya, yb_ref[...], yc, wo_ref)


def _mix_prompt(x, xp, yb, u, v, wp, ps, ws, wb, wo, tm, seq):
    R = x.shape[0]
    row = lambda n: pl.BlockSpec((tm, n), lambda i: (i, 0))
    full = lambda a: pl.BlockSpec(a.shape, lambda i: (0,) * a.ndim)
    halo = pl.BlockSpec((16, 256), lambda i: (jnp.maximum(i * (tm // 16) - 1, 0), 0))
    return pl.pallas_call(
        functools.partial(_mix_prompt_kernel, tm=tm, seq=seq),
        grid=(R // tm,),
        in_specs=[row(D_MODEL), row(256), halo, row(512), row(256), row(256),
                  full(wp), full(ps), full(ws), full(wb), full(wo)],
        out_specs=row(D_MODEL),
        out_shape=jax.ShapeDtypeStruct((R, D_MODEL), F32),
        scratch_shapes=[pltpu.VMEM((tm + 16, 256), F32)],
        compiler_params=_cparams(("parallel",)),
        name="mix_prompt",
    )(x, xp, xp, yb, u, v, wp, ps, ws, wb, wo)


def _mix_sample_kernel(x_ref, ext_ref, yb_ref, u_ref, v_ref, wp_ref, ps_ref, wsx_ref, wb_ref, wo_ref, o_ref,
                       *, nb, ts, past):
    lane = lax.broadcasted_iota(jnp.int32, (nb, POOL_WIDTH), 1)
    wsz = jnp.where(lane < 64, 2, jnp.where(lane < 128, 4, jnp.where(lane < 192, 8, 16)))
    yas, ycs = [], []
    for t in range(ts):
        top = POOL_STATE + t
        acc = ext_ref[top] + ext_ref[top - 1]
        wins = [acc]
        for w in (4, 8, 16):
            for j in range(w // 2, w):
                acc = acc + ext_ref[top - j]
            wins.append(acc)
        cnt = jnp.minimum(past + t + 1, wsz).astype(F32)
        yas.append(_pool_tail(wins[0], wins[1], wins[2], wins[3], ext_ref[top], cnt, wp_ref, ps_ref))
        s = wb_ref[t:t + 1, :]
        s = jnp.broadcast_to(s, (nb, GMLP_WIDTH))
        for j in range(t + 1):
            s = s + wsx_ref[t, j:j + 1, :] * v_ref[j]
        ycs.append(u_ref[t] * s)
    ya = jnp.concatenate(yas, axis=0)
    yc = jnp.concatenate(ycs, axis=0)
    o_ref[...] = _out_proj(x_ref[...], ya, yb_ref[...], yc, wo_ref)


def _mix_sample(x, ext, yb, u, v, wp, ps, wsx, wb4, wo, nb, ts, past):
    R = x.shape[0]
    args = (x, ext, yb, u, v, wp, ps, wsx, wb4, wo)
    full = lambda a: pl.BlockSpec(a.shape, lambda i: (0,) * a.ndim)
    return pl.pallas_call(
        functools.partial(_mix_sample_kernel, nb=nb, ts=ts, past=past),
        grid=(1,),
        in_specs=[full(a) for a in args],
        out_specs=pl.BlockSpec((R, D_MODEL), lambda i: (0, 0)),
        out_shape=jax.ShapeDtypeStruct((R, D_MODEL), F32),
        compiler_params=_cparams(("arbitrary",)),
        name="mix_sample",
    )(*args)


def _ffn_kernel(x_ref, g_ref, wu_ref, wd_ref, gf_ref, o_ref, *, final, fc):
    x = x_ref[...]
    h = _rms(x, g_ref[...]).astype(BF16)
    acc = x
    for c in range(D_FF // fc):
        a = jnp.maximum(_dot(h, wu_ref[:, c * fc:(c + 1) * fc]), 0.0)
        acc = acc + _dot((a * a).astype(BF16), wd_ref[c * fc:(c + 1) * fc, :])
    if final:
        acc = _rms(acc, gf_ref[...])
    o_ref[...] = acc


def _ffn(x, g, wu, wd, gf, tm, final):
    R = x.shape[0]
    row = pl.BlockSpec((tm, D_MODEL), lambda i: (i, 0))
    full = lambda a: pl.BlockSpec(a.shape, lambda i: (0,) * a.ndim)
    wfull = lambda a: pl.BlockSpec(a.shape, lambda i: (0,) * a.ndim, pipeline_mode=pl.Buffered(1))
    return pl.pallas_call(
        functools.partial(_ffn_kernel, final=final, fc=1024),
        grid=(R // tm,),
        in_specs=[row, full(g), wfull(wu), wfull(wd), full(gf)],
        out_specs=row,
        out_shape=jax.ShapeDtypeStruct((R, D_MODEL), F32),
        compiler_params=_cparams(("parallel",)),
        name="ffn",
    )(x, g, wu, wd, gf)


def _compress_rows(read_rows, n, wx):
    lo = jnp.zeros((n, wx.shape[1]), F32)
    hi = jnp.zeros((n, wx.shape[1]), F32)
    for j in range(CMP_STRIDE):
        xj = read_rows(j)
        lo = lo + xj * wx[j:j + 1, :]
        hi = hi + xj * wx[CMP_STRIDE + j:CMP_STRIDE + j + 1, :]
    return lo, hi


def _combine_halves(lo, hi, pe, wx):
    n = lo.shape[0]
    pe_term = jnp.sum(pe * wx, axis=0, keepdims=True)
    comp = lo + pltpu.roll(hi, n - 1, 0) + pe_term
    rowi = lax.broadcasted_iota(jnp.int32, comp.shape, 0)
    return jnp.where(rowi < n - 1, comp, 0.0)


def _compress_prompt_kernel(x_ref, wx_ref, pe_ref, o_ref, *, n):
    wx = wx_ref[...]
    lo, hi = _compress_rows(lambda j: x_ref[0, pl.ds(j, n, stride=CMP_STRIDE), :], n, wx)
    o_ref[0, 0:CMP_PAD, :] = jnp.zeros((CMP_PAD, LANES), F32)
    o_ref[0, CMP_PAD:CMP_PAD + n, :] = _combine_halves(lo, hi, pe_ref[...], wx)


def _compress_prompt(kvc, wx, pe):
    B, T, _ = kvc.shape
    n = T // CMP_STRIDE
    half = lambda rows: pl.BlockSpec((rows, LANES), lambda b, e: (0, e))
    return pl.pallas_call(
        functools.partial(_compress_prompt_kernel, n=n),
        grid=(B, KV_ROW // LANES),
        in_specs=[pl.BlockSpec((1, T, LANES), lambda b, e: (b, 0, e)), half(CMP_BLOCK), half(CMP_BLOCK)],
        out_specs=pl.BlockSpec((1, n + CMP_PAD, LANES), lambda b, e: (b, 0, e)),
        out_shape=jax.ShapeDtypeStruct((B, n + CMP_PAD, KV_ROW), F32),
        compiler_params=_cparams(("parallel", "parallel")),
        name="compress_prompt",
    )(kvc, wx, pe)


def _select_topk(score, n_pick):
    s_iota = lax.broadcasted_iota(jnp.int32, score.shape, 0)
    big = score.shape[0]
    sel = jnp.zeros(score.shape, F32)
    for _ in range(n_pick):
        m = jnp.max(score, axis=0, keepdims=True)
        idx = jnp.min(jnp.where(score == m, s_iota, big), axis=0, keepdims=True)
        pick = s_iota == idx
        sel = jnp.where(pick & (m >= 0.0), 1.0, sel)
        score = jnp.where(pick, -2.0, score)
    return sel


def _softmax_rows(z):
    m = jnp.max(z, axis=0, keepdims=True)
    p = jnp.where(z > 0.5 * NEG, jnp.exp(z - m), 0.0)
    den = jnp.maximum(jnp.sum(p, axis=0, keepdims=True), 1e-30)
    return p * (1.0 / den)


def _cmp_prompt_kernel(q_ref, kc_ref, vct_ref, nb_ref, cf_ref, oc_ref, sn_ref, raw_ref, z_ref, ps_ref, *, ncp):
    i = pl.program_id(2)
    hh = pl.program_id(1)
    tq = q_ref.shape[1]
    r0 = pl.multiple_of(i * (tq // CMP_STRIDE), SUBLANES)
    q = q_ref[0]
    kc = kc_ref[0, 0]
    vct = vct_ref[0, 0]
    rowi = lax.broadcasted_iota(jnp.int32, (ncp, tq), 0)
    far = (rowi >= CMP_PAD) & (rowi < r0)
    rown = lax.broadcasted_iota(jnp.int32, (CMP_NEAR, tq), 0) + r0
    psum = jnp.zeros((ncp, tq), F32)
    octs = []
    for g in range(NSA_GROUP):
        qg = q[:, g * HEAD_DIM:(g + 1) * HEAD_DIM]
        raw_ref[...] = _dot_nt(kc, qg)
        cf = cf_ref[hh * NSA_GROUP + g]
        z_ref[...] = jnp.where(far, raw_ref[...] + cf[0:1, :], NEG)
        nb = nb_ref[hh * NSA_GROUP + g]
        zn = raw_ref[pl.ds(r0, CMP_NEAR), :] + nb
        z_ref[pl.ds(r0, CMP_NEAR), :] = jnp.where((rown >= CMP_PAD) & (nb > 0.5 * NEG), zn, NEG)
        pc = _softmax_rows(z_ref[...])
        psum = psum + pc
        octs.append(_dot(vct, pc.astype(BF16)))
    oc_ref[0] = jnp.concatenate(octs, axis=0).T

    ns = LANES
    halves = []
    for c in range(tq // LANES):
        ps_ref[c, 0:ncp, :] = psum[:, c * LANES:(c + 1) * LANES]
        if ncp < ps_ref.shape[1]:
            ps_ref[c, ncp:, :] = jnp.zeros((ps_ref.shape[1] - ncp, LANES), F32)
        part = ps_ref[c, pl.ds(CMP_PAD, ns, stride=4), :]
        for j in range(1, 4):
            part = part + ps_ref[c, pl.ds(CMP_PAD + j, ns, stride=4), :]
        halves.append(part)
    p_slc = jnp.concatenate(halves, axis=1)
    s_iota = lax.broadcasted_iota(jnp.int32, (ns, tq), 0)
    tt = lax.broadcasted_iota(jnp.int32, (ns, tq), 1)
    cur = (i * tq + tt) // SEL_BLOCK
    forced = (s_iota == 0) | (s_iota == cur) | (s_iota == cur - 1)
    score = jnp.where(s_iota <= cur, jnp.where(forced, FORCE_SCORE, p_slc), -1.0)
    sel = _select_topk(score, N_SELECT)
    sn_ref[0, 0] = jnp.where(sel.T > 0.5, 0.0, NEG).astype(BF16)


def _cmp_prompt(q, kc, vct, nbc, cfar):
    B, T, _ = q.shape
    ncp = kc.shape[2]
    ns = LANES
    nq = T // TQC
    full = lambda a: pl.BlockSpec(a.shape, lambda b, h, i: (0,) * a.ndim)
    return pl.pallas_call(
        functools.partial(_cmp_prompt_kernel, ncp=ncp),
        grid=(B, NSA_KV_HEADS, nq),
        in_specs=[pl.BlockSpec((1, TQC, 256), lambda b, h, i: (b, i, h)),
                  pl.BlockSpec((1, 1, ncp, HEAD_DIM), lambda b, h, i: (b, h, 0, 0)),
                  pl.BlockSpec((1, 1, HEAD_DIM, ncp), lambda b, h, i: (b, h, 0, 0)),
                  full(nbc), full(cfar)],
        out_specs=[pl.BlockSpec((1, TQC, 256), lambda b, h, i: (b, i, h)),
                   pl.BlockSpec((1, 1, TQC, ns), lambda b, h, i: (b, h, i, 0))],
        out_shape=[jax.ShapeDtypeStruct((B, T, NSA_WIDTH), F32),
                   jax.ShapeDtypeStruct((B, NSA_KV_HEADS, T, ns), BF16)],
        scratch_shapes=[pltpu.VMEM((ncp, TQC), F32), pltpu.VMEM((ncp, TQC), F32),
                        pltpu.VMEM((TQC // LANES, max(ncp, CMP_PAD + 4 * LANES), LANES), F32)],
        compiler_params=_cparams(("parallel", "parallel", "parallel")),
        name="cmp_prompt",
    )(q, kc, vct, nbc, cfar)


def _selwin_prompt_kernel(q_ref, sn_ref, ks_ref, vs_ref, kw_ref, vw_ref, oh_ref, g_ref, oc_ref, nbn_ref, nbw_ref,
                          o_ref, ka_ref, va_ref, kwa_ref, vwa_ref, ms_ref, as_ref, za_ref, zb_ref):
    i = pl.program_id(2)
    R4 = NSA_GROUP * TQ
    ncols = ka_ref.shape[1]

    @pl.when(i == 0)
    def _():
        zpad = jnp.zeros((HEAD_DIM, KPAD), BF16)
        ones = jnp.ones((HEAD_DIM, ncols), BF16)
        row = lax.broadcasted_iota(jnp.int32, (HEAD_DIM, ncols), 0)
        col = lax.broadcasted_iota(jnp.int32, (HEAD_DIM, ncols), 1)
        ka_ref[0:LANES, :] = oh_ref[...]
        kwa_ref[HEAD_DIM:, :] = jnp.where((row == 0) & (col < KPAD), NEG, 0.0).astype(BF16)
        for dst, src in ((ka_ref.at[LANES:], ks_ref), (va_ref.at[0:HEAD_DIM], vs_ref),
                         (kwa_ref.at[0:HEAD_DIM], kw_ref), (vwa_ref.at[0:HEAD_DIM], vw_ref)):
            dst[:, 0:KPAD] = zpad
            dst[:, KPAD:] = src[0].astype(BF16)
        va_ref[HEAD_DIM:, :] = ones
        vwa_ref[HEAD_DIM:, :] = ones

    q = q_ref[0]
    q4 = jnp.concatenate([q[:, g * HEAD_DIM:(g + 1) * HEAD_DIM] for g in range(NSA_GROUP)], axis=0)
    sn = sn_ref[0, 0]
    blk = lax.broadcasted_iota(jnp.int32, (TQ, LANES), 1)
    first_near = (i - 1) * (TQ // SEL_BLOCK)
    sn_far = jnp.where(blk >= first_near, NEG, sn.astype(F32)).astype(BF16)
    qa_near = jnp.concatenate([jnp.concatenate([sn] * NSA_GROUP, axis=0), q4], axis=1)
    qa_far = jnp.concatenate([jnp.concatenate([sn_far] * NSA_GROUP, axis=0), q4], axis=1)
    qw = jnp.concatenate([q4, jnp.ones((R4, HEAD_DIM), BF16)], axis=1)

    def one_pass(z, vt):
        m = jnp.max(z, axis=1, keepdims=True)
        acc = _dot_nt(jnp.exp(z - m).astype(BF16), vt)
        return m, acc

    kw0 = pl.multiple_of(i * TQ, TQ)
    zw = _dot(qw, kwa_ref[:, pl.ds(kw0, WINDOW + TQ)]) + nbw_ref[0]
    _, a_w = one_pass(zw, vwa_ref[:, pl.ds(kw0, WINDOW + TQ)])

    kn0 = pl.multiple_of((i - 1) * TQ + KPAD, TQ)
    zn = _dot(qa_near, ka_ref[:, pl.ds(kn0, 2 * TQ)]) + nbn_ref[0]
    m0, a0 = one_pass(zn, va_ref[:, pl.ds(kn0, 2 * TQ)])
    ms_ref[...] = jnp.broadcast_to(m0, (R4, LANES))
    as_ref[...] = a0

    def far_logits(j):
        k0 = pl.multiple_of(j * TK_FAR + KPAD, TK_FAR)
        return _dot(qa_far, ka_ref[:, pl.ds(k0, TK_FAR)])

    def far_update(j, z):
        k0 = pl.multiple_of(j * TK_FAR + KPAD, TK_FAR)
        m_prev = ms_ref[...]
        m_new = jnp.maximum(m_prev, jnp.max(z, axis=1, keepdims=True))
        p = jnp.exp(z - jnp.concatenate([m_new] * (TK_FAR // LANES), axis=1))
        as_ref[...] = jnp.exp(m_prev - m_new) * as_ref[...] + _dot_nt(p.astype(BF16), va_ref[:, pl.ds(k0, TK_FAR)])
        ms_ref[...] = m_new

    n_far = (jnp.maximum(i - 1, 0) * TQ + TK_FAR - 1) // TK_FAR
    odd = n_far % 2
    last = n_far - 1

    @pl.when(n_far > 0)
    def _():
        za_ref[...] = far_logits(-odd)

        def far_body(jj, carry):
            t = 2 * jj - odd
            zb_ref[...] = far_logits(t + 1)
            far_update(t, za_ref[...])
            za_ref[...] = far_logits(jnp.minimum(t + 2, last))
            far_update(t + 1, zb_ref[...])
            return carry

        lax.fori_loop(0, (n_far + 1) // 2, far_body, 0)

    a_s = as_ref[...]
    o_s = a_s[:, 0:HEAD_DIM] / jnp.maximum(a_s[:, HEAD_DIM:2 * HEAD_DIM], 1e-30)
    o_w = a_w[:, 0:HEAD_DIM] / jnp.maximum(a_w[:, HEAD_DIM:2 * HEAD_DIM], 1e-30)
    gt = g_ref[...]
    oc = oc_ref[0]
    outs = []
    for g in range(NSA_GROUP):
        sl = slice(g * TQ, (g + 1) * TQ)
        o = (gt[:, 3 * g:3 * g + 1] * oc[:, g * HEAD_DIM:(g + 1) * HEAD_DIM]
             + gt[:, 3 * g + 1:3 * g + 2] * o_s[sl] + gt[:, 3 * g + 2:3 * g + 3] * o_w[sl])
        outs.append(o)
    o_ref[0] = jnp.concatenate(outs, axis=1).astype(BF16)


def _selwin_prompt(q, sn, kvs_t, kvw_t, onehot_t, gates, oc, nbn, nbw):
    B, T, _ = q.shape
    nq = T // TQ
    R4 = NSA_GROUP * TQ
    ncols = T + KPAD
    k_of = pl.BlockSpec((1, HEAD_DIM, T), lambda b, h, i: (b, h, 0))
    v_of = pl.BlockSpec((1, HEAD_DIM, T), lambda b, h, i: (b, NSA_KV_HEADS + h, 0))
    perh = lambda a: pl.BlockSpec((1,) + a.shape[1:], lambda b, h, i: (h, 0, 0))
    qtile = pl.BlockSpec((1, TQ, 256), lambda b, h, i: (b, i, h))
    return pl.pallas_call(
        _selwin_prompt_kernel,
        grid=(B, NSA_KV_HEADS, nq),
        in_specs=[qtile, pl.BlockSpec((1, 1, TQ, sn.shape[-1]), lambda b, h, i: (b, h, i, 0)),
                  k_of, v_of, k_of, v_of, pl.BlockSpec(onehot_t.shape, lambda b, h, i: (0, 0)),
                  pl.BlockSpec((TQ, LANES), lambda b, h, i: (b * nq + i, h)), qtile, perh(nbn), perh(nbw)],
        out_specs=qtile,
        out_shape=jax.ShapeDtypeStruct((B, T, NSA_WIDTH), BF16),
        scratch_shapes=[pltpu.VMEM((LANES + HEAD_DIM, ncols), BF16)] + [pltpu.VMEM((LANES, ncols), BF16)] * 3
        + [pltpu.VMEM((R4, LANES), F32)] * 2 + [pltpu.VMEM((R4, TK_FAR), F32)] * 2,
        compiler_params=_cparams(("parallel", "parallel", "arbitrary")),
        name="selwin_prompt",
    )(q, sn, kvs_t, kvs_t, kvw_t, kvw_t, onehot_t, gates, oc, nbn, nbw)


def _rel_bucket(dist):
    d = jnp.maximum(dist, 0)
    n_exact = N_BUCKETS // 2
    d_f = jnp.maximum(d, 1).astype(F32)
    large = n_exact + (jnp.log(d_f / n_exact) / math.log(MAX_DISTANCE / n_exact)
                       * (N_BUCKETS - n_exact)).astype(jnp.int32)
    large = jnp.minimum(large, N_BUCKETS - 1)
    return jnp.where(d < n_exact, d, large)


def _head_bias(rel_bias, dist):
    onehot = (_rel_bucket(dist)[..., None] == jnp.arange(N_BUCKETS)).astype(F32)
    return jnp.einsum('...k,kh->...h', onehot, rel_bias.astype(F32), precision=lax.Precision.HIGHEST)


def _prompt_bias_tables(rel_bias):
    far = rel_bias[N_BUCKETS - 1].astype(F32)
    cfar = jnp.broadcast_to(far[:, None, None], (NSA_HEADS, SUBLANES, TQC))
    k = jnp.arange(CMP_NEAR)[:, None]
    tt = jnp.arange(TQC)[None, :]
    d = tt + CMP_PAD * CMP_STRIDE - (CMP_BLOCK - 1) - CMP_STRIDE * k
    nbc = jnp.where((d >= 0)[..., None], _head_bias(rel_bias, d), NEG).transpose(2, 0, 1)
    tq = jnp.arange(TQ)[:, None]

    def tile_table(first_dist, n_keys, max_dist):
        d = first_dist + tq - jnp.arange(n_keys)[None, :]
        b = jnp.where(((d >= 0) & (d < max_dist))[..., None], _head_bias(rel_bias, d) - far, NEG)
        return b.transpose(2, 0, 1).reshape(NSA_KV_HEADS, NSA_GROUP * TQ, n_keys)

    nbn = tile_table(TQ, 2 * TQ, 2 * TQ + 1)
    nbw = tile_table(WINDOW, WINDOW + TQ, WINDOW)
    return cfar, nbc, nbn, nbw


def _prep_layer(l, w_in, w_out, norm_mix, norm_ffn, w_pool, pool_scale, cmp_w, cmp_pe, gmlp_ws, gmlp_b,
                gmlp_norm, w_up, w_down):
    w = w_in[l]
    offs = np.cumsum([0] + [n for _, n in PROJ_SEGS])
    seg = {name: w[:, offs[k]:offs[k + 1]] for k, (name, _) in enumerate(PROJ_SEGS)}
    order = ('xp', 'q', 'kvc', 'kvs', 'kvw', 'u', 'v')
    ng = NSA_GROUP * N_BRANCHES
    gpad = jnp.zeros((D_MODEL, LANES - ng), w.dtype)
    gate_cols = [c for h in range(NSA_KV_HEADS) for c in (seg['gt'][:, h * ng:(h + 1) * ng], gpad)]
    w_perm = jnp.concatenate([seg[n] for n in order] + gate_cols, axis=1)
    eye = jnp.eye(len(POOL_WINDOWS), dtype=F32)
    wp = jnp.einsum('gcd,gh->gchd', w_pool[l], eye).reshape(POOL_WIDTH, POOL_WIDTH)
    wx = jnp.broadcast_to(cmp_w[l].transpose(1, 0, 2)[..., None], (CMP_BLOCK, 2, NSA_KV_HEADS, HEAD_DIM))
    return dict(
        w_in=w_perm.astype(BF16), g_mix=norm_mix[l][None, :],
        w_kv_t=jnp.concatenate([seg['kvc'], seg['kvs'], seg['kvw']], axis=1).T.astype(BF16),
        g_ffn=norm_ffn[l][None, :],
        gn=gmlp_norm[l][None, :], wp=wp.astype(BF16), ps=pool_scale[l][None, :],
        wx=wx.reshape(CMP_BLOCK, KV_ROW), pe=cmp_pe[l].transpose(1, 0, 2, 3).reshape(CMP_BLOCK, KV_ROW),
        ws=gmlp_ws[l], wb=jnp.repeat(gmlp_b[l].T, POOL_GROUP_DIM, axis=1),
        wsx=jnp.repeat(gmlp_ws[l].transpose(1, 2, 0), POOL_GROUP_DIM, axis=2)[:8, :8],
        w_out=w_out[l].astype(BF16), w_up=w_up[l].astype(BF16), w_down=w_down[l].astype(BF16))


def _split_heads(kv, part):
    B, T, _ = kv.shape
    return kv.reshape(B, T, 2, NSA_KV_HEADS, HEAD_DIM)[:, :, part].transpose(0, 2, 1, 3).astype(BF16)


def _prompt_layer(x, p, tabs, B, T, final, gf):
    cfar, nbc, nbn, nbw = tabs
    tm = min(512, B * T)
    xp, q, gt, u, v, kvc, kvc_t, kvs_t, kvw_t = _inproj(x, p['g_mix'], p['w_in'], p['gn'], tm, seq=T,
                                                        wt=p['w_kv_t'])
    comp = _compress_prompt(kvc.reshape(B, T, KV_ROW), p['wx'], p['pe'])
    kc = _split_heads(comp, 0)
    vct = _split_heads(comp, 1).transpose(0, 1, 3, 2)
    q3 = q.reshape(B, T, NSA_WIDTH)
    oc, sn = _cmp_prompt(q3, kc, vct, nbc, cfar)
    pos = jnp.arange(-KPAD, T)
    blk_of = jnp.where(pos >= 0, pos // SEL_BLOCK, LANES - 1)
    onehot_t = (jnp.arange(LANES)[:, None] == blk_of[None, :]).astype(BF16)
    yb = _selwin_prompt(q3, sn, kvs_t, kvw_t, onehot_t, gt, oc, nbn, nbw)
    x1 = _mix_prompt(x, xp, yb.reshape(B * T, NSA_WIDTH), u, v, p['wp'], p['ps'], p['ws'], p['wb'], p['w_out'],
                     tm, T)
    x2 = _ffn(x1, p['g_ffn'], p['w_up'], p['w_down'], gf, tm, final)
    return x2, (xp, kvc_t, kvs_t, kvw_t)


PAGES_PER_STEP = 16
SEL_PER_PAGE = PAGE_SIZE // SEL_BLOCK


def _cmp_sample_kernel(pt_ref, *refs, ts):
    del pt_ref
    npg = PAGES_PER_STEP
    nhalf = KV_ROW // LANES
    pages = refs[:npg]
    wx_ref, pe_ref, q_ref, tab_ref, oc_ref, sel_ref, lo_ref, hi_ref, ps_ref, pg_ref = refs[npg:]
    s = pl.program_id(1)
    rows = PAGE_SIZE // CMP_STRIDE
    for k in range(npg):
        for e in range(nhalf):
            pg_ref[...] = _page_half_rows(pages[k], e)
            lo, hi = _compress_rows(lambda j: pg_ref[pl.ds(j, rows, stride=CMP_STRIDE), :], rows,
                                    wx_ref[:, e * LANES:(e + 1) * LANES])
            lo_ref[s, k * rows:(k + 1) * rows, e * LANES:(e + 1) * LANES] = lo
            hi_ref[s, k * rows:(k + 1) * rows, e * LANES:(e + 1) * LANES] = hi

    @pl.when(s == pl.num_programs(1) - 1)
    def _():
        nc = lo_ref.shape[0] * lo_ref.shape[1]
        comp = _combine_halves(lo_ref[...].reshape(nc, KV_ROW), hi_ref[...].reshape(nc, KV_ROW), pe_ref[...],
                               wx_ref[...])
        ns = nc // 4
        lane = lax.broadcasted_iota(jnp.int32, (ns, LANES), 1)
        s_iota = lax.broadcasted_iota(jnp.int32, (ns, LANES), 0)
        ncol = NSA_GROUP * ts
        selacc = jnp.zeros((ns, LANES), F32)
        for h in range(NSA_KV_HEADS):
            kc = comp[:, h * HEAD_DIM:(h + 1) * HEAD_DIM].astype(BF16)
            vc = comp[:, (NSA_KV_HEADS + h) * HEAD_DIM:(NSA_KV_HEADS + h + 1) * HEAD_DIM].astype(BF16)
            pc = _softmax_rows(_dot_nt(kc, q_ref[0, h]) + tab_ref[h])
            oc_ref[0, h] = pl.dot(pc.astype(BF16), vc, trans_a=True)[0:ncol, :]
            ps = pc
            for g in range(1, NSA_GROUP):
                ps = ps + pltpu.roll(pc, g * ts, 1)
            ps_ref[...] = ps
            p_slc = ps_ref[pl.ds(0, ns, stride=4), :]
            for j in range(1, 4):
                p_slc = p_slc + ps_ref[pl.ds(j, ns, stride=4), :]
            forced = (s_iota == 0) | (s_iota == ns - 1)
            sel = _select_topk(jnp.where(forced, FORCE_SCORE, p_slc), N_SELECT - 1)
            for g in range(NSA_GROUP):
                dst = h * ncol + g * ts
                shift = (dst - (ncol - ts)) % LANES
                moved = pltpu.roll(sel, shift, 1) if shift else sel
                selacc = jnp.where((lane >= dst) & (lane < dst + ts), moved, selacc)
        sel_ref[0] = selacc.reshape(sel_ref.shape[1:])


def _page_half_rows(page_ref, e):
    t = page_ref[0, e]
    return t.reshape(NSA_KV_HEADS * HEAD_DIM, t.shape[-1]).T


def _native_rows(a, lead):
    nd = a.ndim
    t = jnp.transpose(a, tuple(range(nd - 4)) + (nd - 3, nd - 2, nd - 1, nd - 4))
    return t.reshape((lead,) + t.shape[nd - 4:])


def _page_specs(n_pages):
    def spec(k):
        return pl.BlockSpec((1, 2, NSA_KV_HEADS, HEAD_DIM, PAGE_SIZE),
                            lambda b, s, pt: (pt[b * n_pages + s * PAGES_PER_STEP + k], 0, 0, 0, 0))
    return [spec(k) for k in range(PAGES_PER_STEP)]


def _cmp_sample(pt, cache, wx, pe, qs, tab, nb, n_pages, ts):
    steps = n_pages // PAGES_PER_STEP
    nc = n_pages * PAGE_SIZE // CMP_STRIDE
    rows = PAGES_PER_STEP * PAGE_SIZE // CMP_STRIDE
    ns = nc // 4
    full = lambda a: pl.BlockSpec(a.shape, lambda b, s, pt: (0,) * a.ndim)
    grid_spec = pltpu.PrefetchScalarGridSpec(
        num_scalar_prefetch=1,
        grid=(nb, steps),
        in_specs=_page_specs(n_pages) + [
            full(wx), full(pe),
            pl.BlockSpec((1, NSA_KV_HEADS, LANES, HEAD_DIM), lambda b, s, pt: (b, 0, 0, 0)),
            full(tab)],
        out_specs=[pl.BlockSpec((1, NSA_KV_HEADS, NSA_GROUP * ts, HEAD_DIM), lambda b, s, pt: (b, 0, 0, 0)),
                   pl.BlockSpec((1, steps, ns // steps, LANES), lambda b, s, pt: (b, 0, 0, 0))],
        scratch_shapes=[pltpu.VMEM((steps, rows, KV_ROW), F32), pltpu.VMEM((steps, rows, KV_ROW), F32),
                        pltpu.VMEM((nc, LANES), F32), pltpu.VMEM((PAGE_SIZE, LANES), F32)])
    return pl.pallas_call(
        functools.partial(_cmp_sample_kernel, ts=ts),
        grid_spec=grid_spec,
        out_shape=[jax.ShapeDtypeStruct((nb, NSA_KV_HEADS, NSA_GROUP * ts, HEAD_DIM), F32),
                   jax.ShapeDtypeStruct((nb, steps, ns // steps, LANES), F32)],
        compiler_params=_cparams(("parallel", "arbitrary")),
        name="cmp_sample",
    )(pt, *([cache] * PAGES_PER_STEP), wx, pe, qs, tab)


def _local_softmax(z, mask):
    m = jnp.max(z, axis=0, keepdims=True)
    p = jnp.where(mask, jnp.exp(z - m), 0.0)
    return m, p, jnp.sum(p, axis=0, keepdims=True)


def _selwin_sample_kernel(pt_ref, *refs, ts):
    del pt_ref
    npg = PAGES_PER_STEP
    pages = refs[:npg]
    (qa_ref, sel_ref, cf_ref, nbl_ref, st_ref, kvn_ref, nbn_ref, nbw_ref, g_ref, oc_ref,
     o_ref, m_ref, l_ref, a_ref) = refs[npg:]
    s = pl.program_id(1)
    last = pl.num_programs(1) - 1
    ncol = NSA_KV_HEADS * NSA_GROUP * ts
    qa = qa_ref[0]
    rowi = lax.broadcasted_iota(jnp.int32, (PAGE_SIZE, LANES), 0)
    for k in range(npg):
        pg = jnp.concatenate([_page_half_rows(pages[k], 0), _page_half_rows(pages[k], 1)],
                             axis=1).astype(BF16)
        is_last_page = jnp.logical_and(s == last, k == npg - 1)
        bias = jnp.where(is_last_page, nbl_ref[...], cf_ref[0:1, :])
        s0 = sel_ref[0, s, SEL_PER_PAGE * k:SEL_PER_PAGE * k + 1, :]
        s1 = sel_ref[0, s, SEL_PER_PAGE * k + 1:SEL_PER_PAGE * k + 2, :]
        mask = jnp.where(rowi < SEL_BLOCK, s0, s1) > 0.5
        z = jnp.where(mask, _dot(pg, qa) + bias, NEG)
        m, p, l = _local_softmax(z, mask)
        m_ref[s, k:k + 1, :] = m
        l_ref[s, k:k + 1, :] = l
        a_ref[s, k] = pl.dot(p.astype(BF16), pg, trans_a=True)[0:ncol, :]

    @pl.when(s == last)
    def _():
        nbn = nbn_ref[...]
        okn = nbn > 0.5 * NEG
        kn = kvn_ref[0, 0:SUBLANES, :].astype(BF16)
        m_n, p_n, l_n = _local_softmax(jnp.where(okn, _dot(kn, qa) + nbn, NEG), okn)
        a_n = pl.dot(p_n.astype(BF16), kn, trans_a=True)[0:ncol, :]
        nparts = m_ref.shape[0] * m_ref.shape[1]
        m_all = m_ref[...].reshape(nparts, LANES)
        l_all = l_ref[...].reshape(nparts, LANES)
        m_g = jnp.maximum(jnp.max(m_all, axis=0, keepdims=True), m_n)
        w_all = jnp.exp(m_all - m_g)
        w_n = jnp.exp(m_n - m_g)
        l_g = jnp.sum(w_all * l_all, axis=0, keepdims=True) + w_n * l_n

        nbw = nbw_ref[...]
        okw = nbw > 0.5 * NEG
        st = jnp.concatenate([_page_half_rows(st_ref, 0), _page_half_rows(st_ref, 1)],
                             axis=1).astype(BF16)
        kwn = kvn_ref[0, SUBLANES:2 * SUBLANES, :].astype(BF16)
        zw = jnp.where(okw, _dot(st, qa) + nbw, NEG)
        zwn = jnp.where(okn, _dot(kwn, qa) + nbn, NEG)
        m_w = jnp.maximum(jnp.max(zw, axis=0, keepdims=True), jnp.max(zwn, axis=0, keepdims=True))
        pw = jnp.where(okw, jnp.exp(zw - m_w), 0.0)
        pwn = jnp.where(okn, jnp.exp(zwn - m_w), 0.0)
        l_w = jnp.sum(pw, axis=0, keepdims=True) + jnp.sum(pwn, axis=0, keepdims=True)
        a_w = (pl.dot(pw.astype(BF16), st, trans_a=True) + pl.dot(pwn.astype(BF16), kwn, trans_a=True))[0:ncol, :]

        assert nparts + 4 * SUBLANES <= LANES
        rows8 = lambda r: jnp.broadcast_to(r, (SUBLANES, LANES))
        x = jnp.concatenate([w_all, rows8(w_n), rows8(l_g), rows8(l_w),
                             jnp.zeros((LANES - nparts - 3 * SUBLANES, LANES), F32)], axis=0)
        xt = x.T
        acc = xt[0:ncol, nparts:nparts + 1] * a_n
        for pidx in range(nparts):
            acc = acc + xt[0:ncol, pidx:pidx + 1] * a_ref[pidx // npg, pidx % npg]
        o_s = acc / jnp.maximum(xt[0:ncol, nparts + SUBLANES:nparts + SUBLANES + 1], 1e-30)
        o_w = a_w / jnp.maximum(xt[0:ncol, nparts + 2 * SUBLANES:nparts + 2 * SUBLANES + 1], 1e-30)

        ri = lax.broadcasted_iota(jnp.int32, (ncol, HEAD_DIM), 0)
        v_of = lambda t: jnp.where(ri < ncol // NSA_KV_HEADS, t[:, 2 * HEAD_DIM:3 * HEAD_DIM],
                                   t[:, 3 * HEAD_DIM:4 * HEAD_DIM])
        gt = g_ref[0]
        oc = oc_ref[0].reshape(ncol, HEAD_DIM)
        o_ref[0] = gt[:, 0:1] * oc + gt[:, 1:2] * v_of(o_s) + gt[:, 2:3] * v_of(o_w)


def _selwin_sample(pt, cache, qa, sel, cfrow, nbl, state, st_off, kvn, nbn, nbw, gates, oc, nb, n_pages, ts):
    steps = n_pages // PAGES_PER_STEP
    ncol = NSA_KV_HEADS * NSA_GROUP * ts
    full = lambda a: pl.BlockSpec(a.shape, lambda b, s, pt: (0,) * a.ndim)
    perb = lambda a: pl.BlockSpec((1,) + a.shape[1:], lambda b, s, pt: (b,) + (0,) * (a.ndim - 1))
    grid_spec = pltpu.PrefetchScalarGridSpec(
        num_scalar_prefetch=1,
        grid=(nb, steps),
        in_specs=_page_specs(n_pages) + [
            perb(qa), perb(sel), full(cfrow), full(nbl),
            pl.BlockSpec((1,) + state.shape[1:], lambda b, s, pt: (st_off + b, 0, 0, 0, 0)),
            perb(kvn), full(nbn), full(nbw), perb(gates), perb(oc)],
        out_specs=pl.BlockSpec((1, ncol, HEAD_DIM), lambda b, s, pt: (b, 0, 0)),
        scratch_shapes=[pltpu.VMEM((steps, PAGES_PER_STEP, LANES), F32),
                        pltpu.VMEM((steps, PAGES_PER_STEP, LANES), F32),
                        pltpu.VMEM((steps, PAGES_PER_STEP, ncol, KV_ROW), F32)])
    return pl.pallas_call(
        functools.partial(_selwin_sample_kernel, ts=ts),
        grid_spec=grid_spec,
        out_shape=jax.ShapeDtypeStruct((nb, ncol, HEAD_DIM), F32),
        compiler_params=_cparams(("parallel", "arbitrary")),
        name="selwin_sample",
    )(pt, *([cache] * PAGES_PER_STEP), qa, sel, cfrow, nbl, state, kvn, nbn, nbw, gates, oc)


def _sample_bias_tables(rel_bias, past, ts, n_buf):
    ncol = NSA_KV_HEADS * NSA_GROUP * ts
    col = jnp.arange(ncol)
    head = col // ts
    t = col % ts
    pick = lambda b: jnp.take_along_axis(b, jnp.broadcast_to(head, b.shape[:-1])[..., None], axis=-1)[..., 0]
    bias = lambda d: pick(_head_bias(rel_bias, d))
    padc = lambda a, fill=0.0: jnp.pad(a, ((0, 0), (0, LANES - a.shape[1])), constant_values=fill)
    nc = past // CMP_STRIDE
    c_end = jnp.arange(nc)[:, None] * CMP_STRIDE + CMP_BLOCK - 1
    tab = jnp.where(jnp.arange(nc)[:, None] < nc - 1, bias(past + t[None, :] - c_end), NEG)
    hc = ncol // NSA_KV_HEADS
    tab_cmp = jnp.stack([padc(tab[:, h * hc:(h + 1) * hc]) for h in range(NSA_KV_HEADS)])
    far = rel_bias[N_BUCKETS - 1].astype(F32)[head]
    cfrow = jnp.broadcast_to(padc(far[None, :]), (SUBLANES, LANES))
    kk = jnp.arange(PAGE_SIZE)[:, None]
    nbl = padc(bias(PAGE_SIZE + t[None, :] - kk))
    j = jnp.arange(SUBLANES)[:, None]
    dn = t[None, :] - j
    nbn = padc(jnp.where((dn >= 0) & (j < ts), bias(dn), NEG))
    r = jnp.arange(n_buf)[:, None]
    dw = n_buf + t[None, :] - r
    nbw = padc(jnp.where((dw >= 0) & (dw < WINDOW), bias(dw), NEG))
    return tab_cmp, cfrow, nbl, nbn, nbw


def _sample_layer(x, p, tabs, l, pt, cache_cmp, cache_slc, state_win, state_pool_l, nb, ts, n_pages, final, gf):
    tab_cmp, cfrow, nbl, nbn, nbw = tabs
    past = n_pages * PAGE_SIZE
    R = ts * nb
    ncol = NSA_KV_HEADS * NSA_GROUP * ts
    xp, q, gt, u, v, kvc, kvs, kvw = _inproj(x, p['g_mix'], p['w_in'], p['gn'], R)
    q5 = q.reshape(ts, nb, NSA_KV_HEADS, NSA_GROUP, HEAD_DIM)
    qs = q5.transpose(1, 2, 3, 0, 4).reshape(nb, NSA_KV_HEADS, NSA_GROUP * ts, HEAD_DIM)
    qs = jnp.pad(qs, ((0, 0), (0, 0), (0, LANES - NSA_GROUP * ts), (0, 0)))
    oc, sel = _cmp_sample(pt, cache_cmp, p['wx'], p['pe'], qs, tab_cmp, nb, n_pages, ts)
    qt = q5.transpose(1, 2, 4, 3, 0).reshape(nb, NSA_KV_HEADS, HEAD_DIM, NSA_GROUP * ts)
    eye = jnp.eye(NSA_KV_HEADS, dtype=qt.dtype)
    qa = jnp.einsum('bhdc,hk->bhdkc', qt, eye).reshape(nb, NSA_KV_HEADS * HEAD_DIM, ncol)
    qa = jnp.pad(qa, ((0, 0), (0, KV_ROW - NSA_KV_HEADS * HEAD_DIM), (0, LANES - ncol)))
    rows_of = lambda a: jnp.pad(a.reshape(ts, nb, KV_ROW).transpose(1, 0, 2), ((0, 0), (0, SUBLANES - ts), (0, 0)))
    kvn = jnp.concatenate([rows_of(kvs), rows_of(kvw)], axis=1)
    ng = NSA_GROUP * N_BRANCHES
    g5 = jnp.concatenate([gt[:, h * LANES:h * LANES + ng] for h in range(NSA_KV_HEADS)], axis=1)
    g5 = g5.reshape(ts, nb, NSA_KV_HEADS * NSA_GROUP, N_BRANCHES)
    gates = jnp.pad(g5.transpose(1, 2, 0, 3).reshape(nb, ncol, N_BRANCHES), ((0, 0), (0, 0), (0, LANES - N_BRANCHES)))
    o = _selwin_sample(pt, cache_slc, qa, sel, cfrow, nbl, state_win, l * nb, kvn, nbn, nbw, gates, oc,
                       nb, n_pages, ts)
    yb = o.reshape(nb, NSA_KV_HEADS * NSA_GROUP, ts, HEAD_DIM).transpose(2, 0, 1, 3).reshape(R, NSA_WIDTH)
    ext = jnp.concatenate([state_pool_l.transpose(1, 0, 2), xp.reshape(ts, nb, POOL_WIDTH)], axis=0)
    x1 = _mix_sample(x, ext, yb.astype(BF16), u.reshape(ts, nb, GMLP_WIDTH), v.reshape(ts, nb, GMLP_WIDTH),
                     p['wp'], p['ps'], p['wsx'], p['wb'][:SUBLANES], p['w_out'], nb, ts, past)
    x2 = _ffn(x1, p['g_ffn'], p['w_up'], p['w_down'], gf, R, final)
    return x2, (xp, kvc, kvs, kvw, v)


def kernel(x_prompt, x_sample, cache_cmp_kv, cache_slc_kv, state_win_kv, state_pool, page_table, w_in, w_out,
           norm_mix, norm_ffn, norm_final, w_pool, pool_scale, cmp_w, cmp_pe, gmlp_ws, gmlp_b, gmlp_norm, w_up,
           w_down, rel_bias):
    B, T, _ = x_prompt.shape
    nb, ts, _ = x_sample.shape
    n_pages = page_table.shape[1]
    n_phys = cache_cmp_kv.shape[1]
    n_buf = state_win_kv.shape[2]
    past = n_pages * PAGE_SIZE
    depth = w_in.shape[0]
    assert T % TK_FAR == 0 and T >= WINDOW and n_pages % PAGES_PER_STEP == 0
    assert ts <= SUBLANES and ts <= POOL_STATE and n_buf == WINDOW and past >= WINDOW
    kv_tail = (2, NSA_KV_HEADS, HEAD_DIM)

    cache_cmp = _native_rows(cache_cmp_kv, depth * n_phys)
    cache_slc = _native_rows(cache_slc_kv, depth * n_phys)
    state_win = _native_rows(state_win_kv, depth * nb)
    ptabs = _prompt_bias_tables(rel_bias)
    stabs = _sample_bias_tables(rel_bias, past, ts, n_buf)
    gf = norm_final[None, :]
    xp = x_prompt.reshape(B * T, D_MODEL)
    xs = x_sample.transpose(1, 0, 2).reshape(ts * nb, D_MODEL)
    unmajor = lambda a: a.reshape(ts, nb, a.shape[-1]).transpose(1, 0, 2)

    outs = [[] for _ in range(9)]
    for l in range(depth):
        p = _prep_layer(l, w_in, w_out, norm_mix, norm_ffn, w_pool, pool_scale, cmp_w, cmp_pe, gmlp_ws, gmlp_b,
                        gmlp_norm, w_up, w_down)
        final = l == depth - 1
        xp, (pin, kvc_t, kvs_t, kvw_t) = _prompt_layer(xp, p, ptabs, B, T, final, gf)
        rows_of = lambda a: a.reshape((B,) + kv_tail + (a.shape[-1],)).transpose(0, 4, 1, 2, 3)
        outs[0].append(rows_of(kvc_t))
        outs[1].append(rows_of(kvs_t))
        outs[2].append(rows_of(kvw_t[:, :, T - WINDOW:]))
        outs[3].append(pin.reshape(B, T, POOL_WIDTH)[:, T - POOL_STATE:])

        pt = (page_table + l * n_phys).reshape(-1).astype(jnp.int32)
        xs, (sin, kvc_s, kvs_s, kvw_s, v_s) = _sample_layer(
            xs, p, stabs, l, pt, cache_cmp, cache_slc, state_win, state_pool[l], nb, ts, n_pages, final, gf)
        kvw_new = unmajor(kvw_s).reshape((nb, ts) + kv_tail)
        outs[4].append(unmajor(kvc_s).reshape((nb, ts) + kv_tail))
        outs[5].append(unmajor(kvs_s).reshape((nb, ts) + kv_tail))
        outs[6].append(jnp.concatenate([state_win_kv[l][:, ts:], kvw_new], axis=1))
        outs[7].append(jnp.concatenate([state_pool[l][:, ts:], unmajor(sin)], axis=1))
        outs[8].append(unmajor(v_s))

    y_prompt = xp.reshape(B, T, D_MODEL)
    y_sample = unmajor(xs)
    return (y_prompt, y_sample) + tuple(jnp.stack(o) for o in outs)
```

```python
import functools
import math

import numpy as np
import jax
import jax.numpy as jnp
from jax import lax
from jax.experimental import pallas as pl
from jax.experimental.pallas import tpu as pltpu

F32 = jnp.float32
BF16 = jnp.bfloat16

D_MODEL = 1024
DEPTH = 4
PAGE_SIZE = 128
HEAD_DIM = 64
POOL_WINDOWS = (2, 4, 8, 16)
POOL_WIDTH = 256
POOL_GROUP_DIM = 64
POOL_STATE = 15
NSA_WIDTH = 512
NSA_HEADS = 8
NSA_KV_HEADS = 2
NSA_GROUP = 4
KV_ROW = 2 * NSA_KV_HEADS * HEAD_DIM
CMP_STRIDE = 16
CMP_BLOCK = 32
SEL_BLOCK = 64
N_SELECT = 16
WINDOW = 512
N_BRANCHES = 3
FORCE_SCORE = 1000.0
GMLP_WIDTH = 256
GMLP_GROUPS = 4
GMLP_CHUNK = 128
D_FF = 4096
N_BUCKETS = 32
MAX_DISTANCE = 128
EPS = 1e-6
SCALE = HEAD_DIM ** -0.5

LANES = 128
SUBLANES = 8
VMEM_LIMIT = 56 * 1024 * 1024

NEG = -1e30
CMP_PAD = 16
TQ = 128
TQC = 256
CMP_NEAR = CMP_PAD + TQC // CMP_STRIDE
TK_FAR = 512
KPAD = 512
GATE_LANES = 2 * LANES

PROJ_SEGS = (('xp', 256), ('q', 512), ('kvc', 256), ('kvs', 256), ('kvw', 256), ('gt', 24), ('u', 256), ('v', 256))


def _cparams(sem):
    return pltpu.CompilerParams(dimension_semantics=sem, vmem_limit_bytes=VMEM_LIMIT)


def _dot(a, b):
    return jnp.dot(a, b, preferred_element_type=F32)


def _dot_nt(a, b):
    return lax.dot_general(a, b, (((1,), (1,)), ((), ())), preferred_element_type=F32)


def _gelu(x):
    c = math.sqrt(2.0 / math.pi)
    return 0.5 * x * (1.0 + jnp.tanh(c * (x + 0.044715 * (x * x * x))))


def _rms(x, g):
    return x * lax.rsqrt(jnp.mean(x * x, axis=-1, keepdims=True) + EPS) * g


def _inproj_kernel(x_ref, g_ref, w_ref, gn_ref, *refs, channel_major):
    if channel_major:
        wt_ref, xp_ref, q_ref, gt_ref, u_ref, v_ref, kvc_ref, *kv_refs = refs
    else:
        xp_ref, q_ref, gt_ref, u_ref, v_ref, kvc_ref, *kv_refs = refs
    x = x_ref[...]
    h = _rms(x, g_ref[...]).astype(BF16)

    def seg(lo, hi):
        return _dot(h, w_ref[:, lo:hi])

    xp_ref[...] = seg(0, 256)
    q_ref[...] = (seg(256, 768) * SCALE).astype(BF16)
    kvc_ref[...] = seg(768, 1024)
    if channel_major:
        for k, t_ref in enumerate(kv_refs):
            t_ref[0] = _dot_nt(wt_ref[k * KV_ROW:(k + 1) * KV_ROW, :], h)
    else:
        kv_refs[0][...] = seg(1024, 1280)
        kv_refs[1][...] = seg(1280, 1536)
    u_ref[...] = _gelu(seg(1536, 1792))
    gv = _gelu(seg(1792, 2048))
    sq = gv * gv
    lane = lax.broadcasted_iota(jnp.int32, sq.shape, 1)
    ms = jnp.zeros_like(sq)
    for g in range(GMLP_GROUPS):
        in_g = (lane >= g * 64) & (lane < (g + 1) * 64)
        s = jnp.sum(jnp.where(in_g, sq, 0.0), axis=-1, keepdims=True) * (1.0 / 64.0)
        ms = jnp.where(in_g, s, ms)
    v_ref[...] = gv * lax.rsqrt(ms + EPS) * gn_ref[...]
    gt_ref[...] = jax.nn.sigmoid(seg(2048, 2048 + GATE_LANES))


def _inproj(x, g, w, gn, tm, seq=None, wt=None):
    R = x.shape[0]
    f = lambda n, dt=F32: jax.ShapeDtypeStruct((R, n), dt)
    row = lambda n: pl.BlockSpec((tm, n), lambda i: (i, 0))
    full = lambda a: pl.BlockSpec(a.shape, lambda i: (0,) * a.ndim)
    out_specs = [row(256), row(512), row(GATE_LANES), row(256), row(256), row(256)]
    out_shape = [f(256), f(512, BF16), f(GATE_LANES), f(256), f(256), f(256)]
    if seq is None:
        out_specs += [row(256), row(256)]
        out_shape += [f(256), f(256)]
    else:
        per_seq = seq // tm
        out_specs += [pl.BlockSpec((1, KV_ROW, tm), lambda i: (i // per_seq, 0, i % per_seq))] * 3
        out_shape += [jax.ShapeDtypeStruct((R // seq, KV_ROW, seq), F32)] * 3
    args = (x, g, w, gn) if seq is None else (x, g, w, gn, wt)
    return pl.pallas_call(
        functools.partial(_inproj_kernel, channel_major=seq is not None),
        grid=(R // tm,),
        in_specs=[row(D_MODEL)] + [full(a) for a in args[1:]],
        out_specs=out_specs,
        out_shape=out_shape,
        compiler_params=_cparams(("parallel",)),
        name="inproj",
    )(*args)


def _out_proj(x, ya, yb, yc, wo_ref):
    acc = _dot(ya.astype(BF16), wo_ref[0:256, :])
    acc = acc + _dot(yb, wo_ref[256:768, :])
    acc = acc + _dot(yc.astype(BF16), wo_ref[768:1024, :])
    return x + acc


def _pool_tail(win2, win4, win8, win16, cur, cnt, wp_ref, ps_ref):
    lane = lax.broadcasted_iota(jnp.int32, cur.shape, 1)
    win = jnp.where(lane < 64, win2, jnp.where(lane < 128, win4, jnp.where(lane < 192, win8, win16)))
    pooled = win / cnt - cur
    return _dot(pooled.astype(BF16), wp_ref[...]) * ps_ref[...]


def _mix_prompt_kernel(x_ref, xp_ref, halo_ref, yb_ref, u_ref, v_ref, wp_ref, ps_ref, ws_ref, wb_ref, wo_ref,
                       o_ref, ext_ref, *, tm, seq):
    i = pl.program_id(0)
    t0 = (i * tm) % seq
    halo = jnp.where(t0 > 0, halo_ref[...], 0.0)
    cur = xp_ref[...]
    ext_ref[0:16, :] = halo
    ext_ref[16:16 + tm, :] = cur
    e = ext_ref[...]
    b2 = e[1:] + e[:-1]
    b4 = b2[2:] + b2[:-2]
    b8 = b4[4:] + b4[:-4]
    b16 = b8[8:] + b8[:-8]
    win2 = b2[15:15 + tm]
    win4 = b4[13:13 + tm]
    win8 = b8[9:9 + tm]
    win16 = b16[1:1 + tm]
    rowi = lax.broadcasted_iota(jnp.int32, cur.shape, 0)
    lane = lax.broadcasted_iota(jnp.int32, cur.shape, 1)
    wsz = jnp.where(lane < 64, 2, jnp.where(lane < 128, 4, jnp.where(lane < 192, 8, 16)))
    cnt = jnp.minimum(t0 + rowi + 1, wsz).astype(F32)
    ya = _pool_tail(win2, win4, win8, win16, cur, cnt, wp_ref, ps_ref)

    ci = lax.broadcasted_iota(jnp.int32, (GMLP_CHUNK, GMLP_CHUNK), 0)
    cj = lax.broadcasted_iota(jnp.int32, (GMLP_CHUNK, GMLP_CHUNK), 1)
    lane_c = lax.broadcasted_iota(jnp.int32, (GMLP_CHUNK, GMLP_WIDTH), 1)
    wts = [jnp.where(ci >= cj, ws_ref[g], 0.0).astype(BF16) for g in range(GMLP_GROUPS)]
    parts = []
    for c in range(tm // GMLP_CHUNK):
        vc = v_ref[c * GMLP_CHUNK:(c + 1) * GMLP_CHUNK, :].astype(BF16)
        s = jnp.zeros((GMLP_CHUNK, GMLP_WIDTH), F32)
        for g in range(GMLP_GROUPS):
            sg = _dot(wts[g], vc)
            s = jnp.where((lane_c >= g * 64) & (lane_c < (g + 1) * 64), sg, s)
        parts.append(u_ref[c * GMLP_CHUNK:(c + 1) * GMLP_CHUNK, :] * (s + wb_ref[...]))
    yc = jnp.concatenate(parts, axis=0) if len(parts) > 1 else parts[0]

    o_ref[...] = _out_proj(x_ref[...], ya, yb_ref[...], yc, wo_ref)


def _mix_prompt(x, xp, yb, u, v, wp, ps, ws, wb, wo, tm, seq):
    R = x.shape[0]
    row = lambda n: pl.BlockSpec((tm, n), lambda i: (i, 0))
    full = lambda a: pl.BlockSpec(a.shape, lambda i: (0,) * a.ndim)
    halo = pl.BlockSpec((16, 256), lambda i: (jnp.maximum(i * (tm // 16) - 1, 0), 0))
    return pl.pallas_call(
        functools.partial(_mix_prompt_kernel, tm=tm, seq=seq),
        grid=(R // tm,),
        in_specs=[row(D_MODEL), row(256), halo, row(512), row(256), row(256),
                  full(wp), full(ps), full(ws), full(wb), full(wo)],
        out_specs=row(D_MODEL),
        out_shape=jax.ShapeDtypeStruct((R, D_MODEL), F32),
        scratch_shapes=[pltpu.VMEM((tm + 16, 256), F32)],
        compiler_params=_cparams(("parallel",)),
        name="mix_prompt",
    )(x, xp, xp, yb, u, v, wp, ps, ws, wb, wo)


def _mix_sample_kernel(x_ref, ext_ref, yb_ref, u_ref, v_ref, wp_ref, ps_ref, wsx_ref, wb_ref, wo_ref, o_ref,
                       *, nb, ts, past):
    lane = lax.broadcasted_iota(jnp.int32, (nb, POOL_WIDTH), 1)
    wsz = jnp.where(lane < 64, 2, jnp.where(lane < 128, 4, jnp.where(lane < 192, 8, 16)))
    yas, ycs = [], []
    for t in range(ts):
        top = POOL_STATE + t
        acc = ext_ref[top] + ext_ref[top - 1]
        wins = [acc]
        for w in (4, 8, 16):
            for j in range(w // 2, w):
                acc = acc + ext_ref[top - j]
            wins.append(acc)
        cnt = jnp.minimum(past + t + 1, wsz).astype(F32)
        yas.append(_pool_tail(wins[0], wins[1], wins[2], wins[3], ext_ref[top], cnt, wp_ref, ps_ref))
        s = wb_ref[t:t + 1, :]
        s = jnp.broadcast_to(s, (nb, GMLP_WIDTH))
        for j in range(t + 1):
            s = s + wsx_ref[t, j:j + 1, :] * v_ref[j]
        ycs.append(u_ref[t] * s)
    ya = jnp.concatenate(yas, axis=0)
    yc = jnp.concatenate(ycs, axis=0)
    o_ref[...] = _out_proj(x_ref[...], ya, yb_ref[...], yc, wo_ref)


def _mix_sample(x, ext, yb, u, v, wp, ps, wsx, wb4, wo, nb, ts, past):
    R = x.shape[0]
    args = (x, ext, yb, u, v, wp, ps, wsx, wb4, wo)
    full = lambda a: pl.BlockSpec(a.shape, lambda i: (0,) * a.ndim)
    return pl.pallas_call(
        functools.partial(_mix_sample_kernel, nb=nb, ts=ts, past=past),
        grid=(1,),
        in_specs=[full(a) for a in args],
        out_specs=pl.BlockSpec((R, D_MODEL), lambda i: (0, 0)),
        out_shape=jax.ShapeDtypeStruct((R, D_MODEL), F32),
        compiler_params=_cparams(("arbitrary",)),
        name="mix_sample",
    )(*args)


def _ffn_kernel(x_ref, g_ref, wu_ref, wd_ref, gf_ref, o_ref, *, final, fc):
    x = x_ref[...]
    h = _rms(x, g_ref[...]).astype(BF16)
    acc = x
    for c in range(D_FF // fc):
        a = jnp.maximum(_dot(h, wu_ref[:, c * fc:(c + 1) * fc]), 0.0)
        acc = acc + _dot((a * a).astype(BF16), wd_ref[c * fc:(c + 1) * fc, :])
    if final:
        acc = _rms(acc, gf_ref[...])
    o_ref[...] = acc


def _ffn(x, g, wu, wd, gf, tm, final):
    R = x.shape[0]
    row = pl.BlockSpec((tm, D_MODEL), lambda i: (i, 0))
    full = lambda a: pl.BlockSpec(a.shape, lambda i: (0,) * a.ndim)
    wfull = lambda a: pl.BlockSpec(a.shape, lambda i: (0,) * a.ndim, pipeline_mode=pl.Buffered(1))
    return pl.pallas_call(
        functools.partial(_ffn_kernel, final=final, fc=1024),
        grid=(R // tm,),
        in_specs=[row, full(g), wfull(wu), wfull(wd), full(gf)],
        out_specs=row,
        out_shape=jax.ShapeDtypeStruct((R, D_MODEL), F32),
        compiler_params=_cparams(("parallel",)),
        name="ffn",
    )(x, g, wu, wd, gf)


def _compress_rows(read_rows, n, wx):
    lo = jnp.zeros((n, wx.shape[1]), F32)
    hi = jnp.zeros((n, wx.shape[1]), F32)
    for j in range(CMP_STRIDE):
        xj = read_rows(j)
        lo = lo + xj * wx[j:j + 1, :]
        hi = hi + xj * wx[CMP_STRIDE + j:CMP_STRIDE + j + 1, :]
    return lo, hi


def _combine_halves(lo, hi, pe, wx):
    n = lo.shape[0]
    pe_term = jnp.sum(pe * wx, axis=0, keepdims=True)
    comp = lo + pltpu.roll(hi, n - 1, 0) + pe_term
    rowi = lax.broadcasted_iota(jnp.int32, comp.shape, 0)
    return jnp.where(rowi < n - 1, comp, 0.0)


def _compress_prompt_kernel(x_ref, wx_ref, pe_ref, o_ref, *, n):
    wx = wx_ref[...]
    lo, hi = _compress_rows(lambda j: x_ref[0, pl.ds(j, n, stride=CMP_STRIDE), :], n, wx)
    o_ref[0, 0:CMP_PAD, :] = jnp.zeros((CMP_PAD, LANES), F32)
    o_ref[0, CMP_PAD:CMP_PAD + n, :] = _combine_halves(lo, hi, pe_ref[...], wx)


def _compress_prompt(kvc, wx, pe):
    B, T, _ = kvc.shape
    n = T // CMP_STRIDE
    half = lambda rows: pl.BlockSpec((rows, LANES), lambda b, e: (0, e))
    return pl.pallas_call(
        functools.partial(_compress_prompt_kernel, n=n),
        grid=(B, KV_ROW // LANES),
        in_specs=[pl.BlockSpec((1, T, LANES), lambda b, e: (b, 0, e)), half(CMP_BLOCK), half(CMP_BLOCK)],
        out_specs=pl.BlockSpec((1, n + CMP_PAD, LANES), lambda b, e: (b, 0, e)),
        out_shape=jax.ShapeDtypeStruct((B, n + CMP_PAD, KV_ROW), F32),
        compiler_params=_cparams(("parallel", "parallel")),
        name="compress_prompt",
    )(kvc, wx, pe)


def _select_topk(score, n_pick):
    s_iota = lax.broadcasted_iota(jnp.int32, score.shape, 0)
    big = score.shape[0]
    work = score
    for _ in range(n_pick):
        m = jnp.max(work, axis=0, keepdims=True)
        idx = jnp.min(jnp.where(work == m, s_iota, big), axis=0, keepdims=True)
        work = jnp.where(s_iota == idx, -2.0, work)
    return jnp.where((work == -2.0) & (score >= 0.0), 1.0, 0.0)


def _softmax_rows(z):
    m = jnp.maximum(jnp.max(z, axis=0, keepdims=True), 0.1 * NEG)
    p = jnp.exp(z - m)
    den = jnp.maximum(jnp.sum(p, axis=0, keepdims=True), 1e-30)
    return p * (1.0 / den)


def _cmp_prompt_kernel(q_ref, kc_ref, vct_ref, nb_ref, cf_ref, oc_ref, sn_ref, raw_ref, z_ref, ps_ref, *, ncp):
    i = pl.program_id(2)
    hh = pl.program_id(1)
    tq = q_ref.shape[1]
    r0 = pl.multiple_of(i * (tq // CMP_STRIDE), SUBLANES)
    q = q_ref[0]
    vct = vct_ref[0, 0]
    rown = lax.broadcasted_iota(jnp.int32, (CMP_NEAR, tq), 0) + r0

    def attend(rows):
        kc = kc_ref[0, 0, 0:rows, :]
        rowi = lax.broadcasted_iota(jnp.int32, (rows, tq), 0)
        far = (rowi >= CMP_PAD) & (rowi < r0)
        psum = jnp.zeros((rows, tq), F32)
        octs = []
        for g in range(NSA_GROUP):
            qg = q[:, g * HEAD_DIM:(g + 1) * HEAD_DIM]
            raw_ref[0:rows, :] = _dot_nt(kc, qg)
            cf = cf_ref[hh * NSA_GROUP + g]
            z_ref[0:rows, :] = jnp.where(far, raw_ref[0:rows, :] + cf[0:1, :], NEG)
            nb = nb_ref[hh * NSA_GROUP + g]
            zn = raw_ref[pl.ds(r0, CMP_NEAR), :] + nb
            z_ref[pl.ds(r0, CMP_NEAR), :] = jnp.where((rown >= CMP_PAD) & (nb > 0.5 * NEG), zn, NEG)
            pc = _softmax_rows(z_ref[0:rows, :])
            psum = psum + pc
            pcb = pc.astype(BF16)
            if rows < ncp:
                pcb = jnp.concatenate([pcb, jnp.zeros((ncp - rows, tq), BF16)], axis=0)
            octs.append(_dot(vct, pcb))
        oc_ref[0] = jnp.concatenate(octs, axis=0).T
        for c in range(tq // LANES):
            ps_ref[c, 0:rows, :] = psum[:, c * LANES:(c + 1) * LANES]
            if rows < ps_ref.shape[1]:
                ps_ref[c, rows:, :] = jnp.zeros((ps_ref.shape[1] - rows, LANES), F32)

    n_var = (ncp - CMP_PAD + LANES - 1) // LANES
    var = (r0 + CMP_NEAR - CMP_PAD + LANES - 1) // LANES - 1
    for v in range(n_var):
        pl.when(var == v)(functools.partial(attend, min(CMP_PAD + (v + 1) * LANES, ncp)))

    ns = LANES
    halves = []
    for c in range(tq // LANES):
        part = ps_ref[c, pl.ds(CMP_PAD, ns, stride=4), :]
        for j in range(1, 4):
            part = part + ps_ref[c, pl.ds(CMP_PAD + j, ns, stride=4), :]
        halves.append(part)
    p_slc = jnp.concatenate(halves, axis=1)
    s_iota = lax.broadcasted_iota(jnp.int32, (ns, tq), 0)
    tt = lax.broadcasted_iota(jnp.int32, (ns, tq), 1)
    cur = (i * tq + tt) // SEL_BLOCK
    forced = (s_iota == 0) | (s_iota == cur) | (s_iota == cur - 1)
    score = jnp.where(s_iota <= cur, jnp.where(forced, FORCE_SCORE, p_slc), -1.0)
    sel = _select_topk(score, N_SELECT)
    sn_ref[0, 0] = jnp.where(sel.T > 0.5, 0.0, NEG).astype(BF16)


def _cmp_prompt(q, kc, vct, nbc, cfar):
    B, T, _ = q.shape
    ncp = kc.shape[2]
    ns = LANES
    nq = T // TQC
    full = lambda a: pl.BlockSpec(a.shape, lambda b, h, i: (0,) * a.ndim)
    return pl.pallas_call(
        functools.partial(_cmp_prompt_kernel, ncp=ncp),
        grid=(B, NSA_KV_HEADS, nq),
        in_specs=[pl.BlockSpec((1, TQC, 256), lambda b, h, i: (b, i, h)),
                  pl.BlockSpec((1, 1, ncp, HEAD_DIM), lambda b, h, i: (b, h, 0, 0)),
                  pl.BlockSpec((1, 1, HEAD_DIM, ncp), lambda b, h, i: (b, h, 0, 0)),
                  full(nbc), full(cfar)],
        out_specs=[pl.BlockSpec((1, TQC, 256), lambda b, h, i: (b, i, h)),
                   pl.BlockSpec((1, 1, TQC, ns), lambda b, h, i: (b, h, i, 0))],
        out_shape=[jax.ShapeDtypeStruct((B, T, NSA_WIDTH), F32),
                   jax.ShapeDtypeStruct((B, NSA_KV_HEADS, T, ns), BF16)],
        scratch_shapes=[pltpu.VMEM((ncp, TQC), F32), pltpu.VMEM((ncp, TQC), F32),
                        pltpu.VMEM((TQC // LANES, max(ncp, CMP_PAD + 4 * LANES), LANES), F32)],
        compiler_params=_cparams(("parallel", "parallel", "parallel")),
        name="cmp_prompt",
    )(q, kc, vct, nbc, cfar)


def _selwin_prompt_kernel(q_ref, sn_ref, ks_ref, vs_ref, kw_ref, vw_ref, oh_ref, g_ref, oc_ref, nbn_ref, nbw_ref,
                          gsel_ref, o_ref, ka_ref, va_ref, kwa_ref, vwa_ref, ms_ref, as_ref, za_ref, zb_ref):
    i = pl.program_id(2)
    R4 = NSA_GROUP * TQ
    ncols = ka_ref.shape[1]

    @pl.when(i == 0)
    def _():
        zpad = jnp.zeros((HEAD_DIM, KPAD), BF16)
        ones = jnp.ones((HEAD_DIM, ncols), BF16)
        row = lax.broadcasted_iota(jnp.int32, (HEAD_DIM, ncols), 0)
        col = lax.broadcasted_iota(jnp.int32, (HEAD_DIM, ncols), 1)
        ka_ref[0:LANES, :] = oh_ref[...]
        kwa_ref[HEAD_DIM:, :] = jnp.where((row == 0) & (col < KPAD), NEG, 0.0).astype(BF16)
        for dst, src in ((ka_ref.at[LANES:], ks_ref), (va_ref.at[0:HEAD_DIM], vs_ref),
                         (kwa_ref.at[0:HEAD_DIM], kw_ref), (vwa_ref.at[0:HEAD_DIM], vw_ref)):
            dst[:, 0:KPAD] = zpad
            dst[:, KPAD:] = src[0].astype(BF16)
        va_ref[HEAD_DIM:, :] = ones
        vwa_ref[HEAD_DIM:, :] = ones

    q = q_ref[0]
    q4 = jnp.concatenate([q[:, g * HEAD_DIM:(g + 1) * HEAD_DIM] for g in range(NSA_GROUP)], axis=0)
    sn = sn_ref[0, 0]
    blk = lax.broadcasted_iota(jnp.int32, (TQ, LANES), 1)
    first_near = (i - 1) * (TQ // SEL_BLOCK)
    sn_far = jnp.where(blk >= first_near, NEG, sn.astype(F32)).astype(BF16)
    qa_near = jnp.concatenate([jnp.concatenate([sn] * NSA_GROUP, axis=0), q4], axis=1)
    qa_far = jnp.concatenate([jnp.concatenate([sn_far] * NSA_GROUP, axis=0), q4], axis=1)
    qw = jnp.concatenate([q4, jnp.ones((R4, HEAD_DIM), BF16)], axis=1)

    def one_pass(z, vt):
        m = jnp.max(z, axis=1, keepdims=True)
        acc = _dot_nt(jnp.exp(z - m).astype(BF16), vt)
        return m, acc

    kw0 = pl.multiple_of(i * TQ, TQ)
    zw = _dot(qw, kwa_ref[:, pl.ds(kw0, WINDOW + TQ)]) + nbw_ref[0]
    _, a_w = one_pass(zw, vwa_ref[:, pl.ds(kw0, WINDOW + TQ)])

    kn0 = pl.multiple_of((i - 1) * TQ + KPAD, TQ)
    zn = _dot(qa_near, ka_ref[:, pl.ds(kn0, 2 * TQ)]) + nbn_ref[0]
    m0, a0 = one_pass(zn, va_ref[:, pl.ds(kn0, 2 * TQ)])
    ms_ref[...] = jnp.broadcast_to(m0, (R4, LANES))
    as_ref[...] = a0

    def far_logits(j):
        k0 = pl.multiple_of(j * TK_FAR + KPAD, TK_FAR)
        return _dot(qa_far, ka_ref[:, pl.ds(k0, TK_FAR)])

    def far_update(j, z):
        k0 = pl.multiple_of(j * TK_FAR + KPAD, TK_FAR)
        m_prev = ms_ref[...]
        m_new = jnp.maximum(m_prev, jnp.max(z, axis=1, keepdims=True))
        p = jnp.exp(z - jnp.concatenate([m_new] * (TK_FAR // LANES), axis=1))
        as_ref[...] = jnp.exp(m_prev - m_new) * as_ref[...] + _dot_nt(p.astype(BF16), va_ref[:, pl.ds(k0, TK_FAR)])
        ms_ref[...] = m_new

    n_far = (jnp.maximum(i - 1, 0) * TQ + TK_FAR - 1) // TK_FAR
    odd = n_far % 2
    last = n_far - 1

    @pl.when(n_far > 0)
    def _():
        za_ref[...] = far_logits(-odd)

        def far_body(jj, carry):
            t = 2 * jj - odd
            zb_ref[...] = far_logits(t + 1)
            far_update(t, za_ref[...])
            za_ref[...] = far_logits(jnp.minimum(t + 2, last))
            far_update(t + 1, zb_ref[...])
            return carry

        lax.fori_loop(0, (n_far + 1) // 2, far_body, 0)

    gt = g_ref[...]
    lane = lax.broadcasted_iota(jnp.int32, (TQ, LANES), 1)
    own = [jnp.where((lane >= N_BRANCHES * g) & (lane < N_BRANCHES * (g + 1)), gt, 0.0) for g in range(NSA_GROUP)]
    gt4 = jnp.concatenate(own, axis=0)
    hi = gt4.astype(BF16)
    lo = (gt4 - hi.astype(F32)).astype(BF16)
    gates = _dot(jnp.concatenate([hi, lo], axis=1), gsel_ref[...])

    def scaled(acc, gate):
        f = gate / jnp.maximum(acc, 1e-30)
        return acc * pltpu.roll(f, HEAD_DIM, 1)

    o_sw = scaled(as_ref[...], gates[:, LANES:2 * LANES]) + scaled(a_w, gates[:, 2 * LANES:])
    oc = oc_ref[0]
    outs = []
    for g in range(NSA_GROUP):
        sl = slice(g * TQ, (g + 1) * TQ)
        outs.append(gates[sl, 0:HEAD_DIM] * oc[:, g * HEAD_DIM:(g + 1) * HEAD_DIM] + o_sw[sl, 0:HEAD_DIM])
    o_ref[0] = jnp.concatenate(outs, axis=1).astype(BF16)


def _selwin_prompt(q, sn, kvs_t, kvw_t, onehot_t, gates, oc, nbn, nbw):
    B, T, _ = q.shape
    src = jnp.arange(2 * LANES) % LANES
    dst = jnp.arange(N_BRANCHES * LANES) // LANES
    gsel = ((src[:, None] < NSA_GROUP * N_BRANCHES) & (src[:, None] % N_BRANCHES == dst[None, :])).astype(BF16)
    nq = T // TQ
    R4 = NSA_GROUP * TQ
    ncols = T + KPAD
    k_of = pl.BlockSpec((1, HEAD_DIM, T), lambda b, h, i: (b, h, 0))
    v_of = pl.BlockSpec((1, HEAD_DIM, T), lambda b, h, i: (b, NSA_KV_HEADS + h, 0))
    perh = lambda a: pl.BlockSpec((1,) + a.shape[1:], lambda b, h, i: (h, 0, 0))
    qtile = pl.BlockSpec((1, TQ, 256), lambda b, h, i: (b, i, h))
    return pl.pallas_call(
        _selwin_prompt_kernel,
        grid=(B, NSA_KV_HEADS, nq),
        in_specs=[qtile, pl.BlockSpec((1, 1, TQ, sn.shape[-1]), lambda b, h, i: (b, h, i, 0)),
                  k_of, v_of, k_of, v_of, pl.BlockSpec(onehot_t.shape, lambda b, h, i: (0, 0)),
                  pl.BlockSpec((TQ, LANES), lambda b, h, i: (b * nq + i, h)), qtile, perh(nbn), perh(nbw),
                  pl.BlockSpec(gsel.shape, lambda b, h, i: (0, 0))],
        out_specs=qtile,
        out_shape=jax.ShapeDtypeStruct((B, T, NSA_WIDTH), BF16),
        scratch_shapes=[pltpu.VMEM((LANES + HEAD_DIM, ncols), BF16)] + [pltpu.VMEM((LANES, ncols), BF16)] * 3
        + [pltpu.VMEM((R4, LANES), F32)] * 2 + [pltpu.VMEM((R4, TK_FAR), F32)] * 2,
        compiler_params=_cparams(("parallel", "parallel", "arbitrary")),
        name="selwin_prompt",
    )(q, sn, kvs_t, kvs_t, kvw_t, kvw_t, onehot_t, gates, oc, nbn, nbw, gsel)


def _rel_bucket(dist):
    d = jnp.maximum(dist, 0)
    n_exact = N_BUCKETS // 2
    d_f = jnp.maximum(d, 1).astype(F32)
    large = n_exact + (jnp.log(d_f / n_exact) / math.log(MAX_DISTANCE / n_exact)
                       * (N_BUCKETS - n_exact)).astype(jnp.int32)
    large = jnp.minimum(large, N_BUCKETS - 1)
    return jnp.where(d < n_exact, d, large)


def _head_bias(rel_bias, dist):
    onehot = (_rel_bucket(dist)[..., None] == jnp.arange(N_BUCKETS)).astype(F32)
    return jnp.einsum('...k,kh->...h', onehot, rel_bias.astype(F32), precision=lax.Precision.HIGHEST)


def _prompt_bias_tables(rel_bias):
    far = rel_bias[N_BUCKETS - 1].astype(F32)
    cfar = jnp.broadcast_to(far[:, None, None], (NSA_HEADS, SUBLANES, TQC))
    k = jnp.arange(CMP_NEAR)[:, None]
    tt = jnp.arange(TQC)[None, :]
    d = tt + CMP_PAD * CMP_STRIDE - (CMP_BLOCK - 1) - CMP_STRIDE * k
    nbc = jnp.where((d >= 0)[..., None], _head_bias(rel_bias, d), NEG).transpose(2, 0, 1)
    tq = jnp.arange(TQ)[:, None]

    def tile_table(first_dist, n_keys, max_dist):
        d = first_dist + tq - jnp.arange(n_keys)[None, :]
        b = jnp.where(((d >= 0) & (d < max_dist))[..., None], _head_bias(rel_bias, d) - far, NEG)
        return b.transpose(2, 0, 1).reshape(NSA_KV_HEADS, NSA_GROUP * TQ, n_keys)

    nbn = tile_table(TQ, 2 * TQ, 2 * TQ + 1)
    nbw = tile_table(WINDOW, WINDOW + TQ, WINDOW)
    return cfar, nbc, nbn, nbw


def _prep_layer(l, w_in, w_out, norm_mix, norm_ffn, w_pool, pool_scale, cmp_w, cmp_pe, gmlp_ws, gmlp_b,
                gmlp_norm, w_up, w_down):
    w = w_in[l]
    offs = np.cumsum([0] + [n for _, n in PROJ_SEGS])
    seg = {name: w[:, offs[k]:offs[k + 1]] for k, (name, _) in enumerate(PROJ_SEGS)}
    order = ('xp', 'q', 'kvc', 'kvs', 'kvw', 'u', 'v')
    ng = NSA_GROUP * N_BRANCHES
    gpad = jnp.zeros((D_MODEL, LANES - ng), w.dtype)
    gate_cols = [c for h in range(NSA_KV_HEADS) for c in (seg['gt'][:, h * ng:(h + 1) * ng], gpad)]
    w_perm = jnp.concatenate([seg[n] for n in order] + gate_cols, axis=1)
    eye = jnp.eye(len(POOL_WINDOWS), dtype=F32)
    wp = jnp.einsum('gcd,gh->gchd', w_pool[l], eye).reshape(POOL_WIDTH, POOL_WIDTH)
    wx = jnp.broadcast_to(cmp_w[l].transpose(1, 0, 2)[..., None], (CMP_BLOCK, 2, NSA_KV_HEADS, HEAD_DIM))
    return dict(
        w_in=w_perm.astype(BF16), g_mix=norm_mix[l][None, :],
        w_kv_t=jnp.concatenate([seg['kvc'], seg['kvs'], seg['kvw']], axis=1).T.astype(BF16),
        g_ffn=norm_ffn[l][None, :],
        gn=gmlp_norm[l][None, :], wp=wp.astype(BF16), ps=pool_scale[l][None, :],
        wx=wx.reshape(CMP_BLOCK, KV_ROW), pe=cmp_pe[l].transpose(1, 0, 2, 3).reshape(CMP_BLOCK, KV_ROW),
        ws=gmlp_ws[l], wb=jnp.repeat(gmlp_b[l].T, POOL_GROUP_DIM, axis=1),
        wsx=jnp.repeat(gmlp_ws[l].transpose(1, 2, 0), POOL_GROUP_DIM, axis=2)[:8, :8],
        w_out=w_out[l].astype(BF16), w_up=w_up[l].astype(BF16), w_down=w_down[l].astype(BF16))


def _split_heads(kv, part):
    B, T, _ = kv.shape
    return kv.reshape(B, T, 2, NSA_KV_HEADS, HEAD_DIM)[:, :, part].transpose(0, 2, 1, 3).astype(BF16)


def _prompt_layer(x, p, tabs, B, T, final, gf):
    cfar, nbc, nbn, nbw = tabs
    tm = min(512, B * T)
    xp, q, gt, u, v, kvc, kvc_t, kvs_t, kvw_t = _inproj(x, p['g_mix'], p['w_in'], p['gn'], tm, seq=T,
                                                        wt=p['w_kv_t'])
    comp = _compress_prompt(kvc.reshape(B, T, KV_ROW), p['wx'], p['pe'])
    kc = _split_heads(comp, 0)
    vct = _split_heads(comp, 1).transpose(0, 1, 3, 2)
    q3 = q.reshape(B, T, NSA_WIDTH)
    oc, sn = _cmp_prompt(q3, kc, vct, nbc, cfar)
    pos = jnp.arange(-KPAD, T)
    blk_of = jnp.where(pos >= 0, pos // SEL_BLOCK, LANES - 1)
    onehot_t = (jnp.arange(LANES)[:, None] == blk_of[None, :]).astype(BF16)
    yb = _selwin_prompt(q3, sn, kvs_t, kvw_t, onehot_t, gt, oc, nbn, nbw)
    x1 = _mix_prompt(x, xp, yb.reshape(B * T, NSA_WIDTH), u, v, p['wp'], p['ps'], p['ws'], p['wb'], p['w_out'],
                     tm, T)
    x2 = _ffn(x1, p['g_ffn'], p['w_up'], p['w_down'], gf, tm, final)
    return x2, (xp, kvc_t, kvs_t, kvw_t)


PAGES_PER_STEP = 16
SEL_PER_PAGE = PAGE_SIZE // SEL_BLOCK


def _cmp_sample_kernel(pt_ref, *refs, ts):
    del pt_ref
    npg = PAGES_PER_STEP
    nhalf = KV_ROW // LANES
    pages = refs[:npg]
    wx_ref, pe_ref, q_ref, tab_ref, oc_ref, sel_ref, lo_ref, hi_ref, ps_ref, pg_ref = refs[npg:]
    s = pl.program_id(1)
    rows = PAGE_SIZE // CMP_STRIDE
    for k in range(npg):
        for e in range(nhalf):
            pg_ref[...] = _page_half_rows(pages[k], e)
            lo, hi = _compress_rows(lambda j: pg_ref[pl.ds(j, rows, stride=CMP_STRIDE), :], rows,
                                    wx_ref[:, e * LANES:(e + 1) * LANES])
            lo_ref[s, k * rows:(k + 1) * rows, e * LANES:(e + 1) * LANES] = lo
            hi_ref[s, k * rows:(k + 1) * rows, e * LANES:(e + 1) * LANES] = hi

    @pl.when(s == pl.num_programs(1) - 1)
    def _():
        nc = lo_ref.shape[0] * lo_ref.shape[1]
        comp = _combine_halves(lo_ref[...].reshape(nc, KV_ROW), hi_ref[...].reshape(nc, KV_ROW), pe_ref[...],
                               wx_ref[...])
        ns = nc // 4
        lane = lax.broadcasted_iota(jnp.int32, (ns, LANES), 1)
        s_iota = lax.broadcasted_iota(jnp.int32, (ns, LANES), 0)
        ncol = NSA_GROUP * ts
        selacc = jnp.zeros((ns, LANES), F32)
        for h in range(NSA_KV_HEADS):
            kc = comp[:, h * HEAD_DIM:(h + 1) * HEAD_DIM].astype(BF16)
            vc = comp[:, (NSA_KV_HEADS + h) * HEAD_DIM:(NSA_KV_HEADS + h + 1) * HEAD_DIM].astype(BF16)
            pc = _softmax_rows(_dot_nt(kc, q_ref[0, h]) + tab_ref[h])
            oc_ref[0, h] = pl.dot(pc.astype(BF16), vc, trans_a=True)[0:ncol, :]
            ps = pc
            for g in range(1, NSA_GROUP):
                ps = ps + pltpu.roll(pc, g * ts, 1)
            ps_ref[...] = ps
            p_slc = ps_ref[pl.ds(0, ns, stride=4), :]
            for j in range(1, 4):
                p_slc = p_slc + ps_ref[pl.ds(j, ns, stride=4), :]
            forced = (s_iota == 0) | (s_iota == ns - 1)
            sel = _select_topk(jnp.where(forced, FORCE_SCORE, p_slc), N_SELECT - 1)
            for g in range(NSA_GROUP):
                dst = h * ncol + g * ts
                shift = (dst - (ncol - ts)) % LANES
                moved = pltpu.roll(sel, shift, 1) if shift else sel
                selacc = jnp.where((lane >= dst) & (lane < dst + ts), moved, selacc)
        sel_ref[0] = selacc.reshape(sel_ref.shape[1:])


def _page_half_rows(page_ref, e):
    t = page_ref[0, e]
    return t.reshape(NSA_KV_HEADS * HEAD_DIM, t.shape[-1]).T


def _native_rows(a, lead):
    nd = a.ndim
    t = jnp.transpose(a, tuple(range(nd - 4)) + (nd - 3, nd - 2, nd - 1, nd - 4))
    return t.reshape((lead,) + t.shape[nd - 4:])


def _page_specs(n_pages):
    def spec(k):
        return pl.BlockSpec((1, 2, NSA_KV_HEADS, HEAD_DIM, PAGE_SIZE),
                            lambda b, s, pt: (pt[b * n_pages + s * PAGES_PER_STEP + k], 0, 0, 0, 0))
    return [spec(k) for k in range(PAGES_PER_STEP)]


def _cmp_sample(pt, cache, wx, pe, qs, tab, nb, n_pages, ts):
    steps = n_pages // PAGES_PER_STEP
    nc = n_pages * PAGE_SIZE // CMP_STRIDE
    rows = PAGES_PER_STEP * PAGE_SIZE // CMP_STRIDE
    ns = nc // 4
    full = lambda a: pl.BlockSpec(a.shape, lambda b, s, pt: (0,) * a.ndim)
    grid_spec = pltpu.PrefetchScalarGridSpec(
        num_scalar_prefetch=1,
        grid=(nb, steps),
        in_specs=_page_specs(n_pages) + [
            full(wx), full(pe),
            pl.BlockSpec((1, NSA_KV_HEADS, LANES, HEAD_DIM), lambda b, s, pt: (b, 0, 0, 0)),
            full(tab)],
        out_specs=[pl.BlockSpec((1, NSA_KV_HEADS, NSA_GROUP * ts, HEAD_DIM), lambda b, s, pt: (b, 0, 0, 0)),
                   pl.BlockSpec((1, steps, ns // steps, LANES), lambda b, s, pt: (b, 0, 0, 0))],
        scratch_shapes=[pltpu.VMEM((steps, rows, KV_ROW), F32), pltpu.VMEM((steps, rows, KV_ROW), F32),
                        pltpu.VMEM((nc, LANES), F32), pltpu.VMEM((PAGE_SIZE, LANES), F32)])
    return pl.pallas_call(
        functools.partial(_cmp_sample_kernel, ts=ts),
        grid_spec=grid_spec,
        out_shape=[jax.ShapeDtypeStruct((nb, NSA_KV_HEADS, NSA_GROUP * ts, HEAD_DIM), F32),
                   jax.ShapeDtypeStruct((nb, steps, ns // steps, LANES), F32)],
        compiler_params=_cparams(("parallel", "arbitrary")),
        name="cmp_sample",
    )(pt, *([cache] * PAGES_PER_STEP), wx, pe, qs, tab)


def _local_softmax(z, mask):
    m = jnp.max(z, axis=0, keepdims=True)
    p = jnp.where(mask, jnp.exp(z - m), 0.0)
    return m, p, jnp.sum(p, axis=0, keepdims=True)


def _selwin_sample_kernel(pt_ref, *refs, ts):
    del pt_ref
    npg = PAGES_PER_STEP
    pages = refs[:npg]
    (qa_ref, sel_ref, cf_ref, nbl_ref, st_ref, kvn_ref, nbn_ref, nbw_ref, g_ref, oc_ref,
     o_ref, m_ref, l_ref, a_ref) = refs[npg:]
    s = pl.program_id(1)
    last = pl.num_programs(1) - 1
    ncol = NSA_KV_HEADS * NSA_GROUP * ts
    qa = qa_ref[0]
    rowi = lax.broadcasted_iota(jnp.int32, (PAGE_SIZE, LANES), 0)
    for k in range(npg):
        pg = jnp.concatenate([_page_half_rows(pages[k], 0), _page_half_rows(pages[k], 1)],
                             axis=1).astype(BF16)
        is_last_page = jnp.logical_and(s == last, k == npg - 1)
        bias = jnp.where(is_last_page, nbl_ref[...], cf_ref[0:1, :])
        s0 = sel_ref[0, s, SEL_PER_PAGE * k:SEL_PER_PAGE * k + 1, :]
        s1 = sel_ref[0, s, SEL_PER_PAGE * k + 1:SEL_PER_PAGE * k + 2, :]
        mask = jnp.where(rowi < SEL_BLOCK, s0, s1) > 0.5
        z = jnp.where(mask, _dot(pg, qa) + bias, NEG)
        m, p, l = _local_softmax(z, mask)
        m_ref[s, k:k + 1, :] = m
        l_ref[s, k:k + 1, :] = l
        a_ref[s, k] = pl.dot(p.astype(BF16), pg, trans_a=True)[0:ncol, :]

    @pl.when(s == last)
    def _():
        nbn = nbn_ref[...]
        okn = nbn > 0.5 * NEG
        kn = kvn_ref[0, 0:SUBLANES, :].astype(BF16)
        m_n, p_n, l_n = _local_softmax(jnp.where(okn, _dot(kn, qa) + nbn, NEG), okn)
        a_n = pl.dot(p_n.astype(BF16), kn, trans_a=True)[0:ncol, :]
        nparts = m_ref.shape[0] * m_ref.shape[1]
        m_all = m_ref[...].reshape(nparts, LANES)
        l_all = l_ref[...].reshape(nparts, LANES)
        m_g = jnp.maximum(jnp.max(m_all, axis=0, keepdims=True), m_n)
        w_all = jnp.exp(m_all - m_g)
        w_n = jnp.exp(m_n - m_g)
        l_g = jnp.sum(w_all * l_all, axis=0, keepdims=True) + w_n * l_n

        nbw = nbw_ref[...]
        okw = nbw > 0.5 * NEG
        st = jnp.concatenate([_page_half_rows(st_ref, 0), _page_half_rows(st_ref, 1)],
                             axis=1).astype(BF16)
        kwn = kvn_ref[0, SUBLANES:2 * SUBLANES, :].astype(BF16)
        zw = jnp.where(okw, _dot(st, qa) + nbw, NEG)
        zwn = jnp.where(okn, _dot(kwn, qa) + nbn, NEG)
        m_w = jnp.maximum(jnp.max(zw, axis=0, keepdims=True), jnp.max(zwn, axis=0, keepdims=True))
        pw = jnp.where(okw, jnp.exp(zw - m_w), 0.0)
        pwn = jnp.where(okn, jnp.exp(zwn - m_w), 0.0)
        l_w = jnp.sum(pw, axis=0, keepdims=True) + jnp.sum(pwn, axis=0, keepdims=True)
        a_w = (pl.dot(pw.astype(BF16), st, trans_a=True) + pl.dot(pwn.astype(BF16), kwn, trans_a=True))[0:ncol, :]

        assert nparts + 4 * SUBLANES <= LANES
        rows8 = lambda r: jnp.broadcast_to(r, (SUBLANES, LANES))
        x = jnp.concatenate([w_all, rows8(w_n), rows8(l_g), rows8(l_w),
                             jnp.zeros((LANES - nparts - 3 * SUBLANES, LANES), F32)], axis=0)
        xt = x.T
        acc = xt[0:ncol, nparts:nparts + 1] * a_n
        for pidx in range(nparts):
            acc = acc + xt[0:ncol, pidx:pidx + 1] * a_ref[pidx // npg, pidx % npg]
        o_s = acc / jnp.maximum(xt[0:ncol, nparts + SUBLANES:nparts + SUBLANES + 1], 1e-30)
        o_w = a_w / jnp.maximum(xt[0:ncol, nparts + 2 * SUBLANES:nparts + 2 * SUBLANES + 1], 1e-30)

        ri = lax.broadcasted_iota(jnp.int32, (ncol, HEAD_DIM), 0)
        v_of = lambda t: jnp.where(ri < ncol // NSA_KV_HEADS, t[:, 2 * HEAD_DIM:3 * HEAD_DIM],
                                   t[:, 3 * HEAD_DIM:4 * HEAD_DIM])
        gt = g_ref[0]
        oc = oc_ref[0].reshape(ncol, HEAD_DIM)
        o_ref[0] = gt[:, 0:1] * oc + gt[:, 1:2] * v_of(o_s) + gt[:, 2:3] * v_of(o_w)


def _selwin_sample(pt, cache, qa, sel, cfrow, nbl, state, st_off, kvn, nbn, nbw, gates, oc, nb, n_pages, ts):
    steps = n_pages // PAGES_PER_STEP
    ncol = NSA_KV_HEADS * NSA_GROUP * ts
    full = lambda a: pl.BlockSpec(a.shape, lambda b, s, pt: (0,) * a.ndim)
    perb = lambda a: pl.BlockSpec((1,) + a.shape[1:], lambda b, s, pt: (b,) + (0,) * (a.ndim - 1))
    grid_spec = pltpu.PrefetchScalarGridSpec(
        num_scalar_prefetch=1,
        grid=(nb, steps),
        in_specs=_page_specs(n_pages) + [
            perb(qa), perb(sel), full(cfrow), full(nbl),
            pl.BlockSpec((1,) + state.shape[1:], lambda b, s, pt: (st_off + b, 0, 0, 0, 0)),
            perb(kvn), full(nbn), full(nbw), perb(gates), perb(oc)],
        out_specs=pl.BlockSpec((1, ncol, HEAD_DIM), lambda b, s, pt: (b, 0, 0)),
        scratch_shapes=[pltpu.VMEM((steps, PAGES_PER_STEP, LANES), F32),
                        pltpu.VMEM((steps, PAGES_PER_STEP, LANES), F32),
                        pltpu.VMEM((steps, PAGES_PER_STEP, ncol, KV_ROW), F32)])
    return pl.pallas_call(
        functools.partial(_selwin_sample_kernel, ts=ts),
        grid_spec=grid_spec,
        out_shape=jax.ShapeDtypeStruct((nb, ncol, HEAD_DIM), F32),
        compiler_params=_cparams(("parallel", "arbitrary")),
        name="selwin_sample",
    )(pt, *([cache] * PAGES_PER_STEP), qa, sel, cfrow, nbl, state, kvn, nbn, nbw, gates, oc)


def _sample_bias_tables(rel_bias, past, ts, n_buf):
    ncol = NSA_KV_HEADS * NSA_GROUP * ts
    col = jnp.arange(ncol)
    head = col // ts
    t = col % ts
    pick = lambda b: jnp.take_along_axis(b, jnp.broadcast_to(head, b.shape[:-1])[..., None], axis=-1)[..., 0]
    bias = lambda d: pick(_head_bias(rel_bias, d))
    padc = lambda a, fill=0.0: jnp.pad(a, ((0, 0), (0, LANES - a.shape[1])), constant_values=fill)
    nc = past // CMP_STRIDE
    c_end = jnp.arange(nc)[:, None] * CMP_STRIDE + CMP_BLOCK - 1
    tab = jnp.where(jnp.arange(nc)[:, None] < nc - 1, bias(past + t[None, :] - c_end), NEG)
    hc = ncol // NSA_KV_HEADS
    tab_cmp = jnp.stack([padc(tab[:, h * hc:(h + 1) * hc]) for h in range(NSA_KV_HEADS)])
    far = rel_bias[N_BUCKETS - 1].astype(F32)[head]
    cfrow = jnp.broadcast_to(padc(far[None, :]), (SUBLANES, LANES))
    kk = jnp.arange(PAGE_SIZE)[:, None]
    nbl = padc(bias(PAGE_SIZE + t[None, :] - kk))
    j = jnp.arange(SUBLANES)[:, None]
    dn = t[None, :] - j
    nbn = padc(jnp.where((dn >= 0) & (j < ts), bias(dn), NEG))
    r = jnp.arange(n_buf)[:, None]
    dw = n_buf + t[None, :] - r
    nbw = padc(jnp.where((dw >= 0) & (dw < WINDOW), bias(dw), NEG))
    return tab_cmp, cfrow, nbl, nbn, nbw


def _sample_layer(x, p, tabs, l, pt, cache_cmp, cache_slc, state_win, state_pool_l, nb, ts, n_pages, final, gf):
    tab_cmp, cfrow, nbl, nbn, nbw = tabs
    past = n_pages * PAGE_SIZE
    R = ts * nb
    ncol = NSA_KV_HEADS * NSA_GROUP * ts
    xp, q, gt, u, v, kvc, kvs, kvw = _inproj(x, p['g_mix'], p['w_in'], p['gn'], R)
    q5 = q.reshape(ts, nb, NSA_KV_HEADS, NSA_GROUP, HEAD_DIM)
    qs = q5.transpose(1, 2, 3, 0, 4).reshape(nb, NSA_KV_HEADS, NSA_GROUP * ts, HEAD_DIM)
    qs = jnp.pad(qs, ((0, 0), (0, 0), (0, LANES - NSA_GROUP * ts), (0, 0)))
    oc, sel = _cmp_sample(pt, cache_cmp, p['wx'], p['pe'], qs, tab_cmp, nb, n_pages, ts)
    qt = q5.transpose(1, 2, 4, 3, 0).reshape(nb, NSA_KV_HEADS, HEAD_DIM, NSA_GROUP * ts)
    eye = jnp.eye(NSA_KV_HEADS, dtype=qt.dtype)
    qa = jnp.einsum('bhdc,hk->bhdkc', qt, eye).reshape(nb, NSA_KV_HEADS * HEAD_DIM, ncol)
    qa = jnp.pad(qa, ((0, 0), (0, KV_ROW - NSA_KV_HEADS * HEAD_DIM), (0, LANES - ncol)))
    rows_of = lambda a: jnp.pad(a.reshape(ts, nb, KV_ROW).transpose(1, 0, 2), ((0, 0), (0, SUBLANES - ts), (0, 0)))
    kvn = jnp.concatenate([rows_of(kvs), rows_of(kvw)], axis=1)
    ng = NSA_GROUP * N_BRANCHES
    g5 = jnp.concatenate([gt[:, h * LANES:h * LANES + ng] for h in range(NSA_KV_HEADS)], axis=1)
    g5 = g5.reshape(ts, nb, NSA_KV_HEADS * NSA_GROUP, N_BRANCHES)
    gates = jnp.pad(g5.transpose(1, 2, 0, 3).reshape(nb, ncol, N_BRANCHES), ((0, 0), (0, 0), (0, LANES - N_BRANCHES)))
    o = _selwin_sample(pt, cache_slc, qa, sel, cfrow, nbl, state_win, l * nb, kvn, nbn, nbw, gates, oc,
                       nb, n_pages, ts)
    yb = o.reshape(nb, NSA_KV_HEADS * NSA_GROUP, ts, HEAD_DIM).transpose(2, 0, 1, 3).reshape(R, NSA_WIDTH)
    ext = jnp.concatenate([state_pool_l.transpose(1, 0, 2), xp.reshape(ts, nb, POOL_WIDTH)], axis=0)
    x1 = _mix_sample(x, ext, yb.astype(BF16), u.reshape(ts, nb, GMLP_WIDTH), v.reshape(ts, nb, GMLP_WIDTH),
                     p['wp'], p['ps'], p['wsx'], p['wb'][:SUBLANES], p['w_out'], nb, ts, past)
    x2 = _ffn(x1, p['g_ffn'], p['w_up'], p['w_down'], gf, R, final)
    return x2, (xp, kvc, kvs, kvw, v)


def kernel(x_prompt, x_sample, cache_cmp_kv, cache_slc_kv, state_win_kv, state_pool, page_table, w_in, w_out,
           norm_mix, norm_ffn, norm_final, w_pool, pool_scale, cmp_w, cmp_pe, gmlp_ws, gmlp_b, gmlp_norm, w_up,
           w_down, rel_bias):
    B, T, _ = x_prompt.shape
    nb, ts, _ = x_sample.shape
    n_pages = page_table.shape[1]
    n_phys = cache_cmp_kv.shape[1]
    n_buf = state_win_kv.shape[2]
    past = n_pages * PAGE_SIZE
    depth = w_in.shape[0]
    assert T % TK_FAR == 0 and T >= WINDOW and n_pages % PAGES_PER_STEP == 0
    assert ts <= SUBLANES and ts <= POOL_STATE and n_buf == WINDOW and past >= WINDOW
    kv_tail = (2, NSA_KV_HEADS, HEAD_DIM)

    cache_cmp = _native_rows(cache_cmp_kv, depth * n_phys)
    cache_slc = _native_rows(cache_slc_kv, depth * n_phys)
    state_win = _native_rows(state_win_kv, depth * nb)
    ptabs = _prompt_bias_tables(rel_bias)
    stabs = _sample_bias_tables(rel_bias, past, ts, n_buf)
    gf = norm_final[None, :]
    xp = x_prompt.reshape(B * T, D_MODEL)
    xs = x_sample.transpose(1, 0, 2).reshape(ts * nb, D_MODEL)
    unmajor = lambda a: a.reshape(ts, nb, a.shape[-1]).transpose(1, 0, 2)

    outs = [[] for _ in range(9)]
    for l in range(depth):
        p = _prep_layer(l, w_in, w_out, norm_mix, norm_ffn, w_pool, pool_scale, cmp_w, cmp_pe, gmlp_ws, gmlp_b,
                        gmlp_norm, w_up, w_down)
        final = l == depth - 1
        xp, (pin, kvc_t, kvs_t, kvw_t) = _prompt_layer(xp, p, ptabs, B, T, final, gf)
        rows_of = lambda a: a.reshape((B,) + kv_tail + (a.shape[-1],)).transpose(0, 4, 1, 2, 3)
        outs[0].append(rows_of(kvc_t))
        outs[1].append(rows_of(kvs_t))
        outs[2].append(rows_of(kvw_t[:, :, T - WINDOW:]))
        outs[3].append(pin.reshape(B, T, POOL_WIDTH)[:, T - POOL_STATE:])

        pt = (page_table + l * n_phys).reshape(-1).astype(jnp.int32)
        xs, (sin, kvc_s, kvs_s, kvw_s, v_s) = _sample_layer(
            xs, p, stabs, l, pt, cache_cmp, cache_slc, state_win, state_pool[l], nb, ts, n_pages, final, gf)
        kvw_new = unmajor(kvw_s).reshape((nb, ts) + kv_tail)
        outs[4].append(unmajor(kvc_s).reshape((nb, ts) + kv_tail))
        outs[5].append(unmajor(kvs_s).reshape((nb, ts) + kv_tail))
        outs[6].append(jnp.concatenate([state_win_kv[l][:, ts:], kvw_new], axis=1))
        outs[7].append(jnp.concatenate([state_pool[l][:, ts:], unmajor(sin)], axis=1))
        outs[8].append(unmajor(v_s))

    y_prompt = xp.reshape(B, T, D_MODEL)
    y_sample = unmajor(xs)
    return (y_prompt, y_sample) + tuple(jnp.stack(o) for o in outs)
```

```python
import functools
import math

import numpy as np
import jax
import jax.numpy as jnp
from jax import lax
from jax.experimental import pallas as pl
from jax.experimental.pallas import tpu as pltpu

F32 = jnp.float32
BF16 = jnp.bfloat16

D_MODEL = 1024
DEPTH = 4
PAGE_SIZE = 128
HEAD_DIM = 64
POOL_WINDOWS = (2, 4, 8, 16)
POOL_WIDTH = 256
POOL_GROUP_DIM = 64
POOL_STATE = 15
NSA_WIDTH = 512
NSA_HEADS = 8
NSA_KV_HEADS = 2
NSA_GROUP = 4
KV_ROW = 2 * NSA_KV_HEADS * HEAD_DIM
CMP_STRIDE = 16
CMP_BLOCK = 32
SEL_BLOCK = 64
N_SELECT = 16
WINDOW = 512
N_BRANCHES = 3
FORCE_SCORE = 1000.0
GMLP_WIDTH = 256
GMLP_GROUPS = 4
GMLP_CHUNK = 128
D_FF = 4096
N_BUCKETS = 32
MAX_DISTANCE = 128
EPS = 1e-6
SCALE = HEAD_DIM ** -0.5

LANES = 128
SUBLANES = 8
VMEM_LIMIT = 56 * 1024 * 1024

NEG = -1e30
CMP_PAD = 16
TQ = 128
TQC = 512
CMP_NEAR = CMP_PAD + TQC // CMP_STRIDE
TK_FAR = 512
KPAD = 512
GATE_LANES = 2 * LANES

PROJ_SEGS = (('xp', 256), ('q', 512), ('kvc', 256), ('kvs', 256), ('kvw', 256), ('gt', 24), ('u', 256), ('v', 256))


def _cparams(sem):
    return pltpu.CompilerParams(dimension_semantics=sem, vmem_limit_bytes=VMEM_LIMIT)


def _dot(a, b):
    return jnp.dot(a, b, preferred_element_type=F32)


def _dot_nt(a, b):
    return lax.dot_general(a, b, (((1,), (1,)), ((), ())), preferred_element_type=F32)


def _gelu(x):
    c = math.sqrt(2.0 / math.pi)
    return 0.5 * x * (1.0 + jnp.tanh(c * (x + 0.044715 * (x * x * x))))


def _rms(x, g):
    return x * lax.rsqrt(jnp.mean(x * x, axis=-1, keepdims=True) + EPS) * g


def _inproj_kernel(x_ref, g_ref, w_ref, gn_ref, *refs, channel_major):
    if channel_major:
        wt_ref, xp_ref, q_ref, gt_ref, u_ref, v_ref, kvc_ref, *kv_refs = refs
    else:
        xp_ref, q_ref, gt_ref, u_ref, v_ref, kvc_ref, *kv_refs = refs
    x = x_ref[...]
    h = _rms(x, g_ref[...]).astype(BF16)

    def seg(lo, hi):
        return _dot(h, w_ref[:, lo:hi])

    xp_ref[...] = seg(0, 256)
    q_ref[...] = (seg(256, 768) * SCALE).astype(BF16)
    kvc_ref[...] = seg(768, 1024)
    if channel_major:
        for k, t_ref in enumerate(kv_refs):
            t_ref[0] = _dot_nt(wt_ref[k * KV_ROW:(k + 1) * KV_ROW, :], h)
    else:
        kv_refs[0][...] = seg(1024, 1280)
        kv_refs[1][...] = seg(1280, 1536)
    u_ref[...] = _gelu(seg(1536, 1792))
    gv = _gelu(seg(1792, 2048))
    sq = gv * gv
    lane = lax.broadcasted_iota(jnp.int32, sq.shape, 1)
    ms = jnp.zeros_like(sq)
    for g in range(GMLP_GROUPS):
        in_g = (lane >= g * 64) & (lane < (g + 1) * 64)
        s = jnp.sum(jnp.where(in_g, sq, 0.0), axis=-1, keepdims=True) * (1.0 / 64.0)
        ms = jnp.where(in_g, s, ms)
    v_ref[...] = gv * lax.rsqrt(ms + EPS) * gn_ref[...]
    gt_ref[...] = jax.nn.sigmoid(seg(2048, 2048 + GATE_LANES))


def _inproj(x, g, w, gn, tm, seq=None, wt=None):
    R = x.shape[0]
    f = lambda n, dt=F32: jax.ShapeDtypeStruct((R, n), dt)
    row = lambda n: pl.BlockSpec((tm, n), lambda i: (i, 0))
    full = lambda a: pl.BlockSpec(a.shape, lambda i: (0,) * a.ndim)
    out_specs = [row(256), row(512), row(GATE_LANES), row(256), row(256), row(256)]
    out_shape = [f(256), f(512, BF16), f(GATE_LANES), f(256), f(256), f(256)]
    if seq is None:
        out_specs += [row(256), row(256)]
        out_shape += [f(256), f(256)]
    else:
        per_seq = seq // tm
        out_specs += [pl.BlockSpec((1, KV_ROW, tm), lambda i: (i // per_seq, 0, i % per_seq))] * 3
        out_shape += [jax.ShapeDtypeStruct((R // seq, KV_ROW, seq), F32)] * 3
    args = (x, g, w, gn) if seq is None else (x, g, w, gn, wt)
    return pl.pallas_call(
        functools.partial(_inproj_kernel, channel_major=seq is not None),
        grid=(R // tm,),
        in_specs=[row(D_MODEL)] + [full(a) for a in args[1:]],
        out_specs=out_specs,
        out_shape=out_shape,
        compiler_params=_cparams(("parallel",)),
        name="inproj",
    )(*args)


def _out_proj(x, ya, yb, yc, wo_ref):
    acc = _dot(ya.astype(BF16), wo_ref[0:256, :])
    acc = acc + _dot(yb, wo_ref[256:768, :])
    acc = acc + _dot(yc.astype(BF16), wo_ref[768:1024, :])
    return x + acc


def _pool_tail(win2, win4, win8, win16, cur, cnt, wp_ref, ps_ref):
    lane = lax.broadcasted_iota(jnp.int32, cur.shape, 1)
    win = jnp.where(lane < 64, win2, jnp.where(lane < 128, win4, jnp.where(lane < 192, win8, win16)))
    pooled = win / cnt - cur
    return _dot(pooled.astype(BF16), wp_ref[...]) * ps_ref[...]


def _mix_prompt_kernel(x_ref, xp_ref, halo_ref, yb_ref, u_ref, v_ref, wp_ref, ps_ref, ws_ref, wb_ref, wo_ref,
                       o_ref, ext_ref, *, tm, seq):
    i = pl.program_id(0)
    t0 = (i * tm) % seq
    halo = jnp.where(t0 > 0, halo_ref[...], 0.0)
    cur = xp_ref[...]
    ext_ref[0:16, :] = halo
    ext_ref[16:16 + tm, :] = cur
    e = ext_ref[...]
    b2 = e[1:] + e[:-1]
    b4 = b2[2:] + b2[:-2]
    b8 = b4[4:] + b4[:-4]
    b16 = b8[8:] + b8[:-8]
    win2 = b2[15:15 + tm]
    win4 = b4[13:13 + tm]
    win8 = b8[9:9 + tm]
    win16 = b16[1:1 + tm]
    rowi = lax.broadcasted_iota(jnp.int32, cur.shape, 0)
    lane = lax.broadcasted_iota(jnp.int32, cur.shape, 1)
    wsz = jnp.where(lane < 64, 2, jnp.where(lane < 128, 4, jnp.where(lane < 192, 8, 16)))
    cnt = jnp.minimum(t0 + rowi + 1, wsz).astype(F32)
    ya = _pool_tail(win2, win4, win8, win16, cur, cnt, wp_ref, ps_ref)

    ci = lax.broadcasted_iota(jnp.int32, (GMLP_CHUNK, GMLP_CHUNK), 0)
    cj = lax.broadcasted_iota(jnp.int32, (GMLP_CHUNK, GMLP_CHUNK), 1)
    lane_c = lax.broadcasted_iota(jnp.int32, (GMLP_CHUNK, GMLP_WIDTH), 1)
    wts = [jnp.where(ci >= cj, ws_ref[g], 0.0).astype(BF16) for g in range(GMLP_GROUPS)]
    parts = []
    for c in range(tm // GMLP_CHUNK):
        vc = v_ref[c * GMLP_CHUNK:(c + 1) * GMLP_CHUNK, :].astype(BF16)
        s = jnp.zeros((GMLP_CHUNK, GMLP_WIDTH), F32)
        for g in range(GMLP_GROUPS):
            sg = _dot(wts[g], vc)
            s = jnp.where((lane_c >= g * 64) & (lane_c < (g + 1) * 64), sg, s)
        parts.append(u_ref[c * GMLP_CHUNK:(c + 1) * GMLP_CHUNK, :] * (s + wb_ref[...]))
    yc = jnp.concatenate(parts, axis=0) if len(parts) > 1 else parts[0]

    o_ref[...] = _out_proj(x_ref[...], ya, yb_ref[...], yc, wo_ref)


def _mix_prompt(x, xp, yb, u, v, wp, ps, ws, wb, wo, tm, seq):
    R = x.shape[0]
    row = lambda n: pl.BlockSpec((tm, n), lambda i: (i, 0))
    full = lambda a: pl.BlockSpec(a.shape, lambda i: (0,) * a.ndim)
    halo = pl.BlockSpec((16, 256), lambda i: (jnp.maximum(i * (tm // 16) - 1, 0), 0))
    return pl.pallas_call(
        functools.partial(_mix_prompt_kernel, tm=tm, seq=seq),
        grid=(R // tm,),
        in_specs=[row(D_MODEL), row(256), halo, row(512), row(256), row(256),
                  full(wp), full(ps), full(ws), full(wb), full(wo)],
        out_specs=row(D_MODEL),
        out_shape=jax.ShapeDtypeStruct((R, D_MODEL), F32),
        scratch_shapes=[pltpu.VMEM((tm + 16, 256), F32)],
        compiler_params=_cparams(("parallel",)),
        name="mix_prompt",
    )(x, xp, xp, yb, u, v, wp, ps, ws, wb, wo)


def _mix_sample_kernel(x_ref, ext_ref, yb_ref, u_ref, v_ref, wp_ref, ps_ref, wsx_ref, wb_ref, wo_ref, o_ref,
                       *, nb, ts, past):
    lane = lax.broadcasted_iota(jnp.int32, (nb, POOL_WIDTH), 1)
    wsz = jnp.where(lane < 64, 2, jnp.where(lane < 128, 4, jnp.where(lane < 192, 8, 16)))
    yas, ycs = [], []
    for t in range(ts):
        top = POOL_STATE + t
        acc = ext_ref[top] + ext_ref[top - 1]
        wins = [acc]
        for w in (4, 8, 16):
            for j in range(w // 2, w):
                acc = acc + ext_ref[top - j]
            wins.append(acc)
        cnt = jnp.minimum(past + t + 1, wsz).astype(F32)
        yas.append(_pool_tail(wins[0], wins[1], wins[2], wins[3], ext_ref[top], cnt, wp_ref, ps_ref))
        s = wb_ref[t:t + 1, :]
        s = jnp.broadcast_to(s, (nb, GMLP_WIDTH))
        for j in range(t + 1):
            s = s + wsx_ref[t, j:j + 1, :] * v_ref[j]
        ycs.append(u_ref[t] * s)
    ya = jnp.concatenate(yas, axis=0)
    yc = jnp.concatenate(ycs, axis=0)
    o_ref[...] = _out_proj(x_ref[...], ya, yb_ref[...], yc, wo_ref)


def _mix_sample(x, ext, yb, u, v, wp, ps, wsx, wb4, wo, nb, ts, past):
    R = x.shape[0]
    args = (x, ext, yb, u, v, wp, ps, wsx, wb4, wo)
    full = lambda a: pl.BlockSpec(a.shape, lambda i: (0,) * a.ndim)
    return pl.pallas_call(
        functools.partial(_mix_sample_kernel, nb=nb, ts=ts, past=past),
        grid=(1,),
        in_specs=[full(a) for a in args],
        out_specs=pl.BlockSpec((R, D_MODEL), lambda i: (0, 0)),
        out_shape=jax.ShapeDtypeStruct((R, D_MODEL), F32),
        compiler_params=_cparams(("arbitrary",)),
        name="mix_sample",
    )(*args)


def _ffn_kernel(x_ref, g_ref, wu_ref, wd_ref, gf_ref, o_ref, *, final, fc):
    x = x_ref[...]
    h = _rms(x, g_ref[...]).astype(BF16)
    acc = x
    for c in range(D_FF // fc):
        a = jnp.maximum(_dot(h, wu_ref[:, c * fc:(c + 1) * fc]), 0.0)
        acc = acc + _dot((a * a).astype(BF16), wd_ref[c * fc:(c + 1) * fc, :])
    if final:
        acc = _rms(acc, gf_ref[...])
    o_ref[...] = acc


def _ffn(x, g, wu, wd, gf, tm, final):
    R = x.shape[0]
    row = pl.BlockSpec((tm, D_MODEL), lambda i: (i, 0))
    full = lambda a: pl.BlockSpec(a.shape, lambda i: (0,) * a.ndim)
    wfull = lambda a: pl.BlockSpec(a.shape, lambda i: (0,) * a.ndim, pipeline_mode=pl.Buffered(1))
    return pl.pallas_call(
        functools.partial(_ffn_kernel, final=final, fc=1024),
        grid=(R // tm,),
        in_specs=[row, full(g), wfull(wu), wfull(wd), full(gf)],
        out_specs=row,
        out_shape=jax.ShapeDtypeStruct((R, D_MODEL), F32),
        compiler_params=_cparams(("parallel",)),
        name="ffn",
    )(x, g, wu, wd, gf)


def _compress_rows(read_rows, n, wx):
    lo = jnp.zeros((n, wx.shape[1]), F32)
    hi = jnp.zeros((n, wx.shape[1]), F32)
    for j in range(CMP_STRIDE):
        xj = read_rows(j)
        lo = lo + xj * wx[j:j + 1, :]
        hi = hi + xj * wx[CMP_STRIDE + j:CMP_STRIDE + j + 1, :]
    return lo, hi


def _combine_halves(lo, hi, pe, wx):
    n = lo.shape[0]
    pe_term = jnp.sum(pe * wx, axis=0, keepdims=True)
    comp = lo + pltpu.roll(hi, n - 1, 0) + pe_term
    rowi = lax.broadcasted_iota(jnp.int32, comp.shape, 0)
    return jnp.where(rowi < n - 1, comp, 0.0)


def _compress_prompt_kernel(x_ref, wx_ref, pe_ref, o_ref, *, n):
    wx = wx_ref[...]
    lo, hi = _compress_rows(lambda j: x_ref[0, pl.ds(j, n, stride=CMP_STRIDE), :], n, wx)
    o_ref[0, 0:CMP_PAD, :] = jnp.zeros((CMP_PAD, LANES), F32)
    o_ref[0, CMP_PAD:CMP_PAD + n, :] = _combine_halves(lo, hi, pe_ref[...], wx)


def _compress_prompt(kvc, wx, pe):
    B, T, _ = kvc.shape
    n = T // CMP_STRIDE
    half = lambda rows: pl.BlockSpec((rows, LANES), lambda b, e: (0, e))
    return pl.pallas_call(
        functools.partial(_compress_prompt_kernel, n=n),
        grid=(B, KV_ROW // LANES),
        in_specs=[pl.BlockSpec((1, T, LANES), lambda b, e: (b, 0, e)), half(CMP_BLOCK), half(CMP_BLOCK)],
        out_specs=pl.BlockSpec((1, n + CMP_PAD, LANES), lambda b, e: (b, 0, e)),
        out_shape=jax.ShapeDtypeStruct((B, n + CMP_PAD, KV_ROW), F32),
        compiler_params=_cparams(("parallel", "parallel")),
        name="compress_prompt",
    )(kvc, wx, pe)


def _select_topk(score, n_pick):
    s_iota = lax.broadcasted_iota(jnp.int32, score.shape, 0)
    big = score.shape[0]
    work = score
    for _ in range(n_pick):
        m = jnp.max(work, axis=0, keepdims=True)
        idx = jnp.min(jnp.where(work == m, s_iota, big), axis=0, keepdims=True)
        work = jnp.where(s_iota == idx, -2.0, work)
    return jnp.where((work == -2.0) & (score >= 0.0), 1.0, 0.0)


def _softmax_rows(z):
    m = jnp.maximum(jnp.max(z, axis=0, keepdims=True), 0.1 * NEG)
    p = jnp.exp(z - m)
    den = jnp.maximum(jnp.sum(p, axis=0, keepdims=True), 1e-30)
    return p * (1.0 / den)


def _cmp_prompt_kernel(q_ref, kc_ref, vct_ref, nb_ref, cf_ref, oc_ref, sn_ref, raw_ref, z_ref, ps_ref, *, ncp):
    i = pl.program_id(2)
    hh = pl.program_id(1)
    tq = q_ref.shape[1]
    r0 = pl.multiple_of(i * (tq // CMP_STRIDE), SUBLANES)
    q = q_ref[0]
    vct = vct_ref[0, 0]
    rown = lax.broadcasted_iota(jnp.int32, (CMP_NEAR, tq), 0) + r0

    def attend(rows):
        kc = kc_ref[0, 0, 0:rows, :]
        rowi = lax.broadcasted_iota(jnp.int32, (rows, tq), 0)
        far = (rowi >= CMP_PAD) & (rowi < r0)
        psum = jnp.zeros((rows, tq), F32)
        octs = []
        for g in range(NSA_GROUP):
            qg = q[:, g * HEAD_DIM:(g + 1) * HEAD_DIM]
            raw_ref[0:rows, :] = _dot_nt(kc, qg)
            cf = cf_ref[hh * NSA_GROUP + g]
            z_ref[0:rows, :] = jnp.where(far, raw_ref[0:rows, :] + cf[0:1, :], NEG)
            nb = nb_ref[hh * NSA_GROUP + g]
            zn = raw_ref[pl.ds(r0, CMP_NEAR), :] + nb
            z_ref[pl.ds(r0, CMP_NEAR), :] = jnp.where((rown >= CMP_PAD) & (nb > 0.5 * NEG), zn, NEG)
            pc = _softmax_rows(z_ref[0:rows, :])
            psum = psum + pc
            pcb = pc.astype(BF16)
            if rows < ncp:
                pcb = jnp.concatenate([pcb, jnp.zeros((ncp - rows, tq), BF16)], axis=0)
            octs.append(_dot(vct, pcb))
        oc_ref[0] = jnp.concatenate(octs, axis=0).T
        for c in range(tq // LANES):
            ps_ref[c, 0:rows, :] = psum[:, c * LANES:(c + 1) * LANES]
            if rows < ps_ref.shape[1]:
                ps_ref[c, rows:, :] = jnp.zeros((ps_ref.shape[1] - rows, LANES), F32)

    n_var = (ncp - CMP_PAD + LANES - 1) // LANES
    var = (r0 + CMP_NEAR - CMP_PAD + LANES - 1) // LANES - 1
    for v in range(n_var):
        pl.when(var == v)(functools.partial(attend, min(CMP_PAD + (v + 1) * LANES, ncp)))

    ns = LANES
    halves = []
    for c in range(tq // LANES):
        part = ps_ref[c, pl.ds(CMP_PAD, ns, stride=4), :]
        for j in range(1, 4):
            part = part + ps_ref[c, pl.ds(CMP_PAD + j, ns, stride=4), :]
        halves.append(part)
    p_slc = jnp.concatenate(halves, axis=1)
    s_iota = lax.broadcasted_iota(jnp.int32, (ns, tq), 0)
    tt = lax.broadcasted_iota(jnp.int32, (ns, tq), 1)
    cur = (i * tq + tt) // SEL_BLOCK
    forced = (s_iota == 0) | (s_iota == cur) | (s_iota == cur - 1)
    score = jnp.where(s_iota <= cur, jnp.where(forced, FORCE_SCORE, p_slc), -1.0)
    sel = _select_topk(score, N_SELECT)
    sn_ref[0, 0] = jnp.where(sel.T > 0.5, 0.0, NEG).astype(BF16)


def _cmp_prompt(q, kc, vct, nbc, cfar):
    B, T, _ = q.shape
    ncp = kc.shape[2]
    ns = LANES
    nq = T // TQC
    full = lambda a: pl.BlockSpec(a.shape, lambda b, h, i: (0,) * a.ndim)
    return pl.pallas_call(
        functools.partial(_cmp_prompt_kernel, ncp=ncp),
        grid=(B, NSA_KV_HEADS, nq),
        in_specs=[pl.BlockSpec((1, TQC, 256), lambda b, h, i: (b, i, h)),
                  pl.BlockSpec((1, 1, ncp, HEAD_DIM), lambda b, h, i: (b, h, 0, 0)),
                  pl.BlockSpec((1, 1, HEAD_DIM, ncp), lambda b, h, i: (b, h, 0, 0)),
                  full(nbc), full(cfar)],
        out_specs=[pl.BlockSpec((1, TQC, 256), lambda b, h, i: (b, i, h)),
                   pl.BlockSpec((1, 1, TQC, ns), lambda b, h, i: (b, h, i, 0))],
        out_shape=[jax.ShapeDtypeStruct((B, T, NSA_WIDTH), F32),
                   jax.ShapeDtypeStruct((B, NSA_KV_HEADS, T, ns), BF16)],
        scratch_shapes=[pltpu.VMEM((ncp, TQC), F32), pltpu.VMEM((ncp, TQC), F32),
                        pltpu.VMEM((TQC // LANES, max(ncp, CMP_PAD + 4 * LANES), LANES), F32)],
        compiler_params=_cparams(("parallel", "parallel", "parallel")),
        name="cmp_prompt",
    )(q, kc, vct, nbc, cfar)


def _selwin_prompt_kernel(q_ref, sn_ref, ks_ref, vs_ref, kw_ref, vw_ref, oh_ref, g_ref, oc_ref, nbn_ref, nbw_ref,
                          gsel_ref, o_ref, ka_ref, va_ref, kwa_ref, vwa_ref, ms_ref, as_ref, za_ref, zb_ref):
    i = pl.program_id(2)
    R4 = NSA_GROUP * TQ
    ncols = ka_ref.shape[1]

    @pl.when(i == 0)
    def _():
        zpad = jnp.zeros((HEAD_DIM, KPAD), BF16)
        ones = jnp.ones((HEAD_DIM, ncols), BF16)
        row = lax.broadcasted_iota(jnp.int32, (HEAD_DIM, ncols), 0)
        col = lax.broadcasted_iota(jnp.int32, (HEAD_DIM, ncols), 1)
        ka_ref[0:LANES, :] = oh_ref[...]
        kwa_ref[HEAD_DIM:, :] = jnp.where((row == 0) & (col < KPAD), NEG, 0.0).astype(BF16)
        for dst, src in ((ka_ref.at[LANES:], ks_ref), (va_ref.at[0:HEAD_DIM], vs_ref),
                         (kwa_ref.at[0:HEAD_DIM], kw_ref), (vwa_ref.at[0:HEAD_DIM], vw_ref)):
            dst[:, 0:KPAD] = zpad
            dst[:, KPAD:] = src[0].astype(BF16)
        va_ref[HEAD_DIM:, :] = ones
        vwa_ref[HEAD_DIM:, :] = ones

    q = q_ref[0]
    q4 = jnp.concatenate([q[:, g * HEAD_DIM:(g + 1) * HEAD_DIM] for g in range(NSA_GROUP)], axis=0)
    sn = sn_ref[0, 0]
    blk = lax.broadcasted_iota(jnp.int32, (TQ, LANES), 1)
    first_near = (i - 1) * (TQ // SEL_BLOCK)
    sn_far = jnp.where(blk >= first_near, NEG, sn.astype(F32)).astype(BF16)
    qa_near = jnp.concatenate([jnp.concatenate([sn] * NSA_GROUP, axis=0), q4], axis=1)
    qa_far = jnp.concatenate([jnp.concatenate([sn_far] * NSA_GROUP, axis=0), q4], axis=1)
    qw = jnp.concatenate([q4, jnp.ones((R4, HEAD_DIM), BF16)], axis=1)

    def one_pass(z, vt):
        m = jnp.max(z, axis=1, keepdims=True)
        acc = _dot_nt(jnp.exp(z - m).astype(BF16), vt)
        return m, acc

    kw0 = pl.multiple_of(i * TQ, TQ)
    zw = _dot(qw, kwa_ref[:, pl.ds(kw0, WINDOW + TQ)]) + nbw_ref[0]
    _, a_w = one_pass(zw, vwa_ref[:, pl.ds(kw0, WINDOW + TQ)])

    kn0 = pl.multiple_of((i - 1) * TQ + KPAD, TQ)
    zn = _dot(qa_near, ka_ref[:, pl.ds(kn0, 2 * TQ)]) + nbn_ref[0]
    m0, a0 = one_pass(zn, va_ref[:, pl.ds(kn0, 2 * TQ)])
    ms_ref[...] = jnp.broadcast_to(m0, (R4, LANES))
    as_ref[...] = a0

    def far_logits(j):
        k0 = pl.multiple_of(j * TK_FAR + KPAD, TK_FAR)
        return _dot(qa_far, ka_ref[:, pl.ds(k0, TK_FAR)])

    def far_update(j, z):
        k0 = pl.multiple_of(j * TK_FAR + KPAD, TK_FAR)
        m_prev = ms_ref[...]
        m_new = jnp.maximum(m_prev, jnp.max(z, axis=1, keepdims=True))
        p = jnp.exp(z - jnp.concatenate([m_new] * (TK_FAR // LANES), axis=1))
        as_ref[...] = jnp.exp(m_prev - m_new) * as_ref[...] + _dot_nt(p.astype(BF16), va_ref[:, pl.ds(k0, TK_FAR)])
        ms_ref[...] = m_new

    n_far = (jnp.maximum(i - 1, 0) * TQ + TK_FAR - 1) // TK_FAR
    odd = n_far % 2
    last = n_far - 1

    def far_pair(t):
        zb_ref[...] = far_logits(t + 1)
        far_update(t, za_ref[...])
        za_ref[...] = far_logits(jnp.minimum(t + 2, last))
        far_update(t + 1, zb_ref[...])

    n_pairs = (n_far + 1) // 2

    @pl.when(n_far > 0)
    def _():
        za_ref[...] = far_logits(-odd)

    @pl.when(n_pairs % 2 == 1)
    def _():
        far_pair(-odd)

    def far_body(jj, carry):
        t = 4 * jj + 2 * (n_pairs % 2) - odd
        far_pair(t)
        far_pair(t + 2)
        return carry

    lax.fori_loop(0, n_pairs // 2, far_body, 0)

    gt = g_ref[...]
    lane = lax.broadcasted_iota(jnp.int32, (TQ, LANES), 1)
    own = [jnp.where((lane >= N_BRANCHES * g) & (lane < N_BRANCHES * (g + 1)), gt, 0.0) for g in range(NSA_GROUP)]
    gt4 = jnp.concatenate(own, axis=0)
    hi = gt4.astype(BF16)
    lo = (gt4 - hi.astype(F32)).astype(BF16)
    gates = _dot(jnp.concatenate([hi, lo], axis=1), gsel_ref[...])

    def scaled(acc, gate):
        f = gate / jnp.maximum(acc, 1e-30)
        return acc * pltpu.roll(f, HEAD_DIM, 1)

    o_sw = scaled(as_ref[...], gates[:, LANES:2 * LANES]) + scaled(a_w, gates[:, 2 * LANES:])
    oc = oc_ref[0]
    outs = []
    for g in range(NSA_GROUP):
        sl = slice(g * TQ, (g + 1) * TQ)
        outs.append(gates[sl, 0:HEAD_DIM] * oc[:, g * HEAD_DIM:(g + 1) * HEAD_DIM] + o_sw[sl, 0:HEAD_DIM])
    o_ref[0] = jnp.concatenate(outs, axis=1).astype(BF16)


def _selwin_prompt(q, sn, kvs_t, kvw_t, onehot_t, gates, oc, nbn, nbw):
    B, T, _ = q.shape
    src = jnp.arange(2 * LANES) % LANES
    dst = jnp.arange(N_BRANCHES * LANES) // LANES
    gsel = ((src[:, None] < NSA_GROUP * N_BRANCHES) & (src[:, None] % N_BRANCHES == dst[None, :])).astype(BF16)
    nq = T // TQ
    R4 = NSA_GROUP * TQ
    ncols = T + KPAD
    k_of = pl.BlockSpec((1, HEAD_DIM, T), lambda b, h, i: (b, h, 0))
    v_of = pl.BlockSpec((1, HEAD_DIM, T), lambda b, h, i: (b, NSA_KV_HEADS + h, 0))
    perh = lambda a: pl.BlockSpec((1,) + a.shape[1:], lambda b, h, i: (h, 0, 0))
    qtile = pl.BlockSpec((1, TQ, 256), lambda b, h, i: (b, i, h))
    return pl.pallas_call(
        _selwin_prompt_kernel,
        grid=(B, NSA_KV_HEADS, nq),
        in_specs=[qtile, pl.BlockSpec((1, 1, TQ, sn.shape[-1]), lambda b, h, i: (b, h, i, 0)),
                  k_of, v_of, k_of, v_of, pl.BlockSpec(onehot_t.shape, lambda b, h, i: (0, 0)),
                  pl.BlockSpec((TQ, LANES), lambda b, h, i: (b * nq + i, h)), qtile, perh(nbn), perh(nbw),
                  pl.BlockSpec(gsel.shape, lambda b, h, i: (0, 0))],
        out_specs=qtile,
        out_shape=jax.ShapeDtypeStruct((B, T, NSA_WIDTH), BF16),
        scratch_shapes=[pltpu.VMEM((LANES + HEAD_DIM, ncols), BF16)] + [pltpu.VMEM((LANES, ncols), BF16)] * 3
        + [pltpu.VMEM((R4, LANES), F32)] * 2 + [pltpu.VMEM((R4, TK_FAR), F32)] * 2,
        compiler_params=_cparams(("parallel", "parallel", "arbitrary")),
        name="selwin_prompt",
    )(q, sn, kvs_t, kvs_t, kvw_t, kvw_t, onehot_t, gates, oc, nbn, nbw, gsel)


def _rel_bucket(dist):
    d = jnp.maximum(dist, 0)
    n_exact = N_BUCKETS // 2
    d_f = jnp.maximum(d, 1).astype(F32)
    large = n_exact + (jnp.log(d_f / n_exact) / math.log(MAX_DISTANCE / n_exact)
                       * (N_BUCKETS - n_exact)).astype(jnp.int32)
    large = jnp.minimum(large, N_BUCKETS - 1)
    return jnp.where(d < n_exact, d, large)


def _head_bias(rel_bias, dist):
    onehot = (_rel_bucket(dist)[..., None] == jnp.arange(N_BUCKETS)).astype(F32)
    return jnp.einsum('...k,kh->...h', onehot, rel_bias.astype(F32), precision=lax.Precision.HIGHEST)


def _prompt_bias_tables(rel_bias):
    far = rel_bias[N_BUCKETS - 1].astype(F32)
    cfar = jnp.broadcast_to(far[:, None, None], (NSA_HEADS, SUBLANES, TQC))
    k = jnp.arange(CMP_NEAR)[:, None]
    tt = jnp.arange(TQC)[None, :]
    d = tt + CMP_PAD * CMP_STRIDE - (CMP_BLOCK - 1) - CMP_STRIDE * k
    nbc = jnp.where((d >= 0)[..., None], _head_bias(rel_bias, d), NEG).transpose(2, 0, 1)
    tq = jnp.arange(TQ)[:, None]

    def tile_table(first_dist, n_keys, max_dist):
        d = first_dist + tq - jnp.arange(n_keys)[None, :]
        b = jnp.where(((d >= 0) & (d < max_dist))[..., None], _head_bias(rel_bias, d) - far, NEG)
        return b.transpose(2, 0, 1).reshape(NSA_KV_HEADS, NSA_GROUP * TQ, n_keys)

    nbn = tile_table(TQ, 2 * TQ, 2 * TQ + 1)
    nbw = tile_table(WINDOW, WINDOW + TQ, WINDOW)
    return cfar, nbc, nbn, nbw


def _prep_layer(l, w_in, w_out, norm_mix, norm_ffn, w_pool, pool_scale, cmp_w, cmp_pe, gmlp_ws, gmlp_b,
                gmlp_norm, w_up, w_down):
    w = w_in[l]
    offs = np.cumsum([0] + [n for _, n in PROJ_SEGS])
    seg = {name: w[:, offs[k]:offs[k + 1]] for k, (name, _) in enumerate(PROJ_SEGS)}
    order = ('xp', 'q', 'kvc', 'kvs', 'kvw', 'u', 'v')
    ng = NSA_GROUP * N_BRANCHES
    gpad = jnp.zeros((D_MODEL, LANES - ng), w.dtype)
    gate_cols = [c for h in range(NSA_KV_HEADS) for c in (seg['gt'][:, h * ng:(h + 1) * ng], gpad)]
    w_perm = jnp.concatenate([seg[n] for n in order] + gate_cols, axis=1)
    eye = jnp.eye(len(POOL_WINDOWS), dtype=F32)
    wp = jnp.einsum('gcd,gh->gchd', w_pool[l], eye).reshape(POOL_WIDTH, POOL_WIDTH)
    wx = jnp.broadcast_to(cmp_w[l].transpose(1, 0, 2)[..., None], (CMP_BLOCK, 2, NSA_KV_HEADS, HEAD_DIM))
    return dict(
        w_in=w_perm.astype(BF16), g_mix=norm_mix[l][None, :],
        w_kv_t=jnp.concatenate([seg['kvc'], seg['kvs'], seg['kvw']], axis=1).T.astype(BF16),
        g_ffn=norm_ffn[l][None, :],
        gn=gmlp_norm[l][None, :], wp=wp.astype(BF16), ps=pool_scale[l][None, :],
        wx=wx.reshape(CMP_BLOCK, KV_ROW), pe=cmp_pe[l].transpose(1, 0, 2, 3).reshape(CMP_BLOCK, KV_ROW),
        ws=gmlp_ws[l], wb=jnp.repeat(gmlp_b[l].T, POOL_GROUP_DIM, axis=1),
        wsx=jnp.repeat(gmlp_ws[l].transpose(1, 2, 0), POOL_GROUP_DIM, axis=2)[:8, :8],
        w_out=w_out[l].astype(BF16), w_up=w_up[l].astype(BF16), w_down=w_down[l].astype(BF16))


def _split_heads(kv, part):
    B, T, _ = kv.shape
    return kv.reshape(B, T, 2, NSA_KV_HEADS, HEAD_DIM)[:, :, part].transpose(0, 2, 1, 3).astype(BF16)


def _prompt_layer(x, p, tabs, B, T, final, gf):
    cfar, nbc, nbn, nbw = tabs
    tm = min(512, B * T)
    xp, q, gt, u, v, kvc, kvc_t, kvs_t, kvw_t = _inproj(x, p['g_mix'], p['w_in'], p['gn'], tm, seq=T,
                                                        wt=p['w_kv_t'])
    comp = _compress_prompt(kvc.reshape(B, T, KV_ROW), p['wx'], p['pe'])
    kc = _split_heads(comp, 0)
    vct = _split_heads(comp, 1).transpose(0, 1, 3, 2)
    q3 = q.reshape(B, T, NSA_WIDTH)
    oc, sn = _cmp_prompt(q3, kc, vct, nbc, cfar)
    pos = jnp.arange(-KPAD, T)
    blk_of = jnp.where(pos >= 0, pos // SEL_BLOCK, LANES - 1)
    onehot_t = (jnp.arange(LANES)[:, None] == blk_of[None, :]).astype(BF16)
    yb = _selwin_prompt(q3, sn, kvs_t, kvw_t, onehot_t, gt, oc, nbn, nbw)
    x1 = _mix_prompt(x, xp, yb.reshape(B * T, NSA_WIDTH), u, v, p['wp'], p['ps'], p['ws'], p['wb'], p['w_out'],
                     tm, T)
    x2 = _ffn(x1, p['g_ffn'], p['w_up'], p['w_down'], gf, tm, final)
    return x2, (xp, kvc_t, kvs_t, kvw_t)


PAGES_PER_STEP = 32
SEL_PER_PAGE = PAGE_SIZE // SEL_BLOCK


def _cmp_sample_kernel(pt_ref, *refs, ts):
    del pt_ref
    npg = PAGES_PER_STEP
    nhalf = KV_ROW // LANES
    pages = refs[:npg]
    wx_ref, pe_ref, q_ref, tab_ref, oc_ref, sel_ref, lo_ref, hi_ref, ps_ref, pg_ref = refs[npg:]
    s = pl.program_id(1)
    rows = PAGE_SIZE // CMP_STRIDE
    for k in range(npg):
        for e in range(nhalf):
            pg_ref[...] = _page_half_rows(pages[k], e)
            lo, hi = _compress_rows(lambda j: pg_ref[pl.ds(j, rows, stride=CMP_STRIDE), :], rows,
                                    wx_ref[:, e * LANES:(e + 1) * LANES])
            lo_ref[s, k * rows:(k + 1) * rows, e * LANES:(e + 1) * LANES] = lo
            hi_ref[s, k * rows:(k + 1) * rows, e * LANES:(e + 1) * LANES] = hi

    @pl.when(s == pl.num_programs(1) - 1)
    def _():
        nc = lo_ref.shape[0] * lo_ref.shape[1]
        comp = _combine_halves(lo_ref[...].reshape(nc, KV_ROW), hi_ref[...].reshape(nc, KV_ROW), pe_ref[...],
                               wx_ref[...])
        ns = nc // 4
        lane = lax.broadcasted_iota(jnp.int32, (ns, LANES), 1)
        s_iota = lax.broadcasted_iota(jnp.int32, (ns, LANES), 0)
        ncol = NSA_GROUP * ts
        selacc = jnp.zeros((ns, LANES), F32)
        for h in range(NSA_KV_HEADS):
            kc = comp[:, h * HEAD_DIM:(h + 1) * HEAD_DIM].astype(BF16)
            vc = comp[:, (NSA_KV_HEADS + h) * HEAD_DIM:(NSA_KV_HEADS + h + 1) * HEAD_DIM].astype(BF16)
            pc = _softmax_rows(_dot_nt(kc, q_ref[0, h]) + tab_ref[h])
            oc_ref[0, h] = pl.dot(pc.astype(BF16), vc, trans_a=True)[0:ncol, :]
            ps = pc
            for g in range(1, NSA_GROUP):
                ps = ps + pltpu.roll(pc, g * ts, 1)
            ps_ref[...] = ps
            p_slc = ps_ref[pl.ds(0, ns, stride=4), :]
            for j in range(1, 4):
                p_slc = p_slc + ps_ref[pl.ds(j, ns, stride=4), :]
            forced = (s_iota == 0) | (s_iota == ns - 1)
            sel = _select_topk(jnp.where(forced, FORCE_SCORE, p_slc), N_SELECT - 1)
            for g in range(NSA_GROUP):
                dst = h * ncol + g * ts
                shift = (dst - (ncol - ts)) % LANES
                moved = pltpu.roll(sel, shift, 1) if shift else sel
                selacc = jnp.where((lane >= dst) & (lane < dst + ts), moved, selacc)
        sel_ref[0] = selacc.reshape(sel_ref.shape[1:])


def _page_half_rows(page_ref, e):
    t = page_ref[0, e]
    return t.reshape(NSA_KV_HEADS * HEAD_DIM, t.shape[-1]).T


def _native_rows(a, lead):
    nd = a.ndim
    t = jnp.transpose(a, tuple(range(nd - 4)) + (nd - 3, nd - 2, nd - 1, nd - 4))
    return t.reshape((lead,) + t.shape[nd - 4:])


def _page_specs(n_pages):
    def spec(k):
        return pl.BlockSpec((1, 2, NSA_KV_HEADS, HEAD_DIM, PAGE_SIZE),
                            lambda b, s, pt: (pt[b * n_pages + s * PAGES_PER_STEP + k], 0, 0, 0, 0))
    return [spec(k) for k in range(PAGES_PER_STEP)]


def _cmp_sample(pt, cache, wx, pe, qs, tab, nb, n_pages, ts):
    steps = n_pages // PAGES_PER_STEP
    nc = n_pages * PAGE_SIZE // CMP_STRIDE
    rows = PAGES_PER_STEP * PAGE_SIZE // CMP_STRIDE
    ns = nc // 4
    full = lambda a: pl.BlockSpec(a.shape, lambda b, s, pt: (0,) * a.ndim)
    grid_spec = pltpu.PrefetchScalarGridSpec(
        num_scalar_prefetch=1,
        grid=(nb, steps),
        in_specs=_page_specs(n_pages) + [
            full(wx), full(pe),
            pl.BlockSpec((1, NSA_KV_HEADS, LANES, HEAD_DIM), lambda b, s, pt: (b, 0, 0, 0)),
            full(tab)],
        out_specs=[pl.BlockSpec((1, NSA_KV_HEADS, NSA_GROUP * ts, HEAD_DIM), lambda b, s, pt: (b, 0, 0, 0)),
                   pl.BlockSpec((1, steps, ns // steps, LANES), lambda b, s, pt: (b, 0, 0, 0))],
        scratch_shapes=[pltpu.VMEM((steps, rows, KV_ROW), F32), pltpu.VMEM((steps, rows, KV_ROW), F32),
                        pltpu.VMEM((nc, LANES), F32), pltpu.VMEM((PAGE_SIZE, LANES), F32)])
    return pl.pallas_call(
        functools.partial(_cmp_sample_kernel, ts=ts),
        grid_spec=grid_spec,
        out_shape=[jax.ShapeDtypeStruct((nb, NSA_KV_HEADS, NSA_GROUP * ts, HEAD_DIM), F32),
                   jax.ShapeDtypeStruct((nb, steps, ns // steps, LANES), F32)],
        compiler_params=_cparams(("parallel", "arbitrary")),
        name="cmp_sample",
    )(pt, *([cache] * PAGES_PER_STEP), wx, pe, qs, tab)


def _local_softmax(z, mask):
    m = jnp.max(z, axis=0, keepdims=True)
    p = jnp.where(mask, jnp.exp(z - m), 0.0)
    return m, p, jnp.sum(p, axis=0, keepdims=True)


def _selwin_sample_kernel(pt_ref, *refs, ts):
    del pt_ref
    npg = PAGES_PER_STEP
    pages = refs[:npg]
    (qa_ref, sel_ref, cf_ref, nbl_ref, st_ref, kvn_ref, nbn_ref, nbw_ref, g_ref, oc_ref,
     o_ref, m_ref, l_ref, a_ref) = refs[npg:]
    s = pl.program_id(1)
    last = pl.num_programs(1) - 1
    ncol = NSA_KV_HEADS * NSA_GROUP * ts
    qa = qa_ref[0]
    rowi = lax.broadcasted_iota(jnp.int32, (PAGE_SIZE, LANES), 0)
    for k in range(npg):
        pg = jnp.concatenate([_page_half_rows(pages[k], 0), _page_half_rows(pages[k], 1)],
                             axis=1).astype(BF16)
        is_last_page = jnp.logical_and(s == last, k == npg - 1)
        bias = jnp.where(is_last_page, nbl_ref[...], cf_ref[0:1, :])
        s0 = sel_ref[0, s, SEL_PER_PAGE * k:SEL_PER_PAGE * k + 1, :]
        s1 = sel_ref[0, s, SEL_PER_PAGE * k + 1:SEL_PER_PAGE * k + 2, :]
        mask = jnp.where(rowi < SEL_BLOCK, s0, s1) > 0.5
        z = jnp.where(mask, _dot(pg, qa) + bias, NEG)
        m, p, l = _local_softmax(z, mask)
        m_ref[s, k:k + 1, :] = m
        l_ref[s, k:k + 1, :] = l
        a_ref[s, k] = pl.dot(p.astype(BF16), pg, trans_a=True)[0:ncol, :]

    @pl.when(s == last)
    def _():
        nbn = nbn_ref[...]
        okn = nbn > 0.5 * NEG
        kn = kvn_ref[0, 0:SUBLANES, :].astype(BF16)
        m_n, p_n, l_n = _local_softmax(jnp.where(okn, _dot(kn, qa) + nbn, NEG), okn)
        a_n = pl.dot(p_n.astype(BF16), kn, trans_a=True)[0:ncol, :]
        nparts = m_ref.shape[0] * m_ref.shape[1]
        m_all = m_ref[...].reshape(nparts, LANES)
        l_all = l_ref[...].reshape(nparts, LANES)
        m_g = jnp.maximum(jnp.max(m_all, axis=0, keepdims=True), m_n)
        w_all = jnp.exp(m_all - m_g)
        w_n = jnp.exp(m_n - m_g)
        l_g = jnp.sum(w_all * l_all, axis=0, keepdims=True) + w_n * l_n

        nbw = nbw_ref[...]
        okw = nbw > 0.5 * NEG
        st = jnp.concatenate([_page_half_rows(st_ref, 0), _page_half_rows(st_ref, 1)],
                             axis=1).astype(BF16)
        kwn = kvn_ref[0, SUBLANES:2 * SUBLANES, :].astype(BF16)
        zw = jnp.where(okw, _dot(st, qa) + nbw, NEG)
        zwn = jnp.where(okn, _dot(kwn, qa) + nbn, NEG)
        m_w = jnp.maximum(jnp.max(zw, axis=0, keepdims=True), jnp.max(zwn, axis=0, keepdims=True))
        pw = jnp.where(okw, jnp.exp(zw - m_w), 0.0)
        pwn = jnp.where(okn, jnp.exp(zwn - m_w), 0.0)
        l_w = jnp.sum(pw, axis=0, keepdims=True) + jnp.sum(pwn, axis=0, keepdims=True)
        a_w = (pl.dot(pw.astype(BF16), st, trans_a=True) + pl.dot(pwn.astype(BF16), kwn, trans_a=True))[0:ncol, :]

        assert nparts + 4 * SUBLANES <= LANES
        rows8 = lambda r: jnp.broadcast_to(r, (SUBLANES, LANES))
        x = jnp.concatenate([w_all, rows8(w_n), rows8(l_g), rows8(l_w),
                             jnp.zeros((LANES - nparts - 3 * SUBLANES, LANES), F32)], axis=0)
        xt = x.T
        acc = xt[0:ncol, nparts:nparts + 1] * a_n
        for pidx in range(nparts):
            acc = acc + xt[0:ncol, pidx:pidx + 1] * a_ref[pidx // npg, pidx % npg]
        o_s = acc / jnp.maximum(xt[0:ncol, nparts + SUBLANES:nparts + SUBLANES + 1], 1e-30)
        o_w = a_w / jnp.maximum(xt[0:ncol, nparts + 2 * SUBLANES:nparts + 2 * SUBLANES + 1], 1e-30)

        ri = lax.broadcasted_iota(jnp.int32, (ncol, HEAD_DIM), 0)
        v_of = lambda t: jnp.where(ri < ncol // NSA_KV_HEADS, t[:, 2 * HEAD_DIM:3 * HEAD_DIM],
                                   t[:, 3 * HEAD_DIM:4 * HEAD_DIM])
        gt = g_ref[0]
        oc = oc_ref[0].reshape(ncol, HEAD_DIM)
        o_ref[0] = gt[:, 0:1] * oc + gt[:, 1:2] * v_of(o_s) + gt[:, 2:3] * v_of(o_w)


def _selwin_sample(pt, cache, qa, sel, cfrow, nbl, state, st_off, kvn, nbn, nbw, gates, oc, nb, n_pages, ts):
    steps = n_pages // PAGES_PER_STEP
    ncol = NSA_KV_HEADS * NSA_GROUP * ts
    full = lambda a: pl.BlockSpec(a.shape, lambda b, s, pt: (0,) * a.ndim)
    perb = lambda a: pl.BlockSpec((1,) + a.shape[1:], lambda b, s, pt: (b,) + (0,) * (a.ndim - 1))
    grid_spec = pltpu.PrefetchScalarGridSpec(
        num_scalar_prefetch=1,
        grid=(nb, steps),
        in_specs=_page_specs(n_pages) + [
            perb(qa), perb(sel), full(cfrow), full(nbl),
            pl.BlockSpec((1,) + state.shape[1:], lambda b, s, pt: (st_off + b, 0, 0, 0, 0)),
            perb(kvn), full(nbn), full(nbw), perb(gates), perb(oc)],
        out_specs=pl.BlockSpec((1, ncol, HEAD_DIM), lambda b, s, pt: (b, 0, 0)),
        scratch_shapes=[pltpu.VMEM((steps, PAGES_PER_STEP, LANES), F32),
                        pltpu.VMEM((steps, PAGES_PER_STEP, LANES), F32),
                        pltpu.VMEM((steps, PAGES_PER_STEP, ncol, KV_ROW), F32)])
    return pl.pallas_call(
        functools.partial(_selwin_sample_kernel, ts=ts),
        grid_spec=grid_spec,
        out_shape=jax.ShapeDtypeStruct((nb, ncol, HEAD_DIM), F32),
        compiler_params=_cparams(("parallel", "arbitrary")),
        name="selwin_sample",
    )(pt, *([cache] * PAGES_PER_STEP), qa, sel, cfrow, nbl, state, kvn, nbn, nbw, gates, oc)


def _sample_bias_tables(rel_bias, past, ts, n_buf):
    ncol = NSA_KV_HEADS * NSA_GROUP * ts
    col = jnp.arange(ncol)
    head = col // ts
    t = col % ts
    pick = lambda b: jnp.take_along_axis(b, jnp.broadcast_to(head, b.shape[:-1])[..., None], axis=-1)[..., 0]
    bias = lambda d: pick(_head_bias(rel_bias, d))
    padc = lambda a, fill=0.0: jnp.pad(a, ((0, 0), (0, LANES - a.shape[1])), constant_values=fill)
    nc = past // CMP_STRIDE
    c_end = jnp.arange(nc)[:, None] * CMP_STRIDE + CMP_BLOCK - 1
    tab = jnp.where(jnp.arange(nc)[:, None] < nc - 1, bias(past + t[None, :] - c_end), NEG)
    hc = ncol // NSA_KV_HEADS
    tab_cmp = jnp.stack([padc(tab[:, h * hc:(h + 1) * hc]) for h in range(NSA_KV_HEADS)])
    far = rel_bias[N_BUCKETS - 1].astype(F32)[head]
    cfrow = jnp.broadcast_to(padc(far[None, :]), (SUBLANES, LANES))
    kk = jnp.arange(PAGE_SIZE)[:, None]
    nbl = padc(bias(PAGE_SIZE + t[None, :] - kk))
    j = jnp.arange(SUBLANES)[:, None]
    dn = t[None, :] - j
    nbn = padc(jnp.where((dn >= 0) & (j < ts), bias(dn), NEG))
    r = jnp.arange(n_buf)[:, None]
    dw = n_buf + t[None, :] - r
    nbw = padc(jnp.where((dw >= 0) & (dw < WINDOW), bias(dw), NEG))
    return tab_cmp, cfrow, nbl, nbn, nbw


def _sample_layer(x, p, tabs, l, pt, cache_cmp, cache_slc, state_win, state_pool_l, nb, ts, n_pages, final, gf):
    tab_cmp, cfrow, nbl, nbn, nbw = tabs
    past = n_pages * PAGE_SIZE
    R = ts * nb
    ncol = NSA_KV_HEADS * NSA_GROUP * ts
    xp, q, gt, u, v, kvc, kvs, kvw = _inproj(x, p['g_mix'], p['w_in'], p['gn'], R)
    q5 = q.reshape(ts, nb, NSA_KV_HEADS, NSA_GROUP, HEAD_DIM)
    qs = q5.transpose(1, 2, 3, 0, 4).reshape(nb, NSA_KV_HEADS, NSA_GROUP * ts, HEAD_DIM)
    qs = jnp.pad(qs, ((0, 0), (0, 0), (0, LANES - NSA_GROUP * ts), (0, 0)))
    oc, sel = _cmp_sample(pt, cache_cmp, p['wx'], p['pe'], qs, tab_cmp, nb, n_pages, ts)
    qt = q5.transpose(1, 2, 4, 3, 0).reshape(nb, NSA_KV_HEADS, HEAD_DIM, NSA_GROUP * ts)
    eye = jnp.eye(NSA_KV_HEADS, dtype=qt.dtype)
    qa = jnp.einsum('bhdc,hk->bhdkc', qt, eye).reshape(nb, NSA_KV_HEADS * HEAD_DIM, ncol)
    qa = jnp.pad(qa, ((0, 0), (0, KV_ROW - NSA_KV_HEADS * HEAD_DIM), (0, LANES - ncol)))
    rows_of = lambda a: jnp.pad(a.reshape(ts, nb, KV_ROW).transpose(1, 0, 2), ((0, 0), (0, SUBLANES - ts), (0, 0)))
    kvn = jnp.concatenate([rows_of(kvs), rows_of(kvw)], axis=1)
    ng = NSA_GROUP * N_BRANCHES
    g5 = jnp.concatenate([gt[:, h * LANES:h * LANES + ng] for h in range(NSA_KV_HEADS)], axis=1)
    g5 = g5.reshape(ts, nb, NSA_KV_HEADS * NSA_GROUP, N_BRANCHES)
    gates = jnp.pad(g5.transpose(1, 2, 0, 3).reshape(nb, ncol, N_BRANCHES), ((0, 0), (0, 0), (0, LANES - N_BRANCHES)))
    o = _selwin_sample(pt, cache_slc, qa, sel, cfrow, nbl, state_win, l * nb, kvn, nbn, nbw, gates, oc,
                       nb, n_pages, ts)
    yb = o.reshape(nb, NSA_KV_HEADS * NSA_GROUP, ts, HEAD_DIM).transpose(2, 0, 1, 3).reshape(R, NSA_WIDTH)
    ext = jnp.concatenate([state_pool_l.transpose(1, 0, 2), xp.reshape(ts, nb, POOL_WIDTH)], axis=0)
    x1 = _mix_sample(x, ext, yb.astype(BF16), u.reshape(ts, nb, GMLP_WIDTH), v.reshape(ts, nb, GMLP_WIDTH),
                     p['wp'], p['ps'], p['wsx'], p['wb'][:SUBLANES], p['w_out'], nb, ts, past)
    x2 = _ffn(x1, p['g_ffn'], p['w_up'], p['w_down'], gf, R, final)
    return x2, (xp, kvc, kvs, kvw, v)


def kernel(x_prompt, x_sample, cache_cmp_kv, cache_slc_kv, state_win_kv, state_pool, page_table, w_in, w_out,
           norm_mix, norm_ffn, norm_final, w_pool, pool_scale, cmp_w, cmp_pe, gmlp_ws, gmlp_b, gmlp_norm, w_up,
           w_down, rel_bias):
    B, T, _ = x_prompt.shape
    nb, ts, _ = x_sample.shape
    n_pages = page_table.shape[1]
    n_phys = cache_cmp_kv.shape[1]
    n_buf = state_win_kv.shape[2]
    past = n_pages * PAGE_SIZE
    depth = w_in.shape[0]
    assert T % TK_FAR == 0 and T >= WINDOW and n_pages % PAGES_PER_STEP == 0
    assert ts <= SUBLANES and ts <= POOL_STATE and n_buf == WINDOW and past >= WINDOW
    kv_tail = (2, NSA_KV_HEADS, HEAD_DIM)

    cache_cmp = _native_rows(cache_cmp_kv, depth * n_phys)
    cache_slc = _native_rows(cache_slc_kv, depth * n_phys)
    state_win = _native_rows(state_win_kv, depth * nb)
    ptabs = _prompt_bias_tables(rel_bias)
    stabs = _sample_bias_tables(rel_bias, past, ts, n_buf)
    gf = norm_final[None, :]
    xp = x_prompt.reshape(B * T, D_MODEL)
    xs = x_sample.transpose(1, 0, 2).reshape(ts * nb, D_MODEL)
    unmajor = lambda a: a.reshape(ts, nb, a.shape[-1]).transpose(1, 0, 2)

    outs = [[] for _ in range(9)]
    for l in range(depth):
        p = _prep_layer(l, w_in, w_out, norm_mix, norm_ffn, w_pool, pool_scale, cmp_w, cmp_pe, gmlp_ws, gmlp_b,
                        gmlp_norm, w_up, w_down)
        final = l == depth - 1
        xp, (pin, kvc_t, kvs_t, kvw_t) = _prompt_layer(xp, p, ptabs, B, T, final, gf)
        rows_of = lambda a: a.reshape((B,) + kv_tail + (a.shape[-1],)).transpose(0, 4, 1, 2, 3)
        outs[0].append(rows_of(kvc_t))
        outs[1].append(rows_of(kvs_t))
        outs[2].append(rows_of(kvw_t[:, :, T - WINDOW:]))
        outs[3].append(pin.reshape(B, T, POOL_WIDTH)[:, T - POOL_STATE:])

        pt = (page_table + l * n_phys).reshape(-1).astype(jnp.int32)
        xs, (sin, kvc_s, kvs_s, kvw_s, v_s) = _sample_layer(
            xs, p, stabs, l, pt, cache_cmp, cache_slc, state_win, state_pool[l], nb, ts, n_pages, final, gf)
        kvw_new = unmajor(kvw_s).reshape((nb, ts) + kv_tail)
        outs[4].append(unmajor(kvc_s).reshape((nb, ts) + kv_tail))
        outs[5].append(unmajor(kvs_s).reshape((nb, ts) + kv_tail))
        outs[6].append(jnp.concatenate([state_win_kv[l][:, ts:], kvw_new], axis=1))
        outs[7].append(jnp.concatenate([state_pool[l][:, ts:], unmajor(sin)], axis=1))
        outs[8].append(unmajor(v_s))

    y_prompt = xp.reshape(B, T, D_MODEL)
    y_sample = unmajor(xs)
    return (y_prompt, y_sample) + tuple(jnp.stack(o) for o in outs)
```

```python
import functools
import math

import numpy as np
import jax
import jax.numpy as jnp
from jax import lax
from jax.experimental import pallas as pl
from jax.experimental.pallas import tpu as pltpu

F32 = jnp.float32
BF16 = jnp.bfloat16

D_MODEL = 1024
DEPTH = 4
PAGE_SIZE = 128
HEAD_DIM = 64
POOL_WINDOWS = (2, 4, 8, 16)
POOL_WIDTH = 256
POOL_GROUP_DIM = 64
POOL_STATE = 15
NSA_WIDTH = 512
NSA_HEADS = 8
NSA_KV_HEADS = 2
NSA_GROUP = 4
KV_ROW = 2 * NSA_KV_HEADS * HEAD_DIM
CMP_STRIDE = 16
CMP_BLOCK = 32
SEL_BLOCK = 64
N_SELECT = 16
WINDOW = 512
N_BRANCHES = 3
FORCE_SCORE = 1000.0
GMLP_WIDTH = 256
GMLP_GROUPS = 4
GMLP_CHUNK = 128
D_FF = 4096
N_BUCKETS = 32
MAX_DISTANCE = 128
EPS = 1e-6
SCALE = HEAD_DIM ** -0.5

LANES = 128
SUBLANES = 8
VMEM_LIMIT = 56 * 1024 * 1024

NEG = -1e30
CMP_PAD = 16
TQ = 128
TQC = 512
CMP_NEAR = CMP_PAD + TQC // CMP_STRIDE
TK_FAR = 512
KPAD = 512
GATE_LANES = 2 * LANES

PROJ_SEGS = (('xp', 256), ('q', 512), ('kvc', 256), ('kvs', 256), ('kvw', 256), ('gt', 24), ('u', 256), ('v', 256))


def _cparams(sem):
    return pltpu.CompilerParams(dimension_semantics=sem, vmem_limit_bytes=VMEM_LIMIT)


def _dot(a, b):
    return jnp.dot(a, b, preferred_element_type=F32)


def _dot_nt(a, b):
    return lax.dot_general(a, b, (((1,), (1,)), ((), ())), preferred_element_type=F32)


def _gelu(x):
    c = math.sqrt(2.0 / math.pi)
    return 0.5 * x * (1.0 + jnp.tanh(c * (x + 0.044715 * (x * x * x))))


def _rms(x, g):
    return x * lax.rsqrt(jnp.mean(x * x, axis=-1, keepdims=True) + EPS) * g


def _inproj_kernel(x_ref, g_ref, w_ref, gn_ref, *refs, channel_major):
    if channel_major:
        wt_ref, xp_ref, q_ref, gt_ref, u_ref, v_ref, kvc_ref, *kv_refs = refs
    else:
        xp_ref, q_ref, gt_ref, u_ref, v_ref, kvc_ref, *kv_refs = refs
    x = x_ref[...]
    h = _rms(x, g_ref[...]).astype(BF16)

    def seg(lo, hi):
        return _dot(h, w_ref[:, lo:hi])

    xp_ref[...] = seg(0, 256)
    q_ref[...] = (seg(256, 768) * SCALE).astype(BF16)
    kvc_ref[...] = seg(768, 1024)
    if channel_major:
        for k, t_ref in enumerate(kv_refs):
            t_ref[0] = _dot_nt(wt_ref[k * KV_ROW:(k + 1) * KV_ROW, :], h)
    else:
        kv_refs[0][...] = seg(1024, 1280)
        kv_refs[1][...] = seg(1280, 1536)
    u_ref[...] = _gelu(seg(1536, 1792))
    gv = _gelu(seg(1792, 2048))
    sq = gv * gv
    lane = lax.broadcasted_iota(jnp.int32, sq.shape, 1)
    ms = jnp.zeros_like(sq)
    for g in range(GMLP_GROUPS):
        in_g = (lane >= g * 64) & (lane < (g + 1) * 64)
        s = jnp.sum(jnp.where(in_g, sq, 0.0), axis=-1, keepdims=True) * (1.0 / 64.0)
        ms = jnp.where(in_g, s, ms)
    v_ref[...] = gv * lax.rsqrt(ms + EPS) * gn_ref[...]
    gt_ref[...] = jax.nn.sigmoid(seg(2048, 2048 + GATE_LANES))


def _inproj(x, g, w, gn, tm, seq=None, wt=None):
    R = x.shape[0]
    f = lambda n, dt=F32: jax.ShapeDtypeStruct((R, n), dt)
    row = lambda n: pl.BlockSpec((tm, n), lambda i: (i, 0))
    full = lambda a: pl.BlockSpec(a.shape, lambda i: (0,) * a.ndim)
    out_specs = [row(256), row(512), row(GATE_LANES), row(256), row(256), row(256)]
    out_shape = [f(256), f(512, BF16), f(GATE_LANES), f(256), f(256), f(256)]
    if seq is None:
        out_specs += [row(256), row(256)]
        out_shape += [f(256), f(256)]
    else:
        per_seq = seq // tm
        out_specs += [pl.BlockSpec((1, KV_ROW, tm), lambda i: (i // per_seq, 0, i % per_seq))] * 3
        out_shape += [jax.ShapeDtypeStruct((R // seq, KV_ROW, seq), F32)] * 3
    args = (x, g, w, gn) if seq is None else (x, g, w, gn, wt)
    return pl.pallas_call(
        functools.partial(_inproj_kernel, channel_major=seq is not None),
        grid=(R // tm,),
        in_specs=[row(D_MODEL)] + [full(a) for a in args[1:]],
        out_specs=out_specs,
        out_shape=out_shape,
        compiler_params=_cparams(("parallel",)),
        name="inproj",
    )(*args)


def _out_proj(x, ya, yb, yc, wo_ref):
    acc = _dot(ya.astype(BF16), wo_ref[0:256, :])
    acc = acc + _dot(yb, wo_ref[256:768, :])
    acc = acc + _dot(yc.astype(BF16), wo_ref[768:1024, :])
    return x + acc


def _pool_tail(win2, win4, win8, win16, cur, cnt, wp_ref, ps_ref):
    lane = lax.broadcasted_iota(jnp.int32, cur.shape, 1)
    win = jnp.where(lane < 64, win2, jnp.where(lane < 128, win4, jnp.where(lane < 192, win8, win16)))
    pooled = win / cnt - cur
    return _dot(pooled.astype(BF16), wp_ref[...]) * ps_ref[...]


FF_CHUNK = 1024


def _ffn_rows(x, g_ref, wu_ref, wd_ref, gf_ref, final):
    h = _rms(x, g_ref[...]).astype(BF16)
    acc = x
    for c in range(D_FF // FF_CHUNK):
        a = jnp.maximum(_dot(h, wu_ref[:, c * FF_CHUNK:(c + 1) * FF_CHUNK]), 0.0)
        acc = acc + _dot((a * a).astype(BF16), wd_ref[c * FF_CHUNK:(c + 1) * FF_CHUNK, :])
    return _rms(acc, gf_ref[...]) if final else acc


def _ffn_specs(g, wu, wd, gf):
    full = lambda a: pl.BlockSpec(a.shape, lambda i: (0,) * a.ndim)
    once = lambda a: pl.BlockSpec(a.shape, lambda i: (0,) * a.ndim, pipeline_mode=pl.Buffered(1))
    return [full(g), once(wu), once(wd), full(gf)]


def _mix_prompt_kernel(x_ref, xp_ref, halo_ref, yb_ref, u_ref, v_ref, wp_ref, ps_ref, ws_ref, wb_ref, wo_ref,
                       g_ref, wu_ref, wd_ref, gf_ref, o_ref, ext_ref, *, tm, seq, final):
    i = pl.program_id(0)
    t0 = (i * tm) % seq
    halo = jnp.where(t0 > 0, halo_ref[...], 0.0)
    cur = xp_ref[...]
    ext_ref[0:16, :] = halo
    ext_ref[16:16 + tm, :] = cur
    e = ext_ref[...]
    b2 = e[1:] + e[:-1]
    b4 = b2[2:] + b2[:-2]
    b8 = b4[4:] + b4[:-4]
    b16 = b8[8:] + b8[:-8]
    win2 = b2[15:15 + tm]
    win4 = b4[13:13 + tm]
    win8 = b8[9:9 + tm]
    win16 = b16[1:1 + tm]
    rowi = lax.broadcasted_iota(jnp.int32, cur.shape, 0)
    lane = lax.broadcasted_iota(jnp.int32, cur.shape, 1)
    wsz = jnp.where(lane < 64, 2, jnp.where(lane < 128, 4, jnp.where(lane < 192, 8, 16)))
    cnt = jnp.minimum(t0 + rowi + 1, wsz).astype(F32)
    ya = _pool_tail(win2, win4, win8, win16, cur, cnt, wp_ref, ps_ref)

    ci = lax.broadcasted_iota(jnp.int32, (GMLP_CHUNK, GMLP_CHUNK), 0)
    cj = lax.broadcasted_iota(jnp.int32, (GMLP_CHUNK, GMLP_CHUNK), 1)
    lane_c = lax.broadcasted_iota(jnp.int32, (GMLP_CHUNK, GMLP_WIDTH), 1)
    wts = [jnp.where(ci >= cj, ws_ref[g], 0.0).astype(BF16) for g in range(GMLP_GROUPS)]
    parts = []
    for c in range(tm // GMLP_CHUNK):
        vc = v_ref[c * GMLP_CHUNK:(c + 1) * GMLP_CHUNK, :].astype(BF16)
        s = jnp.zeros((GMLP_CHUNK, GMLP_WIDTH), F32)
        for g in range(GMLP_GROUPS):
            sg = _dot(wts[g], vc)
            s = jnp.where((lane_c >= g * 64) & (lane_c < (g + 1) * 64), sg, s)
        parts.append(u_ref[c * GMLP_CHUNK:(c + 1) * GMLP_CHUNK, :] * (s + wb_ref[...]))
    yc = jnp.concatenate(parts, axis=0) if len(parts) > 1 else parts[0]

    x1 = _out_proj(x_ref[...], ya, yb_ref[...], yc, wo_ref)
    o_ref[...] = _ffn_rows(x1, g_ref, wu_ref, wd_ref, gf_ref, final)


def _mix_prompt(x, xp, yb, u, v, wp, ps, ws, wb, wo, g, wu, wd, gf, tm, seq, final):
    R = x.shape[0]
    row = lambda n: pl.BlockSpec((tm, n), lambda i: (i, 0))
    full = lambda a: pl.BlockSpec(a.shape, lambda i: (0,) * a.ndim)
    halo = pl.BlockSpec((16, 256), lambda i: (jnp.maximum(i * (tm // 16) - 1, 0), 0))
    return pl.pallas_call(
        functools.partial(_mix_prompt_kernel, tm=tm, seq=seq, final=final),
        grid=(R // tm,),
        in_specs=[row(D_MODEL), row(256), halo, row(512), row(256), row(256),
                  full(wp), full(ps), full(ws), full(wb), full(wo)] + _ffn_specs(g, wu, wd, gf),
        out_specs=row(D_MODEL),
        out_shape=jax.ShapeDtypeStruct((R, D_MODEL), F32),
        scratch_shapes=[pltpu.VMEM((tm + 16, 256), F32)],
        compiler_params=_cparams(("parallel",)),
        name="mix_prompt",
    )(x, xp, xp, yb, u, v, wp, ps, ws, wb, wo, g, wu, wd, gf)


def _mix_sample_kernel(x_ref, ext_ref, yb_ref, u_ref, v_ref, wp_ref, ps_ref, wsx_ref, wb_ref, wo_ref,
                       g_ref, wu_ref, wd_ref, gf_ref, o_ref, *, nb, ts, past, final):
    lane = lax.broadcasted_iota(jnp.int32, (nb, POOL_WIDTH), 1)
    wsz = jnp.where(lane < 64, 2, jnp.where(lane < 128, 4, jnp.where(lane < 192, 8, 16)))
    yas, ycs = [], []
    for t in range(ts):
        top = POOL_STATE + t
        acc = ext_ref[top] + ext_ref[top - 1]
        wins = [acc]
        for w in (4, 8, 16):
            for j in range(w // 2, w):
                acc = acc + ext_ref[top - j]
            wins.append(acc)
        cnt = jnp.minimum(past + t + 1, wsz).astype(F32)
        yas.append(_pool_tail(wins[0], wins[1], wins[2], wins[3], ext_ref[top], cnt, wp_ref, ps_ref))
        s = wb_ref[t:t + 1, :]
        s = jnp.broadcast_to(s, (nb, GMLP_WIDTH))
        for j in range(t + 1):
            s = s + wsx_ref[t, j:j + 1, :] * v_ref[j]
        ycs.append(u_ref[t] * s)
    ya = jnp.concatenate(yas, axis=0)
    yc = jnp.concatenate(ycs, axis=0)
    x1 = _out_proj(x_ref[...], ya, yb_ref[...], yc, wo_ref)
    o_ref[...] = _ffn_rows(x1, g_ref, wu_ref, wd_ref, gf_ref, final)


def _mix_sample(x, ext, yb, u, v, wp, ps, wsx, wb4, wo, g, wu, wd, gf, nb, ts, past, final):
    R = x.shape[0]
    args = (x, ext, yb, u, v, wp, ps, wsx, wb4, wo)
    full = lambda a: pl.BlockSpec(a.shape, lambda i: (0,) * a.ndim)
    return pl.pallas_call(
        functools.partial(_mix_sample_kernel, nb=nb, ts=ts, past=past, final=final),
        grid=(1,),
        in_specs=[full(a) for a in args] + _ffn_specs(g, wu, wd, gf),
        out_specs=pl.BlockSpec((R, D_MODEL), lambda i: (0, 0)),
        out_shape=jax.ShapeDtypeStruct((R, D_MODEL), F32),
        compiler_params=_cparams(("arbitrary",)),
        name="mix_sample",
    )(*args, g, wu, wd, gf)


def _compress_rows(read_rows, n, wx):
    lo = jnp.zeros((n, wx.shape[1]), F32)
    hi = jnp.zeros((n, wx.shape[1]), F32)
    for j in range(CMP_STRIDE):
        xj = read_rows(j)
        lo = lo + xj * wx[j:j + 1, :]
        hi = hi + xj * wx[CMP_STRIDE + j:CMP_STRIDE + j + 1, :]
    return lo, hi


def _combine_halves(lo, hi, pe, wx):
    n = lo.shape[0]
    pe_term = jnp.sum(pe * wx, axis=0, keepdims=True)
    comp = lo + pltpu.roll(hi, n - 1, 0) + pe_term
    rowi = lax.broadcasted_iota(jnp.int32, comp.shape, 0)
    return jnp.where(rowi < n - 1, comp, 0.0)


def _compress_prompt_kernel(x_ref, wx_ref, pe_ref, o_ref, *, n):
    wx = wx_ref[...]
    lo, hi = _compress_rows(lambda j: x_ref[0, pl.ds(j, n, stride=CMP_STRIDE), :], n, wx)
    o_ref[0, 0:CMP_PAD, :] = jnp.zeros((CMP_PAD, LANES), F32)
    o_ref[0, CMP_PAD:CMP_PAD + n, :] = _combine_halves(lo, hi, pe_ref[...], wx)


def _compress_prompt(kvc, wx, pe):
    B, T, _ = kvc.shape
    n = T // CMP_STRIDE
    half = lambda rows: pl.BlockSpec((rows, LANES), lambda b, e: (0, e))
    return pl.pallas_call(
        functools.partial(_compress_prompt_kernel, n=n),
        grid=(B, KV_ROW // LANES),
        in_specs=[pl.BlockSpec((1, T, LANES), lambda b, e: (b, 0, e)), half(CMP_BLOCK), half(CMP_BLOCK)],
        out_specs=pl.BlockSpec((1, n + CMP_PAD, LANES), lambda b, e: (b, 0, e)),
        out_shape=jax.ShapeDtypeStruct((B, n + CMP_PAD, KV_ROW), F32),
        compiler_params=_cparams(("parallel", "parallel")),
        name="compress_prompt",
    )(kvc, wx, pe)


def _select_topk(score, n_pick):
    s_iota = lax.broadcasted_iota(jnp.int32, score.shape, 0)
    big = score.shape[0]
    work = score
    for _ in range(n_pick):
        m = jnp.max(work, axis=0, keepdims=True)
        idx = jnp.min(jnp.where(work == m, s_iota, big), axis=0, keepdims=True)
        work = jnp.where(s_iota == idx, -2.0, work)
    return jnp.where((work == -2.0) & (score >= 0.0), 1.0, 0.0)


def _softmax_rows(z):
    m = jnp.maximum(jnp.max(z, axis=0, keepdims=True), 0.1 * NEG)
    p = jnp.exp(z - m)
    den = jnp.maximum(jnp.sum(p, axis=0, keepdims=True), 1e-30)
    return p * (1.0 / den)


def _cmp_prompt_kernel(q_ref, kc_ref, vct_ref, nb_ref, cf_ref, oc_ref, sn_ref, raw_ref, z_ref, ps_ref, *, ncp):
    i = pl.program_id(2)
    hh = pl.program_id(1)
    tq = q_ref.shape[1]
    r0 = pl.multiple_of(i * (tq // CMP_STRIDE), SUBLANES)
    q = q_ref[0]
    vct = vct_ref[0, 0]
    rown = lax.broadcasted_iota(jnp.int32, (CMP_NEAR, tq), 0) + r0

    def attend(rows):
        kc = kc_ref[0, 0, 0:rows, :]
        rowi = lax.broadcasted_iota(jnp.int32, (rows, tq), 0)
        far = (rowi >= CMP_PAD) & (rowi < r0)
        psum = jnp.zeros((rows, tq), F32)
        octs = []
        for g in range(NSA_GROUP):
            qg = q[:, g * HEAD_DIM:(g + 1) * HEAD_DIM]
            raw_ref[0:rows, :] = _dot_nt(kc, qg)
            cf = cf_ref[hh * NSA_GROUP + g]
            z_ref[0:rows, :] = jnp.where(far, raw_ref[0:rows, :] + cf[0:1, :], NEG)
            nb = nb_ref[hh * NSA_GROUP + g]
            zn = raw_ref[pl.ds(r0, CMP_NEAR), :] + nb
            z_ref[pl.ds(r0, CMP_NEAR), :] = jnp.where((rown >= CMP_PAD) & (nb > 0.5 * NEG), zn, NEG)
            pc = _softmax_rows(z_ref[0:rows, :])
            psum = psum + pc
            pcb = pc.astype(BF16)
            if rows < ncp:
                pcb = jnp.concatenate([pcb, jnp.zeros((ncp - rows, tq), BF16)], axis=0)
            octs.append(_dot(vct, pcb))
        oc_ref[0] = jnp.concatenate(octs, axis=0).T
        for c in range(tq // LANES):
            ps_ref[c, 0:rows, :] = psum[:, c * LANES:(c + 1) * LANES]
            if rows < ps_ref.shape[1]:
                ps_ref[c, rows:, :] = jnp.zeros((ps_ref.shape[1] - rows, LANES), F32)

    n_var = (ncp - CMP_PAD + LANES - 1) // LANES
    var = (r0 + CMP_NEAR - CMP_PAD + LANES - 1) // LANES - 1
    for v in range(n_var):
        pl.when(var == v)(functools.partial(attend, min(CMP_PAD + (v + 1) * LANES, ncp)))

    ns = LANES
    halves = []
    for c in range(tq // LANES):
        part = ps_ref[c, pl.ds(CMP_PAD, ns, stride=4), :]
        for j in range(1, 4):
            part = part + ps_ref[c, pl.ds(CMP_PAD + j, ns, stride=4), :]
        halves.append(part)
    p_slc = jnp.concatenate(halves, axis=1)
    s_iota = lax.broadcasted_iota(jnp.int32, (ns, tq), 0)
    tt = lax.broadcasted_iota(jnp.int32, (ns, tq), 1)
    cur = (i * tq + tt) // SEL_BLOCK
    forced = (s_iota == 0) | (s_iota == cur) | (s_iota == cur - 1)
    score = jnp.where(s_iota <= cur, jnp.where(forced, FORCE_SCORE, p_slc), -1.0)
    sel = _select_topk(score, N_SELECT)
    sn_ref[0, 0] = jnp.where(sel.T > 0.5, 0.0, NEG).astype(BF16)


def _cmp_prompt(q, kc, vct, nbc, cfar):
    B, T, _ = q.shape
    ncp = kc.shape[2]
    ns = LANES
    nq = T // TQC
    full = lambda a: pl.BlockSpec(a.shape, lambda b, h, i: (0,) * a.ndim)
    return pl.pallas_call(
        functools.partial(_cmp_prompt_kernel, ncp=ncp),
        grid=(B, NSA_KV_HEADS, nq),
        in_specs=[pl.BlockSpec((1, TQC, 256), lambda b, h, i: (b, i, h)),
                  pl.BlockSpec((1, 1, ncp, HEAD_DIM), lambda b, h, i: (b, h, 0, 0)),
                  pl.BlockSpec((1, 1, HEAD_DIM, ncp), lambda b, h, i: (b, h, 0, 0)),
                  full(nbc), full(cfar)],
        out_specs=[pl.BlockSpec((1, TQC, 256), lambda b, h, i: (b, i, h)),
                   pl.BlockSpec((1, 1, TQC, ns), lambda b, h, i: (b, h, i, 0))],
        out_shape=[jax.ShapeDtypeStruct((B, T, NSA_WIDTH), F32),
                   jax.ShapeDtypeStruct((B, NSA_KV_HEADS, T, ns), BF16)],
        scratch_shapes=[pltpu.VMEM((ncp, TQC), F32), pltpu.VMEM((ncp, TQC), F32),
                        pltpu.VMEM((TQC // LANES, max(ncp, CMP_PAD + 4 * LANES), LANES), F32)],
        compiler_params=_cparams(("parallel", "parallel", "parallel")),
        name="cmp_prompt",
    )(q, kc, vct, nbc, cfar)


def _selwin_prompt_kernel(q_ref, sn_ref, ks_ref, vs_ref, kw_ref, vw_ref, oh_ref, g_ref, oc_ref, nbn_ref, nbw_ref,
                          gsel_ref, o_ref, ka_ref, va_ref, kwa_ref, vwa_ref, ms_ref, as_ref, za_ref, zb_ref):
    i = pl.program_id(2)
    R4 = NSA_GROUP * TQ
    ncols = ka_ref.shape[1]

    @pl.when(i == 0)
    def _():
        zpad = jnp.zeros((HEAD_DIM, KPAD), BF16)
        ones = jnp.ones((HEAD_DIM, ncols), BF16)
        row = lax.broadcasted_iota(jnp.int32, (HEAD_DIM, ncols), 0)
        col = lax.broadcasted_iota(jnp.int32, (HEAD_DIM, ncols), 1)
        ka_ref[0:LANES, :] = oh_ref[...]
        kwa_ref[HEAD_DIM:, :] = jnp.where((row == 0) & (col < KPAD), NEG, 0.0).astype(BF16)
        for dst, src in ((ka_ref.at[LANES:], ks_ref), (va_ref.at[0:HEAD_DIM], vs_ref),
                         (kwa_ref.at[0:HEAD_DIM], kw_ref), (vwa_ref.at[0:HEAD_DIM], vw_ref)):
            dst[:, 0:KPAD] = zpad
            dst[:, KPAD:] = src[0].astype(BF16)
        va_ref[HEAD_DIM:, :] = ones
        vwa_ref[HEAD_DIM:, :] = ones

    q = q_ref[0]
    q4 = jnp.concatenate([q[:, g * HEAD_DIM:(g + 1) * HEAD_DIM] for g in range(NSA_GROUP)], axis=0)
    sn = sn_ref[0, 0]
    blk = lax.broadcasted_iota(jnp.int32, (TQ, LANES), 1)
    first_near = (i - 1) * (TQ // SEL_BLOCK)
    sn_far = jnp.where(blk >= first_near, NEG, sn.astype(F32)).astype(BF16)
    qa_near = jnp.concatenate([jnp.concatenate([sn] * NSA_GROUP, axis=0), q4], axis=1)
    qa_far = jnp.concatenate([jnp.concatenate([sn_far] * NSA_GROUP, axis=0), q4], axis=1)
    qw = jnp.concatenate([q4, jnp.ones((R4, HEAD_DIM), BF16)], axis=1)

    def one_pass(z, vt):
        m = jnp.max(z, axis=1, keepdims=True)
        acc = _dot_nt(jnp.exp(z - m).astype(BF16), vt)
        return m, acc

    kw0 = pl.multiple_of(i * TQ, TQ)
    zw = _dot(qw, kwa_ref[:, pl.ds(kw0, WINDOW + TQ)]) + nbw_ref[0]
    _, a_w = one_pass(zw, vwa_ref[:, pl.ds(kw0, WINDOW + TQ)])

    kn0 = pl.multiple_of((i - 1) * TQ + KPAD, TQ)
    zn = _dot(qa_near, ka_ref[:, pl.ds(kn0, 2 * TQ)]) + nbn_ref[0]
    m0, a0 = one_pass(zn, va_ref[:, pl.ds(kn0, 2 * TQ)])
    ms_ref[...] = jnp.broadcast_to(m0, (R4, LANES))
    as_ref[...] = a0

    def far_logits(j):
        k0 = pl.multiple_of(j * TK_FAR + KPAD, TK_FAR)
        return _dot(qa_far, ka_ref[:, pl.ds(k0, TK_FAR)])

    def far_update(j, z):
        k0 = pl.multiple_of(j * TK_FAR + KPAD, TK_FAR)
        m_prev = ms_ref[...]
        m_new = jnp.maximum(m_prev, jnp.max(z, axis=1, keepdims=True))
        p = jnp.exp(z - jnp.concatenate([m_new] * (TK_FAR // LANES), axis=1))
        as_ref[...] = jnp.exp(m_prev - m_new) * as_ref[...] + _dot_nt(p.astype(BF16), va_ref[:, pl.ds(k0, TK_FAR)])
        ms_ref[...] = m_new

    n_far = (jnp.maximum(i - 1, 0) * TQ + TK_FAR - 1) // TK_FAR
    odd = n_far % 2
    last = n_far - 1

    def far_pair(t):
        zb_ref[...] = far_logits(t + 1)
        far_update(t, za_ref[...])
        za_ref[...] = far_logits(jnp.minimum(t + 2, last))
        far_update(t + 1, zb_ref[...])

    n_pairs = (n_far + 1) // 2

    za_ref[...] = far_logits(-odd)

    @pl.when(n_pairs % 2 == 1)
    def _():
        far_pair(-odd)

    def far_body(jj, carry):
        t = 4 * jj + 2 * (n_pairs % 2) - odd
        far_pair(t)
        far_pair(t + 2)
        return carry

    lax.fori_loop(0, n_pairs // 2, far_body, 0)

    gt = g_ref[...]
    lane = lax.broadcasted_iota(jnp.int32, (TQ, LANES), 1)
    own = [jnp.where((lane >= N_BRANCHES * g) & (lane < N_BRANCHES * (g + 1)), gt, 0.0) for g in range(NSA_GROUP)]
    gt4 = jnp.concatenate(own, axis=0)
    hi = gt4.astype(BF16)
    lo = (gt4 - hi.astype(F32)).astype(BF16)
    gates = _dot(jnp.concatenate([hi, lo], axis=1), gsel_ref[...])

    def scaled(acc, gate):
        f = gate / jnp.maximum(acc, 1e-30)
        return acc * pltpu.roll(f, HEAD_DIM, 1)

    o_sw = scaled(as_ref[...], gates[:, LANES:2 * LANES]) + scaled(a_w, gates[:, 2 * LANES:])
    oc = oc_ref[0]
    outs = []
    for g in range(NSA_GROUP):
        sl = slice(g * TQ, (g + 1) * TQ)
        outs.append(gates[sl, 0:HEAD_DIM] * oc[:, g * HEAD_DIM:(g + 1) * HEAD_DIM] + o_sw[sl, 0:HEAD_DIM])
    o_ref[0] = jnp.concatenate(outs, axis=1).astype(BF16)


def _selwin_prompt(q, sn, kvs_t, kvw_t, onehot_t, gates, oc, nbn, nbw):
    B, T, _ = q.shape
    src = jnp.arange(2 * LANES) % LANES
    dst = jnp.arange(N_BRANCHES * LANES) // LANES
    gsel = ((src[:, None] < NSA_GROUP * N_BRANCHES) & (src[:, None] % N_BRANCHES == dst[None, :])).astype(BF16)
    nq = T // TQ
    R4 = NSA_GROUP * TQ
    ncols = T + KPAD
    k_of = pl.BlockSpec((1, HEAD_DIM, T), lambda b, h, i: (b, h, 0))
    v_of = pl.BlockSpec((1, HEAD_DIM, T), lambda b, h, i: (b, NSA_KV_HEADS + h, 0))
    perh = lambda a: pl.BlockSpec((1,) + a.shape[1:], lambda b, h, i: (h, 0, 0))
    qtile = pl.BlockSpec((1, TQ, 256), lambda b, h, i: (b, i, h))
    return pl.pallas_call(
        _selwin_prompt_kernel,
        grid=(B, NSA_KV_HEADS, nq),
        in_specs=[qtile, pl.BlockSpec((1, 1, TQ, sn.shape[-1]), lambda b, h, i: (b, h, i, 0)),
                  k_of, v_of, k_of, v_of, pl.BlockSpec(onehot_t.shape, lambda b, h, i: (0, 0)),
                  pl.BlockSpec((TQ, LANES), lambda b, h, i: (b * nq + i, h)), qtile, perh(nbn), perh(nbw),
                  pl.BlockSpec(gsel.shape, lambda b, h, i: (0, 0))],
        out_specs=qtile,
        out_shape=jax.ShapeDtypeStruct((B, T, NSA_WIDTH), BF16),
        scratch_shapes=[pltpu.VMEM((LANES + HEAD_DIM, ncols), BF16)] + [pltpu.VMEM((LANES, ncols), BF16)] * 3
        + [pltpu.VMEM((R4, LANES), F32)] * 2 + [pltpu.VMEM((R4, TK_FAR), F32)] * 2,
        compiler_params=_cparams(("parallel", "parallel", "arbitrary")),
        name="selwin_prompt",
    )(q, sn, kvs_t, kvs_t, kvw_t, kvw_t, onehot_t, gates, oc, nbn, nbw, gsel)


def _rel_bucket(dist):
    d = jnp.maximum(dist, 0)
    n_exact = N_BUCKETS // 2
    d_f = jnp.maximum(d, 1).astype(F32)
    large = n_exact + (jnp.log(d_f / n_exact) / math.log(MAX_DISTANCE / n_exact)
                       * (N_BUCKETS - n_exact)).astype(jnp.int32)
    large = jnp.minimum(large, N_BUCKETS - 1)
    return jnp.where(d < n_exact, d, large)


def _head_bias(rel_bias, dist):
    onehot = (_rel_bucket(dist)[..., None] == jnp.arange(N_BUCKETS)).astype(F32)
    return jnp.einsum('...k,kh->...h', onehot, rel_bias.astype(F32), precision=lax.Precision.HIGHEST)


def _prompt_bias_tables(rel_bias):
    far = rel_bias[N_BUCKETS - 1].astype(F32)
    cfar = jnp.broadcast_to(far[:, None, None], (NSA_HEADS, SUBLANES, TQC))
    k = jnp.arange(CMP_NEAR)[:, None]
    tt = jnp.arange(TQC)[None, :]
    d = tt + CMP_PAD * CMP_STRIDE - (CMP_BLOCK - 1) - CMP_STRIDE * k
    nbc = jnp.where((d >= 0)[..., None], _head_bias(rel_bias, d), NEG).transpose(2, 0, 1)
    tq = jnp.arange(TQ)[:, None]

    def tile_table(first_dist, n_keys, max_dist):
        d = first_dist + tq - jnp.arange(n_keys)[None, :]
        b = jnp.where(((d >= 0) & (d < max_dist))[..., None], _head_bias(rel_bias, d) - far, NEG)
        return b.transpose(2, 0, 1).reshape(NSA_KV_HEADS, NSA_GROUP * TQ, n_keys)

    nbn = tile_table(TQ, 2 * TQ, 2 * TQ + 1)
    nbw = tile_table(WINDOW, WINDOW + TQ, WINDOW)
    return cfar, nbc, nbn, nbw


def _prep_layer(l, w_in, w_out, norm_mix, norm_ffn, w_pool, pool_scale, cmp_w, cmp_pe, gmlp_ws, gmlp_b,
                gmlp_norm, w_up, w_down):
    w = w_in[l]
    offs = np.cumsum([0] + [n for _, n in PROJ_SEGS])
    seg = {name: w[:, offs[k]:offs[k + 1]] for k, (name, _) in enumerate(PROJ_SEGS)}
    order = ('xp', 'q', 'kvc', 'kvs', 'kvw', 'u', 'v')
    ng = NSA_GROUP * N_BRANCHES
    gpad = jnp.zeros((D_MODEL, LANES - ng), w.dtype)
    gate_cols = [c for h in range(NSA_KV_HEADS) for c in (seg['gt'][:, h * ng:(h + 1) * ng], gpad)]
    w_perm = jnp.concatenate([seg[n] for n in order] + gate_cols, axis=1)
    eye = jnp.eye(len(POOL_WINDOWS), dtype=F32)
    wp = jnp.einsum('gcd,gh->gchd', w_pool[l], eye).reshape(POOL_WIDTH, POOL_WIDTH)
    wx = jnp.broadcast_to(cmp_w[l].transpose(1, 0, 2)[..., None], (CMP_BLOCK, 2, NSA_KV_HEADS, HEAD_DIM))
    return dict(
        w_in=w_perm.astype(BF16), g_mix=norm_mix[l][None, :],
        w_kv_t=jnp.concatenate([seg['kvc'], seg['kvs'], seg['kvw']], axis=1).T.astype(BF16),
        g_ffn=norm_ffn[l][None, :],
        gn=gmlp_norm[l][None, :], wp=wp.astype(BF16), ps=pool_scale[l][None, :],
        wx=wx.reshape(CMP_BLOCK, KV_ROW), pe=cmp_pe[l].transpose(1, 0, 2, 3).reshape(CMP_BLOCK, KV_ROW),
        ws=gmlp_ws[l], wb=jnp.repeat(gmlp_b[l].T, POOL_GROUP_DIM, axis=1),
        wsx=jnp.repeat(gmlp_ws[l].transpose(1, 2, 0), POOL_GROUP_DIM, axis=2)[:8, :8],
        w_out=w_out[l].astype(BF16), w_up=w_up[l].astype(BF16), w_down=w_down[l].astype(BF16))


def _split_heads(kv, part):
    B, T, _ = kv.shape
    return kv.reshape(B, T, 2, NSA_KV_HEADS, HEAD_DIM)[:, :, part].transpose(0, 2, 1, 3).astype(BF16)


def _prompt_layer(x, p, tabs, B, T, final, gf):
    cfar, nbc, nbn, nbw = tabs
    tm = min(512, B * T)
    xp, q, gt, u, v, kvc, kvc_t, kvs_t, kvw_t = _inproj(x, p['g_mix'], p['w_in'], p['gn'], tm, seq=T,
                                                        wt=p['w_kv_t'])
    comp = _compress_prompt(kvc.reshape(B, T, KV_ROW), p['wx'], p['pe'])
    kc = _split_heads(comp, 0)
    vct = _split_heads(comp, 1).transpose(0, 1, 3, 2)
    q3 = q.reshape(B, T, NSA_WIDTH)
    oc, sn = _cmp_prompt(q3, kc, vct, nbc, cfar)
    pos = jnp.arange(-KPAD, T)
    blk_of = jnp.where(pos >= 0, pos // SEL_BLOCK, LANES - 1)
    onehot_t = (jnp.arange(LANES)[:, None] == blk_of[None, :]).astype(BF16)
    yb = _selwin_prompt(q3, sn, kvs_t, kvw_t, onehot_t, gt, oc, nbn, nbw)
    x2 = _mix_prompt(x, xp, yb.reshape(B * T, NSA_WIDTH), u, v, p['wp'], p['ps'], p['ws'], p['wb'], p['w_out'],
                     p['g_ffn'], p['w_up'], p['w_down'], gf, tm, T, final)
    return x2, (xp, kvc_t, kvs_t, kvw_t)


PAGES_PER_STEP = 32
SEL_PER_PAGE = PAGE_SIZE // SEL_BLOCK


def _cmp_sample_kernel(pt_ref, *refs, ts):
    del pt_ref
    npg = PAGES_PER_STEP
    nhalf = KV_ROW // LANES
    pages = refs[:npg]
    wx_ref, pe_ref, q_ref, tab_ref, oc_ref, sel_ref, lo_ref, hi_ref, ps_ref, pg_ref = refs[npg:]
    s = pl.program_id(1)
    rows = PAGE_SIZE // CMP_STRIDE
    for k in range(npg):
        for e in range(nhalf):
            pg_ref[...] = _page_half_rows(pages[k], e)
            lo, hi = _compress_rows(lambda j: pg_ref[pl.ds(j, rows, stride=CMP_STRIDE), :], rows,
                                    wx_ref[:, e * LANES:(e + 1) * LANES])
            lo_ref[s, k * rows:(k + 1) * rows, e * LANES:(e + 1) * LANES] = lo
            hi_ref[s, k * rows:(k + 1) * rows, e * LANES:(e + 1) * LANES] = hi

    @pl.when(s == pl.num_programs(1) - 1)
    def _():
        nc = lo_ref.shape[0] * lo_ref.shape[1]
        comp = _combine_halves(lo_ref[...].reshape(nc, KV_ROW), hi_ref[...].reshape(nc, KV_ROW), pe_ref[...],
                               wx_ref[...])
        ns = nc // 4
        lane = lax.broadcasted_iota(jnp.int32, (ns, LANES), 1)
        s_iota = lax.broadcasted_iota(jnp.int32, (ns, LANES), 0)
        ncol = NSA_GROUP * ts
        selacc = jnp.zeros((ns, LANES), F32)
        for h in range(NSA_KV_HEADS):
            kc = comp[:, h * HEAD_DIM:(h + 1) * HEAD_DIM].astype(BF16)
            vc = comp[:, (NSA_KV_HEADS + h) * HEAD_DIM:(NSA_KV_HEADS + h + 1) * HEAD_DIM].astype(BF16)
            pc = _softmax_rows(_dot_nt(kc, q_ref[0, h]) + tab_ref[h])
            oc_ref[0, h] = pl.dot(pc.astype(BF16), vc, trans_a=True)[0:ncol, :]
            ps = pc
            for g in range(1, NSA_GROUP):
                ps = ps + pltpu.roll(pc, g * ts, 1)
            ps_ref[...] = ps
            p_slc = ps_ref[pl.ds(0, ns, stride=4), :]
            for j in range(1, 4):
                p_slc = p_slc + ps_ref[pl.ds(j, ns, stride=4), :]
            forced = (s_iota == 0) | (s_iota == ns - 1)
            sel = _select_topk(jnp.where(forced, FORCE_SCORE, p_slc), N_SELECT - 1)
            for g in range(NSA_GROUP):
                dst = h * ncol + g * ts
                shift = (dst - (ncol - ts)) % LANES
                moved = pltpu.roll(sel, shift, 1) if shift else sel
                selacc = jnp.where((lane >= dst) & (lane < dst + ts), moved, selacc)
        sel_ref[0] = selacc.reshape(sel_ref.shape[1:])


def _page_half_rows(page_ref, e):
    t = page_ref[0, e]
    return t.reshape(NSA_KV_HEADS * HEAD_DIM, t.shape[-1]).T


def _native_rows(a, lead):
    nd = a.ndim
    t = jnp.transpose(a, tuple(range(nd - 4)) + (nd - 3, nd - 2, nd - 1, nd - 4))
    return t.reshape((lead,) + t.shape[nd - 4:])


def _page_specs(n_pages):
    def spec(k):
        return pl.BlockSpec((1, 2, NSA_KV_HEADS, HEAD_DIM, PAGE_SIZE),
                            lambda b, s, pt: (pt[b * n_pages + s * PAGES_PER_STEP + k], 0, 0, 0, 0))
    return [spec(k) for k in range(PAGES_PER_STEP)]


def _cmp_sample(pt, cache, wx, pe, qs, tab, nb, n_pages, ts):
    steps = n_pages // PAGES_PER_STEP
    nc = n_pages * PAGE_SIZE // CMP_STRIDE
    rows = PAGES_PER_STEP * PAGE_SIZE // CMP_STRIDE
    ns = nc // 4
    full = lambda a: pl.BlockSpec(a.shape, lambda b, s, pt: (0,) * a.ndim)
    grid_spec = pltpu.PrefetchScalarGridSpec(
        num_scalar_prefetch=1,
        grid=(nb, steps),
        in_specs=_page_specs(n_pages) + [
            full(wx), full(pe),
            pl.BlockSpec((1, NSA_KV_HEADS, LANES, HEAD_DIM), lambda b, s, pt: (b, 0, 0, 0)),
            full(tab)],
        out_specs=[pl.BlockSpec((1, NSA_KV_HEADS, NSA_GROUP * ts, HEAD_DIM), lambda b, s, pt: (b, 0, 0, 0)),
                   pl.BlockSpec((1, steps, ns // steps, LANES), lambda b, s, pt: (b, 0, 0, 0))],
        scratch_shapes=[pltpu.VMEM((steps, rows, KV_ROW), F32), pltpu.VMEM((steps, rows, KV_ROW), F32),
                        pltpu.VMEM((nc, LANES), F32), pltpu.VMEM((PAGE_SIZE, LANES), F32)])
    return pl.pallas_call(
        functools.partial(_cmp_sample_kernel, ts=ts),
        grid_spec=grid_spec,
        out_shape=[jax.ShapeDtypeStruct((nb, NSA_KV_HEADS, NSA_GROUP * ts, HEAD_DIM), F32),
                   jax.ShapeDtypeStruct((nb, steps, ns // steps, LANES), F32)],
        compiler_params=_cparams(("parallel", "arbitrary")),
        name="cmp_sample",
    )(pt, *([cache] * PAGES_PER_STEP), wx, pe, qs, tab)


def _local_softmax(z, mask):
    m = jnp.max(z, axis=0, keepdims=True)
    p = jnp.where(mask, jnp.exp(z - m), 0.0)
    return m, p, jnp.sum(p, axis=0, keepdims=True)


def _selwin_sample_kernel(pt_ref, *refs, ts):
    del pt_ref
    npg = PAGES_PER_STEP
    pages = refs[:npg]
    (qa_ref, sel_ref, cf_ref, nbl_ref, st_ref, kvn_ref, nbn_ref, nbw_ref, g_ref, oc_ref,
     o_ref, m_ref, l_ref, a_ref) = refs[npg:]
    s = pl.program_id(1)
    last = pl.num_programs(1) - 1
    ncol = NSA_KV_HEADS * NSA_GROUP * ts
    qa = qa_ref[0]
    rowi = lax.broadcasted_iota(jnp.int32, (PAGE_SIZE, LANES), 0)
    for k in range(npg):
        pg = jnp.concatenate([_page_half_rows(pages[k], 0), _page_half_rows(pages[k], 1)],
                             axis=1).astype(BF16)
        is_last_page = jnp.logical_and(s == last, k == npg - 1)
        bias = jnp.where(is_last_page, nbl_ref[...], cf_ref[0:1, :])
        s0 = sel_ref[0, s, SEL_PER_PAGE * k:SEL_PER_PAGE * k + 1, :]
        s1 = sel_ref[0, s, SEL_PER_PAGE * k + 1:SEL_PER_PAGE * k + 2, :]
        mask = jnp.where(rowi < SEL_BLOCK, s0, s1) > 0.5
        z = jnp.where(mask, _dot(pg, qa) + bias, NEG)
        m, p, l = _local_softmax(z, mask)
        m_ref[s, k:k + 1, :] = m
        l_ref[s, k:k + 1, :] = l
        a_ref[s, k] = pl.dot(p.astype(BF16), pg, trans_a=True)[0:ncol, :]

    @pl.when(s == last)
    def _():
        nbn = nbn_ref[...]
        okn = nbn > 0.5 * NEG
        kn = kvn_ref[0, 0:SUBLANES, :].astype(BF16)
        m_n, p_n, l_n = _local_softmax(jnp.where(okn, _dot(kn, qa) + nbn, NEG), okn)
        a_n = pl.dot(p_n.astype(BF16), kn, trans_a=True)[0:ncol, :]
        nparts = m_ref.shape[0] * m_ref.shape[1]
        m_all = m_ref[...].reshape(nparts, LANES)
        l_all = l_ref[...].reshape(nparts, LANES)
        m_g = jnp.maximum(jnp.max(m_all, axis=0, keepdims=True), m_n)
        w_all = jnp.exp(m_all - m_g)
        w_n = jnp.exp(m_n - m_g)
        l_g = jnp.sum(w_all * l_all, axis=0, keepdims=True) + w_n * l_n

        nbw = nbw_ref[...]
        okw = nbw > 0.5 * NEG
        st = jnp.concatenate([_page_half_rows(st_ref, 0), _page_half_rows(st_ref, 1)],
                             axis=1).astype(BF16)
        kwn = kvn_ref[0, SUBLANES:2 * SUBLANES, :].astype(BF16)
        zw = jnp.where(okw, _dot(st, qa) + nbw, NEG)
        zwn = jnp.where(okn, _dot(kwn, qa) + nbn, NEG)
        m_w = jnp.maximum(jnp.max(zw, axis=0, keepdims=True), jnp.max(zwn, axis=0, keepdims=True))
        pw = jnp.where(okw, jnp.exp(zw - m_w), 0.0)
        pwn = jnp.where(okn, jnp.exp(zwn - m_w), 0.0)
        l_w = jnp.sum(pw, axis=0, keepdims=True) + jnp.sum(pwn, axis=0, keepdims=True)
        a_w = (pl.dot(pw.astype(BF16), st, trans_a=True) + pl.dot(pwn.astype(BF16), kwn, trans_a=True))[0:ncol, :]

        assert nparts + 4 * SUBLANES <= LANES
        rows8 = lambda r: jnp.broadcast_to(r, (SUBLANES, LANES))
        x = jnp.concatenate([w_all, rows8(w_n), rows8(l_g), rows8(l_w),
                             jnp.zeros((LANES - nparts - 3 * SUBLANES, LANES), F32)], axis=0)
        xt = x.T
        acc = xt[0:ncol, nparts:nparts + 1] * a_n
        for pidx in range(nparts):
            acc = acc + xt[0:ncol, pidx:pidx + 1] * a_ref[pidx // npg, pidx % npg]
        o_s = acc / jnp.maximum(xt[0:ncol, nparts + SUBLANES:nparts + SUBLANES + 1], 1e-30)
        o_w = a_w / jnp.maximum(xt[0:ncol, nparts + 2 * SUBLANES:nparts + 2 * SUBLANES + 1], 1e-30)

        ri = lax.broadcasted_iota(jnp.int32, (ncol, HEAD_DIM), 0)
        v_of = lambda t: jnp.where(ri < ncol // NSA_KV_HEADS, t[:, 2 * HEAD_DIM:3 * HEAD_DIM],
                                   t[:, 3 * HEAD_DIM:4 * HEAD_DIM])
        gt = g_ref[0]
        oc = oc_ref[0].reshape(ncol, HEAD_DIM)
        o_ref[0] = gt[:, 0:1] * oc + gt[:, 1:2] * v_of(o_s) + gt[:, 2:3] * v_of(o_w)


def _selwin_sample(pt, cache, qa, sel, cfrow, nbl, state, st_off, kvn, nbn, nbw, gates, oc, nb, n_pages, ts):
    steps = n_pages // PAGES_PER_STEP
    ncol = NSA_KV_HEADS * NSA_GROUP * ts
    full = lambda a: pl.BlockSpec(a.shape, lambda b, s, pt: (0,) * a.ndim)
    perb = lambda a: pl.BlockSpec((1,) + a.shape[1:], lambda b, s, pt: (b,) + (0,) * (a.ndim - 1))
    grid_spec = pltpu.PrefetchScalarGridSpec(
        num_scalar_prefetch=1,
        grid=(nb, steps),
        in_specs=_page_specs(n_pages) + [
            perb(qa), perb(sel), full(cfrow), full(nbl),
            pl.BlockSpec((1,) + state.shape[1:], lambda b, s, pt: (st_off + b, 0, 0, 0, 0)),
            perb(kvn), full(nbn), full(nbw), perb(gates), perb(oc)],
        out_specs=pl.BlockSpec((1, ncol, HEAD_DIM), lambda b, s, pt: (b, 0, 0)),
        scratch_shapes=[pltpu.VMEM((steps, PAGES_PER_STEP, LANES), F32),
                        pltpu.VMEM((steps, PAGES_PER_STEP, LANES), F32),
                        pltpu.VMEM((steps, PAGES_PER_STEP, ncol, KV_ROW), F32)])
    return pl.pallas_call(
        functools.partial(_selwin_sample_kernel, ts=ts),
        grid_spec=grid_spec,
        out_shape=jax.ShapeDtypeStruct((nb, ncol, HEAD_DIM), F32),
        compiler_params=_cparams(("parallel", "arbitrary")),
        name="selwin_sample",
    )(pt, *([cache] * PAGES_PER_STEP), qa, sel, cfrow, nbl, state, kvn, nbn, nbw, gates, oc)


def _sample_bias_tables(rel_bias, past, ts, n_buf):
    ncol = NSA_KV_HEADS * NSA_GROUP * ts
    col = jnp.arange(ncol)
    head = col // ts
    t = col % ts
    pick = lambda b: jnp.take_along_axis(b, jnp.broadcast_to(head, b.shape[:-1])[..., None], axis=-1)[..., 0]
    bias = lambda d: pick(_head_bias(rel_bias, d))
    padc = lambda a, fill=0.0: jnp.pad(a, ((0, 0), (0, LANES - a.shape[1])), constant_values=fill)
    nc = past // CMP_STRIDE
    c_end = jnp.arange(nc)[:, None] * CMP_STRIDE + CMP_BLOCK - 1
    tab = jnp.where(jnp.arange(nc)[:, None] < nc - 1, bias(past + t[None, :] - c_end), NEG)
    hc = ncol // NSA_KV_HEADS
    tab_cmp = jnp.stack([padc(tab[:, h * hc:(h + 1) * hc]) for h in range(NSA_KV_HEADS)])
    far = rel_bias[N_BUCKETS - 1].astype(F32)[head]
    cfrow = jnp.broadcast_to(padc(far[None, :]), (SUBLANES, LANES))
    kk = jnp.arange(PAGE_SIZE)[:, None]
    nbl = padc(bias(PAGE_SIZE + t[None, :] - kk))
    j = jnp.arange(SUBLANES)[:, None]
    dn = t[None, :] - j
    nbn = padc(jnp.where((dn >= 0) & (j < ts), bias(dn), NEG))
    r = jnp.arange(n_buf)[:, None]
    dw = n_buf + t[None, :] - r
    nbw = padc(jnp.where((dw >= 0) & (dw < WINDOW), bias(dw), NEG))
    return tab_cmp, cfrow, nbl, nbn, nbw


def _sample_layer(x, p, tabs, l, pt, cache_cmp, cache_slc, state_win, state_pool_l, nb, ts, n_pages, final, gf):
    tab_cmp, cfrow, nbl, nbn, nbw = tabs
    past = n_pages * PAGE_SIZE
    R = ts * nb
    ncol = NSA_KV_HEADS * NSA_GROUP * ts
    xp, q, gt, u, v, kvc, kvs, kvw = _inproj(x, p['g_mix'], p['w_in'], p['gn'], R)
    q5 = q.reshape(ts, nb, NSA_KV_HEADS, NSA_GROUP, HEAD_DIM)
    qs = q5.transpose(1, 2, 3, 0, 4).reshape(nb, NSA_KV_HEADS, NSA_GROUP * ts, HEAD_DIM)
    qs = jnp.pad(qs, ((0, 0), (0, 0), (0, LANES - NSA_GROUP * ts), (0, 0)))
    oc, sel = _cmp_sample(pt, cache_cmp, p['wx'], p['pe'], qs, tab_cmp, nb, n_pages, ts)
    qt = q5.transpose(1, 2, 4, 3, 0).reshape(nb, NSA_KV_HEADS, HEAD_DIM, NSA_GROUP * ts)
    eye = jnp.eye(NSA_KV_HEADS, dtype=qt.dtype)
    qa = jnp.einsum('bhdc,hk->bhdkc', qt, eye).reshape(nb, NSA_KV_HEADS * HEAD_DIM, ncol)
    qa = jnp.pad(qa, ((0, 0), (0, KV_ROW - NSA_KV_HEADS * HEAD_DIM), (0, LANES - ncol)))
    rows_of = lambda a: jnp.pad(a.reshape(ts, nb, KV_ROW).transpose(1, 0, 2), ((0, 0), (0, SUBLANES - ts), (0, 0)))
    kvn = jnp.concatenate([rows_of(kvs), rows_of(kvw)], axis=1)
    ng = NSA_GROUP * N_BRANCHES
    g5 = jnp.concatenate([gt[:, h * LANES:h * LANES + ng] for h in range(NSA_KV_HEADS)], axis=1)
    g5 = g5.reshape(ts, nb, NSA_KV_HEADS * NSA_GROUP, N_BRANCHES)
    gates = jnp.pad(g5.transpose(1, 2, 0, 3).reshape(nb, ncol, N_BRANCHES), ((0, 0), (0, 0), (0, LANES - N_BRANCHES)))
    o = _selwin_sample(pt, cache_slc, qa, sel, cfrow, nbl, state_win, l * nb, kvn, nbn, nbw, gates, oc,
                       nb, n_pages, ts)
    yb = o.reshape(nb, NSA_KV_HEADS * NSA_GROUP, ts, HEAD_DIM).transpose(2, 0, 1, 3).reshape(R, NSA_WIDTH)
    ext = jnp.concatenate([state_pool_l.transpose(1, 0, 2), xp.reshape(ts, nb, POOL_WIDTH)], axis=0)
    x2 = _mix_sample(x, ext, yb.astype(BF16), u.reshape(ts, nb, GMLP_WIDTH), v.reshape(ts, nb, GMLP_WIDTH),
                     p['wp'], p['ps'], p['wsx'], p['wb'][:SUBLANES], p['w_out'],
                     p['g_ffn'], p['w_up'], p['w_down'], gf, nb, ts, past, final)
    return x2, (xp, kvc, kvs, kvw, v)


def kernel(x_prompt, x_sample, cache_cmp_kv, cache_slc_kv, state_win_kv, state_pool, page_table, w_in, w_out,
           norm_mix, norm_ffn, norm_final, w_pool, pool_scale, cmp_w, cmp_pe, gmlp_ws, gmlp_b, gmlp_norm, w_up,
           w_down, rel_bias):
    B, T, _ = x_prompt.shape
    nb, ts, _ = x_sample.shape
    n_pages = page_table.shape[1]
    n_phys = cache_cmp_kv.shape[1]
    n_buf = state_win_kv.shape[2]
    past = n_pages * PAGE_SIZE
    depth = w_in.shape[0]
    assert T % TK_FAR == 0 and T >= WINDOW and n_pages % PAGES_PER_STEP == 0
    assert ts <= SUBLANES and ts <= POOL_STATE and n_buf == WINDOW and past >= WINDOW
    kv_tail = (2, NSA_KV_HEADS, HEAD_DIM)

    cache_cmp = _native_rows(cache_cmp_kv, depth * n_phys)
    cache_slc = _native_rows(cache_slc_kv, depth * n_phys)
    state_win = _native_rows(state_win_kv, depth * nb)
    ptabs = _prompt_bias_tables(rel_bias)
    stabs = _sample_bias_tables(rel_bias, past, ts, n_buf)
    gf = norm_final[None, :]
    xp = x_prompt.reshape(B * T, D_MODEL)
    xs = x_sample.transpose(1, 0, 2).reshape(ts * nb, D_MODEL)
    unmajor = lambda a: a.reshape(ts, nb, a.shape[-1]).transpose(1, 0, 2)

    outs = [[] for _ in range(9)]
    for l in range(depth):
        p = _prep_layer(l, w_in, w_out, norm_mix, norm_ffn, w_pool, pool_scale, cmp_w, cmp_pe, gmlp_ws, gmlp_b,
                        gmlp_norm, w_up, w_down)
        final = l == depth - 1
        xp, (pin, kvc_t, kvs_t, kvw_t) = _prompt_layer(xp, p, ptabs, B, T, final, gf)
        rows_of = lambda a: a.reshape((B,) + kv_tail + (a.shape[-1],)).transpose(0, 4, 1, 2, 3)
        outs[0].append(rows_of(kvc_t))
        outs[1].append(rows_of(kvs_t))
        outs[2].append(rows_of(kvw_t[:, :, T - WINDOW:]))
        outs[3].append(pin.reshape(B, T, POOL_WIDTH)[:, T - POOL_STATE:])

        pt = (page_table + l * n_phys).reshape(-1).astype(jnp.int32)
        xs, (sin, kvc_s, kvs_s, kvw_s, v_s) = _sample_layer(
            xs, p, stabs, l, pt, cache_cmp, cache_slc, state_win, state_pool[l], nb, ts, n_pages, final, gf)
        kvw_new = unmajor(kvw_s).reshape((nb, ts) + kv_tail)
        outs[4].append(unmajor(kvc_s).reshape((nb, ts) + kv_tail))
        outs[5].append(unmajor(kvs_s).reshape((nb, ts) + kv_tail))
        outs[6].append(jnp.concatenate([state_win_kv[l][:, ts:], kvw_new], axis=1))
        outs[7].append(jnp.concatenate([state_pool[l][:, ts:], unmajor(sin)], axis=1))
        outs[8].append(unmajor(v_s))

    y_prompt = xp.reshape(B, T, D_MODEL)
    y_sample = unmajor(xs)
    return (y_prompt, y_sample) + tuple(jnp.stack(o) for o in outs)
```

```python
import functools
import math

import numpy as np
import jax
import jax.numpy as jnp
from jax import lax
from jax.experimental import pallas as pl
from jax.experimental.pallas import tpu as pltpu

F32 = jnp.float32
BF16 = jnp.bfloat16

D_MODEL = 1024
DEPTH = 4
PAGE_SIZE = 128
HEAD_DIM = 64
POOL_WINDOWS = (2, 4, 8, 16)
POOL_WIDTH = 256
POOL_GROUP_DIM = 64
POOL_STATE = 15
NSA_WIDTH = 512
NSA_HEADS = 8
NSA_KV_HEADS = 2
NSA_GROUP = 4
KV_ROW = 2 * NSA_KV_HEADS * HEAD_DIM
CMP_STRIDE = 16
CMP_BLOCK = 32
SEL_BLOCK = 64
N_SELECT = 16
WINDOW = 512
N_BRANCHES = 3
FORCE_SCORE = 1000.0
GMLP_WIDTH = 256
GMLP_GROUPS = 4
GMLP_CHUNK = 128
D_FF = 4096
N_BUCKETS = 32
MAX_DISTANCE = 128
EPS = 1e-6
SCALE = HEAD_DIM ** -0.5

LANES = 128
SUBLANES = 8
VMEM_LIMIT = 56 * 1024 * 1024

NEG = -1e30
CMP_PAD = 16
TQ = 128
TQC = 512
CMP_NEAR = CMP_PAD + TQC // CMP_STRIDE
TK_FAR = 512
KPAD = 512
GATE_LANES = 2 * LANES

PROJ_SEGS = (('xp', 256), ('q', 512), ('kvc', 256), ('kvs', 256), ('kvw', 256), ('gt', 24), ('u', 256), ('v', 256))


def _cparams(sem):
    return pltpu.CompilerParams(dimension_semantics=sem, vmem_limit_bytes=VMEM_LIMIT)


def _dot(a, b):
    return jnp.dot(a, b, preferred_element_type=F32)


def _dot_nt(a, b):
    return lax.dot_general(a, b, (((1,), (1,)), ((), ())), preferred_element_type=F32)


def _gelu(x):
    c = math.sqrt(2.0 / math.pi)
    return 0.5 * x * (1.0 + jnp.tanh(c * (x + 0.044715 * (x * x * x))))


def _rms(x, g):
    return x * lax.rsqrt(jnp.mean(x * x, axis=-1, keepdims=True) + EPS) * g


def _inproj_kernel(x_ref, g_ref, w_ref, gn_ref, *refs, channel_major):
    if channel_major:
        wt_ref, xp_ref, q_ref, gt_ref, u_ref, v_ref, kvc_ref, *kv_refs = refs
    else:
        xp_ref, q_ref, gt_ref, u_ref, v_ref, kvc_ref, *kv_refs = refs
    x = x_ref[...]
    h = _rms(x, g_ref[...]).astype(BF16)

    def seg(lo, hi):
        return _dot(h, w_ref[:, lo:hi])

    xp_ref[...] = seg(0, 256)
    q_ref[...] = (seg(256, 768) * SCALE).astype(BF16)
    kvc_ref[...] = seg(768, 1024)
    if channel_major:
        for k, t_ref in enumerate(kv_refs):
            t_ref[0] = _dot_nt(wt_ref[k * KV_ROW:(k + 1) * KV_ROW, :], h)
    else:
        kv_refs[0][...] = seg(1024, 1280)
        kv_refs[1][...] = seg(1280, 1536)
    u_ref[...] = _gelu(seg(1536, 1792))
    gv = _gelu(seg(1792, 2048))
    sq = gv * gv
    lane = lax.broadcasted_iota(jnp.int32, sq.shape, 1)
    ms = jnp.zeros_like(sq)
    for g in range(GMLP_GROUPS):
        in_g = (lane >= g * 64) & (lane < (g + 1) * 64)
        s = jnp.sum(jnp.where(in_g, sq, 0.0), axis=-1, keepdims=True) * (1.0 / 64.0)
        ms = jnp.where(in_g, s, ms)
    v_ref[...] = gv * lax.rsqrt(ms + EPS) * gn_ref[...]
    gt_ref[...] = jax.nn.sigmoid(seg(2048, 2048 + GATE_LANES))


def _inproj(x, g, w, gn, tm, seq=None, wt=None):
    R = x.shape[0]
    f = lambda n, dt=F32: jax.ShapeDtypeStruct((R, n), dt)
    row = lambda n: pl.BlockSpec((tm, n), lambda i: (i, 0))
    full = lambda a: pl.BlockSpec(a.shape, lambda i: (0,) * a.ndim)
    out_specs = [row(256), row(512), row(GATE_LANES), row(256), row(256), row(256)]
    out_shape = [f(256), f(512, BF16), f(GATE_LANES), f(256), f(256), f(256)]
    if seq is None:
        out_specs += [row(256), row(256)]
        out_shape += [f(256), f(256)]
    else:
        per_seq = seq // tm
        out_specs += [pl.BlockSpec((1, KV_ROW, tm), lambda i: (i // per_seq, 0, i % per_seq))] * 3
        out_shape += [jax.ShapeDtypeStruct((R // seq, KV_ROW, seq), F32)] * 3
    args = (x, g, w, gn) if seq is None else (x, g, w, gn, wt)
    return pl.pallas_call(
        functools.partial(_inproj_kernel, channel_major=seq is not None),
        grid=(R // tm,),
        in_specs=[row(D_MODEL)] + [full(a) for a in args[1:]],
        out_specs=out_specs,
        out_shape=out_shape,
        compiler_params=_cparams(("parallel",)),
        name="inproj",
    )(*args)


def _out_proj(x, ya, yb, yc, wo_ref):
    acc = _dot(ya.astype(BF16), wo_ref[0:256, :])
    acc = acc + _dot(yb, wo_ref[256:768, :])
    acc = acc + _dot(yc.astype(BF16), wo_ref[768:1024, :])
    return x + acc


def _pool_tail(win2, win4, win8, win16, cur, cnt, wp_ref, ps_ref):
    lane = lax.broadcasted_iota(jnp.int32, cur.shape, 1)
    win = jnp.where(lane < 64, win2, jnp.where(lane < 128, win4, jnp.where(lane < 192, win8, win16)))
    pooled = win / cnt - cur
    return _dot(pooled.astype(BF16), wp_ref[...]) * ps_ref[...]


FF_CHUNK = 1024


def _ffn_rows(x, g_ref, wu_ref, wd_ref, gf_ref, final):
    h = _rms(x, g_ref[...]).astype(BF16)
    acc = x
    for c in range(D_FF // FF_CHUNK):
        a = jnp.maximum(_dot(h, wu_ref[:, c * FF_CHUNK:(c + 1) * FF_CHUNK]), 0.0)
        acc = acc + _dot((a * a).astype(BF16), wd_ref[c * FF_CHUNK:(c + 1) * FF_CHUNK, :])
    return _rms(acc, gf_ref[...]) if final else acc


def _ffn_specs(g, wu, wd, gf):
    full = lambda a: pl.BlockSpec(a.shape, lambda i: (0,) * a.ndim)
    once = lambda a: pl.BlockSpec(a.shape, lambda i: (0,) * a.ndim, pipeline_mode=pl.Buffered(1))
    return [full(g), once(wu), once(wd), full(gf)]


def _mix_prompt_kernel(x_ref, xp_ref, halo_ref, yb_ref, u_ref, v_ref, wp_ref, ps_ref, ws_ref, wb_ref, wo_ref,
                       g_ref, wu_ref, wd_ref, gf_ref, o_ref, ext_ref, *, tm, seq, final):
    i = pl.program_id(0)
    t0 = (i * tm) % seq
    halo = jnp.where(t0 > 0, halo_ref[...], 0.0)
    cur = xp_ref[...]
    ext_ref[0:16, :] = halo
    ext_ref[16:16 + tm, :] = cur
    e = ext_ref[...]
    b2 = e[1:] + e[:-1]
    b4 = b2[2:] + b2[:-2]
    b8 = b4[4:] + b4[:-4]
    b16 = b8[8:] + b8[:-8]
    win2 = b2[15:15 + tm]
    win4 = b4[13:13 + tm]
    win8 = b8[9:9 + tm]
    win16 = b16[1:1 + tm]
    rowi = lax.broadcasted_iota(jnp.int32, cur.shape, 0)
    lane = lax.broadcasted_iota(jnp.int32, cur.shape, 1)
    wsz = jnp.where(lane < 64, 2, jnp.where(lane < 128, 4, jnp.where(lane < 192, 8, 16)))
    cnt = jnp.minimum(t0 + rowi + 1, wsz).astype(F32)
    ya = _pool_tail(win2, win4, win8, win16, cur, cnt, wp_ref, ps_ref)

    ci = lax.broadcasted_iota(jnp.int32, (GMLP_CHUNK, GMLP_CHUNK), 0)
    cj = lax.broadcasted_iota(jnp.int32, (GMLP_CHUNK, GMLP_CHUNK), 1)
    lane_c = lax.broadcasted_iota(jnp.int32, (GMLP_CHUNK, GMLP_WIDTH), 1)
    wts = [jnp.where(ci >= cj, ws_ref[g], 0.0).astype(BF16) for g in range(GMLP_GROUPS)]
    parts = []
    for c in range(tm // GMLP_CHUNK):
        vc = v_ref[c * GMLP_CHUNK:(c + 1) * GMLP_CHUNK, :].astype(BF16)
        s = jnp.zeros((GMLP_CHUNK, GMLP_WIDTH), F32)
        for g in range(GMLP_GROUPS):
            sg = _dot(wts[g], vc)
            s = jnp.where((lane_c >= g * 64) & (lane_c < (g + 1) * 64), sg, s)
        parts.append(u_ref[c * GMLP_CHUNK:(c + 1) * GMLP_CHUNK, :] * (s + wb_ref[...]))
    yc = jnp.concatenate(parts, axis=0) if len(parts) > 1 else parts[0]

    x1 = _out_proj(x_ref[...], ya, yb_ref[...], yc, wo_ref)
    o_ref[...] = _ffn_rows(x1, g_ref, wu_ref, wd_ref, gf_ref, final)


def _mix_prompt(x, xp, yb, u, v, wp, ps, ws, wb, wo, g, wu, wd, gf, tm, seq, final):
    R = x.shape[0]
    row = lambda n: pl.BlockSpec((tm, n), lambda i: (i, 0))
    full = lambda a: pl.BlockSpec(a.shape, lambda i: (0,) * a.ndim)
    halo = pl.BlockSpec((16, 256), lambda i: (jnp.maximum(i * (tm // 16) - 1, 0), 0))
    return pl.pallas_call(
        functools.partial(_mix_prompt_kernel, tm=tm, seq=seq, final=final),
        grid=(R // tm,),
        in_specs=[row(D_MODEL), row(256), halo, row(512), row(256), row(256),
                  full(wp), full(ps), full(ws), full(wb), full(wo)] + _ffn_specs(g, wu, wd, gf),
        out_specs=row(D_MODEL),
        out_shape=jax.ShapeDtypeStruct((R, D_MODEL), F32),
        scratch_shapes=[pltpu.VMEM((tm + 16, 256), F32)],
        compiler_params=_cparams(("parallel",)),
        name="mix_prompt",
    )(x, xp, xp, yb, u, v, wp, ps, ws, wb, wo, g, wu, wd, gf)


def _mix_sample_kernel(x_ref, ext_ref, yb_ref, u_ref, v_ref, wp_ref, ps_ref, wsx_ref, wb_ref, wo_ref,
                       g_ref, wu_ref, wd_ref, gf_ref, o_ref, *, nb, ts, past, final):
    lane = lax.broadcasted_iota(jnp.int32, (nb, POOL_WIDTH), 1)
    wsz = jnp.where(lane < 64, 2, jnp.where(lane < 128, 4, jnp.where(lane < 192, 8, 16)))
    yas, ycs = [], []
    for t in range(ts):
        top = POOL_STATE + t
        acc = ext_ref[top] + ext_ref[top - 1]
        wins = [acc]
        for w in (4, 8, 16):
            for j in range(w // 2, w):
                acc = acc + ext_ref[top - j]
            wins.append(acc)
        cnt = jnp.minimum(past + t + 1, wsz).astype(F32)
        yas.append(_pool_tail(wins[0], wins[1], wins[2], wins[3], ext_ref[top], cnt, wp_ref, ps_ref))
        s = wb_ref[t:t + 1, :]
        s = jnp.broadcast_to(s, (nb, GMLP_WIDTH))
        for j in range(t + 1):
            s = s + wsx_ref[t, j:j + 1, :] * v_ref[j]
        ycs.append(u_ref[t] * s)
    ya = jnp.concatenate(yas, axis=0)
    yc = jnp.concatenate(ycs, axis=0)
    x1 = _out_proj(x_ref[...], ya, yb_ref[...], yc, wo_ref)
    o_ref[...] = _ffn_rows(x1, g_ref, wu_ref, wd_ref, gf_ref, final)


def _mix_sample(x, ext, yb, u, v, wp, ps, wsx, wb4, wo, g, wu, wd, gf, nb, ts, past, final):
    R = x.shape[0]
    args = (x, ext, yb, u, v, wp, ps, wsx, wb4, wo)
    full = lambda a: pl.BlockSpec(a.shape, lambda i: (0,) * a.ndim)
    return pl.pallas_call(
        functools.partial(_mix_sample_kernel, nb=nb, ts=ts, past=past, final=final),
        grid=(1,),
        in_specs=[full(a) for a in args] + _ffn_specs(g, wu, wd, gf),
        out_specs=pl.BlockSpec((R, D_MODEL), lambda i: (0, 0)),
        out_shape=jax.ShapeDtypeStruct((R, D_MODEL), F32),
        compiler_params=_cparams(("arbitrary",)),
        name="mix_sample",
    )(*args, g, wu, wd, gf)


def _compress_rows(read_rows, n, wx):
    lo = jnp.zeros((n, wx.shape[1]), F32)
    hi = jnp.zeros((n, wx.shape[1]), F32)
    for j in range(CMP_STRIDE):
        xj = read_rows(j)
        lo = lo + xj * wx[j:j + 1, :]
        hi = hi + xj * wx[CMP_STRIDE + j:CMP_STRIDE + j + 1, :]
    return lo, hi


def _combine_halves(lo, hi, pe, wx):
    n = lo.shape[0]
    pe_term = jnp.sum(pe * wx, axis=0, keepdims=True)
    comp = lo + pltpu.roll(hi, n - 1, 0) + pe_term
    rowi = lax.broadcasted_iota(jnp.int32, comp.shape, 0)
    return jnp.where(rowi < n - 1, comp, 0.0)


def _compress_prompt_kernel(x_ref, wx_ref, pe_ref, o_ref, *, n):
    wx = wx_ref[...]
    lo, hi = _compress_rows(lambda j: x_ref[0, pl.ds(j, n, stride=CMP_STRIDE), :], n, wx)
    o_ref[0, 0:CMP_PAD, :] = jnp.zeros((CMP_PAD, LANES), F32)
    o_ref[0, CMP_PAD:CMP_PAD + n, :] = _combine_halves(lo, hi, pe_ref[...], wx)


def _compress_prompt(kvc, wx, pe):
    B, T, _ = kvc.shape
    n = T // CMP_STRIDE
    half = lambda rows: pl.BlockSpec((rows, LANES), lambda b, e: (0, e))
    return pl.pallas_call(
        functools.partial(_compress_prompt_kernel, n=n),
        grid=(B, KV_ROW // LANES),
        in_specs=[pl.BlockSpec((1, T, LANES), lambda b, e: (b, 0, e)), half(CMP_BLOCK), half(CMP_BLOCK)],
        out_specs=pl.BlockSpec((1, n + CMP_PAD, LANES), lambda b, e: (b, 0, e)),
        out_shape=jax.ShapeDtypeStruct((B, n + CMP_PAD, KV_ROW), F32),
        compiler_params=_cparams(("parallel", "parallel")),
        name="compress_prompt",
    )(kvc, wx, pe)


def _select_topk(score, n_pick):
    s_iota = lax.broadcasted_iota(jnp.int32, score.shape, 0)
    big = score.shape[0]
    work = score
    for _ in range(n_pick):
        m = jnp.max(work, axis=0, keepdims=True)
        idx = jnp.min(jnp.where(work == m, s_iota, big), axis=0, keepdims=True)
        work = jnp.where(s_iota == idx, -2.0, work)
    return jnp.where((work == -2.0) & (score >= 0.0), 1.0, 0.0)


def _softmax_rows(z):
    m = jnp.maximum(jnp.max(z, axis=0, keepdims=True), 0.1 * NEG)
    p = jnp.exp(z - m)
    den = jnp.maximum(jnp.sum(p, axis=0, keepdims=True), 1e-30)
    return p * (1.0 / den)


def _cmp_prompt_kernel(q_ref, kc_ref, vct_ref, nb_ref, cf_ref, oc_ref, sn_ref, raw_ref, z_ref, ps_ref, *, ncp):
    i = pl.program_id(2)
    hh = pl.program_id(1)
    tq = q_ref.shape[1]
    r0 = pl.multiple_of(i * (tq // CMP_STRIDE), SUBLANES)
    q = q_ref[0]
    vct = vct_ref[0, 0]
    rown = lax.broadcasted_iota(jnp.int32, (CMP_NEAR, tq), 0) + r0

    def attend(rows):
        kc = kc_ref[0, 0, 0:rows, :]
        rowi = lax.broadcasted_iota(jnp.int32, (rows, tq), 0)
        far = (rowi >= CMP_PAD) & (rowi < r0)
        psum = jnp.zeros((rows, tq), F32)
        octs = []
        for g in range(NSA_GROUP):
            qg = q[:, g * HEAD_DIM:(g + 1) * HEAD_DIM]
            raw_ref[0:rows, :] = _dot_nt(kc, qg)
            cf = cf_ref[hh * NSA_GROUP + g]
            z_ref[0:rows, :] = jnp.where(far, raw_ref[0:rows, :] + cf[0:1, :], NEG)
            nb = nb_ref[hh * NSA_GROUP + g]
            zn = raw_ref[pl.ds(r0, CMP_NEAR), :] + nb
            z_ref[pl.ds(r0, CMP_NEAR), :] = jnp.where((rown >= CMP_PAD) & (nb > 0.5 * NEG), zn, NEG)
            pc = _softmax_rows(z_ref[0:rows, :])
            psum = psum + pc
            pcb = pc.astype(BF16)
            if rows < ncp:
                pcb = jnp.concatenate([pcb, jnp.zeros((ncp - rows, tq), BF16)], axis=0)
            octs.append(_dot(vct, pcb))
        oc_ref[0] = jnp.concatenate(octs, axis=0).T
        for c in range(tq // LANES):
            ps_ref[c, 0:rows, :] = psum[:, c * LANES:(c + 1) * LANES]
            if rows < ps_ref.shape[1]:
                ps_ref[c, rows:, :] = jnp.zeros((ps_ref.shape[1] - rows, LANES), F32)

    n_var = (ncp - CMP_PAD + LANES - 1) // LANES
    var = (r0 + CMP_NEAR - CMP_PAD + LANES - 1) // LANES - 1
    for v in range(n_var):
        pl.when(var == v)(functools.partial(attend, min(CMP_PAD + (v + 1) * LANES, ncp)))

    ns = LANES
    halves = []
    for c in range(tq // LANES):
        part = ps_ref[c, pl.ds(CMP_PAD, ns, stride=4), :]
        for j in range(1, 4):
            part = part + ps_ref[c, pl.ds(CMP_PAD + j, ns, stride=4), :]
        halves.append(part)
    p_slc = jnp.concatenate(halves, axis=1)
    s_iota = lax.broadcasted_iota(jnp.int32, (ns, tq), 0)
    tt = lax.broadcasted_iota(jnp.int32, (ns, tq), 1)
    cur = (i * tq + tt) // SEL_BLOCK
    forced = (s_iota == 0) | (s_iota == cur) | (s_iota == cur - 1)
    score = jnp.where(s_iota <= cur, jnp.where(forced, FORCE_SCORE, p_slc), -1.0)
    sel = _select_topk(score, N_SELECT)
    sn_ref[0, 0] = jnp.where(sel.T > 0.5, 0.0, NEG).astype(BF16)


def _cmp_prompt(q, kc, vct, nbc, cfar):
    B, T, _ = q.shape
    ncp = kc.shape[2]
    ns = LANES
    nq = T // TQC
    full = lambda a: pl.BlockSpec(a.shape, lambda b, h, i: (0,) * a.ndim)
    return pl.pallas_call(
        functools.partial(_cmp_prompt_kernel, ncp=ncp),
        grid=(B, NSA_KV_HEADS, nq),
        in_specs=[pl.BlockSpec((1, TQC, 256), lambda b, h, i: (b, i, h)),
                  pl.BlockSpec((1, 1, ncp, HEAD_DIM), lambda b, h, i: (b, h, 0, 0)),
                  pl.BlockSpec((1, 1, HEAD_DIM, ncp), lambda b, h, i: (b, h, 0, 0)),
                  full(nbc), full(cfar)],
        out_specs=[pl.BlockSpec((1, TQC, 256), lambda b, h, i: (b, i, h)),
                   pl.BlockSpec((1, 1, TQC, ns), lambda b, h, i: (b, h, i, 0))],
        out_shape=[jax.ShapeDtypeStruct((B, T, NSA_WIDTH), F32),
                   jax.ShapeDtypeStruct((B, NSA_KV_HEADS, T, ns), BF16)],
        scratch_shapes=[pltpu.VMEM((ncp, TQC), F32), pltpu.VMEM((ncp, TQC), F32),
                        pltpu.VMEM((TQC // LANES, max(ncp, CMP_PAD + 4 * LANES), LANES), F32)],
        compiler_params=_cparams(("parallel", "parallel", "parallel")),
        name="cmp_prompt",
    )(q, kc, vct, nbc, cfar)


def _selwin_prompt_kernel(q_ref, sn_ref, ks_ref, vs_ref, kw_ref, vw_ref, oh_ref, g_ref, oc_ref, nbn_ref, nbw_ref,
                          gsel_ref, o_ref, ka_ref, va_ref, kwa_ref, vwa_ref, ms_ref, as_ref, za_ref, zb_ref):
    i = pl.program_id(2)
    R4 = NSA_GROUP * TQ
    ncols = ka_ref.shape[1]

    @pl.when(i == 0)
    def _():
        zpad = jnp.zeros((HEAD_DIM, KPAD), BF16)
        ones = jnp.ones((HEAD_DIM, ncols), BF16)
        row = lax.broadcasted_iota(jnp.int32, (HEAD_DIM, ncols), 0)
        col = lax.broadcasted_iota(jnp.int32, (HEAD_DIM, ncols), 1)
        ka_ref[0:LANES, :] = oh_ref[...]
        kwa_ref[HEAD_DIM:, :] = jnp.where((row == 0) & (col < KPAD), NEG, 0.0).astype(BF16)
        for dst, src in ((ka_ref.at[LANES:], ks_ref), (va_ref.at[0:HEAD_DIM], vs_ref),
                         (kwa_ref.at[0:HEAD_DIM], kw_ref), (vwa_ref.at[0:HEAD_DIM], vw_ref)):
            dst[:, 0:KPAD] = zpad
            dst[:, KPAD:] = src[0].astype(BF16)
        va_ref[HEAD_DIM:, :] = ones
        vwa_ref[HEAD_DIM:, :] = ones

    q = q_ref[0]
    q4 = jnp.concatenate([q[:, g * HEAD_DIM:(g + 1) * HEAD_DIM] for g in range(NSA_GROUP)], axis=0)
    sn = sn_ref[0, 0]
    blk = lax.broadcasted_iota(jnp.int32, (TQ, LANES), 1)
    first_near = (i - 1) * (TQ // SEL_BLOCK)
    sn_far = jnp.where(blk >= first_near, NEG, sn.astype(F32)).astype(BF16)
    qa_near = jnp.concatenate([jnp.concatenate([sn] * NSA_GROUP, axis=0), q4], axis=1)
    qa_far = jnp.concatenate([jnp.concatenate([sn_far] * NSA_GROUP, axis=0), q4], axis=1)
    qw = jnp.concatenate([q4, jnp.ones((R4, HEAD_DIM), BF16)], axis=1)

    def one_pass(z, vt):
        m = jnp.max(z, axis=1, keepdims=True)
        acc = _dot_nt(jnp.exp(z - m).astype(BF16), vt)
        return m, acc

    kw0 = pl.multiple_of(i * TQ, TQ)
    zw = _dot(qw, kwa_ref[:, pl.ds(kw0, WINDOW + TQ)]) + nbw_ref[0]
    _, a_w = one_pass(zw, vwa_ref[:, pl.ds(kw0, WINDOW + TQ)])

    kn0 = pl.multiple_of((i - 1) * TQ + KPAD, TQ)
    zn = _dot(qa_near, ka_ref[:, pl.ds(kn0, 2 * TQ)]) + nbn_ref[0]
    m0, a0 = one_pass(zn, va_ref[:, pl.ds(kn0, 2 * TQ)])
    ms_ref[...] = jnp.broadcast_to(m0, (R4, LANES))
    as_ref[...] = a0

    def far_logits(j):
        k0 = pl.multiple_of(j * TK_FAR + KPAD, TK_FAR)
        return _dot(qa_far, ka_ref[:, pl.ds(k0, TK_FAR)])

    def far_update(j, z):
        k0 = pl.multiple_of(j * TK_FAR + KPAD, TK_FAR)
        m_prev = ms_ref[...]
        m_new = jnp.maximum(m_prev, jnp.max(z, axis=1, keepdims=True))
        p = jnp.exp(z - jnp.concatenate([m_new] * (TK_FAR // LANES), axis=1))
        as_ref[...] = jnp.exp(m_prev - m_new) * as_ref[...] + _dot_nt(p.astype(BF16), va_ref[:, pl.ds(k0, TK_FAR)])
        ms_ref[...] = m_new

    n_far = (jnp.maximum(i - 1, 0) * TQ + TK_FAR - 1) // TK_FAR
    odd = n_far % 2
    last = n_far - 1

    def far_pair(t):
        zb_ref[...] = far_logits(t + 1)
        far_update(t, za_ref[...])
        za_ref[...] = far_logits(jnp.minimum(t + 2, last))
        far_update(t + 1, zb_ref[...])

    n_pairs = (n_far + 1) // 2

    za_ref[...] = far_logits(-odd)

    @pl.when(n_pairs % 2 == 1)
    def _():
        far_pair(-odd)

    def far_body(jj, carry):
        t = 4 * jj + 2 * (n_pairs % 2) - odd
        far_pair(t)
        far_pair(t + 2)
        return carry

    lax.fori_loop(0, n_pairs // 2, far_body, 0)

    gt = g_ref[...]
    lane = lax.broadcasted_iota(jnp.int32, (TQ, LANES), 1)
    own = [jnp.where((lane >= N_BRANCHES * g) & (lane < N_BRANCHES * (g + 1)), gt, 0.0) for g in range(NSA_GROUP)]
    gt4 = jnp.concatenate(own, axis=0)
    hi = gt4.astype(BF16)
    lo = (gt4 - hi.astype(F32)).astype(BF16)
    gates = _dot(jnp.concatenate([hi, lo], axis=1), gsel_ref[...])

    def scaled(acc, gate):
        f = gate / jnp.maximum(acc, 1e-30)
        return acc * pltpu.roll(f, HEAD_DIM, 1)

    o_sw = scaled(as_ref[...], gates[:, LANES:2 * LANES]) + scaled(a_w, gates[:, 2 * LANES:])
    oc = oc_ref[0]
    outs = []
    for g in range(NSA_GROUP):
        sl = slice(g * TQ, (g + 1) * TQ)
        outs.append(gates[sl, 0:HEAD_DIM] * oc[:, g * HEAD_DIM:(g + 1) * HEAD_DIM] + o_sw[sl, 0:HEAD_DIM])
    o_ref[0] = jnp.concatenate(outs, axis=1).astype(BF16)


def _selwin_prompt(q, sn, kvs_t, kvw_t, onehot_t, gates, oc, nbn, nbw):
    B, T, _ = q.shape
    src = jnp.arange(2 * LANES) % LANES
    dst = jnp.arange(N_BRANCHES * LANES) // LANES
    gsel = ((src[:, None] < NSA_GROUP * N_BRANCHES) & (src[:, None] % N_BRANCHES == dst[None, :])).astype(BF16)
    nq = T // TQ
    R4 = NSA_GROUP * TQ
    ncols = T + KPAD
    k_of = pl.BlockSpec((1, HEAD_DIM, T), lambda b, h, i: (b, h, 0))
    v_of = pl.BlockSpec((1, HEAD_DIM, T), lambda b, h, i: (b, NSA_KV_HEADS + h, 0))
    perh = lambda a: pl.BlockSpec((1,) + a.shape[1:], lambda b, h, i: (h, 0, 0))
    qtile = pl.BlockSpec((1, TQ, 256), lambda b, h, i: (b, i, h))
    return pl.pallas_call(
        _selwin_prompt_kernel,
        grid=(B, NSA_KV_HEADS, nq),
        in_specs=[qtile, pl.BlockSpec((1, 1, TQ, sn.shape[-1]), lambda b, h, i: (b, h, i, 0)),
                  k_of, v_of, k_of, v_of, pl.BlockSpec(onehot_t.shape, lambda b, h, i: (0, 0)),
                  pl.BlockSpec((TQ, LANES), lambda b, h, i: (b * nq + i, h)), qtile, perh(nbn), perh(nbw),
                  pl.BlockSpec(gsel.shape, lambda b, h, i: (0, 0))],
        out_specs=qtile,
        out_shape=jax.ShapeDtypeStruct((B, T, NSA_WIDTH), BF16),
        scratch_shapes=[pltpu.VMEM((LANES + HEAD_DIM, ncols), BF16)] + [pltpu.VMEM((LANES, ncols), BF16)] * 3
        + [pltpu.VMEM((R4, LANES), F32)] * 2 + [pltpu.VMEM((R4, TK_FAR), F32)] * 2,
        compiler_params=_cparams(("parallel", "parallel", "arbitrary")),
        name="selwin_prompt",
    )(q, sn, kvs_t, kvs_t, kvw_t, kvw_t, onehot_t, gates, oc, nbn, nbw, gsel)


def _rel_bucket(dist):
    d = jnp.maximum(dist, 0)
    n_exact = N_BUCKETS // 2
    d_f = jnp.maximum(d, 1).astype(F32)
    large = n_exact + (jnp.log(d_f / n_exact) / math.log(MAX_DISTANCE / n_exact)
                       * (N_BUCKETS - n_exact)).astype(jnp.int32)
    large = jnp.minimum(large, N_BUCKETS - 1)
    return jnp.where(d < n_exact, d, large)


def _head_bias(rel_bias, dist):
    onehot = (_rel_bucket(dist)[..., None] == jnp.arange(N_BUCKETS)).astype(F32)
    return jnp.einsum('...k,kh->...h', onehot, rel_bias.astype(F32), precision=lax.Precision.HIGHEST)


def _prompt_bias_tables(rel_bias):
    far = rel_bias[N_BUCKETS - 1].astype(F32)
    cfar = jnp.broadcast_to(far[:, None, None], (NSA_HEADS, SUBLANES, TQC))
    k = jnp.arange(CMP_NEAR)[:, None]
    tt = jnp.arange(TQC)[None, :]
    d = tt + CMP_PAD * CMP_STRIDE - (CMP_BLOCK - 1) - CMP_STRIDE * k
    nbc = jnp.where((d >= 0)[..., None], _head_bias(rel_bias, d), NEG).transpose(2, 0, 1)
    tq = jnp.arange(TQ)[:, None]

    def tile_table(first_dist, n_keys, max_dist):
        d = first_dist + tq - jnp.arange(n_keys)[None, :]
        b = jnp.where(((d >= 0) & (d < max_dist))[..., None], _head_bias(rel_bias, d) - far, NEG)
        return b.transpose(2, 0, 1).reshape(NSA_KV_HEADS, NSA_GROUP * TQ, n_keys)

    nbn = tile_table(TQ, 2 * TQ, 2 * TQ + 1)
    nbw = tile_table(WINDOW, WINDOW + TQ, WINDOW)
    return cfar, nbc, nbn, nbw


def _prep_layer(l, w_in, w_out, norm_mix, norm_ffn, w_pool, pool_scale, cmp_w, cmp_pe, gmlp_ws, gmlp_b,
                gmlp_norm, w_up, w_down):
    w = w_in[l]
    offs = np.cumsum([0] + [n for _, n in PROJ_SEGS])
    seg = {name: w[:, offs[k]:offs[k + 1]] for k, (name, _) in enumerate(PROJ_SEGS)}
    order = ('xp', 'q', 'kvc', 'kvs', 'kvw', 'u', 'v')
    ng = NSA_GROUP * N_BRANCHES
    gpad = jnp.zeros((D_MODEL, LANES - ng), w.dtype)
    gate_cols = [c for h in range(NSA_KV_HEADS) for c in (seg['gt'][:, h * ng:(h + 1) * ng], gpad)]
    w_perm = jnp.concatenate([seg[n] for n in order] + gate_cols, axis=1)
    eye = jnp.eye(len(POOL_WINDOWS), dtype=F32)
    wp = jnp.einsum('gcd,gh->gchd', w_pool[l], eye).reshape(POOL_WIDTH, POOL_WIDTH)
    wx = jnp.broadcast_to(cmp_w[l].transpose(1, 0, 2)[..., None], (CMP_BLOCK, 2, NSA_KV_HEADS, HEAD_DIM))
    return dict(
        w_in=w_perm.astype(BF16), g_mix=norm_mix[l][None, :],
        w_kv_t=jnp.concatenate([seg['kvc'], seg['kvs'], seg['kvw']], axis=1).T.astype(BF16),
        g_ffn=norm_ffn[l][None, :],
        gn=gmlp_norm[l][None, :], wp=wp.astype(BF16), ps=pool_scale[l][None, :],
        wx=wx.reshape(CMP_BLOCK, KV_ROW), pe=cmp_pe[l].transpose(1, 0, 2, 3).reshape(CMP_BLOCK, KV_ROW),
        ws=gmlp_ws[l], wb=jnp.repeat(gmlp_b[l].T, POOL_GROUP_DIM, axis=1),
        wsx=jnp.repeat(gmlp_ws[l].transpose(1, 2, 0), POOL_GROUP_DIM, axis=2)[:8, :8],
        w_out=w_out[l].astype(BF16), w_up=w_up[l].astype(BF16), w_down=w_down[l].astype(BF16))


def _split_heads(kv, part):
    B, T, _ = kv.shape
    return kv.reshape(B, T, 2, NSA_KV_HEADS, HEAD_DIM)[:, :, part].transpose(0, 2, 1, 3).astype(BF16)


def _prompt_layer(x, p, tabs, B, T, final, gf):
    cfar, nbc, nbn, nbw = tabs
    tm = min(512, B * T)
    xp, q, gt, u, v, kvc, kvc_t, kvs_t, kvw_t = _inproj(x, p['g_mix'], p['w_in'], p['gn'], tm, seq=T,
                                                        wt=p['w_kv_t'])
    comp = _compress_prompt(kvc.reshape(B, T, KV_ROW), p['wx'], p['pe'])
    kc = _split_heads(comp, 0)
    vct = _split_heads(comp, 1).transpose(0, 1, 3, 2)
    q3 = q.reshape(B, T, NSA_WIDTH)
    oc, sn = _cmp_prompt(q3, kc, vct, nbc, cfar)
    pos = jnp.arange(-KPAD, T)
    blk_of = jnp.where(pos >= 0, pos // SEL_BLOCK, LANES - 1)
    onehot_t = (jnp.arange(LANES)[:, None] == blk_of[None, :]).astype(BF16)
    yb = _selwin_prompt(q3, sn, kvs_t, kvw_t, onehot_t, gt, oc, nbn, nbw)
    x2 = _mix_prompt(x, xp, yb.reshape(B * T, NSA_WIDTH), u, v, p['wp'], p['ps'], p['ws'], p['wb'], p['w_out'],
                     p['g_ffn'], p['w_up'], p['w_down'], gf, tm, T, final)
    return x2, (xp, kvc_t, kvs_t, kvw_t)


CMP_PAGES_PER_STEP = 32
SEL_PAGES_PER_STEP = 64
SEL_PER_PAGE = PAGE_SIZE // SEL_BLOCK


def _cmp_sample_kernel(pt_ref, *refs, ts):
    del pt_ref
    npg = CMP_PAGES_PER_STEP
    nhalf = KV_ROW // LANES
    pages = refs[:npg]
    wx_ref, pe_ref, q_ref, tab_ref, oc_ref, sel_ref, lo_ref, hi_ref, ps_ref, pg_ref = refs[npg:]
    s = pl.program_id(1)
    rows = PAGE_SIZE // CMP_STRIDE
    for k in range(npg):
        for e in range(nhalf):
            pg_ref[...] = _page_half_rows(pages[k], e)
            lo, hi = _compress_rows(lambda j: pg_ref[pl.ds(j, rows, stride=CMP_STRIDE), :], rows,
                                    wx_ref[:, e * LANES:(e + 1) * LANES])
            lo_ref[s, k * rows:(k + 1) * rows, e * LANES:(e + 1) * LANES] = lo
            hi_ref[s, k * rows:(k + 1) * rows, e * LANES:(e + 1) * LANES] = hi

    @pl.when(s == pl.num_programs(1) - 1)
    def _():
        nc = lo_ref.shape[0] * lo_ref.shape[1]
        comp = _combine_halves(lo_ref[...].reshape(nc, KV_ROW), hi_ref[...].reshape(nc, KV_ROW), pe_ref[...],
                               wx_ref[...])
        ns = nc // 4
        lane = lax.broadcasted_iota(jnp.int32, (ns, LANES), 1)
        s_iota = lax.broadcasted_iota(jnp.int32, (ns, LANES), 0)
        ncol = NSA_GROUP * ts
        selacc = jnp.zeros((ns, LANES), F32)
        for h in range(NSA_KV_HEADS):
            kc = comp[:, h * HEAD_DIM:(h + 1) * HEAD_DIM].astype(BF16)
            vc = comp[:, (NSA_KV_HEADS + h) * HEAD_DIM:(NSA_KV_HEADS + h + 1) * HEAD_DIM].astype(BF16)
            pc = _softmax_rows(_dot_nt(kc, q_ref[0, h]) + tab_ref[h])
            oc_ref[0, h] = pl.dot(pc.astype(BF16), vc, trans_a=True)[0:ncol, :]
            ps = pc
            for g in range(1, NSA_GROUP):
                ps = ps + pltpu.roll(pc, g * ts, 1)
            ps_ref[...] = ps
            p_slc = ps_ref[pl.ds(0, ns, stride=4), :]
            for j in range(1, 4):
                p_slc = p_slc + ps_ref[pl.ds(j, ns, stride=4), :]
            forced = (s_iota == 0) | (s_iota == ns - 1)
            sel = _select_topk(jnp.where(forced, FORCE_SCORE, p_slc), N_SELECT - 1)
            for g in range(NSA_GROUP):
                dst = h * ncol + g * ts
                shift = (dst - (ncol - ts)) % LANES
                moved = pltpu.roll(sel, shift, 1) if shift else sel
                selacc = jnp.where((lane >= dst) & (lane < dst + ts), moved, selacc)
        sel_ref[0] = selacc.reshape(sel_ref.shape[1:])


def _page_half_rows(page_ref, e):
    t = page_ref[0, e]
    return t.reshape(NSA_KV_HEADS * HEAD_DIM, t.shape[-1]).T


def _native_rows(a, lead):
    nd = a.ndim
    t = jnp.transpose(a, tuple(range(nd - 4)) + (nd - 3, nd - 2, nd - 1, nd - 4))
    return t.reshape((lead,) + t.shape[nd - 4:])


def _page_specs(n_pages, npg):
    def spec(k):
        return pl.BlockSpec((1, 2, NSA_KV_HEADS, HEAD_DIM, PAGE_SIZE),
                            lambda b, s, pt: (pt[b * n_pages + s * npg + k], 0, 0, 0, 0))
    return [spec(k) for k in range(npg)]


def _cmp_sample(pt, cache, wx, pe, qs, tab, nb, n_pages, ts):
    npg = CMP_PAGES_PER_STEP
    steps = n_pages // npg
    nc = n_pages * PAGE_SIZE // CMP_STRIDE
    rows = npg * PAGE_SIZE // CMP_STRIDE
    ns = nc // 4
    full = lambda a: pl.BlockSpec(a.shape, lambda b, s, pt: (0,) * a.ndim)
    grid_spec = pltpu.PrefetchScalarGridSpec(
        num_scalar_prefetch=1,
        grid=(nb, steps),
        in_specs=_page_specs(n_pages, npg) + [
            full(wx), full(pe),
            pl.BlockSpec((1, NSA_KV_HEADS, LANES, HEAD_DIM), lambda b, s, pt: (b, 0, 0, 0)),
            full(tab)],
        out_specs=[pl.BlockSpec((1, NSA_KV_HEADS, NSA_GROUP * ts, HEAD_DIM), lambda b, s, pt: (b, 0, 0, 0)),
                   pl.BlockSpec((1, steps, ns // steps, LANES), lambda b, s, pt: (b, 0, 0, 0))],
        scratch_shapes=[pltpu.VMEM((steps, rows, KV_ROW), F32), pltpu.VMEM((steps, rows, KV_ROW), F32),
                        pltpu.VMEM((nc, LANES), F32), pltpu.VMEM((PAGE_SIZE, LANES), F32)])
    return pl.pallas_call(
        functools.partial(_cmp_sample_kernel, ts=ts),
        grid_spec=grid_spec,
        out_shape=[jax.ShapeDtypeStruct((nb, NSA_KV_HEADS, NSA_GROUP * ts, HEAD_DIM), F32),
                   jax.ShapeDtypeStruct((nb, steps, ns // steps, LANES), F32)],
        compiler_params=_cparams(("parallel", "arbitrary")),
        name="cmp_sample",
    )(pt, *([cache] * npg), wx, pe, qs, tab)


def _local_softmax(z, mask):
    m = jnp.max(z, axis=0, keepdims=True)
    p = jnp.where(mask, jnp.exp(z - m), 0.0)
    return m, p, jnp.sum(p, axis=0, keepdims=True)


def _selwin_sample_kernel(pt_ref, *refs, ts):
    del pt_ref
    npg = SEL_PAGES_PER_STEP
    pages = refs[:npg]
    (qa_ref, sel_ref, cf_ref, nbl_ref, st_ref, kvn_ref, nbn_ref, nbw_ref, g_ref, oc_ref,
     o_ref, m_ref, l_ref, a_ref) = refs[npg:]
    s = pl.program_id(1)
    last = pl.num_programs(1) - 1
    ncol = NSA_KV_HEADS * NSA_GROUP * ts
    qa = qa_ref[0]
    rowi = lax.broadcasted_iota(jnp.int32, (PAGE_SIZE, LANES), 0)
    for k in range(npg):
        pg = jnp.concatenate([_page_half_rows(pages[k], 0), _page_half_rows(pages[k], 1)],
                             axis=1).astype(BF16)
        is_last_page = jnp.logical_and(s == last, k == npg - 1)
        bias = jnp.where(is_last_page, nbl_ref[...], cf_ref[0:1, :])
        s0 = sel_ref[0, s, SEL_PER_PAGE * k:SEL_PER_PAGE * k + 1, :]
        s1 = sel_ref[0, s, SEL_PER_PAGE * k + 1:SEL_PER_PAGE * k + 2, :]
        mask = jnp.where(rowi < SEL_BLOCK, s0, s1) > 0.5
        z = jnp.where(mask, _dot(pg, qa) + bias, NEG)
        m, p, l = _local_softmax(z, mask)
        m_ref[s, k:k + 1, :] = m
        l_ref[s, k:k + 1, :] = l
        a_ref[s, k] = pl.dot(p.astype(BF16), pg, trans_a=True)[0:ncol, :]

    @pl.when(s == last)
    def _():
        nbn = nbn_ref[...]
        okn = nbn > 0.5 * NEG
        kn = kvn_ref[0, 0:SUBLANES, :].astype(BF16)
        m_n, p_n, l_n = _local_softmax(jnp.where(okn, _dot(kn, qa) + nbn, NEG), okn)
        a_n = pl.dot(p_n.astype(BF16), kn, trans_a=True)[0:ncol, :]
        nparts = m_ref.shape[0] * m_ref.shape[1]
        m_all = m_ref[...].reshape(nparts, LANES)
        l_all = l_ref[...].reshape(nparts, LANES)
        m_g = jnp.maximum(jnp.max(m_all, axis=0, keepdims=True), m_n)
        w_all = jnp.exp(m_all - m_g)
        w_n = jnp.exp(m_n - m_g)
        l_g = jnp.sum(w_all * l_all, axis=0, keepdims=True) + w_n * l_n

        nbw = nbw_ref[...]
        okw = nbw > 0.5 * NEG
        st = jnp.concatenate([_page_half_rows(st_ref, 0), _page_half_rows(st_ref, 1)],
                             axis=1).astype(BF16)
        kwn = kvn_ref[0, SUBLANES:2 * SUBLANES, :].astype(BF16)
        zw = jnp.where(okw, _dot(st, qa) + nbw, NEG)
        zwn = jnp.where(okn, _dot(kwn, qa) + nbn, NEG)
        m_w = jnp.maximum(jnp.max(zw, axis=0, keepdims=True), jnp.max(zwn, axis=0, keepdims=True))
        pw = jnp.where(okw, jnp.exp(zw - m_w), 0.0)
        pwn = jnp.where(okn, jnp.exp(zwn - m_w), 0.0)
        l_w = jnp.sum(pw, axis=0, keepdims=True) + jnp.sum(pwn, axis=0, keepdims=True)
        a_w = (pl.dot(pw.astype(BF16), st, trans_a=True) + pl.dot(pwn.astype(BF16), kwn, trans_a=True))[0:ncol, :]

        assert nparts + 4 * SUBLANES <= LANES
        rows8 = lambda r: jnp.broadcast_to(r, (SUBLANES, LANES))
        x = jnp.concatenate([w_all, rows8(w_n), rows8(l_g), rows8(l_w),
                             jnp.zeros((LANES - nparts - 3 * SUBLANES, LANES), F32)], axis=0)
        xt = x.T
        acc = xt[0:ncol, nparts:nparts + 1] * a_n
        for pidx in range(nparts):
            acc = acc + xt[0:ncol, pidx:pidx + 1] * a_ref[pidx // npg, pidx % npg]
        o_s = acc / jnp.maximum(xt[0:ncol, nparts + SUBLANES:nparts + SUBLANES + 1], 1e-30)
        o_w = a_w / jnp.maximum(xt[0:ncol, nparts + 2 * SUBLANES:nparts + 2 * SUBLANES + 1], 1e-30)

        ri = lax.broadcasted_iota(jnp.int32, (ncol, HEAD_DIM), 0)
        v_of = lambda t: jnp.where(ri < ncol // NSA_KV_HEADS, t[:, 2 * HEAD_DIM:3 * HEAD_DIM],
                                   t[:, 3 * HEAD_DIM:4 * HEAD_DIM])
        gt = g_ref[0]
        oc = oc_ref[0].reshape(ncol, HEAD_DIM)
        o_ref[0] = gt[:, 0:1] * oc + gt[:, 1:2] * v_of(o_s) + gt[:, 2:3] * v_of(o_w)


def _selwin_sample(pt, cache, qa, sel, cfrow, nbl, state, st_off, kvn, nbn, nbw, gates, oc, nb, n_pages, ts):
    npg = SEL_PAGES_PER_STEP
    steps = n_pages // npg
    ncol = NSA_KV_HEADS * NSA_GROUP * ts
    sel = sel.reshape(nb, steps, -1, LANES)
    full = lambda a: pl.BlockSpec(a.shape, lambda b, s, pt: (0,) * a.ndim)
    perb = lambda a: pl.BlockSpec((1,) + a.shape[1:], lambda b, s, pt: (b,) + (0,) * (a.ndim - 1))
    grid_spec = pltpu.PrefetchScalarGridSpec(
        num_scalar_prefetch=1,
        grid=(nb, steps),
        in_specs=_page_specs(n_pages, npg) + [
            perb(qa), perb(sel), full(cfrow), full(nbl),
            pl.BlockSpec((1,) + state.shape[1:], lambda b, s, pt: (st_off + b, 0, 0, 0, 0)),
            perb(kvn), full(nbn), full(nbw), perb(gates), perb(oc)],
        out_specs=pl.BlockSpec((1, ncol, HEAD_DIM), lambda b, s, pt: (b, 0, 0)),
        scratch_shapes=[pltpu.VMEM((steps, npg, LANES), F32),
                        pltpu.VMEM((steps, npg, LANES), F32),
                        pltpu.VMEM((steps, npg, ncol, KV_ROW), F32)])
    return pl.pallas_call(
        functools.partial(_selwin_sample_kernel, ts=ts),
        grid_spec=grid_spec,
        out_shape=jax.ShapeDtypeStruct((nb, ncol, HEAD_DIM), F32),
        compiler_params=_cparams(("parallel", "arbitrary")),
        name="selwin_sample",
    )(pt, *([cache] * npg), qa, sel, cfrow, nbl, state, kvn, nbn, nbw, gates, oc)


def _sample_bias_tables(rel_bias, past, ts, n_buf):
    ncol = NSA_KV_HEADS * NSA_GROUP * ts
    col = jnp.arange(ncol)
    head = col // ts
    t = col % ts
    pick = lambda b: jnp.take_along_axis(b, jnp.broadcast_to(head, b.shape[:-1])[..., None], axis=-1)[..., 0]
    bias = lambda d: pick(_head_bias(rel_bias, d))
    padc = lambda a, fill=0.0: jnp.pad(a, ((0, 0), (0, LANES - a.shape[1])), constant_values=fill)
    nc = past // CMP_STRIDE
    c_end = jnp.arange(nc)[:, None] * CMP_STRIDE + CMP_BLOCK - 1
    tab = jnp.where(jnp.arange(nc)[:, None] < nc - 1, bias(past + t[None, :] - c_end), NEG)
    hc = ncol // NSA_KV_HEADS
    tab_cmp = jnp.stack([padc(tab[:, h * hc:(h + 1) * hc]) for h in range(NSA_KV_HEADS)])
    far = rel_bias[N_BUCKETS - 1].astype(F32)[head]
    cfrow = jnp.broadcast_to(padc(far[None, :]), (SUBLANES, LANES))
    kk = jnp.arange(PAGE_SIZE)[:, None]
    nbl = padc(bias(PAGE_SIZE + t[None, :] - kk))
    j = jnp.arange(SUBLANES)[:, None]
    dn = t[None, :] - j
    nbn = padc(jnp.where((dn >= 0) & (j < ts), bias(dn), NEG))
    r = jnp.arange(n_buf)[:, None]
    dw = n_buf + t[None, :] - r
    nbw = padc(jnp.where((dw >= 0) & (dw < WINDOW), bias(dw), NEG))
    return tab_cmp, cfrow, nbl, nbn, nbw


def _sample_layer(x, p, tabs, l, pt, cache_cmp, cache_slc, state_win, state_pool_l, nb, ts, n_pages, final, gf):
    tab_cmp, cfrow, nbl, nbn, nbw = tabs
    past = n_pages * PAGE_SIZE
    R = ts * nb
    ncol = NSA_KV_HEADS * NSA_GROUP * ts
    xp, q, gt, u, v, kvc, kvs, kvw = _inproj(x, p['g_mix'], p['w_in'], p['gn'], R)
    q5 = q.reshape(ts, nb, NSA_KV_HEADS, NSA_GROUP, HEAD_DIM)
    qs = q5.transpose(1, 2, 3, 0, 4).reshape(nb, NSA_KV_HEADS, NSA_GROUP * ts, HEAD_DIM)
    qs = jnp.pad(qs, ((0, 0), (0, 0), (0, LANES - NSA_GROUP * ts), (0, 0)))
    oc, sel = _cmp_sample(pt, cache_cmp, p['wx'], p['pe'], qs, tab_cmp, nb, n_pages, ts)
    qt = q5.transpose(1, 2, 4, 3, 0).reshape(nb, NSA_KV_HEADS, HEAD_DIM, NSA_GROUP * ts)
    eye = jnp.eye(NSA_KV_HEADS, dtype=qt.dtype)
    qa = jnp.einsum('bhdc,hk->bhdkc', qt, eye).reshape(nb, NSA_KV_HEADS * HEAD_DIM, ncol)
    qa = jnp.pad(qa, ((0, 0), (0, KV_ROW - NSA_KV_HEADS * HEAD_DIM), (0, LANES - ncol)))
    rows_of = lambda a: jnp.pad(a.reshape(ts, nb, KV_ROW).transpose(1, 0, 2), ((0, 0), (0, SUBLANES - ts), (0, 0)))
    kvn = jnp.concatenate([rows_of(kvs), rows_of(kvw)], axis=1)
    ng = NSA_GROUP * N_BRANCHES
    g5 = jnp.concatenate([gt[:, h * LANES:h * LANES + ng] for h in range(NSA_KV_HEADS)], axis=1)
    g5 = g5.reshape(ts, nb, NSA_KV_HEADS * NSA_GROUP, N_BRANCHES)
    gates = jnp.pad(g5.transpose(1, 2, 0, 3).reshape(nb, ncol, N_BRANCHES), ((0, 0), (0, 0), (0, LANES - N_BRANCHES)))
    o = _selwin_sample(pt, cache_slc, qa, sel, cfrow, nbl, state_win, l * nb, kvn, nbn, nbw, gates, oc,
                       nb, n_pages, ts)
    yb = o.reshape(nb, NSA_KV_HEADS * NSA_GROUP, ts, HEAD_DIM).transpose(2, 0, 1, 3).reshape(R, NSA_WIDTH)
    ext = jnp.concatenate([state_pool_l.transpose(1, 0, 2), xp.reshape(ts, nb, POOL_WIDTH)], axis=0)
    x2 = _mix_sample(x, ext, yb.astype(BF16), u.reshape(ts, nb, GMLP_WIDTH), v.reshape(ts, nb, GMLP_WIDTH),
                     p['wp'], p['ps'], p['wsx'], p['wb'][:SUBLANES], p['w_out'],
                     p['g_ffn'], p['w_up'], p['w_down'], gf, nb, ts, past, final)
    return x2, (xp, kvc, kvs, kvw, v)


def kernel(x_prompt, x_sample, cache_cmp_kv, cache_slc_kv, state_win_kv, state_pool, page_table, w_in, w_out,
           norm_mix, norm_ffn, norm_final, w_pool, pool_scale, cmp_w, cmp_pe, gmlp_ws, gmlp_b, gmlp_norm, w_up,
           w_down, rel_bias):
    B, T, _ = x_prompt.shape
    nb, ts, _ = x_sample.shape
    n_pages = page_table.shape[1]
    n_phys = cache_cmp_kv.shape[1]
    n_buf = state_win_kv.shape[2]
    past = n_pages * PAGE_SIZE
    depth = w_in.shape[0]
    assert T % TK_FAR == 0 and T >= WINDOW
    assert n_pages % CMP_PAGES_PER_STEP == 0 and n_pages % SEL_PAGES_PER_STEP == 0
    assert ts <= SUBLANES and ts <= POOL_STATE and n_buf == WINDOW and past >= WINDOW
    kv_tail = (2, NSA_KV_HEADS, HEAD_DIM)

    cache_cmp = _native_rows(cache_cmp_kv, depth * n_phys)
    cache_slc = _native_rows(cache_slc_kv, depth * n_phys)
    state_win = _native_rows(state_win_kv, depth * nb)
    ptabs = _prompt_bias_tables(rel_bias)
    stabs = _sample_bias_tables(rel_bias, past, ts, n_buf)
    gf = norm_final[None, :]
    xp = x_prompt.reshape(B * T, D_MODEL)
    xs = x_sample.transpose(1, 0, 2).reshape(ts * nb, D_MODEL)
    unmajor = lambda a: a.reshape(ts, nb, a.shape[-1]).transpose(1, 0, 2)

    outs = [[] for _ in range(9)]
    for l in range(depth):
        p = _prep_layer(l, w_in, w_out, norm_mix, norm_ffn, w_pool, pool_scale, cmp_w, cmp_pe, gmlp_ws, gmlp_b,
                        gmlp_norm, w_up, w_down)
        final = l == depth - 1
        xp, (pin, kvc_t, kvs_t, kvw_t) = _prompt_layer(xp, p, ptabs, B, T, final, gf)
        rows_of = lambda a: a.reshape((B,) + kv_tail + (a.shape[-1],)).transpose(0, 4, 1, 2, 3)
        outs[0].append(rows_of(kvc_t))
        outs[1].append(rows_of(kvs_t))
        outs[2].append(rows_of(kvw_t[:, :, T - WINDOW:]))
        outs[3].append(pin.reshape(B, T, POOL_WIDTH)[:, T - POOL_STATE:])

        pt = (page_table + l * n_phys).reshape(-1).astype(jnp.int32)
        xs, (sin, kvc_s, kvs_s, kvw_s, v_s) = _sample_layer(
            xs, p, stabs, l, pt, cache_cmp, cache_slc, state_win, state_pool[l], nb, ts, n_pages, final, gf)
        kvw_new = unmajor(kvw_s).reshape((nb, ts) + kv_tail)
        outs[4].append(unmajor(kvc_s).reshape((nb, ts) + kv_tail))
        outs[5].append(unmajor(kvs_s).reshape((nb, ts) + kv_tail))
        outs[6].append(jnp.concatenate([state_win_kv[l][:, ts:], kvw_new], axis=1))
        outs[7].append(jnp.concatenate([state_pool[l][:, ts:], unmajor(sin)], axis=1))
        outs[8].append(unmajor(v_s))

    y_prompt = xp.reshape(B, T, D_MODEL)
    y_sample = unmajor(xs)
    return (y_prompt, y_sample) + tuple(jnp.stack(o) for o in outs)
```

```python
import functools
import math

import numpy as np
import jax
import jax.numpy as jnp
from jax import lax
from jax.experimental import pallas as pl
from jax.experimental.pallas import tpu as pltpu

F32 = jnp.float32
BF16 = jnp.bfloat16

D_MODEL = 1024
DEPTH = 4
PAGE_SIZE = 128
HEAD_DIM = 64
POOL_WINDOWS = (2, 4, 8, 16)
POOL_WIDTH = 256
POOL_GROUP_DIM = 64
POOL_STATE = 15
NSA_WIDTH = 512
NSA_HEADS = 8
NSA_KV_HEADS = 2
NSA_GROUP = 4
KV_ROW = 2 * NSA_KV_HEADS * HEAD_DIM
CMP_STRIDE = 16
CMP_BLOCK = 32
SEL_BLOCK = 64
N_SELECT = 16
WINDOW = 512
N_BRANCHES = 3
FORCE_SCORE = 1000.0
GMLP_WIDTH = 256
GMLP_GROUPS = 4
GMLP_CHUNK = 128
D_FF = 4096
N_BUCKETS = 32
MAX_DISTANCE = 128
EPS = 1e-6
SCALE = HEAD_DIM ** -0.5

LANES = 128
SUBLANES = 8
VMEM_LIMIT = 56 * 1024 * 1024

NEG = -1e30
CMP_PAD = 16
TQ = 128
TQC = 512
CMP_NEAR = CMP_PAD + TQC // CMP_STRIDE
TK_FAR = 512
KPAD = 512
GATE_LANES = 2 * LANES

PROJ_SEGS = (('xp', 256), ('q', 512), ('kvc', 256), ('kvs', 256), ('kvw', 256), ('gt', 24), ('u', 256), ('v', 256))


def _cparams(sem):
    return pltpu.CompilerParams(dimension_semantics=sem, vmem_limit_bytes=VMEM_LIMIT)


def _dot(a, b):
    return jnp.dot(a, b, preferred_element_type=F32)


def _dot_nt(a, b):
    return lax.dot_general(a, b, (((1,), (1,)), ((), ())), preferred_element_type=F32)


def _gelu(x):
    c = math.sqrt(2.0 / math.pi)
    return 0.5 * x * (1.0 + jnp.tanh(c * (x + 0.044715 * (x * x * x))))


def _rms(x, g):
    return x * lax.rsqrt(jnp.mean(x * x, axis=-1, keepdims=True) + EPS) * g


def _inproj_kernel(x_ref, g_ref, w_ref, gn_ref, *refs, channel_major):
    if channel_major:
        wt_ref, xp_ref, q_ref, gt_ref, u_ref, v_ref, kvc_ref, *kv_refs = refs
    else:
        xp_ref, q_ref, gt_ref, u_ref, v_ref, kvc_ref, *kv_refs = refs
    x = x_ref[...]
    h = _rms(x, g_ref[...]).astype(BF16)

    def seg(lo, hi):
        return _dot(h, w_ref[:, lo:hi])

    xp_ref[...] = seg(0, 256)
    q_ref[...] = (seg(256, 768) * SCALE).astype(BF16)
    kvc_ref[...] = seg(768, 1024)
    if channel_major:
        for k, t_ref in enumerate(kv_refs):
            t_ref[0] = _dot_nt(wt_ref[k * KV_ROW:(k + 1) * KV_ROW, :], h)
    else:
        kv_refs[0][...] = seg(1024, 1280)
        kv_refs[1][...] = seg(1280, 1536)
    u_ref[...] = _gelu(seg(1536, 1792))
    gv = _gelu(seg(1792, 2048))
    sq = gv * gv
    lane = lax.broadcasted_iota(jnp.int32, sq.shape, 1)
    ms = jnp.zeros_like(sq)
    for g in range(GMLP_GROUPS):
        in_g = (lane >= g * 64) & (lane < (g + 1) * 64)
        s = jnp.sum(jnp.where(in_g, sq, 0.0), axis=-1, keepdims=True) * (1.0 / 64.0)
        ms = jnp.where(in_g, s, ms)
    v_ref[...] = gv * lax.rsqrt(ms + EPS) * gn_ref[...]
    gt_ref[...] = jax.nn.sigmoid(seg(2048, 2048 + GATE_LANES))


def _inproj(x, g, w, gn, tm, seq=None, wt=None):
    R = x.shape[0]
    f = lambda n, dt=F32: jax.ShapeDtypeStruct((R, n), dt)
    row = lambda n: pl.BlockSpec((tm, n), lambda i: (i, 0))
    full = lambda a: pl.BlockSpec(a.shape, lambda i: (0,) * a.ndim)
    out_specs = [row(256), row(512), row(GATE_LANES), row(256), row(256), row(256)]
    out_shape = [f(256), f(512, BF16), f(GATE_LANES), f(256), f(256), f(256)]
    if seq is None:
        out_specs += [row(256), row(256)]
        out_shape += [f(256), f(256)]
    else:
        per_seq = seq // tm
        out_specs += [pl.BlockSpec((1, KV_ROW, tm), lambda i: (i // per_seq, 0, i % per_seq))] * 3
        out_shape += [jax.ShapeDtypeStruct((R // seq, KV_ROW, seq), F32)] * 3
    args = (x, g, w, gn) if seq is None else (x, g, w, gn, wt)
    return pl.pallas_call(
        functools.partial(_inproj_kernel, channel_major=seq is not None),
        grid=(R // tm,),
        in_specs=[row(D_MODEL)] + [full(a) for a in args[1:]],
        out_specs=out_specs,
        out_shape=out_shape,
        compiler_params=_cparams(("parallel",)),
        name="inproj",
    )(*args)


def _out_proj(x, ya, yb, yc, wo_ref):
    acc = _dot(ya.astype(BF16), wo_ref[0:256, :])
    acc = acc + _dot(yb, wo_ref[256:768, :])
    acc = acc + _dot(yc.astype(BF16), wo_ref[768:1024, :])
    return x + acc


def _pool_tail(win2, win4, win8, win16, cur, cnt, wp_ref, ps_ref):
    lane = lax.broadcasted_iota(jnp.int32, cur.shape, 1)
    win = jnp.where(lane < 64, win2, jnp.where(lane < 128, win4, jnp.where(lane < 192, win8, win16)))
    pooled = win / cnt - cur
    return _dot(pooled.astype(BF16), wp_ref[...]) * ps_ref[...]


FF_CHUNK = 1024


def _ffn_rows(x, g_ref, wu_ref, wd_ref, gf_ref, final):
    h = _rms(x, g_ref[...]).astype(BF16)
    acc = x
    for c in range(D_FF // FF_CHUNK):
        a = jnp.maximum(_dot(h, wu_ref[:, c * FF_CHUNK:(c + 1) * FF_CHUNK]), 0.0)
        acc = acc + _dot((a * a).astype(BF16), wd_ref[c * FF_CHUNK:(c + 1) * FF_CHUNK, :])
    return _rms(acc, gf_ref[...]) if final else acc


def _layer_spec(w, l, **kw):
    return pl.BlockSpec((None,) + w.shape[1:], lambda i: (l, 0, 0), **kw)


def _ffn_specs(g, wu, wd, gf, l):
    full = lambda a: pl.BlockSpec(a.shape, lambda i: (0,) * a.ndim)
    return [full(g), _layer_spec(wu, l, pipeline_mode=pl.Buffered(1)),
            _layer_spec(wd, l, pipeline_mode=pl.Buffered(1)), full(gf)]


def _mix_prompt_kernel(x_ref, xp_ref, halo_ref, yb_ref, u_ref, v_ref, wp_ref, ps_ref, ws_ref, wb_ref, wo_ref,
                       g_ref, wu_ref, wd_ref, gf_ref, o_ref, ext_ref, *, tm, seq, final):
    i = pl.program_id(0)
    t0 = (i * tm) % seq
    halo = jnp.where(t0 > 0, halo_ref[...], 0.0)
    cur = xp_ref[...]
    ext_ref[0:16, :] = halo
    ext_ref[16:16 + tm, :] = cur
    e = ext_ref[...]
    b2 = e[1:] + e[:-1]
    b4 = b2[2:] + b2[:-2]
    b8 = b4[4:] + b4[:-4]
    b16 = b8[8:] + b8[:-8]
    win2 = b2[15:15 + tm]
    win4 = b4[13:13 + tm]
    win8 = b8[9:9 + tm]
    win16 = b16[1:1 + tm]
    rowi = lax.broadcasted_iota(jnp.int32, cur.shape, 0)
    lane = lax.broadcasted_iota(jnp.int32, cur.shape, 1)
    wsz = jnp.where(lane < 64, 2, jnp.where(lane < 128, 4, jnp.where(lane < 192, 8, 16)))
    cnt = jnp.minimum(t0 + rowi + 1, wsz).astype(F32)
    ya = _pool_tail(win2, win4, win8, win16, cur, cnt, wp_ref, ps_ref)

    ci = lax.broadcasted_iota(jnp.int32, (GMLP_CHUNK, GMLP_CHUNK), 0)
    cj = lax.broadcasted_iota(jnp.int32, (GMLP_CHUNK, GMLP_CHUNK), 1)
    lane_c = lax.broadcasted_iota(jnp.int32, (GMLP_CHUNK, GMLP_WIDTH), 1)
    wts = [jnp.where(ci >= cj, ws_ref[g], 0.0).astype(BF16) for g in range(GMLP_GROUPS)]
    parts = []
    for c in range(tm // GMLP_CHUNK):
        vc = v_ref[c * GMLP_CHUNK:(c + 1) * GMLP_CHUNK, :].astype(BF16)
        s = jnp.zeros((GMLP_CHUNK, GMLP_WIDTH), F32)
        for g in range(GMLP_GROUPS):
            sg = _dot(wts[g], vc)
            s = jnp.where((lane_c >= g * 64) & (lane_c < (g + 1) * 64), sg, s)
        parts.append(u_ref[c * GMLP_CHUNK:(c + 1) * GMLP_CHUNK, :] * (s + wb_ref[...]))
    yc = jnp.concatenate(parts, axis=0) if len(parts) > 1 else parts[0]

    x1 = _out_proj(x_ref[...], ya, yb_ref[...], yc, wo_ref)
    o_ref[...] = _ffn_rows(x1, g_ref, wu_ref, wd_ref, gf_ref, final)


def _mix_prompt(x, xp, yb, u, v, wp, ps, ws, wb, wo, g, wu, wd, gf, l, tm, seq, final):
    R = x.shape[0]
    row = lambda n: pl.BlockSpec((tm, n), lambda i: (i, 0))
    full = lambda a: pl.BlockSpec(a.shape, lambda i: (0,) * a.ndim)
    halo = pl.BlockSpec((16, 256), lambda i: (jnp.maximum(i * (tm // 16) - 1, 0), 0))
    return pl.pallas_call(
        functools.partial(_mix_prompt_kernel, tm=tm, seq=seq, final=final),
        grid=(R // tm,),
        in_specs=[row(D_MODEL), row(256), halo, row(512), row(256), row(256),
                  full(wp), full(ps), full(ws), full(wb), _layer_spec(wo, l)] + _ffn_specs(g, wu, wd, gf, l),
        out_specs=row(D_MODEL),
        out_shape=jax.ShapeDtypeStruct((R, D_MODEL), F32),
        scratch_shapes=[pltpu.VMEM((tm + 16, 256), F32)],
        compiler_params=_cparams(("parallel",)),
        name="mix_prompt",
    )(x, xp, xp, yb, u, v, wp, ps, ws, wb, wo, g, wu, wd, gf)


def _mix_sample_kernel(x_ref, ext_ref, yb_ref, u_ref, v_ref, wp_ref, ps_ref, wsx_ref, wb_ref, wo_ref,
                       g_ref, wu_ref, wd_ref, gf_ref, o_ref, *, nb, ts, past, final):
    lane = lax.broadcasted_iota(jnp.int32, (nb, POOL_WIDTH), 1)
    wsz = jnp.where(lane < 64, 2, jnp.where(lane < 128, 4, jnp.where(lane < 192, 8, 16)))
    yas, ycs = [], []
    for t in range(ts):
        top = POOL_STATE + t
        acc = ext_ref[top] + ext_ref[top - 1]
        wins = [acc]
        for w in (4, 8, 16):
            for j in range(w // 2, w):
                acc = acc + ext_ref[top - j]
            wins.append(acc)
        cnt = jnp.minimum(past + t + 1, wsz).astype(F32)
        yas.append(_pool_tail(wins[0], wins[1], wins[2], wins[3], ext_ref[top], cnt, wp_ref, ps_ref))
        s = wb_ref[t:t + 1, :]
        s = jnp.broadcast_to(s, (nb, GMLP_WIDTH))
        for j in range(t + 1):
            s = s + wsx_ref[t, j:j + 1, :] * v_ref[j]
        ycs.append(u_ref[t] * s)
    ya = jnp.concatenate(yas, axis=0)
    yc = jnp.concatenate(ycs, axis=0)
    x1 = _out_proj(x_ref[...], ya, yb_ref[...], yc, wo_ref)
    o_ref[...] = _ffn_rows(x1, g_ref, wu_ref, wd_ref, gf_ref, final)


def _mix_sample(x, ext, yb, u, v, wp, ps, wsx, wb4, wo, g, wu, wd, gf, l, nb, ts, past, final):
    R = x.shape[0]
    args = (x, ext, yb, u, v, wp, ps, wsx, wb4, wo)
    full = lambda a: pl.BlockSpec(a.shape, lambda i: (0,) * a.ndim)
    return pl.pallas_call(
        functools.partial(_mix_sample_kernel, nb=nb, ts=ts, past=past, final=final),
        grid=(1,),
        in_specs=[full(a) for a in args[:-1]] + [_layer_spec(wo, l)] + _ffn_specs(g, wu, wd, gf, l),
        out_specs=pl.BlockSpec((R, D_MODEL), lambda i: (0, 0)),
        out_shape=jax.ShapeDtypeStruct((R, D_MODEL), F32),
        compiler_params=_cparams(("arbitrary",)),
        name="mix_sample",
    )(*args, g, wu, wd, gf)


def _compress_rows(read_rows, n, wx):
    lo = jnp.zeros((n, wx.shape[1]), F32)
    hi = jnp.zeros((n, wx.shape[1]), F32)
    for j in range(CMP_STRIDE):
        xj = read_rows(j)
        lo = lo + xj * wx[j:j + 1, :]
        hi = hi + xj * wx[CMP_STRIDE + j:CMP_STRIDE + j + 1, :]
    return lo, hi


def _combine_halves(lo, hi, pe, wx):
    n = lo.shape[0]
    pe_term = jnp.sum(pe * wx, axis=0, keepdims=True)
    comp = lo + pltpu.roll(hi, n - 1, 0) + pe_term
    rowi = lax.broadcasted_iota(jnp.int32, comp.shape, 0)
    return jnp.where(rowi < n - 1, comp, 0.0)


def _compress_prompt_kernel(x_ref, wx_ref, pe_ref, o_ref, *, n):
    wx = wx_ref[...]
    lo, hi = _compress_rows(lambda j: x_ref[0, pl.ds(j, n, stride=CMP_STRIDE), :], n, wx)
    o_ref[0, 0:CMP_PAD, :] = jnp.zeros((CMP_PAD, LANES), F32)
    o_ref[0, CMP_PAD:CMP_PAD + n, :] = _combine_halves(lo, hi, pe_ref[...], wx)


def _compress_prompt(kvc, wx, pe):
    B, T, _ = kvc.shape
    n = T // CMP_STRIDE
    half = lambda rows: pl.BlockSpec((rows, LANES), lambda b, e: (0, e))
    return pl.pallas_call(
        functools.partial(_compress_prompt_kernel, n=n),
        grid=(B, KV_ROW // LANES),
        in_specs=[pl.BlockSpec((1, T, LANES), lambda b, e: (b, 0, e)), half(CMP_BLOCK), half(CMP_BLOCK)],
        out_specs=pl.BlockSpec((1, n + CMP_PAD, LANES), lambda b, e: (b, 0, e)),
        out_shape=jax.ShapeDtypeStruct((B, n + CMP_PAD, KV_ROW), F32),
        compiler_params=_cparams(("parallel", "parallel")),
        name="compress_prompt",
    )(kvc, wx, pe)


def _select_topk(score, n_pick):
    s_iota = lax.broadcasted_iota(jnp.int32, score.shape, 0)
    big = score.shape[0]
    work = score
    for _ in range(n_pick):
        m = jnp.max(work, axis=0, keepdims=True)
        idx = jnp.min(jnp.where(work == m, s_iota, big), axis=0, keepdims=True)
        work = jnp.where(s_iota == idx, -2.0, work)
    return jnp.where((work == -2.0) & (score >= 0.0), 1.0, 0.0)


def _softmax_rows(z):
    m = jnp.maximum(jnp.max(z, axis=0, keepdims=True), 0.1 * NEG)
    p = jnp.exp(z - m)
    den = jnp.maximum(jnp.sum(p, axis=0, keepdims=True), 1e-30)
    return p * (1.0 / den)


def _cmp_prompt_kernel(q_ref, kc_ref, vct_ref, nb_ref, cf_ref, oc_ref, sn_ref, raw_ref, z_ref, ps_ref, *, ncp):
    i = pl.program_id(2)
    hh = pl.program_id(1)
    tq = q_ref.shape[1]
    r0 = pl.multiple_of(i * (tq // CMP_STRIDE), SUBLANES)
    q = q_ref[0]
    vct = vct_ref[0, 0]
    rown = lax.broadcasted_iota(jnp.int32, (CMP_NEAR, tq), 0) + r0

    def attend(rows):
        kc = kc_ref[0, 0, 0:rows, :]
        rowi = lax.broadcasted_iota(jnp.int32, (rows, tq), 0)
        far = (rowi >= CMP_PAD) & (rowi < r0)
        psum = jnp.zeros((rows, tq), F32)
        octs = []
        for g in range(NSA_GROUP):
            qg = q[:, g * HEAD_DIM:(g + 1) * HEAD_DIM]
            raw_ref[0:rows, :] = _dot_nt(kc, qg)
            cf = cf_ref[hh * NSA_GROUP + g]
            z_ref[0:rows, :] = jnp.where(far, raw_ref[0:rows, :] + cf[0:1, :], NEG)
            nb = nb_ref[hh * NSA_GROUP + g]
            zn = raw_ref[pl.ds(r0, CMP_NEAR), :] + nb
            z_ref[pl.ds(r0, CMP_NEAR), :] = jnp.where((rown >= CMP_PAD) & (nb > 0.5 * NEG), zn, NEG)
            pc = _softmax_rows(z_ref[0:rows, :])
            psum = psum + pc
            pcb = pc.astype(BF16)
            if rows < ncp:
                pcb = jnp.concatenate([pcb, jnp.zeros((ncp - rows, tq), BF16)], axis=0)
            octs.append(_dot(vct, pcb))
        oc_ref[0] = jnp.concatenate(octs, axis=0).T
        for c in range(tq // LANES):
            ps_ref[c, 0:rows, :] = psum[:, c * LANES:(c + 1) * LANES]
            if rows < ps_ref.shape[1]:
                ps_ref[c, rows:, :] = jnp.zeros((ps_ref.shape[1] - rows, LANES), F32)

    n_var = (ncp - CMP_PAD + LANES - 1) // LANES
    var = (r0 + CMP_NEAR - CMP_PAD + LANES - 1) // LANES - 1
    for v in range(n_var):
        pl.when(var == v)(functools.partial(attend, min(CMP_PAD + (v + 1) * LANES, ncp)))

    ns = LANES
    halves = []
    for c in range(tq // LANES):
        part = ps_ref[c, pl.ds(CMP_PAD, ns, stride=4), :]
        for j in range(1, 4):
            part = part + ps_ref[c, pl.ds(CMP_PAD + j, ns, stride=4), :]
        halves.append(part)
    p_slc = jnp.concatenate(halves, axis=1)
    s_iota = lax.broadcasted_iota(jnp.int32, (ns, tq), 0)
    tt = lax.broadcasted_iota(jnp.int32, (ns, tq), 1)
    cur = (i * tq + tt) // SEL_BLOCK
    forced = (s_iota == 0) | (s_iota == cur) | (s_iota == cur - 1)
    score = jnp.where(s_iota <= cur, jnp.where(forced, FORCE_SCORE, p_slc), -1.0)
    sel = _select_topk(score, N_SELECT)
    sn_ref[0, 0] = jnp.where(sel.T > 0.5, 0.0, NEG).astype(BF16)


def _cmp_prompt(q, kc, vct, nbc, cfar):
    B, T, _ = q.shape
    ncp = kc.shape[2]
    ns = LANES
    nq = T // TQC
    full = lambda a: pl.BlockSpec(a.shape, lambda b, h, i: (0,) * a.ndim)
    return pl.pallas_call(
        functools.partial(_cmp_prompt_kernel, ncp=ncp),
        grid=(B, NSA_KV_HEADS, nq),
        in_specs=[pl.BlockSpec((1, TQC, 256), lambda b, h, i: (b, i, h)),
                  pl.BlockSpec((1, 1, ncp, HEAD_DIM), lambda b, h, i: (b, h, 0, 0)),
                  pl.BlockSpec((1, 1, HEAD_DIM, ncp), lambda b, h, i: (b, h, 0, 0)),
                  full(nbc), full(cfar)],
        out_specs=[pl.BlockSpec((1, TQC, 256), lambda b, h, i: (b, i, h)),
                   pl.BlockSpec((1, 1, TQC, ns), lambda b, h, i: (b, h, i, 0))],
        out_shape=[jax.ShapeDtypeStruct((B, T, NSA_WIDTH), F32),
                   jax.ShapeDtypeStruct((B, NSA_KV_HEADS, T, ns), BF16)],
        scratch_shapes=[pltpu.VMEM((ncp, TQC), F32), pltpu.VMEM((ncp, TQC), F32),
                        pltpu.VMEM((TQC // LANES, max(ncp, CMP_PAD + 4 * LANES), LANES), F32)],
        compiler_params=_cparams(("parallel", "parallel", "parallel")),
        name="cmp_prompt",
    )(q, kc, vct, nbc, cfar)


def _selwin_prompt_kernel(q_ref, sn_ref, ks_ref, vs_ref, kw_ref, vw_ref, oh_ref, g_ref, oc_ref, nbn_ref, nbw_ref,
                          gsel_ref, o_ref, ka_ref, va_ref, kwa_ref, vwa_ref, ms_ref, as_ref, za_ref, zb_ref):
    i = pl.program_id(2)
    R4 = NSA_GROUP * TQ
    ncols = ka_ref.shape[1]

    @pl.when(i == 0)
    def _():
        zpad = jnp.zeros((HEAD_DIM, KPAD), BF16)
        ones = jnp.ones((HEAD_DIM, ncols), BF16)
        row = lax.broadcasted_iota(jnp.int32, (HEAD_DIM, ncols), 0)
        col = lax.broadcasted_iota(jnp.int32, (HEAD_DIM, ncols), 1)
        ka_ref[0:LANES, :] = oh_ref[...]
        kwa_ref[HEAD_DIM:, :] = jnp.where((row == 0) & (col < KPAD), NEG, 0.0).astype(BF16)
        for dst, src in ((ka_ref.at[LANES:], ks_ref), (va_ref.at[0:HEAD_DIM], vs_ref),
                         (kwa_ref.at[0:HEAD_DIM], kw_ref), (vwa_ref.at[0:HEAD_DIM], vw_ref)):
            dst[:, 0:KPAD] = zpad
            dst[:, KPAD:] = src[0].astype(BF16)
        va_ref[HEAD_DIM:, :] = ones
        vwa_ref[HEAD_DIM:, :] = ones

    q = q_ref[0]
    q4 = jnp.concatenate([q[:, g * HEAD_DIM:(g + 1) * HEAD_DIM] for g in range(NSA_GROUP)], axis=0)
    sn = sn_ref[0, 0]
    blk = lax.broadcasted_iota(jnp.int32, (TQ, LANES), 1)
    first_near = (i - 1) * (TQ // SEL_BLOCK)
    sn_far = jnp.where(blk >= first_near, NEG, sn.astype(F32)).astype(BF16)
    qa_near = jnp.concatenate([jnp.concatenate([sn] * NSA_GROUP, axis=0), q4], axis=1)
    qa_far = jnp.concatenate([jnp.concatenate([sn_far] * NSA_GROUP, axis=0), q4], axis=1)
    qw = jnp.concatenate([q4, jnp.ones((R4, HEAD_DIM), BF16)], axis=1)

    def one_pass(z, vt):
        m = jnp.max(z, axis=1, keepdims=True)
        acc = _dot_nt(jnp.exp(z - m).astype(BF16), vt)
        return m, acc

    kw0 = pl.multiple_of(i * TQ, TQ)
    zw = _dot(qw, kwa_ref[:, pl.ds(kw0, WINDOW + TQ)]) + nbw_ref[0]
    _, a_w = one_pass(zw, vwa_ref[:, pl.ds(kw0, WINDOW + TQ)])

    kn0 = pl.multiple_of((i - 1) * TQ + KPAD, TQ)
    zn = _dot(qa_near, ka_ref[:, pl.ds(kn0, 2 * TQ)]) + nbn_ref[0]
    m0, a0 = one_pass(zn, va_ref[:, pl.ds(kn0, 2 * TQ)])
    ms_ref[...] = jnp.broadcast_to(m0, (R4, LANES))
    as_ref[...] = a0

    def far_logits(j):
        k0 = pl.multiple_of(j * TK_FAR + KPAD, TK_FAR)
        return _dot(qa_far, ka_ref[:, pl.ds(k0, TK_FAR)])

    def far_update(j, z):
        k0 = pl.multiple_of(j * TK_FAR + KPAD, TK_FAR)
        m_prev = ms_ref[...]
        m_new = jnp.maximum(m_prev, jnp.max(z, axis=1, keepdims=True))
        p = jnp.exp(z - jnp.concatenate([m_new] * (TK_FAR // LANES), axis=1))
        as_ref[...] = jnp.exp(m_prev - m_new) * as_ref[...] + _dot_nt(p.astype(BF16), va_ref[:, pl.ds(k0, TK_FAR)])
        ms_ref[...] = m_new

    n_far = (jnp.maximum(i - 1, 0) * TQ + TK_FAR - 1) // TK_FAR
    odd = n_far % 2
    last = n_far - 1

    def far_pair(t):
        zb_ref[...] = far_logits(t + 1)
        far_update(t, za_ref[...])
        za_ref[...] = far_logits(jnp.minimum(t + 2, last))
        far_update(t + 1, zb_ref[...])

    n_pairs = (n_far + 1) // 2

    za_ref[...] = far_logits(-odd)

    @pl.when(n_pairs % 2 == 1)
    def _():
        far_pair(-odd)

    def far_body(jj, carry):
        t = 4 * jj + 2 * (n_pairs % 2) - odd
        far_pair(t)
        far_pair(t + 2)
        return carry

    lax.fori_loop(0, n_pairs // 2, far_body, 0)

    gt = g_ref[...]
    lane = lax.broadcasted_iota(jnp.int32, (TQ, LANES), 1)
    own = [jnp.where((lane >= N_BRANCHES * g) & (lane < N_BRANCHES * (g + 1)), gt, 0.0) for g in range(NSA_GROUP)]
    gt4 = jnp.concatenate(own, axis=0)
    hi = gt4.astype(BF16)
    lo = (gt4 - hi.astype(F32)).astype(BF16)
    gates = _dot(jnp.concatenate([hi, lo], axis=1), gsel_ref[...])

    def scaled(acc, gate):
        f = gate / jnp.maximum(acc, 1e-30)
        return acc * pltpu.roll(f, HEAD_DIM, 1)

    o_sw = scaled(as_ref[...], gates[:, LANES:2 * LANES]) + scaled(a_w, gates[:, 2 * LANES:])
    oc = oc_ref[0]
    outs = []
    for g in range(NSA_GROUP):
        sl = slice(g * TQ, (g + 1) * TQ)
        outs.append(gates[sl, 0:HEAD_DIM] * oc[:, g * HEAD_DIM:(g + 1) * HEAD_DIM] + o_sw[sl, 0:HEAD_DIM])
    o_ref[0] = jnp.concatenate(outs, axis=1).astype(BF16)


def _selwin_prompt(q, sn, kvs_t, kvw_t, onehot_t, gates, oc, nbn, nbw):
    B, T, _ = q.shape
    src = jnp.arange(2 * LANES) % LANES
    dst = jnp.arange(N_BRANCHES * LANES) // LANES
    gsel = ((src[:, None] < NSA_GROUP * N_BRANCHES) & (src[:, None] % N_BRANCHES == dst[None, :])).astype(BF16)
    nq = T // TQ
    R4 = NSA_GROUP * TQ
    ncols = T + KPAD
    k_of = pl.BlockSpec((1, HEAD_DIM, T), lambda b, h, i: (b, h, 0))
    v_of = pl.BlockSpec((1, HEAD_DIM, T), lambda b, h, i: (b, NSA_KV_HEADS + h, 0))
    perh = lambda a: pl.BlockSpec((1,) + a.shape[1:], lambda b, h, i: (h, 0, 0))
    qtile = pl.BlockSpec((1, TQ, 256), lambda b, h, i: (b, i, h))
    return pl.pallas_call(
        _selwin_prompt_kernel,
        grid=(B, NSA_KV_HEADS, nq),
        in_specs=[qtile, pl.BlockSpec((1, 1, TQ, sn.shape[-1]), lambda b, h, i: (b, h, i, 0)),
                  k_of, v_of, k_of, v_of, pl.BlockSpec(onehot_t.shape, lambda b, h, i: (0, 0)),
                  pl.BlockSpec((TQ, LANES), lambda b, h, i: (b * nq + i, h)), qtile, perh(nbn), perh(nbw),
                  pl.BlockSpec(gsel.shape, lambda b, h, i: (0, 0))],
        out_specs=qtile,
        out_shape=jax.ShapeDtypeStruct((B, T, NSA_WIDTH), BF16),
        scratch_shapes=[pltpu.VMEM((LANES + HEAD_DIM, ncols), BF16)] + [pltpu.VMEM((LANES, ncols), BF16)] * 3
        + [pltpu.VMEM((R4, LANES), F32)] * 2 + [pltpu.VMEM((R4, TK_FAR), F32)] * 2,
        compiler_params=_cparams(("parallel", "parallel", "arbitrary")),
        name="selwin_prompt",
    )(q, sn, kvs_t, kvs_t, kvw_t, kvw_t, onehot_t, gates, oc, nbn, nbw, gsel)


def _rel_bucket(dist):
    d = jnp.maximum(dist, 0)
    n_exact = N_BUCKETS // 2
    d_f = jnp.maximum(d, 1).astype(F32)
    large = n_exact + (jnp.log(d_f / n_exact) / math.log(MAX_DISTANCE / n_exact)
                       * (N_BUCKETS - n_exact)).astype(jnp.int32)
    large = jnp.minimum(large, N_BUCKETS - 1)
    return jnp.where(d < n_exact, d, large)


def _head_bias(rel_bias, dist):
    onehot = (_rel_bucket(dist)[..., None] == jnp.arange(N_BUCKETS)).astype(F32)
    return jnp.einsum('...k,kh->...h', onehot, rel_bias.astype(F32), precision=lax.Precision.HIGHEST)


def _prompt_bias_tables(rel_bias):
    far = rel_bias[N_BUCKETS - 1].astype(F32)
    cfar = jnp.broadcast_to(far[:, None, None], (NSA_HEADS, SUBLANES, TQC))
    k = jnp.arange(CMP_NEAR)[:, None]
    tt = jnp.arange(TQC)[None, :]
    d = tt + CMP_PAD * CMP_STRIDE - (CMP_BLOCK - 1) - CMP_STRIDE * k
    nbc = jnp.where((d >= 0)[..., None], _head_bias(rel_bias, d), NEG).transpose(2, 0, 1)
    tq = jnp.arange(TQ)[:, None]

    def tile_table(first_dist, n_keys, max_dist):
        d = first_dist + tq - jnp.arange(n_keys)[None, :]
        b = jnp.where(((d >= 0) & (d < max_dist))[..., None], _head_bias(rel_bias, d) - far, NEG)
        return b.transpose(2, 0, 1).reshape(NSA_KV_HEADS, NSA_GROUP * TQ, n_keys)

    nbn = tile_table(TQ, 2 * TQ, 2 * TQ + 1)
    nbw = tile_table(WINDOW, WINDOW + TQ, WINDOW)
    return cfar, nbc, nbn, nbw


def _prep_layer(l, w_in, w_out, norm_mix, norm_ffn, w_pool, pool_scale, cmp_w, cmp_pe, gmlp_ws, gmlp_b,
                gmlp_norm, w_up, w_down):
    w = w_in[l]
    offs = np.cumsum([0] + [n for _, n in PROJ_SEGS])
    seg = {name: w[:, offs[k]:offs[k + 1]] for k, (name, _) in enumerate(PROJ_SEGS)}
    order = ('xp', 'q', 'kvc', 'kvs', 'kvw', 'u', 'v')
    ng = NSA_GROUP * N_BRANCHES
    gpad = jnp.zeros((D_MODEL, LANES - ng), w.dtype)
    gate_cols = [c for h in range(NSA_KV_HEADS) for c in (seg['gt'][:, h * ng:(h + 1) * ng], gpad)]
    w_perm = jnp.concatenate([seg[n] for n in order] + gate_cols, axis=1)
    eye = jnp.eye(len(POOL_WINDOWS), dtype=F32)
    wp = jnp.einsum('gcd,gh->gchd', w_pool[l], eye).reshape(POOL_WIDTH, POOL_WIDTH)
    wx = jnp.broadcast_to(cmp_w[l].transpose(1, 0, 2)[..., None], (CMP_BLOCK, 2, NSA_KV_HEADS, HEAD_DIM))
    return dict(
        w_in=w_perm.astype(BF16), g_mix=norm_mix[l][None, :],
        w_kv_t=jnp.concatenate([seg['kvc'], seg['kvs'], seg['kvw']], axis=1).T.astype(BF16),
        g_ffn=norm_ffn[l][None, :],
        gn=gmlp_norm[l][None, :], wp=wp.astype(BF16), ps=pool_scale[l][None, :],
        wx=wx.reshape(CMP_BLOCK, KV_ROW), pe=cmp_pe[l].transpose(1, 0, 2, 3).reshape(CMP_BLOCK, KV_ROW),
        ws=gmlp_ws[l], wb=jnp.repeat(gmlp_b[l].T, POOL_GROUP_DIM, axis=1),
        wsx=jnp.repeat(gmlp_ws[l].transpose(1, 2, 0), POOL_GROUP_DIM, axis=2)[:8, :8],
        l=l, w_out=w_out.astype(BF16), w_up=w_up.astype(BF16), w_down=w_down.astype(BF16))


def _split_heads(kv, part):
    B, T, _ = kv.shape
    return kv.reshape(B, T, 2, NSA_KV_HEADS, HEAD_DIM)[:, :, part].transpose(0, 2, 1, 3).astype(BF16)


def _prompt_layer(x, p, tabs, B, T, final, gf):
    cfar, nbc, nbn, nbw = tabs
    tm = min(512, B * T)
    xp, q, gt, u, v, kvc, kvc_t, kvs_t, kvw_t = _inproj(x, p['g_mix'], p['w_in'], p['gn'], tm, seq=T,
                                                        wt=p['w_kv_t'])
    comp = _compress_prompt(kvc.reshape(B, T, KV_ROW), p['wx'], p['pe'])
    kc = _split_heads(comp, 0)
    vct = _split_heads(comp, 1).transpose(0, 1, 3, 2)
    q3 = q.reshape(B, T, NSA_WIDTH)
    oc, sn = _cmp_prompt(q3, kc, vct, nbc, cfar)
    pos = jnp.arange(-KPAD, T)
    blk_of = jnp.where(pos >= 0, pos // SEL_BLOCK, LANES - 1)
    onehot_t = (jnp.arange(LANES)[:, None] == blk_of[None, :]).astype(BF16)
    yb = _selwin_prompt(q3, sn, kvs_t, kvw_t, onehot_t, gt, oc, nbn, nbw)
    x2 = _mix_prompt(x, xp, yb.reshape(B * T, NSA_WIDTH), u, v, p['wp'], p['ps'], p['ws'], p['wb'], p['w_out'],
                     p['g_ffn'], p['w_up'], p['w_down'], gf, p['l'], tm, T, final)
    return x2, (xp, kvc_t, kvs_t, kvw_t)


CMP_PAGES_PER_STEP = 32
SEL_PAGES_PER_STEP = 64
SEL_PER_PAGE = PAGE_SIZE // SEL_BLOCK


def _cmp_sample_kernel(pt_ref, *refs, ts):
    del pt_ref
    npg = CMP_PAGES_PER_STEP
    nhalf = KV_ROW // LANES
    pages = refs[:npg]
    wx_ref, pe_ref, q_ref, tab_ref, oc_ref, sel_ref, lo_ref, hi_ref, ps_ref, pg_ref = refs[npg:]
    s = pl.program_id(1)
    rows = PAGE_SIZE // CMP_STRIDE
    for k in range(npg):
        for e in range(nhalf):
            pg_ref[...] = _page_half_rows(pages[k], e)
            lo, hi = _compress_rows(lambda j: pg_ref[pl.ds(j, rows, stride=CMP_STRIDE), :], rows,
                                    wx_ref[:, e * LANES:(e + 1) * LANES])
            lo_ref[s, k * rows:(k + 1) * rows, e * LANES:(e + 1) * LANES] = lo
            hi_ref[s, k * rows:(k + 1) * rows, e * LANES:(e + 1) * LANES] = hi

    @pl.when(s == pl.num_programs(1) - 1)
    def _():
        nc = lo_ref.shape[0] * lo_ref.shape[1]
        comp = _combine_halves(lo_ref[...].reshape(nc, KV_ROW), hi_ref[...].reshape(nc, KV_ROW), pe_ref[...],
                               wx_ref[...])
        ns = nc // 4
        lane = lax.broadcasted_iota(jnp.int32, (ns, LANES), 1)
        s_iota = lax.broadcasted_iota(jnp.int32, (ns, LANES), 0)
        ncol = NSA_GROUP * ts
        selacc = jnp.zeros((ns, LANES), F32)
        for h in range(NSA_KV_HEADS):
            kc = comp[:, h * HEAD_DIM:(h + 1) * HEAD_DIM].astype(BF16)
            vc = comp[:, (NSA_KV_HEADS + h) * HEAD_DIM:(NSA_KV_HEADS + h + 1) * HEAD_DIM].astype(BF16)
            pc = _softmax_rows(_dot_nt(kc, q_ref[0, h]) + tab_ref[h])
            oc_ref[0, h] = pl.dot(pc.astype(BF16), vc, trans_a=True)[0:ncol, :]
            ps = pc
            for g in range(1, NSA_GROUP):
                ps = ps + pltpu.roll(pc, g * ts, 1)
            ps_ref[...] = ps
            p_slc = ps_ref[pl.ds(0, ns, stride=4), :]
            for j in range(1, 4):
                p_slc = p_slc + ps_ref[pl.ds(j, ns, stride=4), :]
            forced = (s_iota == 0) | (s_iota == ns - 1)
            sel = _select_topk(jnp.where(forced, FORCE_SCORE, p_slc), N_SELECT - 1)
            for g in range(NSA_GROUP):
                dst = h * ncol + g * ts
                shift = (dst - (ncol - ts)) % LANES
                moved = pltpu.roll(sel, shift, 1) if shift else sel
                selacc = jnp.where((lane >= dst) & (lane < dst + ts), moved, selacc)
        sel_ref[0] = selacc.reshape(sel_ref.shape[1:])


def _page_half_rows(page_ref, e):
    t = page_ref[0, e]
    return t.reshape(NSA_KV_HEADS * HEAD_DIM, t.shape[-1]).T


def _native_rows(a, lead):
    nd = a.ndim
    t = jnp.transpose(a, tuple(range(nd - 4)) + (nd - 3, nd - 2, nd - 1, nd - 4))
    return t.reshape((lead,) + t.shape[nd - 4:])


def _page_specs(n_pages, npg):
    def spec(k):
        return pl.BlockSpec((1, 2, NSA_KV_HEADS, HEAD_DIM, PAGE_SIZE),
                            lambda b, s, pt: (pt[b * n_pages + s * npg + k], 0, 0, 0, 0))
    return [spec(k) for k in range(npg)]


def _cmp_sample(pt, cache, wx, pe, qs, tab, nb, n_pages, ts):
    npg = CMP_PAGES_PER_STEP
    steps = n_pages // npg
    nc = n_pages * PAGE_SIZE // CMP_STRIDE
    rows = npg * PAGE_SIZE // CMP_STRIDE
    ns = nc // 4
    full = lambda a: pl.BlockSpec(a.shape, lambda b, s, pt: (0,) * a.ndim)
    grid_spec = pltpu.PrefetchScalarGridSpec(
        num_scalar_prefetch=1,
        grid=(nb, steps),
        in_specs=_page_specs(n_pages, npg) + [
            full(wx), full(pe),
            pl.BlockSpec((1, NSA_KV_HEADS, LANES, HEAD_DIM), lambda b, s, pt: (b, 0, 0, 0)),
            full(tab)],
        out_specs=[pl.BlockSpec((1, NSA_KV_HEADS, NSA_GROUP * ts, HEAD_DIM), lambda b, s, pt: (b, 0, 0, 0)),
                   pl.BlockSpec((1, steps, ns // steps, LANES), lambda b, s, pt: (b, 0, 0, 0))],
        scratch_shapes=[pltpu.VMEM((steps, rows, KV_ROW), F32), pltpu.VMEM((steps, rows, KV_ROW), F32),
                        pltpu.VMEM((nc, LANES), F32), pltpu.VMEM((PAGE_SIZE, LANES), F32)])
    return pl.pallas_call(
        functools.partial(_cmp_sample_kernel, ts=ts),
        grid_spec=grid_spec,
        out_shape=[jax.ShapeDtypeStruct((nb, NSA_KV_HEADS, NSA_GROUP * ts, HEAD_DIM), F32),
                   jax.ShapeDtypeStruct((nb, steps, ns // steps, LANES), F32)],
        compiler_params=_cparams(("parallel", "arbitrary")),
        name="cmp_sample",
    )(pt, *([cache] * npg), wx, pe, qs, tab)


def _local_softmax(z, mask):
    m = jnp.max(z, axis=0, keepdims=True)
    p = jnp.where(mask, jnp.exp(z - m), 0.0)
    return m, p, jnp.sum(p, axis=0, keepdims=True)


def _selwin_sample_kernel(pt_ref, *refs, ts):
    del pt_ref
    npg = SEL_PAGES_PER_STEP
    pages = refs[:npg]
    (qa_ref, sel_ref, cf_ref, nbl_ref, st_ref, kvn_ref, nbn_ref, nbw_ref, g_ref, oc_ref,
     o_ref, m_ref, l_ref, a_ref) = refs[npg:]
    s = pl.program_id(1)
    last = pl.num_programs(1) - 1
    ncol = NSA_KV_HEADS * NSA_GROUP * ts
    qa = qa_ref[0]
    rowi = lax.broadcasted_iota(jnp.int32, (PAGE_SIZE, LANES), 0)
    for k in range(npg):
        pg = jnp.concatenate([_page_half_rows(pages[k], 0), _page_half_rows(pages[k], 1)],
                             axis=1).astype(BF16)
        is_last_page = jnp.logical_and(s == last, k == npg - 1)
        bias = jnp.where(is_last_page, nbl_ref[...], cf_ref[0:1, :])
        s0 = sel_ref[0, s, SEL_PER_PAGE * k:SEL_PER_PAGE * k + 1, :]
        s1 = sel_ref[0, s, SEL_PER_PAGE * k + 1:SEL_PER_PAGE * k + 2, :]
        mask = jnp.where(rowi < SEL_BLOCK, s0, s1) > 0.5
        z = jnp.where(mask, _dot(pg, qa) + bias, NEG)
        m, p, l = _local_softmax(z, mask)
        m_ref[s, k:k + 1, :] = m
        l_ref[s, k:k + 1, :] = l
        a_ref[s, k] = pl.dot(p.astype(BF16), pg, trans_a=True)[0:ncol, :]

    @pl.when(s == last)
    def _():
        nbn = nbn_ref[...]
        okn = nbn > 0.5 * NEG
        kn = kvn_ref[0, 0:SUBLANES, :].astype(BF16)
        m_n, p_n, l_n = _local_softmax(jnp.where(okn, _dot(kn, qa) + nbn, NEG), okn)
        a_n = pl.dot(p_n.astype(BF16), kn, trans_a=True)[0:ncol, :]
        nparts = m_ref.shape[0] * m_ref.shape[1]
        m_all = m_ref[...].reshape(nparts, LANES)
        l_all = l_ref[...].reshape(nparts, LANES)
        m_g = jnp.maximum(jnp.max(m_all, axis=0, keepdims=True), m_n)
        w_all = jnp.exp(m_all - m_g)
        w_n = jnp.exp(m_n - m_g)
        l_g = jnp.sum(w_all * l_all, axis=0, keepdims=True) + w_n * l_n

        nbw = nbw_ref[...]
        okw = nbw > 0.5 * NEG
        st = jnp.concatenate([_page_half_rows(st_ref, 0), _page_half_rows(st_ref, 1)],
                             axis=1).astype(BF16)
        kwn = kvn_ref[0, SUBLANES:2 * SUBLANES, :].astype(BF16)
        zw = jnp.where(okw, _dot(st, qa) + nbw, NEG)
        zwn = jnp.where(okn, _dot(kwn, qa) + nbn, NEG)
        m_w = jnp.maximum(jnp.max(zw, axis=0, keepdims=True), jnp.max(zwn, axis=0, keepdims=True))
        pw = jnp.where(okw, jnp.exp(zw - m_w), 0.0)
        pwn = jnp.where(okn, jnp.exp(zwn - m_w), 0.0)
        l_w = jnp.sum(pw, axis=0, keepdims=True) + jnp.sum(pwn, axis=0, keepdims=True)
        a_w = (pl.dot(pw.astype(BF16), st, trans_a=True) + pl.dot(pwn.astype(BF16), kwn, trans_a=True))[0:ncol, :]

        assert nparts + 4 * SUBLANES <= LANES
        rows8 = lambda r: jnp.broadcast_to(r, (SUBLANES, LANES))
        x = jnp.concatenate([w_all, rows8(w_n), rows8(l_g), rows8(l_w),
                             jnp.zeros((LANES - nparts - 3 * SUBLANES, LANES), F32)], axis=0)
        xt = x.T
        acc = xt[0:ncol, nparts:nparts + 1] * a_n
        for pidx in range(nparts):
            acc = acc + xt[0:ncol, pidx:pidx + 1] * a_ref[pidx // npg, pidx % npg]
        o_s = acc / jnp.maximum(xt[0:ncol, nparts + SUBLANES:nparts + SUBLANES + 1], 1e-30)
        o_w = a_w / jnp.maximum(xt[0:ncol, nparts + 2 * SUBLANES:nparts + 2 * SUBLANES + 1], 1e-30)

        ri = lax.broadcasted_iota(jnp.int32, (ncol, HEAD_DIM), 0)
        v_of = lambda t: jnp.where(ri < ncol // NSA_KV_HEADS, t[:, 2 * HEAD_DIM:3 * HEAD_DIM],
                                   t[:, 3 * HEAD_DIM:4 * HEAD_DIM])
        gt = g_ref[0]
        oc = oc_ref[0].reshape(ncol, HEAD_DIM)
        o_ref[0] = gt[:, 0:1] * oc + gt[:, 1:2] * v_of(o_s) + gt[:, 2:3] * v_of(o_w)


def _selwin_sample(pt, cache, qa, sel, cfrow, nbl, state, st_off, kvn, nbn, nbw, gates, oc, nb, n_pages, ts):
    npg = SEL_PAGES_PER_STEP
    steps = n_pages // npg
    ncol = NSA_KV_HEADS * NSA_GROUP * ts
    sel = sel.reshape(nb, steps, -1, LANES)
    full = lambda a: pl.BlockSpec(a.shape, lambda b, s, pt: (0,) * a.ndim)
    perb = lambda a: pl.BlockSpec((1,) + a.shape[1:], lambda b, s, pt: (b,) + (0,) * (a.ndim - 1))
    grid_spec = pltpu.PrefetchScalarGridSpec(
        num_scalar_prefetch=1,
        grid=(nb, steps),
        in_specs=_page_specs(n_pages, npg) + [
            perb(qa), perb(sel), full(cfrow), full(nbl),
            pl.BlockSpec((1,) + state.shape[1:], lambda b, s, pt: (st_off + b, 0, 0, 0, 0)),
            perb(kvn), full(nbn), full(nbw), perb(gates), perb(oc)],
        out_specs=pl.BlockSpec((1, ncol, HEAD_DIM), lambda b, s, pt: (b, 0, 0)),
        scratch_shapes=[pltpu.VMEM((steps, npg, LANES), F32),
                        pltpu.VMEM((steps, npg, LANES), F32),
                        pltpu.VMEM((steps, npg, ncol, KV_ROW), F32)])
    return pl.pallas_call(
        functools.partial(_selwin_sample_kernel, ts=ts),
        grid_spec=grid_spec,
        out_shape=jax.ShapeDtypeStruct((nb, ncol, HEAD_DIM), F32),
        compiler_params=_cparams(("parallel", "arbitrary")),
        name="selwin_sample",
    )(pt, *([cache] * npg), qa, sel, cfrow, nbl, state, kvn, nbn, nbw, gates, oc)


def _sample_bias_tables(rel_bias, past, ts, n_buf):
    ncol = NSA_KV_HEADS * NSA_GROUP * ts
    col = jnp.arange(ncol)
    head = col // ts
    t = col % ts
    pick = lambda b: jnp.take_along_axis(b, jnp.broadcast_to(head, b.shape[:-1])[..., None], axis=-1)[..., 0]
    bias = lambda d: pick(_head_bias(rel_bias, d))
    padc = lambda a, fill=0.0: jnp.pad(a, ((0, 0), (0, LANES - a.shape[1])), constant_values=fill)
    nc = past // CMP_STRIDE
    c_end = jnp.arange(nc)[:, None] * CMP_STRIDE + CMP_BLOCK - 1
    tab = jnp.where(jnp.arange(nc)[:, None] < nc - 1, bias(past + t[None, :] - c_end), NEG)
    hc = ncol // NSA_KV_HEADS
    tab_cmp = jnp.stack([padc(tab[:, h * hc:(h + 1) * hc]) for h in range(NSA_KV_HEADS)])
    far = rel_bias[N_BUCKETS - 1].astype(F32)[head]
    cfrow = jnp.broadcast_to(padc(far[None, :]), (SUBLANES, LANES))
    kk = jnp.arange(PAGE_SIZE)[:, None]
    nbl = padc(bias(PAGE_SIZE + t[None, :] - kk))
    j = jnp.arange(SUBLANES)[:, None]
    dn = t[None, :] - j
    nbn = padc(jnp.where((dn >= 0) & (j < ts), bias(dn), NEG))
    r = jnp.arange(n_buf)[:, None]
    dw = n_buf + t[None, :] - r
    nbw = padc(jnp.where((dw >= 0) & (dw < WINDOW), bias(dw), NEG))
    return tab_cmp, cfrow, nbl, nbn, nbw


def _sample_layer(x, p, tabs, l, pt, cache_cmp, cache_slc, state_win, state_pool_l, nb, ts, n_pages, final, gf):
    tab_cmp, cfrow, nbl, nbn, nbw = tabs
    past = n_pages * PAGE_SIZE
    R = ts * nb
    ncol = NSA_KV_HEADS * NSA_GROUP * ts
    xp, q, gt, u, v, kvc, kvs, kvw = _inproj(x, p['g_mix'], p['w_in'], p['gn'], R)
    q5 = q.reshape(ts, nb, NSA_KV_HEADS, NSA_GROUP, HEAD_DIM)
    qs = q5.transpose(1, 2, 3, 0, 4).reshape(nb, NSA_KV_HEADS, NSA_GROUP * ts, HEAD_DIM)
    qs = jnp.pad(qs, ((0, 0), (0, 0), (0, LANES - NSA_GROUP * ts), (0, 0)))
    oc, sel = _cmp_sample(pt, cache_cmp, p['wx'], p['pe'], qs, tab_cmp, nb, n_pages, ts)
    qt = q5.transpose(1, 2, 4, 3, 0).reshape(nb, NSA_KV_HEADS, HEAD_DIM, NSA_GROUP * ts)
    eye = jnp.eye(NSA_KV_HEADS, dtype=qt.dtype)
    qa = jnp.einsum('bhdc,hk->bhdkc', qt, eye).reshape(nb, NSA_KV_HEADS * HEAD_DIM, ncol)
    qa = jnp.pad(qa, ((0, 0), (0, KV_ROW - NSA_KV_HEADS * HEAD_DIM), (0, LANES - ncol)))
    rows_of = lambda a: jnp.pad(a.reshape(ts, nb, KV_ROW).transpose(1, 0, 2), ((0, 0), (0, SUBLANES - ts), (0, 0)))
    kvn = jnp.concatenate([rows_of(kvs), rows_of(kvw)], axis=1)
    ng = NSA_GROUP * N_BRANCHES
    g5 = jnp.concatenate([gt[:, h * LANES:h * LANES + ng] for h in range(NSA_KV_HEADS)], axis=1)
    g5 = g5.reshape(ts, nb, NSA_KV_HEADS * NSA_GROUP, N_BRANCHES)
    gates = jnp.pad(g5.transpose(1, 2, 0, 3).reshape(nb, ncol, N_BRANCHES), ((0, 0), (0, 0), (0, LANES - N_BRANCHES)))
    o = _selwin_sample(pt, cache_slc, qa, sel, cfrow, nbl, state_win, l * nb, kvn, nbn, nbw, gates, oc,
                       nb, n_pages, ts)
    yb = o.reshape(nb, NSA_KV_HEADS * NSA_GROUP, ts, HEAD_DIM).transpose(2, 0, 1, 3).reshape(R, NSA_WIDTH)
    ext = jnp.concatenate([state_pool_l.transpose(1, 0, 2), xp.reshape(ts, nb, POOL_WIDTH)], axis=0)
    x2 = _mix_sample(x, ext, yb.astype(BF16), u.reshape(ts, nb, GMLP_WIDTH), v.reshape(ts, nb, GMLP_WIDTH),
                     p['wp'], p['ps'], p['wsx'], p['wb'][:SUBLANES], p['w_out'],
                     p['g_ffn'], p['w_up'], p['w_down'], gf, p['l'], nb, ts, past, final)
    return x2, (xp, kvc, kvs, kvw, v)


def kernel(x_prompt, x_sample, cache_cmp_kv, cache_slc_kv, state_win_kv, state_pool, page_table, w_in, w_out,
           norm_mix, norm_ffn, norm_final, w_pool, pool_scale, cmp_w, cmp_pe, gmlp_ws, gmlp_b, gmlp_norm, w_up,
           w_down, rel_bias):
    B, T, _ = x_prompt.shape
    nb, ts, _ = x_sample.shape
    n_pages = page_table.shape[1]
    n_phys = cache_cmp_kv.shape[1]
    n_buf = state_win_kv.shape[2]
    past = n_pages * PAGE_SIZE
    depth = w_in.shape[0]
    assert T % TK_FAR == 0 and T >= WINDOW
    assert n_pages % CMP_PAGES_PER_STEP == 0 and n_pages % SEL_PAGES_PER_STEP == 0
    assert ts <= SUBLANES and ts <= POOL_STATE and n_buf == WINDOW and past >= WINDOW
    kv_tail = (2, NSA_KV_HEADS, HEAD_DIM)

    cache_cmp = _native_rows(cache_cmp_kv, depth * n_phys)
    cache_slc = _native_rows(cache_slc_kv, depth * n_phys)
    state_win = _native_rows(state_win_kv, depth * nb)
    ptabs = _prompt_bias_tables(rel_bias)
    stabs = _sample_bias_tables(rel_bias, past, ts, n_buf)
    gf = norm_final[None, :]
    xp = x_prompt.reshape(B * T, D_MODEL)
    xs = x_sample.transpose(1, 0, 2).reshape(ts * nb, D_MODEL)
    unmajor = lambda a: a.reshape(ts, nb, a.shape[-1]).transpose(1, 0, 2)

    outs = [[] for _ in range(9)]
    for l in range(depth):
        p = _prep_layer(l, w_in, w_out, norm_mix, norm_ffn, w_pool, pool_scale, cmp_w, cmp_pe, gmlp_ws, gmlp_b,
                        gmlp_norm, w_up, w_down)
        final = l == depth - 1
        xp, (pin, kvc_t, kvs_t, kvw_t) = _prompt_layer(xp, p, ptabs, B, T, final, gf)
        rows_of = lambda a: a.reshape((B,) + kv_tail + (a.shape[-1],)).transpose(0, 4, 1, 2, 3)
        outs[0].append(rows_of(kvc_t))
        outs[1].append(rows_of(kvs_t))
        outs[2].append(rows_of(kvw_t[:, :, T - WINDOW:]))
        outs[3].append(pin.reshape(B, T, POOL_WIDTH)[:, T - POOL_STATE:])

        pt = (page_table + l * n_phys).reshape(-1).astype(jnp.int32)
        xs, (sin, kvc_s, kvs_s, kvw_s, v_s) = _sample_layer(
            xs, p, stabs, l, pt, cache_cmp, cache_slc, state_win, state_pool[l], nb, ts, n_pages, final, gf)
        kvw_new = unmajor(kvw_s).reshape((nb, ts) + kv_tail)
        outs[4].append(unmajor(kvc_s).reshape((nb, ts) + kv_tail))
        outs[5].append(unmajor(kvs_s).reshape((nb, ts) + kv_tail))
        outs[6].append(jnp.concatenate([state_win_kv[l][:, ts:], kvw_new], axis=1))
        outs[7].append(jnp.concatenate([state_pool[l][:, ts:], unmajor(sin)], axis=1))
        outs[8].append(unmajor(v_s))

    y_prompt = xp.reshape(B, T, D_MODEL)
    y_sample = unmajor(xs)
    return (y_prompt, y_sample) + tuple(jnp.stack(o) for o in outs)
```

```python
import functools
import math

import numpy as np
import jax
import jax.numpy as jnp
from jax import lax
from jax.experimental import pallas as pl
from jax.experimental.pallas import tpu as pltpu

F32 = jnp.float32
BF16 = jnp.bfloat16

D_MODEL = 1024
DEPTH = 4
PAGE_SIZE = 128
HEAD_DIM = 64
POOL_WINDOWS = (2, 4, 8, 16)
POOL_WIDTH = 256
POOL_GROUP_DIM = 64
POOL_STATE = 15
NSA_WIDTH = 512
NSA_HEADS = 8
NSA_KV_HEADS = 2
NSA_GROUP = 4
KV_ROW = 2 * NSA_KV_HEADS * HEAD_DIM
CMP_STRIDE = 16
CMP_BLOCK = 32
SEL_BLOCK = 64
N_SELECT = 16
WINDOW = 512
N_BRANCHES = 3
FORCE_SCORE = 1000.0
GMLP_WIDTH = 256
GMLP_GROUPS = 4
GMLP_CHUNK = 128
D_FF = 4096
N_BUCKETS = 32
MAX_DISTANCE = 128
EPS = 1e-6
SCALE = HEAD_DIM ** -0.5

LANES = 128
SUBLANES = 8
VMEM_LIMIT = 56 * 1024 * 1024

NEG = -1e30
CMP_PAD = 16
TQ = 128
TQC = 1024
CMP_NEAR = CMP_PAD + TQC // CMP_STRIDE
TK_FAR = 512
KPAD = 512
GATE_LANES = 2 * LANES

PROJ_SEGS = (('xp', 256), ('q', 512), ('kvc', 256), ('kvs', 256), ('kvw', 256), ('gt', 24), ('u', 256), ('v', 256))


def _cparams(sem):
    return pltpu.CompilerParams(dimension_semantics=sem, vmem_limit_bytes=VMEM_LIMIT)


def _dot(a, b):
    return jnp.dot(a, b, preferred_element_type=F32)


def _dot_nt(a, b):
    return lax.dot_general(a, b, (((1,), (1,)), ((), ())), preferred_element_type=F32)


def _gelu(x):
    c = math.sqrt(2.0 / math.pi)
    return 0.5 * x * (1.0 + jnp.tanh(c * (x + 0.044715 * (x * x * x))))


def _rms(x, g):
    return x * lax.rsqrt(jnp.mean(x * x, axis=-1, keepdims=True) + EPS) * g


def _inproj_kernel(x_ref, g_ref, w_ref, gn_ref, *refs, channel_major):
    if channel_major:
        wt_ref, xp_ref, q_ref, gt_ref, u_ref, v_ref, kvc_ref, *kv_refs = refs
    else:
        xp_ref, q_ref, gt_ref, u_ref, v_ref, kvc_ref, *kv_refs = refs
    x = x_ref[...]
    h = _rms(x, g_ref[...]).astype(BF16)

    def seg(lo, hi):
        return _dot(h, w_ref[:, lo:hi])

    xp_ref[...] = seg(0, 256)
    q_ref[...] = (seg(256, 768) * SCALE).astype(BF16)
    kvc_ref[...] = seg(768, 1024)
    if channel_major:
        for k, t_ref in enumerate(kv_refs):
            t_ref[0] = _dot_nt(wt_ref[k * KV_ROW:(k + 1) * KV_ROW, :], h)
    else:
        kv_refs[0][...] = seg(1024, 1280)
        kv_refs[1][...] = seg(1280, 1536)
    u_ref[...] = _gelu(seg(1536, 1792))
    gv = _gelu(seg(1792, 2048))
    sq = gv * gv
    lane = lax.broadcasted_iota(jnp.int32, sq.shape, 1)
    ms = jnp.zeros_like(sq)
    for g in range(GMLP_GROUPS):
        in_g = (lane >= g * 64) & (lane < (g + 1) * 64)
        s = jnp.sum(jnp.where(in_g, sq, 0.0), axis=-1, keepdims=True) * (1.0 / 64.0)
        ms = jnp.where(in_g, s, ms)
    v_ref[...] = gv * lax.rsqrt(ms + EPS) * gn_ref[...]
    gt_ref[...] = jax.nn.sigmoid(seg(2048, 2048 + GATE_LANES))


def _inproj(x, g, w, gn, tm, seq=None, wt=None):
    R = x.shape[0]
    f = lambda n, dt=F32: jax.ShapeDtypeStruct((R, n), dt)
    row = lambda n: pl.BlockSpec((tm, n), lambda i: (i, 0))
    full = lambda a: pl.BlockSpec(a.shape, lambda i: (0,) * a.ndim)
    out_specs = [row(256), row(512), row(GATE_LANES), row(256), row(256), row(256)]
    out_shape = [f(256), f(512, BF16), f(GATE_LANES), f(256), f(256), f(256)]
    if seq is None:
        out_specs += [row(256), row(256)]
        out_shape += [f(256), f(256)]
    else:
        per_seq = seq // tm
        out_specs += [pl.BlockSpec((1, KV_ROW, tm), lambda i: (i // per_seq, 0, i % per_seq))] * 3
        out_shape += [jax.ShapeDtypeStruct((R // seq, KV_ROW, seq), F32)] * 3
    args = (x, g, w, gn) if seq is None else (x, g, w, gn, wt)
    return pl.pallas_call(
        functools.partial(_inproj_kernel, channel_major=seq is not None),
        grid=(R // tm,),
        in_specs=[row(D_MODEL)] + [full(a) for a in args[1:]],
        out_specs=out_specs,
        out_shape=out_shape,
        compiler_params=_cparams(("parallel",)),
        name="inproj",
    )(*args)


def _out_proj(x, ya, yb, yc, wo_ref):
    acc = _dot(ya.astype(BF16), wo_ref[0:256, :])
    acc = acc + _dot(yb, wo_ref[256:768, :])
    acc = acc + _dot(yc.astype(BF16), wo_ref[768:1024, :])
    return x + acc


def _pool_tail(win2, win4, win8, win16, cur, cnt, wp_ref, ps_ref):
    lane = lax.broadcasted_iota(jnp.int32, cur.shape, 1)
    win = jnp.where(lane < 64, win2, jnp.where(lane < 128, win4, jnp.where(lane < 192, win8, win16)))
    pooled = win / cnt - cur
    return _dot(pooled.astype(BF16), wp_ref[...]) * ps_ref[...]


FF_CHUNK = 1024


def _ffn_rows(x, g_ref, wu_ref, wd_ref, gf_ref, final):
    h = _rms(x, g_ref[...]).astype(BF16)
    acc = x
    for c in range(D_FF // FF_CHUNK):
        a = jnp.maximum(_dot(h, wu_ref[:, c * FF_CHUNK:(c + 1) * FF_CHUNK]), 0.0)
        acc = acc + _dot((a * a).astype(BF16), wd_ref[c * FF_CHUNK:(c + 1) * FF_CHUNK, :])
    return _rms(acc, gf_ref[...]) if final else acc


def _layer_spec(w, l, **kw):
    return pl.BlockSpec((None,) + w.shape[1:], lambda i: (l, 0, 0), **kw)


def _ffn_specs(g, wu, wd, gf, l):
    full = lambda a: pl.BlockSpec(a.shape, lambda i: (0,) * a.ndim)
    return [full(g), _layer_spec(wu, l, pipeline_mode=pl.Buffered(1)),
            _layer_spec(wd, l, pipeline_mode=pl.Buffered(1)), full(gf)]


def _mix_prompt_kernel(x_ref, xp_ref, halo_ref, yb_ref, u_ref, v_ref, wp_ref, ps_ref, ws_ref, wb_ref, wo_ref,
                       g_ref, wu_ref, wd_ref, gf_ref, o_ref, ext_ref, *, tm, seq, final):
    i = pl.program_id(0)
    t0 = (i * tm) % seq
    halo = jnp.where(t0 > 0, halo_ref[...], 0.0)
    cur = xp_ref[...]
    ext_ref[0:16, :] = halo
    ext_ref[16:16 + tm, :] = cur
    e = ext_ref[...]
    b2 = e[1:] + e[:-1]
    b4 = b2[2:] + b2[:-2]
    b8 = b4[4:] + b4[:-4]
    b16 = b8[8:] + b8[:-8]
    win2 = b2[15:15 + tm]
    win4 = b4[13:13 + tm]
    win8 = b8[9:9 + tm]
    win16 = b16[1:1 + tm]
    rowi = lax.broadcasted_iota(jnp.int32, cur.shape, 0)
    lane = lax.broadcasted_iota(jnp.int32, cur.shape, 1)
    wsz = jnp.where(lane < 64, 2, jnp.where(lane < 128, 4, jnp.where(lane < 192, 8, 16)))
    cnt = jnp.minimum(t0 + rowi + 1, wsz).astype(F32)
    ya = _pool_tail(win2, win4, win8, win16, cur, cnt, wp_ref, ps_ref)

    ci = lax.broadcasted_iota(jnp.int32, (GMLP_CHUNK, GMLP_CHUNK), 0)
    cj = lax.broadcasted_iota(jnp.int32, (GMLP_CHUNK, GMLP_CHUNK), 1)
    lane_c = lax.broadcasted_iota(jnp.int32, (GMLP_CHUNK, GMLP_WIDTH), 1)
    wts = [jnp.where(ci >= cj, ws_ref[g], 0.0).astype(BF16) for g in range(GMLP_GROUPS)]
    parts = []
    for c in range(tm // GMLP_CHUNK):
        vc = v_ref[c * GMLP_CHUNK:(c + 1) * GMLP_CHUNK, :].astype(BF16)
        s = jnp.zeros((GMLP_CHUNK, GMLP_WIDTH), F32)
        for g in range(GMLP_GROUPS):
            sg = _dot(wts[g], vc)
            s = jnp.where((lane_c >= g * 64) & (lane_c < (g + 1) * 64), sg, s)
        parts.append(u_ref[c * GMLP_CHUNK:(c + 1) * GMLP_CHUNK, :] * (s + wb_ref[...]))
    yc = jnp.concatenate(parts, axis=0) if len(parts) > 1 else parts[0]

    x1 = _out_proj(x_ref[...], ya, yb_ref[...], yc, wo_ref)
    o_ref[...] = _ffn_rows(x1, g_ref, wu_ref, wd_ref, gf_ref, final)


def _mix_prompt(x, xp, yb, u, v, wp, ps, ws, wb, wo, g, wu, wd, gf, l, tm, seq, final):
    R = x.shape[0]
    row = lambda n: pl.BlockSpec((tm, n), lambda i: (i, 0))
    full = lambda a: pl.BlockSpec(a.shape, lambda i: (0,) * a.ndim)
    halo = pl.BlockSpec((16, 256), lambda i: (jnp.maximum(i * (tm // 16) - 1, 0), 0))
    return pl.pallas_call(
        functools.partial(_mix_prompt_kernel, tm=tm, seq=seq, final=final),
        grid=(R // tm,),
        in_specs=[row(D_MODEL), row(256), halo, row(512), row(256), row(256),
                  full(wp), full(ps), full(ws), full(wb), _layer_spec(wo, l)] + _ffn_specs(g, wu, wd, gf, l),
        out_specs=row(D_MODEL),
        out_shape=jax.ShapeDtypeStruct((R, D_MODEL), F32),
        scratch_shapes=[pltpu.VMEM((tm + 16, 256), F32)],
        compiler_params=_cparams(("parallel",)),
        name="mix_prompt",
    )(x, xp, xp, yb, u, v, wp, ps, ws, wb, wo, g, wu, wd, gf)


def _mix_sample_kernel(x_ref, ext_ref, yb_ref, u_ref, v_ref, wp_ref, ps_ref, wsx_ref, wb_ref, wo_ref,
                       g_ref, wu_ref, wd_ref, gf_ref, o_ref, *, nb, ts, past, final):
    lane = lax.broadcasted_iota(jnp.int32, (nb, POOL_WIDTH), 1)
    wsz = jnp.where(lane < 64, 2, jnp.where(lane < 128, 4, jnp.where(lane < 192, 8, 16)))
    yas, ycs = [], []
    for t in range(ts):
        top = POOL_STATE + t
        acc = ext_ref[top] + ext_ref[top - 1]
        wins = [acc]
        for w in (4, 8, 16):
            for j in range(w // 2, w):
                acc = acc + ext_ref[top - j]
            wins.append(acc)
        cnt = jnp.minimum(past + t + 1, wsz).astype(F32)
        yas.append(_pool_tail(wins[0], wins[1], wins[2], wins[3], ext_ref[top], cnt, wp_ref, ps_ref))
        s = wb_ref[t:t + 1, :]
        s = jnp.broadcast_to(s, (nb, GMLP_WIDTH))
        for j in range(t + 1):
            s = s + wsx_ref[t, j:j + 1, :] * v_ref[j]
        ycs.append(u_ref[t] * s)
    ya = jnp.concatenate(yas, axis=0)
    yc = jnp.concatenate(ycs, axis=0)
    x1 = _out_proj(x_ref[...], ya, yb_ref[...], yc, wo_ref)
    o_ref[...] = _ffn_rows(x1, g_ref, wu_ref, wd_ref, gf_ref, final)


def _mix_sample(x, ext, yb, u, v, wp, ps, wsx, wb4, wo, g, wu, wd, gf, l, nb, ts, past, final):
    R = x.shape[0]
    args = (x, ext, yb, u, v, wp, ps, wsx, wb4, wo)
    full = lambda a: pl.BlockSpec(a.shape, lambda i: (0,) * a.ndim)
    return pl.pallas_call(
        functools.partial(_mix_sample_kernel, nb=nb, ts=ts, past=past, final=final),
        grid=(1,),
        in_specs=[full(a) for a in args[:-1]] + [_layer_spec(wo, l)] + _ffn_specs(g, wu, wd, gf, l),
        out_specs=pl.BlockSpec((R, D_MODEL), lambda i: (0, 0)),
        out_shape=jax.ShapeDtypeStruct((R, D_MODEL), F32),
        compiler_params=_cparams(("arbitrary",)),
        name="mix_sample",
    )(*args, g, wu, wd, gf)


def _compress_rows(read_rows, n, wx):
    lo = jnp.zeros((n, wx.shape[1]), F32)
    hi = jnp.zeros((n, wx.shape[1]), F32)
    for j in range(CMP_STRIDE):
        xj = read_rows(j)
        lo = lo + xj * wx[j:j + 1, :]
        hi = hi + xj * wx[CMP_STRIDE + j:CMP_STRIDE + j + 1, :]
    return lo, hi


def _combine_halves(lo, hi, pe, wx):
    n = lo.shape[0]
    pe_term = jnp.sum(pe * wx, axis=0, keepdims=True)
    comp = lo + pltpu.roll(hi, n - 1, 0) + pe_term
    rowi = lax.broadcasted_iota(jnp.int32, comp.shape, 0)
    return jnp.where(rowi < n - 1, comp, 0.0)


def _compress_prompt_kernel(x_ref, wx_ref, pe_ref, o_ref, *, n):
    wx = wx_ref[...]
    lo, hi = _compress_rows(lambda j: x_ref[0, pl.ds(j, n, stride=CMP_STRIDE), :], n, wx)
    o_ref[0, 0:CMP_PAD, :] = jnp.zeros((CMP_PAD, LANES), F32)
    o_ref[0, CMP_PAD:CMP_PAD + n, :] = _combine_halves(lo, hi, pe_ref[...], wx)


def _compress_prompt(kvc, wx, pe):
    B, T, _ = kvc.shape
    n = T // CMP_STRIDE
    half = lambda rows: pl.BlockSpec((rows, LANES), lambda b, e: (0, e))
    return pl.pallas_call(
        functools.partial(_compress_prompt_kernel, n=n),
        grid=(B, KV_ROW // LANES),
        in_specs=[pl.BlockSpec((1, T, LANES), lambda b, e: (b, 0, e)), half(CMP_BLOCK), half(CMP_BLOCK)],
        out_specs=pl.BlockSpec((1, n + CMP_PAD, LANES), lambda b, e: (b, 0, e)),
        out_shape=jax.ShapeDtypeStruct((B, n + CMP_PAD, KV_ROW), F32),
        compiler_params=_cparams(("parallel", "parallel")),
        name="compress_prompt",
    )(kvc, wx, pe)


def _select_topk(score, n_pick):
    s_iota = lax.broadcasted_iota(jnp.int32, score.shape, 0)
    big = score.shape[0]
    work = score
    for _ in range(n_pick):
        m = jnp.max(work, axis=0, keepdims=True)
        idx = jnp.min(jnp.where(work == m, s_iota, big), axis=0, keepdims=True)
        work = jnp.where(s_iota == idx, -2.0, work)
    return jnp.where((work == -2.0) & (score >= 0.0), 1.0, 0.0)


def _softmax_rows(z):
    m = jnp.maximum(jnp.max(z, axis=0, keepdims=True), 0.1 * NEG)
    p = jnp.exp(z - m)
    den = jnp.maximum(jnp.sum(p, axis=0, keepdims=True), 1e-30)
    return p * (1.0 / den)


def _cmp_prompt_kernel(q_ref, kc_ref, vct_ref, nb_ref, cf_ref, oc_ref, sn_ref, raw_ref, z_ref, ps_ref, *, ncp):
    i = pl.program_id(2)
    hh = pl.program_id(1)
    tq = q_ref.shape[1]
    r0 = pl.multiple_of(i * (tq // CMP_STRIDE), SUBLANES)
    q = q_ref[0]
    vct = vct_ref[0, 0]
    rown = lax.broadcasted_iota(jnp.int32, (CMP_NEAR, tq), 0) + r0

    def attend(rows):
        kc = kc_ref[0, 0, 0:rows, :]
        rowi = lax.broadcasted_iota(jnp.int32, (rows, tq), 0)
        far = (rowi >= CMP_PAD) & (rowi < r0)
        psum = jnp.zeros((rows, tq), F32)
        octs = []
        for g in range(NSA_GROUP):
            qg = q[:, g * HEAD_DIM:(g + 1) * HEAD_DIM]
            raw_ref[0:rows, :] = _dot_nt(kc, qg)
            cf = cf_ref[hh * NSA_GROUP + g]
            z_ref[0:rows, :] = jnp.where(far, raw_ref[0:rows, :] + cf[0:1, :], NEG)
            nb = nb_ref[hh * NSA_GROUP + g]
            zn = raw_ref[pl.ds(r0, CMP_NEAR), :] + nb
            z_ref[pl.ds(r0, CMP_NEAR), :] = jnp.where((rown >= CMP_PAD) & (nb > 0.5 * NEG), zn, NEG)
            pc = _softmax_rows(z_ref[0:rows, :])
            psum = psum + pc
            pcb = pc.astype(BF16)
            if rows < ncp:
                pcb = jnp.concatenate([pcb, jnp.zeros((ncp - rows, tq), BF16)], axis=0)
            octs.append(_dot(vct, pcb))
        oc_ref[0] = jnp.concatenate(octs, axis=0).T
        for c in range(tq // LANES):
            ps_ref[c, 0:rows, :] = psum[:, c * LANES:(c + 1) * LANES]
            if rows < ps_ref.shape[1]:
                ps_ref[c, rows:, :] = jnp.zeros((ps_ref.shape[1] - rows, LANES), F32)

    n_var = (ncp - CMP_PAD + LANES - 1) // LANES
    var = (r0 + CMP_NEAR - CMP_PAD + LANES - 1) // LANES - 1
    for v in range(n_var):
        pl.when(var == v)(functools.partial(attend, min(CMP_PAD + (v + 1) * LANES, ncp)))

    ns = LANES
    halves = []
    for c in range(tq // LANES):
        part = ps_ref[c, pl.ds(CMP_PAD, ns, stride=4), :]
        for j in range(1, 4):
            part = part + ps_ref[c, pl.ds(CMP_PAD + j, ns, stride=4), :]
        halves.append(part)
    p_slc = jnp.concatenate(halves, axis=1)
    s_iota = lax.broadcasted_iota(jnp.int32, (ns, tq), 0)
    tt = lax.broadcasted_iota(jnp.int32, (ns, tq), 1)
    cur = (i * tq + tt) // SEL_BLOCK
    forced = (s_iota == 0) | (s_iota == cur) | (s_iota == cur - 1)
    score = jnp.where(s_iota <= cur, jnp.where(forced, FORCE_SCORE, p_slc), -1.0)
    sel = _select_topk(score, N_SELECT)
    sn_ref[0, 0] = jnp.where(sel.T > 0.5, 0.0, NEG).astype(BF16)


def _cmp_prompt(q, kc, vct, nbc, cfar):
    B, T, _ = q.shape
    ncp = kc.shape[2]
    ns = LANES
    nq = T // TQC
    full = lambda a: pl.BlockSpec(a.shape, lambda b, h, i: (0,) * a.ndim)
    return pl.pallas_call(
        functools.partial(_cmp_prompt_kernel, ncp=ncp),
        grid=(B, NSA_KV_HEADS, nq),
        in_specs=[pl.BlockSpec((1, TQC, 256), lambda b, h, i: (b, i, h)),
                  pl.BlockSpec((1, 1, ncp, HEAD_DIM), lambda b, h, i: (b, h, 0, 0)),
                  pl.BlockSpec((1, 1, HEAD_DIM, ncp), lambda b, h, i: (b, h, 0, 0)),
                  full(nbc), full(cfar)],
        out_specs=[pl.BlockSpec((1, TQC, 256), lambda b, h, i: (b, i, h)),
                   pl.BlockSpec((1, 1, TQC, ns), lambda b, h, i: (b, h, i, 0))],
        out_shape=[jax.ShapeDtypeStruct((B, T, NSA_WIDTH), F32),
                   jax.ShapeDtypeStruct((B, NSA_KV_HEADS, T, ns), BF16)],
        scratch_shapes=[pltpu.VMEM((ncp, TQC), F32), pltpu.VMEM((ncp, TQC), F32),
                        pltpu.VMEM((TQC // LANES, max(ncp, CMP_PAD + 4 * LANES), LANES), F32)],
        compiler_params=_cparams(("parallel", "parallel", "parallel")),
        name="cmp_prompt",
    )(q, kc, vct, nbc, cfar)


def _selwin_prompt_kernel(q_ref, sn_ref, ks_ref, vs_ref, kw_ref, vw_ref, oh_ref, g_ref, oc_ref, nbn_ref, nbw_ref,
                          gsel_ref, o_ref, ka_ref, va_ref, kwa_ref, vwa_ref, ms_ref, as_ref, za_ref, zb_ref):
    i = pl.program_id(2)
    R4 = NSA_GROUP * TQ
    ncols = ka_ref.shape[1]

    @pl.when(i == 0)
    def _():
        zpad = jnp.zeros((HEAD_DIM, KPAD), BF16)
        ones = jnp.ones((HEAD_DIM, ncols), BF16)
        row = lax.broadcasted_iota(jnp.int32, (HEAD_DIM, ncols), 0)
        col = lax.broadcasted_iota(jnp.int32, (HEAD_DIM, ncols), 1)
        ka_ref[0:LANES, :] = oh_ref[...]
        kwa_ref[HEAD_DIM:, :] = jnp.where((row == 0) & (col < KPAD), NEG, 0.0).astype(BF16)
        for dst, src in ((ka_ref.at[LANES:], ks_ref), (va_ref.at[0:HEAD_DIM], vs_ref),
                         (kwa_ref.at[0:HEAD_DIM], kw_ref), (vwa_ref.at[0:HEAD_DIM], vw_ref)):
            dst[:, 0:KPAD] = zpad
            dst[:, KPAD:] = src[0].astype(BF16)
        va_ref[HEAD_DIM:, :] = ones
        vwa_ref[HEAD_DIM:, :] = ones

    q = q_ref[0]
    q4 = jnp.concatenate([q[:, g * HEAD_DIM:(g + 1) * HEAD_DIM] for g in range(NSA_GROUP)], axis=0)
    sn = sn_ref[0, 0]
    blk = lax.broadcasted_iota(jnp.int32, (TQ, LANES), 1)
    first_near = (i - 1) * (TQ // SEL_BLOCK)
    sn_far = jnp.where(blk >= first_near, NEG, sn.astype(F32)).astype(BF16)
    qa_near = jnp.concatenate([jnp.concatenate([sn] * NSA_GROUP, axis=0), q4], axis=1)
    qa_far = jnp.concatenate([jnp.concatenate([sn_far] * NSA_GROUP, axis=0), q4], axis=1)
    qw = jnp.concatenate([q4, jnp.ones((R4, HEAD_DIM), BF16)], axis=1)

    def one_pass(z, vt):
        m = jnp.max(z, axis=1, keepdims=True)
        acc = _dot_nt(jnp.exp(z - m).astype(BF16), vt)
        return m, acc

    kw0 = pl.multiple_of(i * TQ, TQ)
    zw = _dot(qw, kwa_ref[:, pl.ds(kw0, WINDOW + TQ)]) + nbw_ref[0]
    _, a_w = one_pass(zw, vwa_ref[:, pl.ds(kw0, WINDOW + TQ)])

    kn0 = pl.multiple_of((i - 1) * TQ + KPAD, TQ)
    zn = _dot(qa_near, ka_ref[:, pl.ds(kn0, 2 * TQ)]) + nbn_ref[0]
    m0, a0 = one_pass(zn, va_ref[:, pl.ds(kn0, 2 * TQ)])
    ms_ref[...] = jnp.broadcast_to(m0, (R4, LANES))
    as_ref[...] = a0

    def far_logits(j):
        k0 = pl.multiple_of(j * TK_FAR + KPAD, TK_FAR)
        return _dot(qa_far, ka_ref[:, pl.ds(k0, TK_FAR)])

    def far_update(j, z):
        k0 = pl.multiple_of(j * TK_FAR + KPAD, TK_FAR)
        m_prev = ms_ref[...]
        m_new = jnp.maximum(m_prev, jnp.max(z, axis=1, keepdims=True))
        p = jnp.exp(z - jnp.concatenate([m_new] * (TK_FAR // LANES), axis=1))
        as_ref[...] = jnp.exp(m_prev - m_new) * as_ref[...] + _dot_nt(p.astype(BF16), va_ref[:, pl.ds(k0, TK_FAR)])
        ms_ref[...] = m_new

    n_far = (jnp.maximum(i - 1, 0) * TQ + TK_FAR - 1) // TK_FAR
    odd = n_far % 2
    last = n_far - 1

    def far_pair(t):
        zb_ref[...] = far_logits(t + 1)
        far_update(t, za_ref[...])
        za_ref[...] = far_logits(jnp.minimum(t + 2, last))
        far_update(t + 1, zb_ref[...])

    n_pairs = (n_far + 1) // 2

    za_ref[...] = far_logits(-odd)

    @pl.when(n_pairs % 2 == 1)
    def _():
        far_pair(-odd)

    def far_body(jj, carry):
        t = 4 * jj + 2 * (n_pairs % 2) - odd
        far_pair(t)
        far_pair(t + 2)
        return carry

    lax.fori_loop(0, n_pairs // 2, far_body, 0)

    gt = g_ref[...]
    lane = lax.broadcasted_iota(jnp.int32, (TQ, LANES), 1)
    own = [jnp.where((lane >= N_BRANCHES * g) & (lane < N_BRANCHES * (g + 1)), gt, 0.0) for g in range(NSA_GROUP)]
    gt4 = jnp.concatenate(own, axis=0)
    hi = gt4.astype(BF16)
    lo = (gt4 - hi.astype(F32)).astype(BF16)
    gates = _dot(jnp.concatenate([hi, lo], axis=1), gsel_ref[...])

    def scaled(acc, gate):
        f = gate / jnp.maximum(acc, 1e-30)
        return acc * pltpu.roll(f, HEAD_DIM, 1)

    o_sw = scaled(as_ref[...], gates[:, LANES:2 * LANES]) + scaled(a_w, gates[:, 2 * LANES:])
    oc = oc_ref[0]
    outs = []
    for g in range(NSA_GROUP):
        sl = slice(g * TQ, (g + 1) * TQ)
        outs.append(gates[sl, 0:HEAD_DIM] * oc[:, g * HEAD_DIM:(g + 1) * HEAD_DIM] + o_sw[sl, 0:HEAD_DIM])
    o_ref[0] = jnp.concatenate(outs, axis=1).astype(BF16)


def _selwin_prompt(q, sn, kvs_t, kvw_t, onehot_t, gates, oc, nbn, nbw):
    B, T, _ = q.shape
    src = jnp.arange(2 * LANES) % LANES
    dst = jnp.arange(N_BRANCHES * LANES) // LANES
    gsel = ((src[:, None] < NSA_GROUP * N_BRANCHES) & (src[:, None] % N_BRANCHES == dst[None, :])).astype(BF16)
    nq = T // TQ
    R4 = NSA_GROUP * TQ
    ncols = T + KPAD
    k_of = pl.BlockSpec((1, HEAD_DIM, T), lambda b, h, i: (b, h, 0))
    v_of = pl.BlockSpec((1, HEAD_DIM, T), lambda b, h, i: (b, NSA_KV_HEADS + h, 0))
    perh = lambda a: pl.BlockSpec((1,) + a.shape[1:], lambda b, h, i: (h, 0, 0))
    qtile = pl.BlockSpec((1, TQ, 256), lambda b, h, i: (b, i, h))
    return pl.pallas_call(
        _selwin_prompt_kernel,
        grid=(B, NSA_KV_HEADS, nq),
        in_specs=[qtile, pl.BlockSpec((1, 1, TQ, sn.shape[-1]), lambda b, h, i: (b, h, i, 0)),
                  k_of, v_of, k_of, v_of, pl.BlockSpec(onehot_t.shape, lambda b, h, i: (0, 0)),
                  pl.BlockSpec((TQ, LANES), lambda b, h, i: (b * nq + i, h)), qtile, perh(nbn), perh(nbw),
                  pl.BlockSpec(gsel.shape, lambda b, h, i: (0, 0))],
        out_specs=qtile,
        out_shape=jax.ShapeDtypeStruct((B, T, NSA_WIDTH), BF16),
        scratch_shapes=[pltpu.VMEM((LANES + HEAD_DIM, ncols), BF16)] + [pltpu.VMEM((LANES, ncols), BF16)] * 3
        + [pltpu.VMEM((R4, LANES), F32)] * 2 + [pltpu.VMEM((R4, TK_FAR), F32)] * 2,
        compiler_params=_cparams(("parallel", "parallel", "arbitrary")),
        name="selwin_prompt",
    )(q, sn, kvs_t, kvs_t, kvw_t, kvw_t, onehot_t, gates, oc, nbn, nbw, gsel)


def _rel_bucket(dist):
    d = jnp.maximum(dist, 0)
    n_exact = N_BUCKETS // 2
    d_f = jnp.maximum(d, 1).astype(F32)
    large = n_exact + (jnp.log(d_f / n_exact) / math.log(MAX_DISTANCE / n_exact)
                       * (N_BUCKETS - n_exact)).astype(jnp.int32)
    large = jnp.minimum(large, N_BUCKETS - 1)
    return jnp.where(d < n_exact, d, large)


def _head_bias(rel_bias, dist):
    onehot = (_rel_bucket(dist)[..., None] == jnp.arange(N_BUCKETS)).astype(F32)
    return jnp.einsum('...k,kh->...h', onehot, rel_bias.astype(F32), precision=lax.Precision.HIGHEST)


def _prompt_bias_tables(rel_bias):
    far = rel_bias[N_BUCKETS - 1].astype(F32)
    cfar = jnp.broadcast_to(far[:, None, None], (NSA_HEADS, SUBLANES, TQC))
    k = jnp.arange(CMP_NEAR)[:, None]
    tt = jnp.arange(TQC)[None, :]
    d = tt + CMP_PAD * CMP_STRIDE - (CMP_BLOCK - 1) - CMP_STRIDE * k
    nbc = jnp.where((d >= 0)[..., None], _head_bias(rel_bias, d), NEG).transpose(2, 0, 1)
    tq = jnp.arange(TQ)[:, None]

    def tile_table(first_dist, n_keys, max_dist):
        d = first_dist + tq - jnp.arange(n_keys)[None, :]
        b = jnp.where(((d >= 0) & (d < max_dist))[..., None], _head_bias(rel_bias, d) - far, NEG)
        return b.transpose(2, 0, 1).reshape(NSA_KV_HEADS, NSA_GROUP * TQ, n_keys)

    nbn = tile_table(TQ, 2 * TQ, 2 * TQ + 1)
    nbw = tile_table(WINDOW, WINDOW + TQ, WINDOW)
    return cfar, nbc, nbn, nbw


def _prep_layer(l, w_in, w_out, norm_mix, norm_ffn, w_pool, pool_scale, cmp_w, cmp_pe, gmlp_ws, gmlp_b,
                gmlp_norm, w_up, w_down):
    w = w_in[l]
    offs = np.cumsum([0] + [n for _, n in PROJ_SEGS])
    seg = {name: w[:, offs[k]:offs[k + 1]] for k, (name, _) in enumerate(PROJ_SEGS)}
    order = ('xp', 'q', 'kvc', 'kvs', 'kvw', 'u', 'v')
    ng = NSA_GROUP * N_BRANCHES
    gpad = jnp.zeros((D_MODEL, LANES - ng), w.dtype)
    gate_cols = [c for h in range(NSA_KV_HEADS) for c in (seg['gt'][:, h * ng:(h + 1) * ng], gpad)]
    w_perm = jnp.concatenate([seg[n] for n in order] + gate_cols, axis=1)
    eye = jnp.eye(len(POOL_WINDOWS), dtype=F32)
    wp = jnp.einsum('gcd,gh->gchd', w_pool[l], eye).reshape(POOL_WIDTH, POOL_WIDTH)
    wx = jnp.broadcast_to(cmp_w[l].transpose(1, 0, 2)[..., None], (CMP_BLOCK, 2, NSA_KV_HEADS, HEAD_DIM))
    return dict(
        w_in=w_perm.astype(BF16), g_mix=norm_mix[l][None, :],
        w_kv_t=jnp.concatenate([seg['kvc'], seg['kvs'], seg['kvw']], axis=1).T.astype(BF16),
        g_ffn=norm_ffn[l][None, :],
        gn=gmlp_norm[l][None, :], wp=wp.astype(BF16), ps=pool_scale[l][None, :],
        wx=wx.reshape(CMP_BLOCK, KV_ROW), pe=cmp_pe[l].transpose(1, 0, 2, 3).reshape(CMP_BLOCK, KV_ROW),
        ws=gmlp_ws[l], wb=jnp.repeat(gmlp_b[l].T, POOL_GROUP_DIM, axis=1),
        wsx=jnp.repeat(gmlp_ws[l].transpose(1, 2, 0), POOL_GROUP_DIM, axis=2)[:8, :8],
        l=l, w_out=w_out.astype(BF16), w_up=w_up.astype(BF16), w_down=w_down.astype(BF16))


def _split_heads(kv, part):
    B, T, _ = kv.shape
    return kv.reshape(B, T, 2, NSA_KV_HEADS, HEAD_DIM)[:, :, part].transpose(0, 2, 1, 3).astype(BF16)


def _prompt_layer(x, p, tabs, B, T, final, gf):
    cfar, nbc, nbn, nbw = tabs
    tm = min(512, B * T)
    xp, q, gt, u, v, kvc, kvc_t, kvs_t, kvw_t = _inproj(x, p['g_mix'], p['w_in'], p['gn'], tm, seq=T,
                                                        wt=p['w_kv_t'])
    comp = _compress_prompt(kvc.reshape(B, T, KV_ROW), p['wx'], p['pe'])
    kc = _split_heads(comp, 0)
    vct = _split_heads(comp, 1).transpose(0, 1, 3, 2)
    q3 = q.reshape(B, T, NSA_WIDTH)
    oc, sn = _cmp_prompt(q3, kc, vct, nbc, cfar)
    pos = jnp.arange(-KPAD, T)
    blk_of = jnp.where(pos >= 0, pos // SEL_BLOCK, LANES - 1)
    onehot_t = (jnp.arange(LANES)[:, None] == blk_of[None, :]).astype(BF16)
    yb = _selwin_prompt(q3, sn, kvs_t, kvw_t, onehot_t, gt, oc, nbn, nbw)
    x2 = _mix_prompt(x, xp, yb.reshape(B * T, NSA_WIDTH), u, v, p['wp'], p['ps'], p['ws'], p['wb'], p['w_out'],
                     p['g_ffn'], p['w_up'], p['w_down'], gf, p['l'], tm, T, final)
    return x2, (xp, kvc_t, kvs_t, kvw_t)


CMP_PAGES_PER_STEP = 32
SEL_PAGES_PER_STEP = 64
SEL_PER_PAGE = PAGE_SIZE // SEL_BLOCK


def _cmp_sample_kernel(pt_ref, *refs, ts):
    del pt_ref
    npg = CMP_PAGES_PER_STEP
    nhalf = KV_ROW // LANES
    pages = refs[:npg]
    wx_ref, pe_ref, q_ref, tab_ref, oc_ref, sel_ref, lo_ref, hi_ref, ps_ref, pg_ref = refs[npg:]
    s = pl.program_id(1)
    rows = PAGE_SIZE // CMP_STRIDE
    for k in range(npg):
        for e in range(nhalf):
            pg_ref[...] = _page_half_rows(pages[k], e)
            lo, hi = _compress_rows(lambda j: pg_ref[pl.ds(j, rows, stride=CMP_STRIDE), :], rows,
                                    wx_ref[:, e * LANES:(e + 1) * LANES])
            lo_ref[s, k * rows:(k + 1) * rows, e * LANES:(e + 1) * LANES] = lo
            hi_ref[s, k * rows:(k + 1) * rows, e * LANES:(e + 1) * LANES] = hi

    @pl.when(s == pl.num_programs(1) - 1)
    def _():
        nc = lo_ref.shape[0] * lo_ref.shape[1]
        comp = _combine_halves(lo_ref[...].reshape(nc, KV_ROW), hi_ref[...].reshape(nc, KV_ROW), pe_ref[...],
                               wx_ref[...])
        ns = nc // 4
        lane = lax.broadcasted_iota(jnp.int32, (ns, LANES), 1)
        s_iota = lax.broadcasted_iota(jnp.int32, (ns, LANES), 0)
        ncol = NSA_GROUP * ts
        selacc = jnp.zeros((ns, LANES), F32)
        for h in range(NSA_KV_HEADS):
            kc = comp[:, h * HEAD_DIM:(h + 1) * HEAD_DIM].astype(BF16)
            vc = comp[:, (NSA_KV_HEADS + h) * HEAD_DIM:(NSA_KV_HEADS + h + 1) * HEAD_DIM].astype(BF16)
            pc = _softmax_rows(_dot_nt(kc, q_ref[0, h]) + tab_ref[h])
            oc_ref[0, h] = pl.dot(pc.astype(BF16), vc, trans_a=True)[0:ncol, :]
            ps = pc
            for g in range(1, NSA_GROUP):
                ps = ps + pltpu.roll(pc, g * ts, 1)
            ps_ref[...] = ps
            p_slc = ps_ref[pl.ds(0, ns, stride=4), :]
            for j in range(1, 4):
                p_slc = p_slc + ps_ref[pl.ds(j, ns, stride=4), :]
            forced = (s_iota == 0) | (s_iota == ns - 1)
            sel = _select_topk(jnp.where(forced, FORCE_SCORE, p_slc), N_SELECT - 1)
            for g in range(NSA_GROUP):
                dst = h * ncol + g * ts
                shift = (dst - (ncol - ts)) % LANES
                moved = pltpu.roll(sel, shift, 1) if shift else sel
                selacc = jnp.where((lane >= dst) & (lane < dst + ts), moved, selacc)
        sel_ref[0] = selacc.reshape(sel_ref.shape[1:])


def _page_half_rows(page_ref, e):
    t = page_ref[0, e]
    return t.reshape(NSA_KV_HEADS * HEAD_DIM, t.shape[-1]).T


def _native_rows(a, lead):
    nd = a.ndim
    t = jnp.transpose(a, tuple(range(nd - 4)) + (nd - 3, nd - 2, nd - 1, nd - 4))
    return t.reshape((lead,) + t.shape[nd - 4:])


def _page_specs(n_pages, npg):
    def spec(k):
        return pl.BlockSpec((1, 2, NSA_KV_HEADS, HEAD_DIM, PAGE_SIZE),
                            lambda b, s, pt: (pt[b * n_pages + s * npg + k], 0, 0, 0, 0))
    return [spec(k) for k in range(npg)]


def _cmp_sample(pt, cache, wx, pe, qs, tab, nb, n_pages, ts):
    npg = CMP_PAGES_PER_STEP
    steps = n_pages // npg
    nc = n_pages * PAGE_SIZE // CMP_STRIDE
    rows = npg * PAGE_SIZE // CMP_STRIDE
    ns = nc // 4
    full = lambda a: pl.BlockSpec(a.shape, lambda b, s, pt: (0,) * a.ndim)
    grid_spec = pltpu.PrefetchScalarGridSpec(
        num_scalar_prefetch=1,
        grid=(nb, steps),
        in_specs=_page_specs(n_pages, npg) + [
            full(wx), full(pe),
            pl.BlockSpec((1, NSA_KV_HEADS, LANES, HEAD_DIM), lambda b, s, pt: (b, 0, 0, 0)),
            full(tab)],
        out_specs=[pl.BlockSpec((1, NSA_KV_HEADS, NSA_GROUP * ts, HEAD_DIM), lambda b, s, pt: (b, 0, 0, 0)),
                   pl.BlockSpec((1, steps, ns // steps, LANES), lambda b, s, pt: (b, 0, 0, 0))],
        scratch_shapes=[pltpu.VMEM((steps, rows, KV_ROW), F32), pltpu.VMEM((steps, rows, KV_ROW), F32),
                        pltpu.VMEM((nc, LANES), F32), pltpu.VMEM((PAGE_SIZE, LANES), F32)])
    return pl.pallas_call(
        functools.partial(_cmp_sample_kernel, ts=ts),
        grid_spec=grid_spec,
        out_shape=[jax.ShapeDtypeStruct((nb, NSA_KV_HEADS, NSA_GROUP * ts, HEAD_DIM), F32),
                   jax.ShapeDtypeStruct((nb, steps, ns // steps, LANES), F32)],
        compiler_params=_cparams(("parallel", "arbitrary")),
        name="cmp_sample",
    )(pt, *([cache] * npg), wx, pe, qs, tab)


def _local_softmax(z, mask):
    m = jnp.max(z, axis=0, keepdims=True)
    p = jnp.where(mask, jnp.exp(z - m), 0.0)
    return m, p, jnp.sum(p, axis=0, keepdims=True)


def _selwin_sample_kernel(pt_ref, *refs, ts):
    del pt_ref
    npg = SEL_PAGES_PER_STEP
    pages = refs[:npg]
    (qa_ref, sel_ref, cf_ref, nbl_ref, st_ref, kvn_ref, nbn_ref, nbw_ref, g_ref, oc_ref,
     o_ref, m_ref, l_ref, a_ref) = refs[npg:]
    s = pl.program_id(1)
    last = pl.num_programs(1) - 1
    ncol = NSA_KV_HEADS * NSA_GROUP * ts
    qa = qa_ref[0]
    rowi = lax.broadcasted_iota(jnp.int32, (PAGE_SIZE, LANES), 0)
    for k in range(npg):
        pg = jnp.concatenate([_page_half_rows(pages[k], 0), _page_half_rows(pages[k], 1)],
                             axis=1).astype(BF16)
        is_last_page = jnp.logical_and(s == last, k == npg - 1)
        bias = jnp.where(is_last_page, nbl_ref[...], cf_ref[0:1, :])
        s0 = sel_ref[0, s, SEL_PER_PAGE * k:SEL_PER_PAGE * k + 1, :]
        s1 = sel_ref[0, s, SEL_PER_PAGE * k + 1:SEL_PER_PAGE * k + 2, :]
        mask = jnp.where(rowi < SEL_BLOCK, s0, s1) > 0.5
        z = jnp.where(mask, _dot(pg, qa) + bias, NEG)
        m, p, l = _local_softmax(z, mask)
        m_ref[s, k:k + 1, :] = m
        l_ref[s, k:k + 1, :] = l
        a_ref[s, k] = pl.dot(p.astype(BF16), pg, trans_a=True)[0:ncol, :]

    @pl.when(s == last)
    def _():
        nbn = nbn_ref[...]
        okn = nbn > 0.5 * NEG
        kn = kvn_ref[0, 0:SUBLANES, :].astype(BF16)
        m_n, p_n, l_n = _local_softmax(jnp.where(okn, _dot(kn, qa) + nbn, NEG), okn)
        a_n = pl.dot(p_n.astype(BF16), kn, trans_a=True)[0:ncol, :]
        nparts = m_ref.shape[0] * m_ref.shape[1]
        m_all = m_ref[...].reshape(nparts, LANES)
        l_all = l_ref[...].reshape(nparts, LANES)
        m_g = jnp.maximum(jnp.max(m_all, axis=0, keepdims=True), m_n)
        w_all = jnp.exp(m_all - m_g)
        w_n = jnp.exp(m_n - m_g)
        l_g = jnp.sum(w_all * l_all, axis=0, keepdims=True) + w_n * l_n

        nbw = nbw_ref[...]
        okw = nbw > 0.5 * NEG
        st = jnp.concatenate([_page_half_rows(st_ref, 0), _page_half_rows(st_ref, 1)],
                             axis=1).astype(BF16)
        kwn = kvn_ref[0, SUBLANES:2 * SUBLANES, :].astype(BF16)
        zw = jnp.where(okw, _dot(st, qa) + nbw, NEG)
        zwn = jnp.where(okn, _dot(kwn, qa) + nbn, NEG)
        m_w = jnp.maximum(jnp.max(zw, axis=0, keepdims=True), jnp.max(zwn, axis=0, keepdims=True))
        pw = jnp.where(okw, jnp.exp(zw - m_w), 0.0)
        pwn = jnp.where(okn, jnp.exp(zwn - m_w), 0.0)
        l_w = jnp.sum(pw, axis=0, keepdims=True) + jnp.sum(pwn, axis=0, keepdims=True)
        a_w = (pl.dot(pw.astype(BF16), st, trans_a=True) + pl.dot(pwn.astype(BF16), kwn, trans_a=True))[0:ncol, :]

        assert nparts + 4 * SUBLANES <= LANES
        rows8 = lambda r: jnp.broadcast_to(r, (SUBLANES, LANES))
        x = jnp.concatenate([w_all, rows8(w_n), rows8(l_g), rows8(l_w),
                             jnp.zeros((LANES - nparts - 3 * SUBLANES, LANES), F32)], axis=0)
        xt = x.T
        acc = xt[0:ncol, nparts:nparts + 1] * a_n
        for pidx in range(nparts):
            acc = acc + xt[0:ncol, pidx:pidx + 1] * a_ref[pidx // npg, pidx % npg]
        o_s = acc / jnp.maximum(xt[0:ncol, nparts + SUBLANES:nparts + SUBLANES + 1], 1e-30)
        o_w = a_w / jnp.maximum(xt[0:ncol, nparts + 2 * SUBLANES:nparts + 2 * SUBLANES + 1], 1e-30)

        ri = lax.broadcasted_iota(jnp.int32, (ncol, HEAD_DIM), 0)
        v_of = lambda t: jnp.where(ri < ncol // NSA_KV_HEADS, t[:, 2 * HEAD_DIM:3 * HEAD_DIM],
                                   t[:, 3 * HEAD_DIM:4 * HEAD_DIM])
        gt = g_ref[0]
        oc = oc_ref[0].reshape(ncol, HEAD_DIM)
        o_ref[0] = gt[:, 0:1] * oc + gt[:, 1:2] * v_of(o_s) + gt[:, 2:3] * v_of(o_w)


def _selwin_sample(pt, cache, qa, sel, cfrow, nbl, state, st_off, kvn, nbn, nbw, gates, oc, nb, n_pages, ts):
    npg = SEL_PAGES_PER_STEP
    steps = n_pages // npg
    ncol = NSA_KV_HEADS * NSA_GROUP * ts
    sel = sel.reshape(nb, steps, -1, LANES)
    full = lambda a: pl.BlockSpec(a.shape, lambda b, s, pt: (0,) * a.ndim)
    perb = lambda a: pl.BlockSpec((1,) + a.shape[1:], lambda b, s, pt: (b,) + (0,) * (a.ndim - 1))
    grid_spec = pltpu.PrefetchScalarGridSpec(
        num_scalar_prefetch=1,
        grid=(nb, steps),
        in_specs=_page_specs(n_pages, npg) + [
            perb(qa), perb(sel), full(cfrow), full(nbl),
            pl.BlockSpec((1,) + state.shape[1:], lambda b, s, pt: (st_off + b, 0, 0, 0, 0)),
            perb(kvn), full(nbn), full(nbw), perb(gates), perb(oc)],
        out_specs=pl.BlockSpec((1, ncol, HEAD_DIM), lambda b, s, pt: (b, 0, 0)),
        scratch_shapes=[pltpu.VMEM((steps, npg, LANES), F32),
                        pltpu.VMEM((steps, npg, LANES), F32),
                        pltpu.VMEM((steps, npg, ncol, KV_ROW), F32)])
    return pl.pallas_call(
        functools.partial(_selwin_sample_kernel, ts=ts),
        grid_spec=grid_spec,
        out_shape=jax.ShapeDtypeStruct((nb, ncol, HEAD_DIM), F32),
        compiler_params=_cparams(("parallel", "arbitrary")),
        name="selwin_sample",
    )(pt, *([cache] * npg), qa, sel, cfrow, nbl, state, kvn, nbn, nbw, gates, oc)


def _sample_bias_tables(rel_bias, past, ts, n_buf):
    ncol = NSA_KV_HEADS * NSA_GROUP * ts
    col = jnp.arange(ncol)
    head = col // ts
    t = col % ts
    pick = lambda b: jnp.take_along_axis(b, jnp.broadcast_to(head, b.shape[:-1])[..., None], axis=-1)[..., 0]
    bias = lambda d: pick(_head_bias(rel_bias, d))
    padc = lambda a, fill=0.0: jnp.pad(a, ((0, 0), (0, LANES - a.shape[1])), constant_values=fill)
    nc = past // CMP_STRIDE
    c_end = jnp.arange(nc)[:, None] * CMP_STRIDE + CMP_BLOCK - 1
    tab = jnp.where(jnp.arange(nc)[:, None] < nc - 1, bias(past + t[None, :] - c_end), NEG)
    hc = ncol // NSA_KV_HEADS
    tab_cmp = jnp.stack([padc(tab[:, h * hc:(h + 1) * hc]) for h in range(NSA_KV_HEADS)])
    far = rel_bias[N_BUCKETS - 1].astype(F32)[head]
    cfrow = jnp.broadcast_to(padc(far[None, :]), (SUBLANES, LANES))
    kk = jnp.arange(PAGE_SIZE)[:, None]
    nbl = padc(bias(PAGE_SIZE + t[None, :] - kk))
    j = jnp.arange(SUBLANES)[:, None]
    dn = t[None, :] - j
    nbn = padc(jnp.where((dn >= 0) & (j < ts), bias(dn), NEG))
    r = jnp.arange(n_buf)[:, None]
    dw = n_buf + t[None, :] - r
    nbw = padc(jnp.where((dw >= 0) & (dw < WINDOW), bias(dw), NEG))
    return tab_cmp, cfrow, nbl, nbn, nbw


def _sample_layer(x, p, tabs, l, pt, cache_cmp, cache_slc, state_win, state_pool_l, nb, ts, n_pages, final, gf):
    tab_cmp, cfrow, nbl, nbn, nbw = tabs
    past = n_pages * PAGE_SIZE
    R = ts * nb
    ncol = NSA_KV_HEADS * NSA_GROUP * ts
    xp, q, gt, u, v, kvc, kvs, kvw = _inproj(x, p['g_mix'], p['w_in'], p['gn'], R)
    q5 = q.reshape(ts, nb, NSA_KV_HEADS, NSA_GROUP, HEAD_DIM)
    qs = q5.transpose(1, 2, 3, 0, 4).reshape(nb, NSA_KV_HEADS, NSA_GROUP * ts, HEAD_DIM)
    qs = jnp.pad(qs, ((0, 0), (0, 0), (0, LANES - NSA_GROUP * ts), (0, 0)))
    oc, sel = _cmp_sample(pt, cache_cmp, p['wx'], p['pe'], qs, tab_cmp, nb, n_pages, ts)
    qt = q5.transpose(1, 2, 4, 3, 0).reshape(nb, NSA_KV_HEADS, HEAD_DIM, NSA_GROUP * ts)
    eye = jnp.eye(NSA_KV_HEADS, dtype=qt.dtype)
    qa = jnp.einsum('bhdc,hk->bhdkc', qt, eye).reshape(nb, NSA_KV_HEADS * HEAD_DIM, ncol)
    qa = jnp.pad(qa, ((0, 0), (0, KV_ROW - NSA_KV_HEADS * HEAD_DIM), (0, LANES - ncol)))
    rows_of = lambda a: jnp.pad(a.reshape(ts, nb, KV_ROW).transpose(1, 0, 2), ((0, 0), (0, SUBLANES - ts), (0, 0)))
    kvn = jnp.concatenate([rows_of(kvs), rows_of(kvw)], axis=1)
    ng = NSA_GROUP * N_BRANCHES
    g5 = jnp.concatenate([gt[:, h * LANES:h * LANES + ng] for h in range(NSA_KV_HEADS)], axis=1)
    g5 = g5.reshape(ts, nb, NSA_KV_HEADS * NSA_GROUP, N_BRANCHES)
    gates = jnp.pad(g5.transpose(1, 2, 0, 3).reshape(nb, ncol, N_BRANCHES), ((0, 0), (0, 0), (0, LANES - N_BRANCHES)))
    o = _selwin_sample(pt, cache_slc, qa, sel, cfrow, nbl, state_win, l * nb, kvn, nbn, nbw, gates, oc,
                       nb, n_pages, ts)
    yb = o.reshape(nb, NSA_KV_HEADS * NSA_GROUP, ts, HEAD_DIM).transpose(2, 0, 1, 3).reshape(R, NSA_WIDTH)
    ext = jnp.concatenate([state_pool_l.transpose(1, 0, 2), xp.reshape(ts, nb, POOL_WIDTH)], axis=0)
    x2 = _mix_sample(x, ext, yb.astype(BF16), u.reshape(ts, nb, GMLP_WIDTH), v.reshape(ts, nb, GMLP_WIDTH),
                     p['wp'], p['ps'], p['wsx'], p['wb'][:SUBLANES], p['w_out'],
                     p['g_ffn'], p['w_up'], p['w_down'], gf, p['l'], nb, ts, past, final)
    return x2, (xp, kvc, kvs, kvw, v)


def kernel(x_prompt, x_sample, cache_cmp_kv, cache_slc_kv, state_win_kv, state_pool, page_table, w_in, w_out,
           norm_mix, norm_ffn, norm_final, w_pool, pool_scale, cmp_w, cmp_pe, gmlp_ws, gmlp_b, gmlp_norm, w_up,
           w_down, rel_bias):
    B, T, _ = x_prompt.shape
    nb, ts, _ = x_sample.shape
    n_pages = page_table.shape[1]
    n_phys = cache_cmp_kv.shape[1]
    n_buf = state_win_kv.shape[2]
    past = n_pages * PAGE_SIZE
    depth = w_in.shape[0]
    assert T % TK_FAR == 0 and T >= WINDOW
    assert n_pages % CMP_PAGES_PER_STEP == 0 and n_pages % SEL_PAGES_PER_STEP == 0
    assert ts <= SUBLANES and ts <= POOL_STATE and n_buf == WINDOW and past >= WINDOW
    kv_tail = (2, NSA_KV_HEADS, HEAD_DIM)

    cache_cmp = _native_rows(cache_cmp_kv, depth * n_phys)
    cache_slc = _native_rows(cache_slc_kv, depth * n_phys)
    state_win = _native_rows(state_win_kv, depth * nb)
    ptabs = _prompt_bias_tables(rel_bias)
    stabs = _sample_bias_tables(rel_bias, past, ts, n_buf)
    gf = norm_final[None, :]
    xp = x_prompt.reshape(B * T, D_MODEL)
    xs = x_sample.transpose(1, 0, 2).reshape(ts * nb, D_MODEL)
    unmajor = lambda a: a.reshape(ts, nb, a.shape[-1]).transpose(1, 0, 2)

    outs = [[] for _ in range(9)]
    for l in range(depth):
        p = _prep_layer(l, w_in, w_out, norm_mix, norm_ffn, w_pool, pool_scale, cmp_w, cmp_pe, gmlp_ws, gmlp_b,
                        gmlp_norm, w_up, w_down)
        final = l == depth - 1
        xp, (pin, kvc_t, kvs_t, kvw_t) = _prompt_layer(xp, p, ptabs, B, T, final, gf)
        rows_of = lambda a: a.reshape((B,) + kv_tail + (a.shape[-1],)).transpose(0, 4, 1, 2, 3)
        outs[0].append(rows_of(kvc_t))
        outs[1].append(rows_of(kvs_t))
        outs[2].append(rows_of(kvw_t[:, :, T - WINDOW:]))
        outs[3].append(pin.reshape(B, T, POOL_WIDTH)[:, T - POOL_STATE:])

        pt = (page_table + l * n_phys).reshape(-1).astype(jnp.int32)
        xs, (sin, kvc_s, kvs_s, kvw_s, v_s) = _sample_layer(
            xs, p, stabs, l, pt, cache_cmp, cache_slc, state_win, state_pool[l], nb, ts, n_pages, final, gf)
        kvw_new = unmajor(kvw_s).reshape((nb, ts) + kv_tail)
        outs[4].append(unmajor(kvc_s).reshape((nb, ts) + kv_tail))
        outs[5].append(unmajor(kvs_s).reshape((nb, ts) + kv_tail))
        outs[6].append(jnp.concatenate([state_win_kv[l][:, ts:], kvw_new], axis=1))
        outs[7].append(jnp.concatenate([state_pool[l][:, ts:], unmajor(sin)], axis=1))
        outs[8].append(unmajor(v_s))

    y_prompt = xp.reshape(B, T, D_MODEL)
    y_sample = unmajor(xs)
    return (y_prompt, y_sample) + tuple(jnp.stack(o) for o in outs)
```

```python
import functools
import math

import numpy as np
import jax
import jax.numpy as jnp
from jax import lax
from jax.experimental import pallas as pl
from jax.experimental.pallas import tpu as pltpu

F32 = jnp.float32
BF16 = jnp.bfloat16

D_MODEL = 1024
DEPTH = 4
PAGE_SIZE = 128
HEAD_DIM = 64
POOL_WINDOWS = (2, 4, 8, 16)
POOL_WIDTH = 256
POOL_GROUP_DIM = 64
POOL_STATE = 15
NSA_WIDTH = 512
NSA_HEADS = 8
NSA_KV_HEADS = 2
NSA_GROUP = 4
KV_ROW = 2 * NSA_KV_HEADS * HEAD_DIM
CMP_STRIDE = 16
CMP_BLOCK = 32
SEL_BLOCK = 64
N_SELECT = 16
WINDOW = 512
N_BRANCHES = 3
FORCE_SCORE = 1000.0
GMLP_WIDTH = 256
GMLP_GROUPS = 4
GMLP_CHUNK = 128
D_FF = 4096
N_BUCKETS = 32
MAX_DISTANCE = 128
EPS = 1e-6
SCALE = HEAD_DIM ** -0.5

LANES = 128
SUBLANES = 8
VMEM_LIMIT = 56 * 1024 * 1024

NEG = -1e30
CMP_PAD = 16
TQ = 128
TQC = 1024
CMP_NEAR = CMP_PAD + TQC // CMP_STRIDE
TK_FAR = 512
KPAD = 512
GATE_LANES = 2 * LANES

PROJ_SEGS = (('xp', 256), ('q', 512), ('kvc', 256), ('kvs', 256), ('kvw', 256), ('gt', 24), ('u', 256), ('v', 256))


def _cparams(sem):
    return pltpu.CompilerParams(dimension_semantics=sem, vmem_limit_bytes=VMEM_LIMIT)


def _dot(a, b):
    return jnp.dot(a, b, preferred_element_type=F32)


def _dot_nt(a, b):
    return lax.dot_general(a, b, (((1,), (1,)), ((), ())), preferred_element_type=F32)


def _gelu(x):
    c = math.sqrt(2.0 / math.pi)
    return 0.5 * x * (1.0 + jnp.tanh(c * (x + 0.044715 * (x * x * x))))


def _rms(x, g):
    return x * lax.rsqrt(jnp.mean(x * x, axis=-1, keepdims=True) + EPS) * g


def _inproj_kernel(x_ref, g_ref, w_ref, gn_ref, *refs, channel_major):
    if channel_major:
        wt_ref, xp_ref, q_ref, gt_ref, u_ref, v_ref, kvc_ref, *kv_refs = refs
    else:
        xp_ref, q_ref, gt_ref, u_ref, v_ref, kvc_ref, *kv_refs = refs
    x = x_ref[...]
    h = _rms(x, g_ref[...]).astype(BF16)

    def seg(lo, hi):
        return _dot(h, w_ref[:, lo:hi])

    xp_ref[...] = seg(0, 256)
    q_ref[...] = (seg(256, 768) * SCALE).astype(BF16)
    kvc_ref[...] = seg(768, 1024)
    if channel_major:
        for k, t_ref in enumerate(kv_refs):
            t_ref[0] = _dot_nt(wt_ref[k * KV_ROW:(k + 1) * KV_ROW, :], h)
    else:
        kv_refs[0][...] = seg(1024, 1280)
        kv_refs[1][...] = seg(1280, 1536)
    u_ref[...] = _gelu(seg(1536, 1792))
    gv = _gelu(seg(1792, 2048))
    sq = gv * gv
    lane = lax.broadcasted_iota(jnp.int32, sq.shape, 1)
    ms = jnp.zeros_like(sq)
    for g in range(GMLP_GROUPS):
        in_g = (lane >= g * 64) & (lane < (g + 1) * 64)
        s = jnp.sum(jnp.where(in_g, sq, 0.0), axis=-1, keepdims=True) * (1.0 / 64.0)
        ms = jnp.where(in_g, s, ms)
    v_ref[...] = gv * lax.rsqrt(ms + EPS) * gn_ref[...]
    gt_ref[...] = jax.nn.sigmoid(seg(2048, 2048 + GATE_LANES))


def _inproj(x, g, w, gn, tm, seq=None, wt=None):
    R = x.shape[0]
    f = lambda n, dt=F32: jax.ShapeDtypeStruct((R, n), dt)
    row = lambda n: pl.BlockSpec((tm, n), lambda i: (i, 0))
    full = lambda a: pl.BlockSpec(a.shape, lambda i: (0,) * a.ndim)
    out_specs = [row(256), row(512), row(GATE_LANES), row(256), row(256), row(256)]
    out_shape = [f(256), f(512, BF16), f(GATE_LANES), f(256), f(256), f(256)]
    if seq is None:
        out_specs += [row(256), row(256)]
        out_shape += [f(256), f(256)]
    else:
        per_seq = seq // tm
        out_specs += [pl.BlockSpec((1, KV_ROW, tm), lambda i: (i // per_seq, 0, i % per_seq))] * 3
        out_shape += [jax.ShapeDtypeStruct((R // seq, KV_ROW, seq), F32)] * 3
    args = (x, g, w, gn) if seq is None else (x, g, w, gn, wt)
    return pl.pallas_call(
        functools.partial(_inproj_kernel, channel_major=seq is not None),
        grid=(R // tm,),
        in_specs=[row(D_MODEL)] + [full(a) for a in args[1:]],
        out_specs=out_specs,
        out_shape=out_shape,
        compiler_params=_cparams(("parallel",)),
        name="inproj",
    )(*args)


def _out_proj(x, ya, yb, yc, wo_ref):
    acc = _dot(ya.astype(BF16), wo_ref[0:256, :])
    acc = acc + _dot(yb, wo_ref[256:768, :])
    acc = acc + _dot(yc.astype(BF16), wo_ref[768:1024, :])
    return x + acc


def _pool_tail(win2, win4, win8, win16, cur, cnt, wp_ref, ps_ref):
    lane = lax.broadcasted_iota(jnp.int32, cur.shape, 1)
    win = jnp.where(lane < 64, win2, jnp.where(lane < 128, win4, jnp.where(lane < 192, win8, win16)))
    pooled = win / cnt - cur
    return _dot(pooled.astype(BF16), wp_ref[...]) * ps_ref[...]


FF_CHUNK = 1024


def _ffn_rows(x, g_ref, wu_ref, wd_ref, gf_ref, final):
    h = _rms(x, g_ref[...]).astype(BF16)
    acc = x
    for c in range(D_FF // FF_CHUNK):
        a = jnp.maximum(_dot(h, wu_ref[:, c * FF_CHUNK:(c + 1) * FF_CHUNK]), 0.0)
        acc = acc + _dot((a * a).astype(BF16), wd_ref[c * FF_CHUNK:(c + 1) * FF_CHUNK, :])
    return _rms(acc, gf_ref[...]) if final else acc


def _layer_spec(w, l, **kw):
    return pl.BlockSpec((None,) + w.shape[1:], lambda i: (l, 0, 0), **kw)


def _ffn_specs(g, wu, wd, gf, l):
    full = lambda a: pl.BlockSpec(a.shape, lambda i: (0,) * a.ndim)
    return [full(g), _layer_spec(wu, l, pipeline_mode=pl.Buffered(1)),
            _layer_spec(wd, l, pipeline_mode=pl.Buffered(1)), full(gf)]


def _mix_prompt_kernel(x_ref, xp_ref, halo_ref, yb_ref, u_ref, v_ref, wp_ref, ps_ref, ws_ref, wb_ref, wo_ref,
                       g_ref, wu_ref, wd_ref, gf_ref, o_ref, ext_ref, *, tm, seq, final):
    i = pl.program_id(0)
    t0 = (i * tm) % seq
    halo = jnp.where(t0 > 0, halo_ref[...], 0.0)
    cur = xp_ref[...]
    ext_ref[0:16, :] = halo
    ext_ref[16:16 + tm, :] = cur
    e = ext_ref[...]
    b2 = e[1:] + e[:-1]
    b4 = b2[2:] + b2[:-2]
    b8 = b4[4:] + b4[:-4]
    b16 = b8[8:] + b8[:-8]
    win2 = b2[15:15 + tm]
    win4 = b4[13:13 + tm]
    win8 = b8[9:9 + tm]
    win16 = b16[1:1 + tm]
    rowi = lax.broadcasted_iota(jnp.int32, cur.shape, 0)
    lane = lax.broadcasted_iota(jnp.int32, cur.shape, 1)
    wsz = jnp.where(lane < 64, 2, jnp.where(lane < 128, 4, jnp.where(lane < 192, 8, 16)))
    cnt = jnp.minimum(t0 + rowi + 1, wsz).astype(F32)
    ya = _pool_tail(win2, win4, win8, win16, cur, cnt, wp_ref, ps_ref)

    ci = lax.broadcasted_iota(jnp.int32, (GMLP_CHUNK, GMLP_CHUNK), 0)
    cj = lax.broadcasted_iota(jnp.int32, (GMLP_CHUNK, GMLP_CHUNK), 1)
    lane_c = lax.broadcasted_iota(jnp.int32, (GMLP_CHUNK, GMLP_WIDTH), 1)
    wts = [jnp.where(ci >= cj, ws_ref[g], 0.0).astype(BF16) for g in range(GMLP_GROUPS)]
    parts = []
    for c in range(tm // GMLP_CHUNK):
        vc = v_ref[c * GMLP_CHUNK:(c + 1) * GMLP_CHUNK, :].astype(BF16)
        s = jnp.zeros((GMLP_CHUNK, GMLP_WIDTH), F32)
        for g in range(GMLP_GROUPS):
            sg = _dot(wts[g], vc)
            s = jnp.where((lane_c >= g * 64) & (lane_c < (g + 1) * 64), sg, s)
        parts.append(u_ref[c * GMLP_CHUNK:(c + 1) * GMLP_CHUNK, :] * (s + wb_ref[...]))
    yc = jnp.concatenate(parts, axis=0) if len(parts) > 1 else parts[0]

    x1 = _out_proj(x_ref[...], ya, yb_ref[...], yc, wo_ref)
    o_ref[...] = _ffn_rows(x1, g_ref, wu_ref, wd_ref, gf_ref, final)


def _mix_prompt(x, xp, yb, u, v, wp, ps, ws, wb, wo, g, wu, wd, gf, l, tm, seq, final):
    R = x.shape[0]
    row = lambda n: pl.BlockSpec((tm, n), lambda i: (i, 0))
    full = lambda a: pl.BlockSpec(a.shape, lambda i: (0,) * a.ndim)
    halo = pl.BlockSpec((16, 256), lambda i: (jnp.maximum(i * (tm // 16) - 1, 0), 0))
    return pl.pallas_call(
        functools.partial(_mix_prompt_kernel, tm=tm, seq=seq, final=final),
        grid=(R // tm,),
        in_specs=[row(D_MODEL), row(256), halo, row(512), row(256), row(256),
                  full(wp), full(ps), full(ws), full(wb), _layer_spec(wo, l)] + _ffn_specs(g, wu, wd, gf, l),
        out_specs=row(D_MODEL),
        out_shape=jax.ShapeDtypeStruct((R, D_MODEL), F32),
        scratch_shapes=[pltpu.VMEM((tm + 16, 256), F32)],
        compiler_params=_cparams(("parallel",)),
        name="mix_prompt",
    )(x, xp, xp, yb, u, v, wp, ps, ws, wb, wo, g, wu, wd, gf)


def _mix_sample_kernel(x_ref, ext_ref, yb_ref, u_ref, v_ref, wp_ref, ps_ref, wsx_ref, wb_ref, wo_ref,
                       g_ref, wu_ref, wd_ref, gf_ref, o_ref, *, nb, ts, past, final):
    lane = lax.broadcasted_iota(jnp.int32, (nb, POOL_WIDTH), 1)
    wsz = jnp.where(lane < 64, 2, jnp.where(lane < 128, 4, jnp.where(lane < 192, 8, 16)))
    yas, ycs = [], []
    for t in range(ts):
        top = POOL_STATE + t
        acc = ext_ref[top] + ext_ref[top - 1]
        wins = [acc]
        for w in (4, 8, 16):
            for j in range(w // 2, w):
                acc = acc + ext_ref[top - j]
            wins.append(acc)
        cnt = jnp.minimum(past + t + 1, wsz).astype(F32)
        yas.append(_pool_tail(wins[0], wins[1], wins[2], wins[3], ext_ref[top], cnt, wp_ref, ps_ref))
        s = wb_ref[t:t + 1, :]
        s = jnp.broadcast_to(s, (nb, GMLP_WIDTH))
        for j in range(t + 1):
            s = s + wsx_ref[t, j:j + 1, :] * v_ref[j]
        ycs.append(u_ref[t] * s)
    ya = jnp.concatenate(yas, axis=0)
    yc = jnp.concatenate(ycs, axis=0)
    x1 = _out_proj(x_ref[...], ya, yb_ref[...], yc, wo_ref)
    o_ref[...] = _ffn_rows(x1, g_ref, wu_ref, wd_ref, gf_ref, final)


def _mix_sample(x, ext, yb, u, v, wp, ps, wsx, wb4, wo, g, wu, wd, gf, l, nb, ts, past, final):
    R = x.shape[0]
    args = (x, ext, yb, u, v, wp, ps, wsx, wb4, wo)
    full = lambda a: pl.BlockSpec(a.shape, lambda i: (0,) * a.ndim)
    return pl.pallas_call(
        functools.partial(_mix_sample_kernel, nb=nb, ts=ts, past=past, final=final),
        grid=(1,),
        in_specs=[full(a) for a in args[:-1]] + [_layer_spec(wo, l)] + _ffn_specs(g, wu, wd, gf, l),
        out_specs=pl.BlockSpec((R, D_MODEL), lambda i: (0, 0)),
        out_shape=jax.ShapeDtypeStruct((R, D_MODEL), F32),
        compiler_params=_cparams(("arbitrary",)),
        name="mix_sample",
    )(*args, g, wu, wd, gf)


def _compress_rows(read_rows, n, wx):
    lo = jnp.zeros((n, wx.shape[1]), F32)
    hi = jnp.zeros((n, wx.shape[1]), F32)
    for j in range(CMP_STRIDE):
        xj = read_rows(j)
        lo = lo + xj * wx[j:j + 1, :]
        hi = hi + xj * wx[CMP_STRIDE + j:CMP_STRIDE + j + 1, :]
    return lo, hi


def _combine_halves(lo, hi, pe, wx):
    n = lo.shape[0]
    pe_term = jnp.sum(pe * wx, axis=0, keepdims=True)
    comp = lo + pltpu.roll(hi, n - 1, 0) + pe_term
    rowi = lax.broadcasted_iota(jnp.int32, comp.shape, 0)
    return jnp.where(rowi < n - 1, comp, 0.0)


def _compress_prompt_kernel(x_ref, wx_ref, pe_ref, o_ref, *, n):
    wx = wx_ref[...]
    lo, hi = _compress_rows(lambda j: x_ref[0, pl.ds(j, n, stride=CMP_STRIDE), :], n, wx)
    o_ref[0, 0:CMP_PAD, :] = jnp.zeros((CMP_PAD, LANES), F32)
    o_ref[0, CMP_PAD:CMP_PAD + n, :] = _combine_halves(lo, hi, pe_ref[...], wx)


def _compress_prompt(kvc, wx, pe):
    B, T, _ = kvc.shape
    n = T // CMP_STRIDE
    half = lambda rows: pl.BlockSpec((rows, LANES), lambda b, e: (0, e))
    return pl.pallas_call(
        functools.partial(_compress_prompt_kernel, n=n),
        grid=(B, KV_ROW // LANES),
        in_specs=[pl.BlockSpec((1, T, LANES), lambda b, e: (b, 0, e)), half(CMP_BLOCK), half(CMP_BLOCK)],
        out_specs=pl.BlockSpec((1, n + CMP_PAD, LANES), lambda b, e: (b, 0, e)),
        out_shape=jax.ShapeDtypeStruct((B, n + CMP_PAD, KV_ROW), F32),
        compiler_params=_cparams(("parallel", "parallel")),
        name="compress_prompt",
    )(kvc, wx, pe)


def _select_topk(score, n_pick):
    s_iota = lax.broadcasted_iota(jnp.int32, score.shape, 0)
    big = score.shape[0]
    work = score
    for _ in range(n_pick):
        m = jnp.max(work, axis=0, keepdims=True)
        idx = jnp.min(jnp.where(work == m, s_iota, big), axis=0, keepdims=True)
        work = jnp.where(s_iota == idx, -2.0, work)
    return jnp.where((work == -2.0) & (score >= 0.0), 1.0, 0.0)


def _softmax_rows(z):
    m = jnp.maximum(jnp.max(z, axis=0, keepdims=True), 0.1 * NEG)
    p = jnp.exp(z - m)
    den = jnp.maximum(jnp.sum(p, axis=0, keepdims=True), 1e-30)
    return p * (1.0 / den)


def _cmp_prompt_kernel(q_ref, kc_ref, vct_ref, nb_ref, cf_ref, oc_ref, sn_ref, raw_ref, z_ref, ps_ref, *, ncp):
    i = pl.program_id(2)
    hh = pl.program_id(1)
    tq = q_ref.shape[1]
    r0 = pl.multiple_of(i * (tq // CMP_STRIDE), SUBLANES)
    q = q_ref[0]
    vct = vct_ref[0, 0]
    rown = lax.broadcasted_iota(jnp.int32, (CMP_NEAR, tq), 0) + r0

    def attend(rows):
        kc = kc_ref[0, 0, 0:rows, :]
        rowi = lax.broadcasted_iota(jnp.int32, (rows, tq), 0)
        far = (rowi >= CMP_PAD) & (rowi < r0)
        psum = jnp.zeros((rows, tq), F32)
        octs = []
        for g in range(NSA_GROUP):
            qg = q[:, g * HEAD_DIM:(g + 1) * HEAD_DIM]
            raw_ref[0:rows, :] = _dot_nt(kc, qg)
            cf = cf_ref[hh * NSA_GROUP + g]
            z_ref[0:rows, :] = jnp.where(far, raw_ref[0:rows, :] + cf[0:1, :], NEG)
            nb = nb_ref[hh * NSA_GROUP + g]
            zn = raw_ref[pl.ds(r0, CMP_NEAR), :] + nb
            z_ref[pl.ds(r0, CMP_NEAR), :] = jnp.where((rown >= CMP_PAD) & (nb > 0.5 * NEG), zn, NEG)
            pc = _softmax_rows(z_ref[0:rows, :])
            psum = psum + pc
            pcb = pc.astype(BF16)
            if rows < ncp:
                pcb = jnp.concatenate([pcb, jnp.zeros((ncp - rows, tq), BF16)], axis=0)
            octs.append(_dot(vct, pcb))
        oc_ref[0] = jnp.concatenate(octs, axis=0).T
        for c in range(tq // LANES):
            ps_ref[c, 0:rows, :] = psum[:, c * LANES:(c + 1) * LANES]
            if rows < ps_ref.shape[1]:
                ps_ref[c, rows:, :] = jnp.zeros((ps_ref.shape[1] - rows, LANES), F32)

    n_var = (ncp - CMP_PAD + LANES - 1) // LANES
    var = (r0 + CMP_NEAR - CMP_PAD + LANES - 1) // LANES - 1
    for v in range(n_var):
        pl.when(var == v)(functools.partial(attend, min(CMP_PAD + (v + 1) * LANES, ncp)))

    ns = LANES
    halves = []
    for c in range(tq // LANES):
        part = ps_ref[c, pl.ds(CMP_PAD, ns, stride=4), :]
        for j in range(1, 4):
            part = part + ps_ref[c, pl.ds(CMP_PAD + j, ns, stride=4), :]
        halves.append(part)
    p_slc = jnp.concatenate(halves, axis=1)
    s_iota = lax.broadcasted_iota(jnp.int32, (ns, tq), 0)
    tt = lax.broadcasted_iota(jnp.int32, (ns, tq), 1)
    cur = (i * tq + tt) // SEL_BLOCK
    forced = (s_iota == 0) | (s_iota == cur) | (s_iota == cur - 1)
    score = jnp.where(s_iota <= cur, jnp.where(forced, FORCE_SCORE, p_slc), -1.0)
    sel = _select_topk(score, N_SELECT)
    sn_ref[0, 0] = jnp.where(sel.T > 0.5, 0.0, NEG).astype(BF16)


def _cmp_prompt(q, kc, vct, nbc, cfar):
    B, T, _ = q.shape
    ncp = kc.shape[2]
    ns = LANES
    nq = T // TQC
    full = lambda a: pl.BlockSpec(a.shape, lambda b, h, i: (0,) * a.ndim)
    return pl.pallas_call(
        functools.partial(_cmp_prompt_kernel, ncp=ncp),
        grid=(B, NSA_KV_HEADS, nq),
        in_specs=[pl.BlockSpec((1, TQC, 256), lambda b, h, i: (b, i, h)),
                  pl.BlockSpec((1, 1, ncp, HEAD_DIM), lambda b, h, i: (b, h, 0, 0)),
                  pl.BlockSpec((1, 1, HEAD_DIM, ncp), lambda b, h, i: (b, h, 0, 0)),
                  full(nbc), full(cfar)],
        out_specs=[pl.BlockSpec((1, TQC, 256), lambda b, h, i: (b, i, h)),
                   pl.BlockSpec((1, 1, TQC, ns), lambda b, h, i: (b, h, i, 0))],
        out_shape=[jax.ShapeDtypeStruct((B, T, NSA_WIDTH), F32),
                   jax.ShapeDtypeStruct((B, NSA_KV_HEADS, T, ns), BF16)],
        scratch_shapes=[pltpu.VMEM((ncp, TQC), F32), pltpu.VMEM((ncp, TQC), F32),
                        pltpu.VMEM((TQC // LANES, max(ncp, CMP_PAD + 4 * LANES), LANES), F32)],
        compiler_params=_cparams(("parallel", "parallel", "parallel")),
        name="cmp_prompt",
    )(q, kc, vct, nbc, cfar)


def _selwin_prompt_kernel(q_ref, sn_ref, ks_ref, vs_ref, kw_ref, vw_ref, oh_ref, g_ref, oc_ref, nbn_ref, nbw_ref,
                          gsel_ref, o_ref, ka_ref, va_ref, kwa_ref, vwa_ref, ms_ref, as_ref, za_ref, zb_ref):
    i = pl.program_id(2)
    R4 = NSA_GROUP * TQ
    ncols = ka_ref.shape[1]

    @pl.when(i == 0)
    def _():
        zpad = jnp.zeros((HEAD_DIM, KPAD), BF16)
        ones = jnp.ones((HEAD_DIM, ncols), BF16)
        row = lax.broadcasted_iota(jnp.int32, (HEAD_DIM, ncols), 0)
        col = lax.broadcasted_iota(jnp.int32, (HEAD_DIM, ncols), 1)
        ka_ref[0:LANES, :] = oh_ref[...]
        kwa_ref[HEAD_DIM:, :] = jnp.where((row == 0) & (col < KPAD), NEG, 0.0).astype(BF16)
        for dst, src in ((ka_ref.at[LANES:], ks_ref), (va_ref.at[0:HEAD_DIM], vs_ref),
                         (kwa_ref.at[0:HEAD_DIM], kw_ref), (vwa_ref.at[0:HEAD_DIM], vw_ref)):
            dst[:, 0:KPAD] = zpad
            dst[:, KPAD:] = src[0].astype(BF16)
        va_ref[HEAD_DIM:, :] = ones
        vwa_ref[HEAD_DIM:, :] = ones

    q = q_ref[0]
    q4 = jnp.concatenate([q[:, g * HEAD_DIM:(g + 1) * HEAD_DIM] for g in range(NSA_GROUP)], axis=0)
    sn = sn_ref[0, 0]
    blk = lax.broadcasted_iota(jnp.int32, (TQ, LANES), 1)
    first_near = (i - 1) * (TQ // SEL_BLOCK)
    sn_far = jnp.where(blk >= first_near, NEG, sn.astype(F32)).astype(BF16)
    qa_near = jnp.concatenate([jnp.concatenate([sn] * NSA_GROUP, axis=0), q4], axis=1)
    qa_far = jnp.concatenate([jnp.concatenate([sn_far] * NSA_GROUP, axis=0), q4], axis=1)
    qw = jnp.concatenate([q4, jnp.ones((R4, HEAD_DIM), BF16)], axis=1)

    def one_pass(z, vt):
        m = jnp.max(z, axis=1, keepdims=True)
        acc = _dot_nt(jnp.exp(z - m).astype(BF16), vt)
        return m, acc

    kw0 = pl.multiple_of(i * TQ, TQ)
    zw = _dot(qw, kwa_ref[:, pl.ds(kw0, WINDOW + TQ)]) + nbw_ref[0]
    _, a_w = one_pass(zw, vwa_ref[:, pl.ds(kw0, WINDOW + TQ)])

    kn0 = pl.multiple_of((i - 1) * TQ + KPAD, TQ)
    zn = _dot(qa_near, ka_ref[:, pl.ds(kn0, 2 * TQ)]) + nbn_ref[0]
    m0, a0 = one_pass(zn, va_ref[:, pl.ds(kn0, 2 * TQ)])
    ms_ref[...] = jnp.broadcast_to(m0, (R4, LANES))
    as_ref[...] = a0

    def far_logits(j):
        k0 = pl.multiple_of(j * TK_FAR + KPAD, TK_FAR)
        return _dot(qa_far, ka_ref[:, pl.ds(k0, TK_FAR)])

    def far_update(j, z):
        k0 = pl.multiple_of(j * TK_FAR + KPAD, TK_FAR)
        m_prev = ms_ref[...]
        m_new = jnp.maximum(m_prev, jnp.max(z, axis=1, keepdims=True))
        p = jnp.exp(z - jnp.concatenate([m_new] * (TK_FAR // LANES), axis=1))
        as_ref[...] = jnp.exp(m_prev - m_new) * as_ref[...] + _dot_nt(p.astype(BF16), va_ref[:, pl.ds(k0, TK_FAR)])
        ms_ref[...] = m_new

    n_far = (jnp.maximum(i - 1, 0) * TQ + TK_FAR - 1) // TK_FAR
    odd = n_far % 2
    last = n_far - 1

    def far_pair(t):
        zb_ref[...] = far_logits(t + 1)
        far_update(t, za_ref[...])
        za_ref[...] = far_logits(jnp.minimum(t + 2, last))
        far_update(t + 1, zb_ref[...])

    n_pairs = (n_far + 1) // 2

    za_ref[...] = far_logits(-odd)

    @pl.when(n_pairs % 2 == 1)
    def _():
        far_pair(-odd)

    def far_body(jj, carry):
        t = 4 * jj + 2 * (n_pairs % 2) - odd
        far_pair(t)
        far_pair(t + 2)
        return carry

    lax.fori_loop(0, n_pairs // 2, far_body, 0)

    gt = g_ref[...]
    lane = lax.broadcasted_iota(jnp.int32, (TQ, LANES), 1)
    own = [jnp.where((lane >= N_BRANCHES * g) & (lane < N_BRANCHES * (g + 1)), gt, 0.0) for g in range(NSA_GROUP)]
    gt4 = jnp.concatenate(own, axis=0)
    hi = gt4.astype(BF16)
    lo = (gt4 - hi.astype(F32)).astype(BF16)
    gates = _dot(jnp.concatenate([hi, lo], axis=1), gsel_ref[...])

    def scaled(acc, gate):
        f = gate / jnp.maximum(acc, 1e-30)
        return acc * pltpu.roll(f, HEAD_DIM, 1)

    o_sw = scaled(as_ref[...], gates[:, LANES:2 * LANES]) + scaled(a_w, gates[:, 2 * LANES:])
    oc = oc_ref[0]
    outs = []
    for g in range(NSA_GROUP):
        sl = slice(g * TQ, (g + 1) * TQ)
        outs.append(gates[sl, 0:HEAD_DIM] * oc[:, g * HEAD_DIM:(g + 1) * HEAD_DIM] + o_sw[sl, 0:HEAD_DIM])
    o_ref[0] = jnp.concatenate(outs, axis=1).astype(BF16)


def _selwin_prompt(q, sn, kvs_t, kvw_t, onehot_t, gates, oc, nbn, nbw):
    B, T, _ = q.shape
    src = jnp.arange(2 * LANES) % LANES
    dst = jnp.arange(N_BRANCHES * LANES) // LANES
    gsel = ((src[:, None] < NSA_GROUP * N_BRANCHES) & (src[:, None] % N_BRANCHES == dst[None, :])).astype(BF16)
    nq = T // TQ
    R4 = NSA_GROUP * TQ
    ncols = T + KPAD
    k_of = pl.BlockSpec((1, HEAD_DIM, T), lambda b, h, i: (b, h, 0))
    v_of = pl.BlockSpec((1, HEAD_DIM, T), lambda b, h, i: (b, NSA_KV_HEADS + h, 0))
    perh = lambda a: pl.BlockSpec((1,) + a.shape[1:], lambda b, h, i: (h, 0, 0))
    qtile = pl.BlockSpec((1, TQ, 256), lambda b, h, i: (b, i, h))
    return pl.pallas_call(
        _selwin_prompt_kernel,
        grid=(B, NSA_KV_HEADS, nq),
        in_specs=[qtile, pl.BlockSpec((1, 1, TQ, sn.shape[-1]), lambda b, h, i: (b, h, i, 0)),
                  k_of, v_of, k_of, v_of, pl.BlockSpec(onehot_t.shape, lambda b, h, i: (0, 0)),
                  pl.BlockSpec((TQ, LANES), lambda b, h, i: (b * nq + i, h)), qtile, perh(nbn), perh(nbw),
                  pl.BlockSpec(gsel.shape, lambda b, h, i: (0, 0))],
        out_specs=qtile,
        out_shape=jax.ShapeDtypeStruct((B, T, NSA_WIDTH), BF16),
        scratch_shapes=[pltpu.VMEM((LANES + HEAD_DIM, ncols), BF16)] + [pltpu.VMEM((LANES, ncols), BF16)] * 3
        + [pltpu.VMEM((R4, LANES), F32)] * 2 + [pltpu.VMEM((R4, TK_FAR), F32)] * 2,
        compiler_params=_cparams(("parallel", "parallel", "arbitrary")),
        name="selwin_prompt",
    )(q, sn, kvs_t, kvs_t, kvw_t, kvw_t, onehot_t, gates, oc, nbn, nbw, gsel)


def _rel_bucket(dist):
    d = jnp.maximum(dist, 0)
    n_exact = N_BUCKETS // 2
    d_f = jnp.maximum(d, 1).astype(F32)
    large = n_exact + (jnp.log(d_f / n_exact) / math.log(MAX_DISTANCE / n_exact)
                       * (N_BUCKETS - n_exact)).astype(jnp.int32)
    large = jnp.minimum(large, N_BUCKETS - 1)
    return jnp.where(d < n_exact, d, large)


def _head_bias(rel_bias, dist):
    onehot = (_rel_bucket(dist)[..., None] == jnp.arange(N_BUCKETS)).astype(F32)
    return jnp.einsum('...k,kh->...h', onehot, rel_bias.astype(F32), precision=lax.Precision.HIGHEST)


def _prompt_bias_tables(rel_bias):
    far = rel_bias[N_BUCKETS - 1].astype(F32)
    cfar = jnp.broadcast_to(far[:, None, None], (NSA_HEADS, SUBLANES, TQC))
    k = jnp.arange(CMP_NEAR)[:, None]
    tt = jnp.arange(TQC)[None, :]
    d = tt + CMP_PAD * CMP_STRIDE - (CMP_BLOCK - 1) - CMP_STRIDE * k
    nbc = jnp.where((d >= 0)[..., None], _head_bias(rel_bias, d), NEG).transpose(2, 0, 1)
    tq = jnp.arange(TQ)[:, None]

    def tile_table(first_dist, n_keys, max_dist):
        d = first_dist + tq - jnp.arange(n_keys)[None, :]
        b = jnp.where(((d >= 0) & (d < max_dist))[..., None], _head_bias(rel_bias, d) - far, NEG)
        return b.transpose(2, 0, 1).reshape(NSA_KV_HEADS, NSA_GROUP * TQ, n_keys)

    nbn = tile_table(TQ, 2 * TQ, 2 * TQ + 1)
    nbw = tile_table(WINDOW, WINDOW + TQ, WINDOW)
    return cfar, nbc, nbn, nbw


def _prep_layer(l, w_in, w_out, norm_mix, norm_ffn, w_pool, pool_scale, cmp_w, cmp_pe, gmlp_ws, gmlp_b,
                gmlp_norm, w_up, w_down):
    w = w_in[l]
    offs = np.cumsum([0] + [n for _, n in PROJ_SEGS])
    seg = {name: w[:, offs[k]:offs[k + 1]] for k, (name, _) in enumerate(PROJ_SEGS)}
    order = ('xp', 'q', 'kvc', 'kvs', 'kvw', 'u', 'v')
    ng = NSA_GROUP * N_BRANCHES
    gpad = jnp.zeros((D_MODEL, LANES - ng), w.dtype)
    gate_cols = [c for h in range(NSA_KV_HEADS) for c in (seg['gt'][:, h * ng:(h + 1) * ng], gpad)]
    w_perm = jnp.concatenate([seg[n] for n in order] + gate_cols, axis=1)
    eye = jnp.eye(len(POOL_WINDOWS), dtype=F32)
    wp = jnp.einsum('gcd,gh->gchd', w_pool[l], eye).reshape(POOL_WIDTH, POOL_WIDTH)
    wx = jnp.broadcast_to(cmp_w[l].transpose(1, 0, 2)[..., None], (CMP_BLOCK, 2, NSA_KV_HEADS, HEAD_DIM))
    return dict(
        w_in=w_perm.astype(BF16), g_mix=norm_mix[l][None, :],
        w_kv_t=jnp.concatenate([seg['kvc'], seg['kvs'], seg['kvw']], axis=1).T.astype(BF16),
        g_ffn=norm_ffn[l][None, :],
        gn=gmlp_norm[l][None, :], wp=wp.astype(BF16), ps=pool_scale[l][None, :],
        wx=wx.reshape(CMP_BLOCK, KV_ROW), pe=cmp_pe[l].transpose(1, 0, 2, 3).reshape(CMP_BLOCK, KV_ROW),
        ws=gmlp_ws[l], wb=jnp.repeat(gmlp_b[l].T, POOL_GROUP_DIM, axis=1),
        wsx=jnp.repeat(gmlp_ws[l].transpose(1, 2, 0), POOL_GROUP_DIM, axis=2)[:8, :8],
        l=l, w_out=w_out.astype(BF16), w_up=w_up.astype(BF16), w_down=w_down.astype(BF16))


def _split_heads(kv, part):
    B, T, _ = kv.shape
    return kv.reshape(B, T, 2, NSA_KV_HEADS, HEAD_DIM)[:, :, part].transpose(0, 2, 1, 3).astype(BF16)


def _prompt_layer(x, p, tabs, B, T, final, gf):
    cfar, nbc, nbn, nbw = tabs
    tm = min(512, B * T)
    xp, q, gt, u, v, kvc, kvc_t, kvs_t, kvw_t = _inproj(x, p['g_mix'], p['w_in'], p['gn'], tm, seq=T,
                                                        wt=p['w_kv_t'])
    comp = _compress_prompt(kvc.reshape(B, T, KV_ROW), p['wx'], p['pe'])
    kc = _split_heads(comp, 0)
    vct = _split_heads(comp, 1).transpose(0, 1, 3, 2)
    q3 = q.reshape(B, T, NSA_WIDTH)
    oc, sn = _cmp_prompt(q3, kc, vct, nbc, cfar)
    pos = jnp.arange(-KPAD, T)
    blk_of = jnp.where(pos >= 0, pos // SEL_BLOCK, LANES - 1)
    onehot_t = (jnp.arange(LANES)[:, None] == blk_of[None, :]).astype(BF16)
    yb = _selwin_prompt(q3, sn, kvs_t, kvw_t, onehot_t, gt, oc, nbn, nbw)
    x2 = _mix_prompt(x, xp, yb.reshape(B * T, NSA_WIDTH), u, v, p['wp'], p['ps'], p['ws'], p['wb'], p['w_out'],
                     p['g_ffn'], p['w_up'], p['w_down'], gf, p['l'], tm, T, final)
    return x2, (xp, kvc_t, kvs_t, kvw_t)


CMP_PAGES_PER_STEP = 32
SEL_PAGES_PER_STEP = 64
SEL_PER_PAGE = PAGE_SIZE // SEL_BLOCK


def _cmp_sample_kernel(pt_ref, *refs, ts):
    del pt_ref
    npg = CMP_PAGES_PER_STEP
    nhalf = KV_ROW // LANES
    pages = refs[:npg]
    wx_ref, pe_ref, q_ref, tab_ref, oc_ref, sel_ref, lo_ref, hi_ref, ps_ref, pg_ref = refs[npg:]
    s = pl.program_id(1)
    rows = PAGE_SIZE // CMP_STRIDE
    for k in range(npg):
        for e in range(nhalf):
            pg_ref[...] = _page_half_rows(pages[k], e)
            lo, hi = _compress_rows(lambda j: pg_ref[pl.ds(j, rows, stride=CMP_STRIDE), :], rows,
                                    wx_ref[:, e * LANES:(e + 1) * LANES])
            lo_ref[s, k * rows:(k + 1) * rows, e * LANES:(e + 1) * LANES] = lo
            hi_ref[s, k * rows:(k + 1) * rows, e * LANES:(e + 1) * LANES] = hi

    @pl.when(s == pl.num_programs(1) - 1)
    def _():
        nc = lo_ref.shape[0] * lo_ref.shape[1]
        comp = _combine_halves(lo_ref[...].reshape(nc, KV_ROW), hi_ref[...].reshape(nc, KV_ROW), pe_ref[...],
                               wx_ref[...])
        ns = nc // 4
        lane = lax.broadcasted_iota(jnp.int32, (ns, LANES), 1)
        s_iota = lax.broadcasted_iota(jnp.int32, (ns, LANES), 0)
        ncol = NSA_GROUP * ts
        z = tab_ref[...]
        vcs = []
        for h in range(NSA_KV_HEADS):
            kc = comp[:, h * HEAD_DIM:(h + 1) * HEAD_DIM].astype(BF16)
            vcs.append(comp[:, (NSA_KV_HEADS + h) * HEAD_DIM:(NSA_KV_HEADS + h + 1) * HEAD_DIM].astype(BF16))
            z = z + _dot_nt(kc, q_ref[0, h])
        pc = _softmax_rows(z)
        pcb = pc.astype(BF16)
        for h in range(NSA_KV_HEADS):
            oc_ref[0, h] = pl.dot(pcb, vcs[h], trans_a=True)[h * ncol:(h + 1) * ncol, :]
        ps = pc
        for g in range(1, NSA_GROUP):
            ps = ps + pltpu.roll(pc, g * ts, 1)
        ps_ref[...] = ps
        p_slc = ps_ref[pl.ds(0, ns, stride=4), :]
        for j in range(1, 4):
            p_slc = p_slc + ps_ref[pl.ds(j, ns, stride=4), :]
        forced = (s_iota == 0) | (s_iota == ns - 1)
        sel = _select_topk(jnp.where(forced, FORCE_SCORE, p_slc), N_SELECT - 1)
        selacc = jnp.zeros((ns, LANES), F32)
        in_heads = lane < NSA_KV_HEADS * ncol
        for g in range(NSA_GROUP):
            shift = (g * ts - (ncol - ts)) % LANES
            moved = pltpu.roll(sel, shift, 1) if shift else sel
            here = in_heads & (lane % ncol >= g * ts) & (lane % ncol < (g + 1) * ts)
            selacc = jnp.where(here, moved, selacc)
        sel_ref[0] = selacc.reshape(sel_ref.shape[1:])


def _page_half_rows(page_ref, e):
    t = page_ref[0, e]
    return t.reshape(NSA_KV_HEADS * HEAD_DIM, t.shape[-1]).T


def _native_rows(a, lead):
    nd = a.ndim
    t = jnp.transpose(a, tuple(range(nd - 4)) + (nd - 3, nd - 2, nd - 1, nd - 4))
    return t.reshape((lead,) + t.shape[nd - 4:])


def _page_specs(n_pages, npg):
    def spec(k):
        return pl.BlockSpec((1, 2, NSA_KV_HEADS, HEAD_DIM, PAGE_SIZE),
                            lambda b, s, pt: (pt[b * n_pages + s * npg + k], 0, 0, 0, 0))
    return [spec(k) for k in range(npg)]


def _cmp_sample(pt, cache, wx, pe, qs, tab, nb, n_pages, ts):
    npg = CMP_PAGES_PER_STEP
    steps = n_pages // npg
    nc = n_pages * PAGE_SIZE // CMP_STRIDE
    rows = npg * PAGE_SIZE // CMP_STRIDE
    ns = nc // 4
    full = lambda a: pl.BlockSpec(a.shape, lambda b, s, pt: (0,) * a.ndim)
    grid_spec = pltpu.PrefetchScalarGridSpec(
        num_scalar_prefetch=1,
        grid=(nb, steps),
        in_specs=_page_specs(n_pages, npg) + [
            full(wx), full(pe),
            pl.BlockSpec((1, NSA_KV_HEADS, LANES, HEAD_DIM), lambda b, s, pt: (b, 0, 0, 0)),
            full(tab)],
        out_specs=[pl.BlockSpec((1, NSA_KV_HEADS, NSA_GROUP * ts, HEAD_DIM), lambda b, s, pt: (b, 0, 0, 0)),
                   pl.BlockSpec((1, steps, ns // steps, LANES), lambda b, s, pt: (b, 0, 0, 0))],
        scratch_shapes=[pltpu.VMEM((steps, rows, KV_ROW), F32), pltpu.VMEM((steps, rows, KV_ROW), F32),
                        pltpu.VMEM((nc, LANES), F32), pltpu.VMEM((PAGE_SIZE, LANES), F32)])
    return pl.pallas_call(
        functools.partial(_cmp_sample_kernel, ts=ts),
        grid_spec=grid_spec,
        out_shape=[jax.ShapeDtypeStruct((nb, NSA_KV_HEADS, NSA_GROUP * ts, HEAD_DIM), F32),
                   jax.ShapeDtypeStruct((nb, steps, ns // steps, LANES), F32)],
        compiler_params=_cparams(("parallel", "arbitrary")),
        name="cmp_sample",
    )(pt, *([cache] * npg), wx, pe, qs, tab)


def _local_softmax(z, mask):
    m = jnp.max(z, axis=0, keepdims=True)
    p = jnp.where(mask, jnp.exp(z - m), 0.0)
    return m, p, jnp.sum(p, axis=0, keepdims=True)


def _selwin_sample_kernel(pt_ref, *refs, ts):
    del pt_ref
    npg = SEL_PAGES_PER_STEP
    pages = refs[:npg]
    (qa_ref, sel_ref, cf_ref, nbl_ref, st_ref, kvn_ref, nbn_ref, nbw_ref, g_ref, oc_ref,
     o_ref, m_ref, l_ref, a_ref) = refs[npg:]
    s = pl.program_id(1)
    last = pl.num_programs(1) - 1
    ncol = NSA_KV_HEADS * NSA_GROUP * ts
    qa = qa_ref[0]
    rowi = lax.broadcasted_iota(jnp.int32, (PAGE_SIZE, LANES), 0)
    for k in range(npg):
        pg = jnp.concatenate([_page_half_rows(pages[k], 0), _page_half_rows(pages[k], 1)],
                             axis=1).astype(BF16)
        is_last_page = jnp.logical_and(s == last, k == npg - 1)
        bias = jnp.where(is_last_page, nbl_ref[...], cf_ref[0:1, :])
        s0 = sel_ref[0, s, SEL_PER_PAGE * k:SEL_PER_PAGE * k + 1, :]
        s1 = sel_ref[0, s, SEL_PER_PAGE * k + 1:SEL_PER_PAGE * k + 2, :]
        mask = jnp.where(rowi < SEL_BLOCK, s0, s1) > 0.5
        z = jnp.where(mask, _dot(pg, qa) + bias, NEG)
        m, p, l = _local_softmax(z, mask)
        m_ref[s, k:k + 1, :] = m
        l_ref[s, k:k + 1, :] = l
        a_ref[s, k] = pl.dot(p.astype(BF16), pg, trans_a=True)[0:ncol, :]

    @pl.when(s == last)
    def _():
        nbn = nbn_ref[...]
        okn = nbn > 0.5 * NEG
        kn = kvn_ref[0, 0:SUBLANES, :].astype(BF16)
        m_n, p_n, l_n = _local_softmax(jnp.where(okn, _dot(kn, qa) + nbn, NEG), okn)
        a_n = pl.dot(p_n.astype(BF16), kn, trans_a=True)[0:ncol, :]
        nparts = m_ref.shape[0] * m_ref.shape[1]
        m_all = m_ref[...].reshape(nparts, LANES)
        l_all = l_ref[...].reshape(nparts, LANES)
        m_g = jnp.maximum(jnp.max(m_all, axis=0, keepdims=True), m_n)
        w_all = jnp.exp(m_all - m_g)
        w_n = jnp.exp(m_n - m_g)
        l_g = jnp.sum(w_all * l_all, axis=0, keepdims=True) + w_n * l_n

        nbw = nbw_ref[...]
        okw = nbw > 0.5 * NEG
        st = jnp.concatenate([_page_half_rows(st_ref, 0), _page_half_rows(st_ref, 1)],
                             axis=1).astype(BF16)
        kwn = kvn_ref[0, SUBLANES:2 * SUBLANES, :].astype(BF16)
        zw = jnp.where(okw, _dot(st, qa) + nbw, NEG)
        zwn = jnp.where(okn, _dot(kwn, qa) + nbn, NEG)
        m_w = jnp.maximum(jnp.max(zw, axis=0, keepdims=True), jnp.max(zwn, axis=0, keepdims=True))
        pw = jnp.where(okw, jnp.exp(zw - m_w), 0.0)
        pwn = jnp.where(okn, jnp.exp(zwn - m_w), 0.0)
        l_w = jnp.sum(pw, axis=0, keepdims=True) + jnp.sum(pwn, axis=0, keepdims=True)
        a_w = (pl.dot(pw.astype(BF16), st, trans_a=True) + pl.dot(pwn.astype(BF16), kwn, trans_a=True))[0:ncol, :]

        assert nparts + 4 * SUBLANES <= LANES
        rows8 = lambda r: jnp.broadcast_to(r, (SUBLANES, LANES))
        x = jnp.concatenate([w_all, rows8(w_n), rows8(l_g), rows8(l_w),
                             jnp.zeros((LANES - nparts - 3 * SUBLANES, LANES), F32)], axis=0)
        xt = x.T
        acc = xt[0:ncol, nparts:nparts + 1] * a_n
        for pidx in range(nparts):
            acc = acc + xt[0:ncol, pidx:pidx + 1] * a_ref[pidx // npg, pidx % npg]
        o_s = acc / jnp.maximum(xt[0:ncol, nparts + SUBLANES:nparts + SUBLANES + 1], 1e-30)
        o_w = a_w / jnp.maximum(xt[0:ncol, nparts + 2 * SUBLANES:nparts + 2 * SUBLANES + 1], 1e-30)

        ri = lax.broadcasted_iota(jnp.int32, (ncol, HEAD_DIM), 0)
        v_of = lambda t: jnp.where(ri < ncol // NSA_KV_HEADS, t[:, 2 * HEAD_DIM:3 * HEAD_DIM],
                                   t[:, 3 * HEAD_DIM:4 * HEAD_DIM])
        gt = g_ref[0]
        oc = oc_ref[0].reshape(ncol, HEAD_DIM)
        o_ref[0] = gt[:, 0:1] * oc + gt[:, 1:2] * v_of(o_s) + gt[:, 2:3] * v_of(o_w)


def _selwin_sample(pt, cache, qa, sel, cfrow, nbl, state, st_off, kvn, nbn, nbw, gates, oc, nb, n_pages, ts):
    npg = SEL_PAGES_PER_STEP
    steps = n_pages // npg
    ncol = NSA_KV_HEADS * NSA_GROUP * ts
    sel = sel.reshape(nb, steps, -1, LANES)
    full = lambda a: pl.BlockSpec(a.shape, lambda b, s, pt: (0,) * a.ndim)
    perb = lambda a: pl.BlockSpec((1,) + a.shape[1:], lambda b, s, pt: (b,) + (0,) * (a.ndim - 1))
    grid_spec = pltpu.PrefetchScalarGridSpec(
        num_scalar_prefetch=1,
        grid=(nb, steps),
        in_specs=_page_specs(n_pages, npg) + [
            perb(qa), perb(sel), full(cfrow), full(nbl),
            pl.BlockSpec((1,) + state.shape[1:], lambda b, s, pt: (st_off + b, 0, 0, 0, 0)),
            perb(kvn), full(nbn), full(nbw), perb(gates), perb(oc)],
        out_specs=pl.BlockSpec((1, ncol, HEAD_DIM), lambda b, s, pt: (b, 0, 0)),
        scratch_shapes=[pltpu.VMEM((steps, npg, LANES), F32),
                        pltpu.VMEM((steps, npg, LANES), F32),
                        pltpu.VMEM((steps, npg, ncol, KV_ROW), F32)])
    return pl.pallas_call(
        functools.partial(_selwin_sample_kernel, ts=ts),
        grid_spec=grid_spec,
        out_shape=jax.ShapeDtypeStruct((nb, ncol, HEAD_DIM), F32),
        compiler_params=_cparams(("parallel", "arbitrary")),
        name="selwin_sample",
    )(pt, *([cache] * npg), qa, sel, cfrow, nbl, state, kvn, nbn, nbw, gates, oc)


def _sample_bias_tables(rel_bias, past, ts, n_buf):
    ncol = NSA_KV_HEADS * NSA_GROUP * ts
    col = jnp.arange(ncol)
    head = col // ts
    t = col % ts
    pick = lambda b: jnp.take_along_axis(b, jnp.broadcast_to(head, b.shape[:-1])[..., None], axis=-1)[..., 0]
    bias = lambda d: pick(_head_bias(rel_bias, d))
    padc = lambda a, fill=0.0: jnp.pad(a, ((0, 0), (0, LANES - a.shape[1])), constant_values=fill)
    nc = past // CMP_STRIDE
    c_end = jnp.arange(nc)[:, None] * CMP_STRIDE + CMP_BLOCK - 1
    tab = jnp.where(jnp.arange(nc)[:, None] < nc - 1, bias(past + t[None, :] - c_end), NEG)
    tab_cmp = padc(tab)
    far = rel_bias[N_BUCKETS - 1].astype(F32)[head]
    cfrow = jnp.broadcast_to(padc(far[None, :]), (SUBLANES, LANES))
    kk = jnp.arange(PAGE_SIZE)[:, None]
    nbl = padc(bias(PAGE_SIZE + t[None, :] - kk))
    j = jnp.arange(SUBLANES)[:, None]
    dn = t[None, :] - j
    nbn = padc(jnp.where((dn >= 0) & (j < ts), bias(dn), NEG))
    r = jnp.arange(n_buf)[:, None]
    dw = n_buf + t[None, :] - r
    nbw = padc(jnp.where((dw >= 0) & (dw < WINDOW), bias(dw), NEG))
    return tab_cmp, cfrow, nbl, nbn, nbw


def _sample_layer(x, p, tabs, l, pt, cache_cmp, cache_slc, state_win, state_pool_l, nb, ts, n_pages, final, gf):
    tab_cmp, cfrow, nbl, nbn, nbw = tabs
    past = n_pages * PAGE_SIZE
    R = ts * nb
    ncol = NSA_KV_HEADS * NSA_GROUP * ts
    xp, q, gt, u, v, kvc, kvs, kvw = _inproj(x, p['g_mix'], p['w_in'], p['gn'], R)
    q5 = q.reshape(ts, nb, NSA_KV_HEADS, NSA_GROUP, HEAD_DIM)
    qs = q5.transpose(1, 2, 3, 0, 4).reshape(nb, NSA_KV_HEADS, NSA_GROUP * ts, HEAD_DIM)
    hc = NSA_GROUP * ts
    qs = jnp.stack([jnp.pad(qs[:, h], ((0, 0), (h * hc, LANES - (h + 1) * hc), (0, 0)))
                    for h in range(NSA_KV_HEADS)], axis=1)
    oc, sel = _cmp_sample(pt, cache_cmp, p['wx'], p['pe'], qs, tab_cmp, nb, n_pages, ts)
    qt = q5.transpose(1, 2, 4, 3, 0).reshape(nb, NSA_KV_HEADS, HEAD_DIM, NSA_GROUP * ts)
    eye = jnp.eye(NSA_KV_HEADS, dtype=qt.dtype)
    qa = jnp.einsum('bhdc,hk->bhdkc', qt, eye).reshape(nb, NSA_KV_HEADS * HEAD_DIM, ncol)
    qa = jnp.pad(qa, ((0, 0), (0, KV_ROW - NSA_KV_HEADS * HEAD_DIM), (0, LANES - ncol)))
    rows_of = lambda a: jnp.pad(a.reshape(ts, nb, KV_ROW).transpose(1, 0, 2), ((0, 0), (0, SUBLANES - ts), (0, 0)))
    kvn = jnp.concatenate([rows_of(kvs), rows_of(kvw)], axis=1)
    ng = NSA_GROUP * N_BRANCHES
    g5 = jnp.concatenate([gt[:, h * LANES:h * LANES + ng] for h in range(NSA_KV_HEADS)], axis=1)
    g5 = g5.reshape(ts, nb, NSA_KV_HEADS * NSA_GROUP, N_BRANCHES)
    gates = jnp.pad(g5.transpose(1, 2, 0, 3).reshape(nb, ncol, N_BRANCHES), ((0, 0), (0, 0), (0, LANES - N_BRANCHES)))
    o = _selwin_sample(pt, cache_slc, qa, sel, cfrow, nbl, state_win, l * nb, kvn, nbn, nbw, gates, oc,
                       nb, n_pages, ts)
    yb = o.reshape(nb, NSA_KV_HEADS * NSA_GROUP, ts, HEAD_DIM).transpose(2, 0, 1, 3).reshape(R, NSA_WIDTH)
    ext = jnp.concatenate([state_pool_l.transpose(1, 0, 2), xp.reshape(ts, nb, POOL_WIDTH)], axis=0)
    x2 = _mix_sample(x, ext, yb.astype(BF16), u.reshape(ts, nb, GMLP_WIDTH), v.reshape(ts, nb, GMLP_WIDTH),
                     p['wp'], p['ps'], p['wsx'], p['wb'][:SUBLANES], p['w_out'],
                     p['g_ffn'], p['w_up'], p['w_down'], gf, p['l'], nb, ts, past, final)
    return x2, (xp, kvc, kvs, kvw, v)


def kernel(x_prompt, x_sample, cache_cmp_kv, cache_slc_kv, state_win_kv, state_pool, page_table, w_in, w_out,
           norm_mix, norm_ffn, norm_final, w_pool, pool_scale, cmp_w, cmp_pe, gmlp_ws, gmlp_b, gmlp_norm, w_up,
           w_down, rel_bias):
    B, T, _ = x_prompt.shape
    nb, ts, _ = x_sample.shape
    n_pages = page_table.shape[1]
    n_phys = cache_cmp_kv.shape[1]
    n_buf = state_win_kv.shape[2]
    past = n_pages * PAGE_SIZE
    depth = w_in.shape[0]
    assert T % TK_FAR == 0 and T >= WINDOW
    assert n_pages % CMP_PAGES_PER_STEP == 0 and n_pages % SEL_PAGES_PER_STEP == 0
    assert ts <= SUBLANES and ts <= POOL_STATE and n_buf == WINDOW and past >= WINDOW
    kv_tail = (2, NSA_KV_HEADS, HEAD_DIM)

    cache_cmp = _native_rows(cache_cmp_kv, depth * n_phys)
    cache_slc = _native_rows(cache_slc_kv, depth * n_phys)
    state_win = _native_rows(state_win_kv, depth * nb)
    ptabs = _prompt_bias_tables(rel_bias)
    stabs = _sample_bias_tables(rel_bias, past, ts, n_buf)
    gf = norm_final[None, :]
    xp = x_prompt.reshape(B * T, D_MODEL)
    xs = x_sample.transpose(1, 0, 2).reshape(ts * nb, D_MODEL)
    unmajor = lambda a: a.reshape(ts, nb, a.shape[-1]).transpose(1, 0, 2)

    outs = [[] for _ in range(9)]
    for l in range(depth):
        p = _prep_layer(l, w_in, w_out, norm_mix, norm_ffn, w_pool, pool_scale, cmp_w, cmp_pe, gmlp_ws, gmlp_b,
                        gmlp_norm, w_up, w_down)
        final = l == depth - 1
        xp, (pin, kvc_t, kvs_t, kvw_t) = _prompt_layer(xp, p, ptabs, B, T, final, gf)
        rows_of = lambda a: a.reshape((B,) + kv_tail + (a.shape[-1],)).transpose(0, 4, 1, 2, 3)
        outs[0].append(rows_of(kvc_t))
        outs[1].append(rows_of(kvs_t))
        outs[2].append(rows_of(kvw_t[:, :, T - WINDOW:]))
        outs[3].append(pin.reshape(B, T, POOL_WIDTH)[:, T - POOL_STATE:])

        pt = (page_table + l * n_phys).reshape(-1).astype(jnp.int32)
        xs, (sin, kvc_s, kvs_s, kvw_s, v_s) = _sample_layer(
            xs, p, stabs, l, pt, cache_cmp, cache_slc, state_win, state_pool[l], nb, ts, n_pages, final, gf)
        kvw_new = unmajor(kvw_s).reshape((nb, ts) + kv_tail)
        outs[4].append(unmajor(kvc_s).reshape((nb, ts) + kv_tail))
        outs[5].append(unmajor(kvs_s).reshape((nb, ts) + kv_tail))
        outs[6].append(jnp.concatenate([state_win_kv[l][:, ts:], kvw_new], axis=1))
        outs[7].append(jnp.concatenate([state_pool[l][:, ts:], unmajor(sin)], axis=1))
        outs[8].append(unmajor(v_s))

    y_prompt = xp.reshape(B, T, D_MODEL)
    y_sample = unmajor(xs)
    return (y_prompt, y_sample) + tuple(jnp.stack(o) for o in outs)
```
